```python
import jax, jax.numpy as jnp
from jax import lax
import numpy as np

D_MODEL = 2048
BATCH = 2
SEQ = 4096
DEPTH = 2

GRID_W = 64
CTX_LEN = 256
MIX_WIDTH = D_MODEL
FNET_WIDTH = MIX_WIDTH // 2
FNET_GROUPS = 4
FNET_GROUP_DIM = FNET_WIDTH // FNET_GROUPS
GMLP_WIDTH = MIX_WIDTH - FNET_WIDTH
GMLP_HEADS = 8
GMLP_HEAD_DIM = GMLP_WIDTH // GMLP_HEADS
GMLP_CHUNK = 128
AB_COLS = FNET_WIDTH + 2 * GMLP_WIDTH
HGRN_WIDTH = MIX_WIDTH // 2
HGRN_EXPAND = 128
HGRN_HEADS = HGRN_WIDTH // HGRN_EXPAND
HGRN_KEY = HGRN_HEADS * HGRN_EXPAND
HGRN_CHUNK = 32
HGRN_COLS = 3 * HGRN_KEY + 2 * HGRN_WIDTH
LRU_WIDTH = MIX_WIDTH - HGRN_WIDTH
LRU_HEADS = 8
LRU_HEAD_DIM = LRU_WIDTH // LRU_HEADS
LRU_CONV = 4
LRU_C = 8.0
CD_COLS = HGRN_COLS + 2 * LRU_WIDTH
N_EXPERTS = 32
TOP_K = 4
D_EXPERT = D_MODEL // 2
SWIGLU_LIMIT = 7.0
SWIGLU_ALPHA = 1.702
MOE_BLOCK = 128
EPS = 1e-6
N_EVEN = (DEPTH + 1) // 2
N_ODD = DEPTH // 2

kernel_name = "hybrid_fnet_gmlp_hgrn2_rglru_moe_dit"


def rmsnorm(x, g):
    xf = x.astype(jnp.float32)
    y = xf * lax.rsqrt(jnp.mean(xf * xf, axis=-1, keepdims=True) + EPS)
    return (y * g.astype(jnp.float32)).astype(x.dtype)


def modulate(h, shift, scale):
    return h * (1 + scale) + shift


def fourier_mix(a):
    b, l, _ = a.shape
    ag = a.reshape(b, l, FNET_GROUPS, FNET_GROUP_DIM).astype(jnp.float32)
    y = jnp.fft.fft2(ag, axes=(1, 3), norm="ortho").real
    return y.reshape(b, l, FNET_WIDTH).astype(a.dtype)


def chunk_gmlp(uv, v_g, w_s, b_s):
    b, l, _ = uv.shape
    z = jax.nn.gelu(uv)
    u, v = jnp.split(z, 2, axis=-1)
    vf = v.astype(jnp.float32)
    mu = jnp.mean(vf, axis=-1, keepdims=True)
    var = jnp.mean(jnp.square(vf - mu), axis=-1, keepdims=True)
    v = ((vf - mu) * lax.rsqrt(var + EPS) * v_g).astype(uv.dtype)
    n = l // GMLP_CHUNK
    vc = v.reshape(b, n, GMLP_CHUNK, GMLP_HEADS, GMLP_HEAD_DIM)
    s = jnp.einsum('hpq,bnqhd->bnphd', w_s, vc) + b_s.T[:, :, None]
    return u * s.reshape(b, l, GMLP_WIDTH)


def ab_mix(p, v_g, w_s, b_s):
    a = fourier_mix(p[..., :FNET_WIDTH])
    g = chunk_gmlp(p[..., FNET_WIDTH:], v_g, w_s, b_s)
    return jnp.concatenate([a, g], axis=-1)


def hgrn_chunk_scan(q, k, v, logf, s0):
    b, h, l, _ = q.shape
    dv = v.shape[-1]
    n = l // HGRN_CHUNK
    blocks = lambda t: t.reshape(b, h, n, HGRN_CHUNK, t.shape[-1])
    q, k, v, logf = blocks(q), blocks(k), blocks(v), blocks(logf)
    cum = jnp.cumsum(logf, axis=3)
    q_dec = q * jnp.exp(cum)
    k_inv = k * jnp.exp(-cum)
    k_end = k * jnp.exp(cum[..., -1:, :] - cum)
    mask = jnp.tril(jnp.ones((HGRN_CHUNK, HGRN_CHUNK), dtype=bool))
    scores = jnp.where(mask, jnp.einsum('bhnik,bhnjk->bhnij', q_dec, k_inv), 0.0)
    intra = jnp.einsum('bhnij,bhnjv->bhniv', scores, v)
    decay_end = jnp.exp(cum[..., -1, :])

    def step(s, inp):
        qd, ke, vc, de = inp
        out = jnp.einsum('bhik,bhkv->bhiv', qd, s)
        s = s * de[..., None] + jnp.einsum('bhik,bhiv->bhkv', ke, vc)
        return s, out

    xs = tuple(jnp.moveaxis(t, 2, 0) for t in (q_dec, k_end, v, decay_end))
    s_final, inter = lax.scan(step, s0, xs)
    o = intra + jnp.moveaxis(inter, 0, 2)
    return o.reshape(b, h, l, dv), s_final


def hgrn_dir(q, k, v, logf, reverse, s0):
    flip = (lambda t: jnp.flip(t, axis=2)) if reverse else (lambda t: t)
    o, s = hgrn_chunk_scan(flip(q), flip(k), flip(v), flip(logf), s0)
    return flip(o), s


def hgrn_inputs(p, lb):
    b, l, _ = p.shape
    q, f_fw, f_bw, i, g = jnp.split(p.astype(jnp.float32),
                                    [HGRN_KEY, 2 * HGRN_KEY, 3 * HGRN_KEY, 3 * HGRN_KEY + HGRN_WIDTH], axis=-1)
    heads = lambda t: t.reshape(b, l, HGRN_HEADS, -1).transpose(0, 2, 1, 3)

    def forget(fr):
        f = lb + (1 - lb) * jax.nn.sigmoid(fr)
        return heads(jnp.log(f)), heads((1 - lb) * jax.nn.sigmoid(-fr))

    return heads(jax.nn.silu(q)), heads(i), forget(f_fw), forget(f_bw), g


def hgrn_readout(o, g, onorm_g, dtype):
    b, h, l, dv = o.shape
    o = o * lax.rsqrt(jnp.mean(o * o, axis=-1, keepdims=True) + EPS)
    o = o.transpose(0, 2, 1, 3).reshape(b, l, h * dv) * onorm_g.astype(jnp.float32)
    return (o * jax.nn.silu(g)).astype(dtype)


def hgrn_mix(p_lat, p_ctx, lb, onorm_g, need_ctx):
    q_l, v_l, (lf_fl, k_fl), (lf_bl, k_bl), g_l = hgrn_inputs(p_lat, lb)
    q_c, v_c, (lf_fc, k_fc), (lf_bc, k_bc), g_c = hgrn_inputs(p_ctx, lb)
    s0 = jnp.zeros((p_lat.shape[0], HGRN_HEADS, HGRN_EXPAND, HGRN_WIDTH // HGRN_HEADS), jnp.float32)
    oc_f, sc_f = hgrn_dir(q_c, k_fc, v_c, lf_fc, False, s0)
    oc_b, sc_b = hgrn_dir(q_c, k_bc, v_c, lf_bc, True, s0)
    ol_f, _ = hgrn_dir(q_l, k_fl, v_l, lf_fl, False, sc_f)
    ol_b, _ = hgrn_dir(q_l, k_bl, v_l, lf_bl, True, sc_b)
    out_l = hgrn_readout(ol_f + ol_b, g_l, onorm_g, p_lat.dtype)
    out_c = hgrn_readout(oc_f + oc_b, g_c, onorm_g, p_ctx.dtype) if need_ctx else None
    return out_l, out_c


def short_conv(x, w, bias):
    l = x.shape[1]
    left = LRU_CONV // 2
    xp = jnp.pad(x, ((0, 0), (left, LRU_CONV - 1 - left), (0, 0)))
    y = xp[:, 0:l] * w[0]
    for j in range(1, LRU_CONV):
        y = y + xp[:, j:j + l] * w[j]
    return y + bias


def blockdiag(x, w):
    b, l, _ = x.shape
    xh = x.reshape(b, l, LRU_HEADS, LRU_HEAD_DIM)
    return jnp.einsum('blhi,hij->blhj', xh, w).reshape(b, l, LRU_WIDTH)


def lru_coeffs(xc, w_r, b_r, w_i, b_i, lam):
    xf = xc.astype(jnp.float32)
    r = jax.nn.sigmoid(blockdiag(xf, w_r.astype(jnp.float32)) + b_r)
    i = jax.nn.sigmoid(blockdiag(xf, w_i.astype(jnp.float32)) + b_i)
    log_a = -LRU_C * r * jax.nn.softplus(-lam.astype(jnp.float32))
    return jnp.exp(log_a), jnp.sqrt(-jnp.expm1(2 * log_a)) * (i * xf)


def linear_scan(a, u, h0, reverse):
    if reverse:
        a, u = jnp.flip(a, axis=1), jnp.flip(u, axis=1)
    u = u.at[:, 0].add(a[:, 0] * h0)
    combine = lambda e, f: (e[0] * f[0], f[0] * e[1] + f[1])
    _, h = lax.associative_scan(combine, (a, u), axis=1)
    last = h[:, -1]
    if reverse:
        h = jnp.flip(h, axis=1)
    return h, last


def lru_mix(p_lat, p_ctx, conv_w, conv_b, w_r, b_r, w_i, b_i, lam, need_ctx):
    gate_l, x_l = jnp.split(p_lat, 2, axis=-1)
    gate_c, x_c = jnp.split(p_ctx, 2, axis=-1)
    x_l = short_conv(x_l, conv_w, conv_b)
    x_c = short_conv(x_c, conv_w, conv_b)
    h0 = jnp.zeros((p_lat.shape[0], LRU_WIDTH), jnp.float32)
    hs_l, hs_c = [], []
    for direction in range(2):
        rev = direction == 1
        a_c, u_c = lru_coeffs(x_c, w_r[direction], b_r[direction], w_i[direction], b_i[direction], lam[direction])
        h_c, last_c = linear_scan(a_c, u_c, h0, rev)
        a_l, u_l = lru_coeffs(x_l, w_r[direction], b_r[direction], w_i[direction], b_i[direction], lam[direction])
        h_l, _ = linear_scan(a_l, u_l, last_c, rev)
        hs_l.append(h_l)
        hs_c.append(h_c)
    out_l = ((hs_l[0] + hs_l[1]) * jax.nn.gelu(gate_l.astype(jnp.float32))).astype(p_lat.dtype)
    out_c = ((hs_c[0] + hs_c[1]) * jax.nn.gelu(gate_c.astype(jnp.float32))).astype(p_ctx.dtype) if need_ctx else None
    return out_l, out_c


def moe(h, w_router, b_router, w_gu, b_gu, w_down, b_down):
    t, d = h.shape
    logits = (h @ w_router + b_router).astype(jnp.float32)
    top_v, top_e = lax.top_k(logits, TOP_K)
    probs = jax.nn.softmax(top_v, axis=-1)
    flat_e = top_e.reshape(-1)
    tk = t * TOP_K
    order = jnp.argsort(flat_e)
    sorted_e = flat_e[order]
    counts = jnp.zeros((N_EXPERTS,), jnp.int32).at[flat_e].add(1)
    starts = jnp.cumsum(counts) - counts
    padded = (counts + MOE_BLOCK - 1) // MOE_BLOCK * MOE_BLOCK
    pend = jnp.cumsum(padded)
    pstart = pend - padded
    dest = pstart[sorted_e] + (jnp.arange(tk, dtype=jnp.int32) - starts[sorted_e])
    n_blocks = -(-(tk + N_EXPERTS * (MOE_BLOCK - 1)) // MOE_BLOCK)
    n_slots = n_blocks * MOE_BLOCK
    slot_tok = jnp.full((n_slots,), t, jnp.int32).at[dest].set((order // TOP_K).astype(jnp.int32))
    slot_w = jnp.zeros((n_slots,), jnp.float32).at[dest].set(probs.reshape(-1)[order])
    block_e = jnp.minimum(jnp.searchsorted(pend, jnp.arange(n_blocks, dtype=jnp.int32) * MOE_BLOCK, side='right'),
                          N_EXPERTS - 1)
    h_pad = jnp.concatenate([h, jnp.zeros((1, d), h.dtype)], axis=0)
    xb = h_pad[slot_tok].reshape(n_blocks, MOE_BLOCK, d)

    def expert_block(args):
        xblk, e = args
        gu = xblk @ w_gu[e] + b_gu[e]
        gate = jnp.minimum(gu[:, :D_EXPERT], SWIGLU_LIMIT)
        up = jnp.clip(gu[:, D_EXPERT:], -SWIGLU_LIMIT, SWIGLU_LIMIT)
        act = gate * jax.nn.sigmoid(SWIGLU_ALPHA * gate) * (up + 1)
        return act @ w_down[e] + b_down[e]

    yb = lax.map(expert_block, (xb, block_e)).reshape(n_slots, d)
    out = jnp.zeros((t + 1, d), yb.dtype).at[slot_tok].add(yb * slot_w[:, None].astype(yb.dtype))
    return out[:t]


def setup_inputs(seed: int = 0) -> dict:
    key = jax.random.key(seed)
    ks = jax.random.split(key, 32)
    nrm = lambda k, shape, s: jax.random.normal(k, shape, jnp.float32) * s
    D = D_MODEL
    a_c = jax.random.uniform(ks[22], (N_ODD, 2, LRU_WIDTH), jnp.float32, minval=0.9, maxval=0.999)
    a = a_c ** (1.0 / LRU_C)
    return {
        "x": nrm(ks[0], (BATCH, SEQ, D), 1.0),
        "c": nrm(ks[1], (BATCH, D), 1.0),
        "ctx": nrm(ks[2], (BATCH, CTX_LEN, D), 1.0),
        "c_ctx": nrm(ks[3], (D,), 1.0),
        "mod_w": nrm(ks[4], (DEPTH, D, 6 * D), 0.5 * D ** -0.5),
        "mod_b": nrm(ks[5], (DEPTH, 6 * D), 0.02),
        "norm_g": 1.0 + nrm(ks[6], (DEPTH, 4, D), 0.02),
        "ab_w_in": nrm(ks[7], (N_EVEN, D, AB_COLS), D ** -0.5),
        "ab_w_out": nrm(ks[8], (N_EVEN, MIX_WIDTH, D), MIX_WIDTH ** -0.5),
        "gmlp_v_g": 1.0 + nrm(ks[9], (N_EVEN, GMLP_WIDTH), 0.02),
        "gmlp_ws": nrm(ks[10], (N_EVEN, GMLP_HEADS, GMLP_CHUNK, GMLP_CHUNK), GMLP_CHUNK ** -0.5),
        "gmlp_bs": 1.0 + nrm(ks[11], (N_EVEN, GMLP_HEADS, GMLP_CHUNK), 0.02),
        "cd_w_in": nrm(ks[12], (N_ODD, D, CD_COLS), D ** -0.5),
        "cd_w_out": nrm(ks[13], (N_ODD, MIX_WIDTH, D), MIX_WIDTH ** -0.5),
        "hgrn_lb": nrm(ks[14], (DEPTH, HGRN_KEY), 0.1),
        "hgrn_onorm_g": 1.0 + nrm(ks[15], (N_ODD, HGRN_WIDTH), 0.02),
        "lru_conv_w": nrm(ks[16], (N_ODD, LRU_CONV, LRU_WIDTH), LRU_CONV ** -0.5),
        "lru_conv_b": nrm(ks[17], (N_ODD, LRU_WIDTH), 0.02),
        "lru_wr": nrm(ks[18], (N_ODD, 2, LRU_HEADS, LRU_HEAD_DIM, LRU_HEAD_DIM), LRU_HEAD_DIM ** -0.5),
        "lru_br": nrm(ks[19], (N_ODD, 2, LRU_WIDTH), 0.02),
        "lru_wi": nrm(ks[20], (N_ODD, 2, LRU_HEADS, LRU_HEAD_DIM, LRU_HEAD_DIM), LRU_HEAD_DIM ** -0.5),
        "lru_bi": nrm(ks[21], (N_ODD, 2, LRU_WIDTH), 0.02),
        "lru_lambda": jnp.log(a) - jnp.log1p(-a),
        "router_w": nrm(ks[23], (DEPTH, D, N_EXPERTS), D ** -0.5),
        "router_b": nrm(ks[24], (DEPTH, N_EXPERTS), 0.01),
        "exp_w_gu": nrm(ks[25], (DEPTH, N_EXPERTS, D, 2 * D_EXPERT), D ** -0.5),
        "exp_b_gu": nrm(ks[26], (DEPTH, N_EXPERTS, 2 * D_EXPERT), 0.02),
        "exp_w_down": nrm(ks[27], (DEPTH, N_EXPERTS, D_EXPERT, D), D_EXPERT ** -0.5),
        "exp_b_down": nrm(ks[28], (DEPTH, N_EXPERTS, D), 0.02),
    }


def reference(x, c, ctx, c_ctx, mod_w, mod_b, norm_g, ab_w_in, ab_w_out, gmlp_v_g, gmlp_ws, gmlp_bs,
              cd_w_in, cd_w_out, hgrn_lb, hgrn_onorm_g, lru_conv_w, lru_conv_b, lru_wr, lru_br, lru_wi, lru_bi,
              lru_lambda, router_w, router_b, exp_w_gu, exp_b_gu, exp_w_down, exp_b_down):
    b, l, d = x.shape
    n_lat = b * l
    lb_soft = jax.nn.softmax(hgrn_lb.astype(jnp.float32), axis=0)
    lb_all = jnp.cumsum(lb_soft, axis=0) - lb_soft[0]
    cond_l = jax.nn.silu(c)[:, None, :]
    cond_c = jax.nn.silu(c_ctx)[None, None, :]
    xl, xc = x, ctx
    for layer in range(DEPTH):
        last = layer == DEPTH - 1
        odd = layer % 2 == 1
        j = layer // 2
        use_ctx = (not last) or odd
        sh1, sc1, ga1, sh2, sc2, ga2 = jnp.split(cond_l @ mod_w[layer] + mod_b[layer], 6, axis=-1)
        hl = modulate(rmsnorm(xl, norm_g[layer, 0]), sh1, sc1)
        if use_ctx:
            csh1, csc1, cga1, csh2, csc2, cga2 = jnp.split(cond_c @ mod_w[layer] + mod_b[layer], 6, axis=-1)
            hc = modulate(rmsnorm(xc, norm_g[layer, 0]), csh1, csc1)
        if not odd:
            w_out = ab_w_out[j]
            mix_l = ab_mix(hl @ ab_w_in[j], gmlp_v_g[j], gmlp_ws[j], gmlp_bs[j])
            mix_c = None if last else ab_mix(hc @ ab_w_in[j], gmlp_v_g[j], gmlp_ws[j], gmlp_bs[j])
        else:
            w_out = cd_w_out[j]
            pl = hl @ cd_w_in[j]
            pc = hc @ cd_w_in[j]
            hg_l, hg_c = hgrn_mix(pl[..., :HGRN_COLS], pc[..., :HGRN_COLS], lb_all[layer], hgrn_onorm_g[j], not last)
            lr_l, lr_c = lru_mix(pl[..., HGRN_COLS:], pc[..., HGRN_COLS:], lru_conv_w[j], lru_conv_b[j],
                                 lru_wr[j], lru_br[j], lru_wi[j], lru_bi[j], lru_lambda[j], not last)
            mix_l = jnp.concatenate([hg_l, lr_l], axis=-1)
            mix_c = None if last else jnp.concatenate([hg_c, lr_c], axis=-1)
        xl = xl + ga1 * rmsnorm(mix_l @ w_out, norm_g[layer, 1])
        hl2 = modulate(rmsnorm(xl, norm_g[layer, 2]), sh2, sc2).reshape(n_lat, d)
        moe_args = (router_w[layer], router_b[layer], exp_w_gu[layer], exp_b_gu[layer],
                    exp_w_down[layer], exp_b_down[layer])
        if last:
            y = moe(hl2, *moe_args)
        else:
            xc = xc + cga1 * rmsnorm(mix_c @ w_out, norm_g[layer, 1])
            hc2 = modulate(rmsnorm(xc, norm_g[layer, 2]), csh2, csc2)
            y = moe(jnp.concatenate([hl2, hc2.reshape(-1, d)], axis=0), *moe_args)
            xc = xc + cga2 * rmsnorm(y[n_lat:].reshape(xc.shape), norm_g[layer, 3])
        xl = xl + ga2 * rmsnorm(y[:n_lat].reshape(b, l, d), norm_g[layer, 3])
    return xl
```

```python
import functools

import numpy as np
import jax
import jax.numpy as jnp
from jax import lax
from jax.experimental import pallas as pl
from jax.experimental.pallas import tpu as pltpu

F32 = jnp.float32
BF16 = jnp.bfloat16
HI = lax.Precision.HIGHEST

D = 2048
B = 2
L = 4096
LC = 256
T_LAT = B * L
T_CTX = B * LC
T_ALL = T_LAT + T_CTX
DEPTH = 2
EPS = 1e-6

HALF = D // 2
N_EXPERTS = 32
TOP_K = 4
D_EXPERT = D // 2
SWIGLU_LIMIT = 7.0
SWIGLU_ALPHA = 1.702
MOE_TM = 256

FNET_GROUPS = 4
FNET_GD = HALF // FNET_GROUPS
GMLP_HEADS = 8
GMLP_CHUNK = 128
HGRN_HEADS = 8
HGRN_R = 128
HGRN_C = 32
LRU_HEADS = 8
LRU_R = 128
LRU_C = 8.0

VMEM_LIMIT = 56 * 1024 * 1024


def _cparams(sem, vmem=None):
    return pltpu.CompilerParams(dimension_semantics=sem, vmem_limit_bytes=vmem)


def _mod_row(i, tm):
    return jnp.where(i < L // tm, 0, jnp.where(i < 2 * L // tm, 1, 2))


def _rms(x):
    return x * lax.rsqrt(jnp.mean(x * x, axis=-1, keepdims=True) + EPS)


def _bdot(a, b):
    return jnp.dot(a.astype(BF16), b.astype(BF16), preferred_element_type=F32)


ROW_S = D // 128


def _store_rows(ref, val):
    for s in range(ROW_S):
        ref[:, s, :] = val[:, s * 128:(s + 1) * 128]


def _load_rows(ref):
    return jnp.concatenate([ref[:, s, :] for s in range(ROW_S)], axis=1)


def _mod_kernel(c_ref, w_ref, b_ref, o_ref):
    cnd = c_ref[...]
    s = cnd * jax.nn.sigmoid(cnd)
    o_ref[...] = jnp.dot(s, w_ref[...], precision=HI, preferred_element_type=F32) + b_ref[...]


def _modulation(cond8, mod_w, mod_b):
    tn = 1024
    n = mod_w.shape[-1]
    return pl.pallas_call(
        _mod_kernel,
        grid=(DEPTH, n // tn),
        in_specs=[
            pl.BlockSpec((8, D), lambda l, j: (0, 0)),
            pl.BlockSpec((None, D, tn), lambda l, j: (l, 0, j)),
            pl.BlockSpec((None, 1, tn), lambda l, j: (l, 0, j)),
        ],
        out_specs=pl.BlockSpec((None, 8, tn), lambda l, j: (l, 0, j)),
        out_shape=jax.ShapeDtypeStruct((DEPTH, 8, n), F32),
        compiler_params=_cparams(("parallel", "parallel"), VMEM_LIMIT),
        name="modulation",
    )(cond8, mod_w, mod_b.reshape(DEPTH, 1, n))


def _inproj_kernel(x_ref, g_ref, sh_ref, sc_ref, w_ref, o_ref, h_scr):
    @pl.when(pl.program_id(1) == 0)
    def _():
        y = _rms(x_ref[...]) * g_ref[...]
        h_scr[...] = (y * (1.0 + sc_ref[...]) + sh_ref[...]).astype(BF16)

    o_ref[...] = jnp.dot(h_scr[...], w_ref[...], preferred_element_type=F32)


def _inproj(xs, g, modm, layer, w_bf):
    tm, tn = 512, 512
    t, n = xs.shape[0], w_bf.shape[1]
    row = lambda i: layer * 8 + _mod_row(i, tm)
    return pl.pallas_call(
        _inproj_kernel,
        grid=(t // tm, n // tn),
        in_specs=[
            pl.BlockSpec((tm, D), lambda i, j: (i, 0)),
            pl.BlockSpec((1, D), lambda i, j: (0, 0)),
            pl.BlockSpec((None, 1, D), lambda i, j: (row(i), 0, 0)),
            pl.BlockSpec((None, 1, D), lambda i, j: (row(i), 0, 1)),
            pl.BlockSpec((D, tn), lambda i, j: (0, j)),
        ],
        out_specs=pl.BlockSpec((tm, tn), lambda i, j: (i, j)),
        out_shape=jax.ShapeDtypeStruct((t, n), F32),
        scratch_shapes=[pltpu.VMEM((tm, D), BF16)],
        compiler_params=_cparams(("parallel", "arbitrary"), VMEM_LIMIT),
        name="inproj",
    )(xs, g.reshape(1, D), modm, modm, w_bf)


def _fnet_chan_kernel(p_ref, cs_ref, zc_ref, zs_ref):
    for g in range(FNET_GROUPS):
        sl = slice(g * FNET_GD, (g + 1) * FNET_GD)
        z = jnp.dot(p_ref[:, sl].astype(BF16), cs_ref[...], preferred_element_type=F32)
        zc_ref[:, sl] = z[:, :FNET_GD].astype(BF16)
        zs_ref[:, sl] = z[:, FNET_GD:].astype(BF16)


def _fnet_chan(p):
    tm = 512
    t = p.shape[0]
    k = np.arange(FNET_GD)
    ang = 2.0 * np.pi * ((k[:, None] * k[None, :]) % FNET_GD) / FNET_GD
    cs = np.concatenate([np.cos(ang), np.sin(ang)], axis=1) / np.sqrt(FNET_GD)
    cs = jnp.asarray(cs, F32).astype(BF16)
    return pl.pallas_call(
        _fnet_chan_kernel,
        grid=(t // tm,),
        in_specs=[
            pl.BlockSpec((tm, HALF), lambda i: (i, 0)),
            pl.BlockSpec((FNET_GD, 2 * FNET_GD), lambda i: (0, 0)),
        ],
        out_specs=[pl.BlockSpec((tm, HALF), lambda i: (i, 0))] * 2,
        out_shape=[jax.ShapeDtypeStruct((t, HALF), BF16)] * 2,
        compiler_params=_cparams(("parallel",), VMEM_LIMIT),
        name="fnet_chan",
    )(p, cs)


def _fnet_pos_kernel(bre_ref, bim_ref, cre_ref, cim_ref, rre_ref, rim_ref, zc_ref, zs_ref, o_ref, acc):
    nt = pl.program_id(2)
    br, bi = bre_ref[...], bim_ref[...]
    cr, ci = cre_ref[...], cim_ref[...]
    rr, ri = rre_ref[...], rim_ref[...]
    tr = br * cr - bi * ci
    ti = br * ci + bi * cr
    er = (tr * rr - ti * ri).astype(BF16)
    ei = (tr * ri + ti * rr).astype(BF16)
    part = (jnp.dot(er, zc_ref[...], preferred_element_type=F32)
            + jnp.dot(ei, zs_ref[...], preferred_element_type=F32))

    @pl.when(nt == 0)
    def _():
        acc[...] = part

    @pl.when(nt > 0)
    def _():
        acc[...] += part

    @pl.when(nt == pl.num_programs(2) - 1)
    def _():
        o_ref[...] = acc[...].astype(BF16)


def _fnet_pos(zc, zs, row0, seq, tk, tn):
    nkt, nnt = seq // tk, seq // tn
    th = 2.0 * np.pi / seq
    kk = np.arange(tk)[:, None]
    nn = np.arange(tn)[None, :]
    base = th * ((kk * nn) % seq)
    n0 = (np.arange(nnt) * tn)[:, None, None]
    col = th * ((np.arange(tk)[None, :, None] * n0) % seq)
    k0 = (np.arange(nkt) * tk)[:, None, None, None]
    nfull = (np.arange(nnt) * tn)[None, :, None, None] + np.arange(tn)[None, None, None, :]
    row = th * ((k0 * nfull) % seq)
    scale = 1.0 / np.sqrt(seq)
    f = lambda a: jnp.asarray(a, F32)
    tabs = (f(np.cos(base)), f(-np.sin(base)), f(np.cos(col)), f(-np.sin(col)),
            f(np.cos(row) * scale), f(-np.sin(row) * scale))
    rb = row0 // tn
    ob = row0 // tk
    return pl.pallas_call(
        _fnet_pos_kernel,
        grid=(B, nkt, nnt),
        in_specs=[
            pl.BlockSpec((tk, tn), lambda b, k, n: (0, 0)),
            pl.BlockSpec((tk, tn), lambda b, k, n: (0, 0)),
            pl.BlockSpec((None, tk, 1), lambda b, k, n: (n, 0, 0)),
            pl.BlockSpec((None, tk, 1), lambda b, k, n: (n, 0, 0)),
            pl.BlockSpec((None, None, 1, tn), lambda b, k, n: (k, n, 0, 0)),
            pl.BlockSpec((None, None, 1, tn), lambda b, k, n: (k, n, 0, 0)),
            pl.BlockSpec((tn, HALF), lambda b, k, n: (rb + b * nnt + n, 0)),
            pl.BlockSpec((tn, HALF), lambda b, k, n: (rb + b * nnt + n, 0)),
        ],
        out_specs=pl.BlockSpec((tk, HALF), lambda b, k, n: (b * nkt + k, 0)),
        out_shape=jax.ShapeDtypeStruct((B * seq, HALF), BF16),
        scratch_shapes=[pltpu.VMEM((tk, HALF), F32)],
        compiler_params=_cparams(("parallel", "parallel", "arbitrary"), VMEM_LIMIT),
        name=f"fnet_pos_{seq}",
    )(*tabs, zc, zs)


def _gmlp_kernel(u_ref, v_ref, vg_ref, ws_ref, bst_ref, o_ref, *, tm):
    zu = jax.nn.gelu(u_ref[...])
    zv = jax.nn.gelu(v_ref[...])
    mu = jnp.mean(zv, axis=-1, keepdims=True)
    dv = zv - mu
    var = jnp.mean(dv * dv, axis=-1, keepdims=True)
    vn = dv * lax.rsqrt(var + EPS) * vg_ref[...]
    hd = HALF // GMLP_HEADS
    for c in range(tm // GMLP_CHUNK):
        rs = slice(c * GMLP_CHUNK, (c + 1) * GMLP_CHUNK)
        for h in range(GMLP_HEADS):
            cs = slice(h * hd, (h + 1) * hd)
            s = jnp.dot(ws_ref[h], vn[rs, cs], precision=HI, preferred_element_type=F32)
            s = s + bst_ref[:, h:h + 1]
            o_ref[rs, cs] = (zu[rs, cs] * s).astype(BF16)


def _gmlp(p, v_g, w_s, b_s):
    tm = 256
    t = p.shape[0]
    return pl.pallas_call(
        functools.partial(_gmlp_kernel, tm=tm),
        grid=(t // tm,),
        in_specs=[
            pl.BlockSpec((tm, HALF), lambda i: (i, 1)),
            pl.BlockSpec((tm, HALF), lambda i: (i, 2)),
            pl.BlockSpec((1, HALF), lambda i: (0, 0)),
            pl.BlockSpec((GMLP_HEADS, GMLP_CHUNK, GMLP_CHUNK), lambda i: (0, 0, 0)),
            pl.BlockSpec((GMLP_CHUNK, GMLP_HEADS), lambda i: (0, 0)),
        ],
        out_specs=pl.BlockSpec((tm, HALF), lambda i: (i, 0)),
        out_shape=jax.ShapeDtypeStruct((t, HALF), BF16),
        compiler_params=_cparams(("parallel",), VMEM_LIMIT),
        name="gmlp",
    )(p, p, v_g.reshape(1, HALF), w_s, b_s.T)


def _outproj_kernel(ma_ref, mb_ref, w_ref, x_ref, g1_ref, g2_ref, ga1_ref, sh2_ref, sc2_ref,
                    rw_ref, rb_ref, xo_ref, h2_ref, e_ref, p_ref, rk_ref, cnt_ref, carry, *, tm):
    @pl.when(pl.program_id(0) == 0)
    def _():
        carry[...] = jnp.zeros_like(carry)

    m = (jnp.dot(ma_ref[...], w_ref[:HALF, :], preferred_element_type=F32)
         + jnp.dot(mb_ref[...], w_ref[HALF:, :], preferred_element_type=F32))
    xn = x_ref[...] + ga1_ref[...] * (_rms(m) * g1_ref[...])
    xo_ref[...] = xn
    h2 = (_rms(xn) * g2_ref[...]) * (1.0 + sc2_ref[...]) + sh2_ref[...]
    _store_rows(h2_ref, h2)

    logits = jnp.dot(h2, rw_ref[...], precision=HI, preferred_element_type=F32) + rb_ref[...]
    iota = lax.broadcasted_iota(jnp.int32, (tm, N_EXPERTS), 1).astype(F32)
    lane4 = lax.broadcasted_iota(jnp.int32, (tm, TOP_K), 1)
    lg = logits
    idxs, vals = [], []
    for _ in range(TOP_K):
        mx = jnp.max(lg, axis=-1, keepdims=True)
        ix = jnp.min(jnp.where(lg == mx, iota, float(N_EXPERTS)), axis=-1, keepdims=True)
        idxs.append(ix)
        vals.append(mx)
        lg = jnp.where(iota == ix, -jnp.inf, lg)
    exs = [jnp.exp(v - vals[0]) for v in vals]
    den = exs[0] + exs[1] + exs[2] + exs[3]

    onehot = [(iota == ix) for ix in idxs]
    cnt = jnp.zeros((tm, N_EXPERTS), F32)
    for oh in onehot:
        cnt = cnt + oh.astype(F32)
    r_i = lax.broadcasted_iota(jnp.int32, (tm, tm), 0)
    c_i = lax.broadcasted_iota(jnp.int32, (tm, tm), 1)
    tri = (c_i < r_i).astype(BF16)
    prefix = jnp.dot(tri, cnt.astype(BF16), preferred_element_type=F32) + carry[...]

    e4 = jnp.zeros((tm, TOP_K), jnp.int32)
    p4 = jnp.zeros((tm, TOP_K), F32)
    r4 = jnp.zeros((tm, TOP_K), jnp.int32)
    for k in range(TOP_K):
        rk = jnp.sum(jnp.where(onehot[k], prefix, 0.0), axis=-1, keepdims=True)
        e4 = jnp.where(lane4 == k, idxs[k].astype(jnp.int32), e4)
        p4 = jnp.where(lane4 == k, exs[k] / den, p4)
        r4 = jnp.where(lane4 == k, rk.astype(jnp.int32), r4)
    e_ref[...] = e4
    p_ref[...] = p4
    rk_ref[...] = r4
    carry[...] += jnp.sum(cnt, axis=0, keepdims=True)
    cnt_ref[...] = carry[...]


def _outproj(ma, mb, w_bf, xs, t, g1, g2, modm, layer, rw, rb):
    tm = 256
    row = lambda i: layer * 8 + _mod_row(i, tm)
    modspec = lambda c: pl.BlockSpec((None, 1, D), lambda i: (row(i), 0, c))
    vec = pl.BlockSpec((1, D), lambda i: (0, 0))
    small = lambda dt: jax.ShapeDtypeStruct((t, TOP_K), dt)
    return pl.pallas_call(
        functools.partial(_outproj_kernel, tm=tm),
        grid=(t // tm,),
        in_specs=[
            pl.BlockSpec((tm, HALF), lambda i: (i, 0)),
            pl.BlockSpec((tm, HALF), lambda i: (i, 0)),
            pl.BlockSpec((D, D), lambda i: (0, 0)),
            pl.BlockSpec((tm, D), lambda i: (i, 0)),
            vec, vec, modspec(2), modspec(3), modspec(4),
            pl.BlockSpec((D, N_EXPERTS), lambda i: (0, 0)),
            pl.BlockSpec((1, N_EXPERTS), lambda i: (0, 0)),
        ],
        out_specs=[
            pl.BlockSpec((tm, D), lambda i: (i, 0)),
            pl.BlockSpec((tm, ROW_S, 128), lambda i: (i, 0, 0)),
            pl.BlockSpec((tm, TOP_K), lambda i: (i, 0)),
            pl.BlockSpec((tm, TOP_K), lambda i: (i, 0)),
            pl.BlockSpec((tm, TOP_K), lambda i: (i, 0)),
            pl.BlockSpec((1, N_EXPERTS), lambda i: (0, 0)),
        ],
        out_shape=[
            jax.ShapeDtypeStruct((t, D), F32),
            jax.ShapeDtypeStruct((t, ROW_S, 128), F32),
            small(jnp.int32), small(F32), small(jnp.int32),
            jax.ShapeDtypeStruct((1, N_EXPERTS), F32),
        ],
        scratch_shapes=[pltpu.VMEM((1, N_EXPERTS), F32)],
        compiler_params=_cparams(("arbitrary",), VMEM_LIMIT),
        name="outproj_router",
    )(ma, mb, w_bf, xs, g1.reshape(1, D), g2.reshape(1, D), modm, modm, modm,
      rw, rb.reshape(1, N_EXPERTS))


def _dispatch_kernel(pstart_ref, padded_ref, nused_ref, dest_ref, h_ref, xs_ref, zbuf, sem, *, tm, nblk):
    i = pl.program_id(0)

    @pl.when(i == 0)
    def _():
        zbuf[...] = jnp.zeros_like(zbuf)

        def zero_block(row):
            cp = pltpu.make_async_copy(zbuf, xs_ref.at[pl.ds(row, MOE_TM)], sem)
            cp.start()
            cp.wait()

        def per_expert(e, c):
            @pl.when(padded_ref[e] > 0)
            def _():
                zero_block(pstart_ref[e] + padded_ref[e] - MOE_TM)
            return c

        lax.fori_loop(0, N_EXPERTS, per_expert, 0)

        def per_tail(b, c):
            zero_block(b * MOE_TM)
            return c

        lax.fori_loop(nused_ref[0], nblk, per_tail, 0)

    def row_copy(r, k):
        return pltpu.make_async_copy(h_ref.at[r], xs_ref.at[dest_ref[r * TOP_K + k]], sem)

    def issue(r, c):
        for k in range(TOP_K):
            row_copy(r, k).start()
        return c

    def drain(r, c):
        for k in range(TOP_K):
            row_copy(r, k).wait()
        return c

    lax.fori_loop(0, tm, issue, 0)
    lax.fori_loop(0, tm, drain, 0)


def _dispatch(h2, dest, pstart, padded, nused, nblk):
    tm = 256
    t = h2.shape[0]
    return pl.pallas_call(
        functools.partial(_dispatch_kernel, tm=tm, nblk=nblk),
        grid_spec=pltpu.PrefetchScalarGridSpec(
            num_scalar_prefetch=3,
            grid=(t // tm,),
            in_specs=[
                pl.BlockSpec((tm * TOP_K,), lambda i, *_: (i,), memory_space=pltpu.SMEM),
                pl.BlockSpec((tm, ROW_S, 128), lambda i, *_: (i, 0, 0)),
            ],
            out_specs=pl.BlockSpec(memory_space=pl.ANY),
            scratch_shapes=[pltpu.VMEM((MOE_TM, ROW_S, 128), F32), pltpu.SemaphoreType.DMA(())],
        ),
        out_shape=jax.ShapeDtypeStruct((nblk * MOE_TM, ROW_S, 128), F32),
        compiler_params=_cparams(("arbitrary",), VMEM_LIMIT),
        name="moe_dispatch",
    )(pstart, padded, nused, dest, h2)


def _expert_kernel(be_ref, nused_ref, x_ref, wgu_ref, bgu_ref, wd_ref, bd_ref, o_ref):
    i = pl.program_id(0)

    @pl.when(i < nused_ref[0])
    def _():
        gu = jnp.dot(_load_rows(x_ref).astype(BF16), wgu_ref[...], preferred_element_type=F32) + bgu_ref[...]
        gate = jnp.minimum(gu[:, :D_EXPERT], SWIGLU_LIMIT)
        up = jnp.clip(gu[:, D_EXPERT:], -SWIGLU_LIMIT, SWIGLU_LIMIT)
        act = gate * jax.nn.sigmoid(SWIGLU_ALPHA * gate) * (up + 1.0)
        _store_rows(o_ref, jnp.dot(act.astype(BF16), wd_ref[...], preferred_element_type=F32) + bd_ref[...])

    @pl.when(i >= nused_ref[0])
    def _():
        o_ref[...] = jnp.zeros_like(o_ref)


def _experts(xs, block_e, nused, wgu_bf, bgu, wd_bf, bd, nblk):
    live = lambda i, nu: jnp.minimum(i, nu[0] - 1)
    return pl.pallas_call(
        _expert_kernel,
        grid_spec=pltpu.PrefetchScalarGridSpec(
            num_scalar_prefetch=2,
            grid=(nblk,),
            in_specs=[
                pl.BlockSpec((MOE_TM, ROW_S, 128), lambda i, be, nu: (live(i, nu), 0, 0)),
                pl.BlockSpec((None, D, 2 * D_EXPERT), lambda i, be, nu: (be[live(i, nu)], 0, 0)),
                pl.BlockSpec((None, 1, 2 * D_EXPERT), lambda i, be, nu: (be[live(i, nu)], 0, 0)),
                pl.BlockSpec((None, D_EXPERT, D), lambda i, be, nu: (be[live(i, nu)], 0, 0)),
                pl.BlockSpec((None, 1, D), lambda i, be, nu: (be[live(i, nu)], 0, 0)),
            ],
            out_specs=pl.BlockSpec((MOE_TM, ROW_S, 128), lambda i, be, nu: (i, 0, 0)),
        ),
        out_shape=jax.ShapeDtypeStruct((nblk * MOE_TM, ROW_S, 128), F32),
        compiler_params=_cparams(("arbitrary",), VMEM_LIMIT),
        name="moe_experts",
    )(block_e, nused, xs, wgu_bf, bgu.reshape(N_EXPERTS, 1, -1), wd_bf, bd.reshape(N_EXPERTS, 1, -1))


def _combine_kernel(dest_ref, p_ref, x_ref, g3_ref, ga2_ref, yb_ref, xo_ref, buf, sem, *, tm):
    def row_copy(r, k):
        return pltpu.make_async_copy(yb_ref.at[dest_ref[r * TOP_K + k]], buf.at[k, r], sem)

    def issue(r, c):
        for k in range(TOP_K):
            row_copy(r, k).start()
        return c

    def drain(r, c):
        for k in range(TOP_K):
            row_copy(r, k).wait()
        return c

    lax.fori_loop(0, tm, issue, 0)
    lax.fori_loop(0, tm, drain, 0)
    p = p_ref[...]
    y = _load_rows(buf.at[0]) * p[:, 0:1]
    for k in range(1, TOP_K):
        y = y + _load_rows(buf.at[k]) * p[:, k:k + 1]
    xo_ref[...] = x_ref[...] + ga2_ref[...] * (_rms(y) * g3_ref[...])


def _combine(yb, dest, p4, xs, g3, modm, layer):
    tm = 128
    t = xs.shape[0]
    row = lambda i: layer * 8 + _mod_row(i, tm)
    return pl.pallas_call(
        functools.partial(_combine_kernel, tm=tm),
        grid=(t // tm,),
        in_specs=[
            pl.BlockSpec((tm * TOP_K,), lambda i: (i,), memory_space=pltpu.SMEM),
            pl.BlockSpec((tm, TOP_K), lambda i: (i, 0)),
            pl.BlockSpec((tm, D), lambda i: (i, 0)),
            pl.BlockSpec((1, D), lambda i: (0, 0)),
            pl.BlockSpec((None, 1, D), lambda i: (row(i), 0, 5)),
            pl.BlockSpec(memory_space=pl.ANY),
        ],
        out_specs=pl.BlockSpec((tm, D), lambda i: (i, 0)),
        out_shape=jax.ShapeDtypeStruct((t, D), F32),
        scratch_shapes=[pltpu.VMEM((TOP_K, tm, ROW_S, 128), F32), pltpu.SemaphoreType.DMA(())],
        compiler_params=_cparams(("arbitrary",), VMEM_LIMIT),
        name="moe_combine",
    )(dest, p4, xs, g3.reshape(1, D), modm, yb)


def _moe(h2, e4, p4, r4, counts, xs, g3, modm, layer, wgu_bf, bgu, wd_bf, bd):
    t = h2.shape[0]
    nblk = -(-(t * TOP_K + N_EXPERTS * (MOE_TM - 1)) // MOE_TM)
    cnt = counts.reshape(N_EXPERTS).astype(jnp.int32)
    padded = (cnt + MOE_TM - 1) // MOE_TM * MOE_TM
    pend = jnp.cumsum(padded)
    pstart = pend - padded
    dest = (pstart[e4] + r4).reshape(-1).astype(jnp.int32)
    nused = (pend[-1:] // MOE_TM).astype(jnp.int32)
    block_e = jnp.minimum(
        jnp.searchsorted(pend, jnp.arange(nblk, dtype=jnp.int32) * MOE_TM, side="right"),
        N_EXPERTS - 1).astype(jnp.int32)
    xs_sorted = _dispatch(h2, dest, pstart.astype(jnp.int32), padded.astype(jnp.int32), nused, nblk)
    yb = _experts(xs_sorted, block_e, nused, wgu_bf, bgu, wd_bf, bd, nblk)
    return _combine(yb, dest, p4, xs, g3, modm, layer)


N_CTX_BLK = LC // HGRN_R
N_LAT_BLK = L // HGRN_R
N_SEQ_BLK = N_CTX_BLK + N_LAT_BLK


def _scan_block(b, s, rev):
    if rev:
        return jnp.where(s < N_CTX_BLK, T_LAT // HGRN_R + N_CTX_BLK * b + (N_CTX_BLK - 1 - s),
                         N_LAT_BLK * b + (N_SEQ_BLK - 1 - s))
    return jnp.where(s < N_CTX_BLK, T_LAT // HGRN_R + N_CTX_BLK * b + s, N_LAT_BLK * b + (s - N_CTX_BLK))


def _hgrn_kernel(*refs, rev):
    if rev:
        q_ref, f_ref, v_ref, lb_ref, of_ref, g_ref, on_ref, o_ref, st = refs
    else:
        q_ref, f_ref, v_ref, lb_ref, o_ref, st = refs

    @pl.when(pl.program_id(2) == 0)
    def _():
        st[...] = jnp.zeros_like(st)

    c = HGRN_C
    lb = lb_ref[...]
    sg = jax.nn.sigmoid(f_ref[...])
    logf = jnp.log(lb + (1.0 - lb) * sg)
    kk = (1.0 - lb) * (1.0 - sg)
    q = q_ref[...]
    qs = q * jax.nn.sigmoid(q)
    v = v_ref[...]
    r_i = lax.broadcasted_iota(jnp.int32, (c, c), 0)
    c_i = lax.broadcasted_iota(jnp.int32, (c, c), 1)
    mask = (c_i >= r_i) if rev else (c_i <= r_i)
    tri = mask.astype(F32)
    order = range(HGRN_R // c - 1, -1, -1) if rev else range(HGRN_R // c)
    state = st[...]
    outs = [None] * (HGRN_R // c)
    for ci in order:
        rs = slice(ci * c, (ci + 1) * c)
        cum = jnp.dot(tri, logf[rs], precision=HI, preferred_element_type=F32)
        total = cum[0:1] if rev else cum[c - 1:c]
        mid = cum[c // 2:c // 2 + 1]
        q_in = qs[rs] * jnp.exp(cum)
        q_d = qs[rs] * jnp.exp(cum - mid)
        k_i = kk[rs] * jnp.exp(mid - cum)
        k_e = kk[rs] * jnp.exp(total - cum)
        sc = lax.dot_general(q_d.astype(BF16), k_i.astype(BF16), (((1,), (1,)), ((), ())),
                             preferred_element_type=F32)
        sc = jnp.where(mask, sc, 0.0)
        vb = v[rs].astype(BF16)
        intra = jnp.dot(sc.astype(BF16), vb, preferred_element_type=F32)
        inter = lax.dot_general(q_in.astype(BF16), state.astype(BF16), (((1,), (1,)), ((), ())),
                                preferred_element_type=F32)
        outs[ci] = intra + inter
        state = state * jnp.exp(total) + lax.dot_general(
            vb, k_e.astype(BF16), (((0,), (0,)), ((), ())), preferred_element_type=F32)
    st[...] = state
    o = jnp.concatenate(outs, axis=0)
    if rev:
        o = o + of_ref[...]
        g = g_ref[...]
        o_ref[...] = (_rms(o) * on_ref[...] * (g * jax.nn.sigmoid(g))).astype(BF16)
    else:
        o_ref[...] = o


def _hgrn(p, lb, rev, of=None, onorm=None):
    r = HGRN_R
    blk = lambda b, s: _scan_block(b, s, rev)
    col = lambda off: pl.BlockSpec((r, 128), lambda b, h, s: (blk(b, s), off + h))
    in_specs = [col(0), col(8 + (8 if rev else 0)), col(24), pl.BlockSpec((1, 128), lambda b, h, s: (0, h))]
    args = [p, p, p, lb.reshape(1, HALF)]
    if rev:
        in_specs += [pl.BlockSpec((r, 128), lambda b, h, s: (blk(b, s), h)), col(32),
                     pl.BlockSpec((1, 128), lambda b, h, s: (0, h))]
        args += [of, p, onorm.reshape(1, HALF)]
    return pl.pallas_call(
        functools.partial(_hgrn_kernel, rev=rev),
        grid=(B, HGRN_HEADS, N_SEQ_BLK),
        in_specs=in_specs,
        out_specs=pl.BlockSpec((r, 128), lambda b, h, s: (blk(b, s), h)),
        out_shape=jax.ShapeDtypeStruct((T_ALL, HALF), BF16 if rev else F32),
        scratch_shapes=[pltpu.VMEM((128, 128), F32)],
        compiler_params=_cparams(("parallel", "parallel", "arbitrary"), VMEM_LIMIT),
        name="hgrn_bwd" if rev else "hgrn_fwd",
    )(*args)


def _lru_kernel(*refs, rev):
    if rev:
        (x_ref, xp_ref, xn_ref, cw_ref, cb_ref, wr_ref, br_ref, wi_ref, bi_ref, lam_ref,
         hf_ref, gt_ref, o_ref, xcat, a_s, u_s, h_s, carry) = refs
    else:
        (x_ref, xp_ref, xn_ref, cw_ref, cb_ref, wr_ref, br_ref, wi_ref, bi_ref, lam_ref,
         o_ref, xcat, a_s, u_s, h_s, carry) = refs
    s = pl.program_id(1)
    r = LRU_R

    @pl.when(s == 0)
    def _():
        carry[...] = jnp.zeros_like(carry)

    is_ctx = s < N_CTX_BLK
    if rev:
        j = jnp.where(is_ctx, N_CTX_BLK - 1 - s, N_SEQ_BLK - 1 - s)
    else:
        j = jnp.where(is_ctx, s, s - N_CTX_BLK)
    nb = jnp.where(is_ctx, N_CTX_BLK, N_LAT_BLK)
    xcat[0:8, :] = jnp.where(j == 0, 0.0, xp_ref[...])
    xcat[8:8 + r, :] = x_ref[...]
    xcat[8 + r:16 + r, :] = jnp.where(j == nb - 1, 0.0, xn_ref[...])
    xc = cb_ref[...] + xcat[pl.ds(6, r), :] * cw_ref[0:1, :]
    for t in range(1, 4):
        xc = xc + xcat[pl.ds(6 + t, r), :] * cw_ref[t:t + 1, :]

    lam = lam_ref[...]
    sp = jnp.maximum(-lam, 0.0) + jnp.log1p(jnp.exp(-jnp.abs(lam)))
    hd = HALF // LRU_HEADS
    for h in range(LRU_HEADS):
        cs = slice(h * hd, (h + 1) * hd)
        xh = xc[:, cs]
        rg = jax.nn.sigmoid(_bdot(xh, wr_ref[h]) + br_ref[:, cs])
        ig = jax.nn.sigmoid(_bdot(xh, wi_ref[h]) + bi_ref[:, cs])
        log_a = -LRU_C * rg * sp[:, cs]
        a = jnp.exp(log_a)
        a_s[:, cs] = a
        u_s[:, cs] = jnp.sqrt(-jnp.tanh(log_a) * (a * a + 1.0)) * (ig * xh)

    def step(t, h):
        tt = (r - 1 - t) if rev else t
        h = a_s[pl.ds(tt, 1), :] * h + u_s[pl.ds(tt, 1), :]
        h_s[pl.ds(tt, 1), :] = h
        return h

    carry[...] = lax.fori_loop(0, r, step, carry[...])
    if rev:
        o_ref[...] = ((h_s[...] + hf_ref[...]) * jax.nn.gelu(gt_ref[...])).astype(BF16)
    else:
        o_ref[...] = h_s[...]


def _lru(p, conv_w, conv_b, w_r, b_r, w_i, b_i, lam, rev, hf=None):
    r = LRU_R
    d = 1 if rev else 0
    blk = lambda b, s: _scan_block(b, s, rev)
    xcol = 7168 // HALF - 1
    n8 = T_ALL // 8
    vec = lambda: pl.BlockSpec((None, 1, HALF), lambda b, s: (d, 0, 0))
    wspec = lambda: pl.BlockSpec((None, LRU_HEADS, 128, 128), lambda b, s: (d, 0, 0, 0))
    in_specs = [
        pl.BlockSpec((r, HALF), lambda b, s: (blk(b, s), xcol)),
        pl.BlockSpec((8, HALF), lambda b, s: (jnp.maximum(blk(b, s) * (r // 8) - 1, 0), xcol)),
        pl.BlockSpec((8, HALF), lambda b, s: (jnp.minimum((blk(b, s) + 1) * (r // 8), n8 - 1), xcol)),
        pl.BlockSpec((4, HALF), lambda b, s: (0, 0)),
        pl.BlockSpec((1, HALF), lambda b, s: (0, 0)),
        wspec(), vec(), wspec(), vec(), vec(),
    ]
    args = [p, p, p, conv_w, conv_b.reshape(1, HALF), w_r, b_r.reshape(2, 1, HALF), w_i,
            b_i.reshape(2, 1, HALF), lam.reshape(2, 1, HALF)]
    if rev:
        in_specs += [pl.BlockSpec((r, HALF), lambda b, s: (blk(b, s), 0)),
                     pl.BlockSpec((r, HALF), lambda b, s: (blk(b, s), xcol - 1))]
        args += [hf, p]
    return pl.pallas_call(
        functools.partial(_lru_kernel, rev=rev),
        grid=(B, N_SEQ_BLK),
        in_specs=in_specs,
        out_specs=pl.BlockSpec((r, HALF), lambda b, s: (blk(b, s), 0)),
        out_shape=jax.ShapeDtypeStruct((T_ALL, HALF), BF16 if rev else F32),
        scratch_shapes=[pltpu.VMEM((r + 16, HALF), F32), pltpu.VMEM((r, HALF), F32),
                        pltpu.VMEM((r, HALF), F32), pltpu.VMEM((r, HALF), F32),
                        pltpu.VMEM((1, HALF), F32)],
        compiler_params=_cparams(("parallel", "arbitrary"), VMEM_LIMIT),
        name="lru_bwd" if rev else "lru_fwd",
    )(*args)


def kernel(x, c, ctx, c_ctx, mod_w, mod_b, norm_g, ab_w_in, ab_w_out, gmlp_v_g, gmlp_ws, gmlp_bs, cd_w_in, cd_w_out, hgrn_lb, hgrn_onorm_g, lru_conv_w, lru_conv_b, lru_wr, lru_br, lru_wi, lru_bi, lru_lambda, router_w, router_b, exp_w_gu, exp_b_gu, exp_w_down, exp_b_down):
    cond8 = jnp.concatenate([c, c_ctx[None, :], jnp.zeros((5, D), F32)], axis=0)
    modm = _modulation(cond8, mod_w, mod_b).reshape(DEPTH * 8, 1, 6 * D)
    xs = jnp.concatenate([x.reshape(T_LAT, D), ctx.reshape(T_CTX, D)], axis=0)

    lb_soft = jax.nn.softmax(hgrn_lb.astype(F32), axis=0)
    lb_all = jnp.cumsum(lb_soft, axis=0) - lb_soft[0]

    p0 = _inproj(xs, norm_g[0, 0], modm, 0, ab_w_in[0].astype(BF16))
    zc, zs = _fnet_chan(p0)
    mix_a = jnp.concatenate([_fnet_pos(zc, zs, 0, L, 512, 1024),
                             _fnet_pos(zc, zs, T_LAT, LC, LC, LC)], axis=0)
    mix_g = _gmlp(p0, gmlp_v_g[0], gmlp_ws[0], gmlp_bs[0])
    xs, h2, e4, p4, r4, counts = _outproj(mix_a, mix_g, ab_w_out[0].astype(BF16), xs, T_ALL, norm_g[0, 1],
                                          norm_g[0, 2], modm, 0, router_w[0], router_b[0])
    xs = _moe(h2, e4, p4, r4, counts, xs, norm_g[0, 3], modm, 0,
              exp_w_gu[0].astype(BF16), exp_b_gu[0], exp_w_down[0].astype(BF16), exp_b_down[0])

    p1 = _inproj(xs, norm_g[1, 0], modm, 1, cd_w_in[0].astype(BF16))
    o_f = _hgrn(p1, lb_all[1], False)
    hg = _hgrn(p1, lb_all[1], True, o_f, hgrn_onorm_g[0])
    lru_args = (lru_conv_w[0], lru_conv_b[0], lru_wr[0], lru_br[0], lru_wi[0], lru_bi[0], lru_lambda[0])
    h_f = _lru(p1, *lru_args, False)
    lr = _lru(p1, *lru_args, True, h_f)
    xl, h2, e4, p4, r4, counts = _outproj(hg, lr, cd_w_out[0].astype(BF16), xs, T_LAT, norm_g[1, 1],
                                          norm_g[1, 2], modm, 1, router_w[1], router_b[1])
    xl = _moe(h2, e4, p4, r4, counts, xl, norm_g[1, 3], modm, 1,
              exp_w_gu[1].astype(BF16), exp_b_gu[1], exp_w_down[1].astype(BF16), exp_b_down[1])
    return xl.reshape(B, L, D)
```

```python
import functools

import numpy as np
import jax
import jax.numpy as jnp
from jax import lax
from jax.experimental import pallas as pl
from jax.experimental.pallas import tpu as pltpu

F32 = jnp.float32
BF16 = jnp.bfloat16
HI = lax.Precision.HIGHEST

D = 2048
B = 2
L = 4096
LC = 256
T_LAT = B * L
T_CTX = B * LC
T_ALL = T_LAT + T_CTX
DEPTH = 2
EPS = 1e-6

HALF = D // 2
N_EXPERTS = 32
TOP_K = 4
D_EXPERT = D // 2
SWIGLU_LIMIT = 7.0
SWIGLU_ALPHA = 1.702
MOE_TM = 256

FNET_GROUPS = 4
FNET_GD = HALF // FNET_GROUPS
GMLP_HEADS = 8
GMLP_CHUNK = 128
HGRN_HEADS = 8
HGRN_R = 128
HGRN_C = 64
LRU_HEADS = 8
LRU_R = 128
LRU_C = 8.0

VMEM_LIMIT = 56 * 1024 * 1024


def _cparams(sem, vmem=None):
    return pltpu.CompilerParams(dimension_semantics=sem, vmem_limit_bytes=vmem)


def _mod_row(i, tm):
    return jnp.where(i < L // tm, 0, jnp.where(i < 2 * L // tm, 1, 2))


def _rms(x):
    return x * lax.rsqrt(jnp.mean(x * x, axis=-1, keepdims=True) + EPS)


def _bdot(a, b):
    return jnp.dot(a.astype(BF16), b.astype(BF16), preferred_element_type=F32)


ROW_S = D // 128


def _row_slab(ref, r):
    return ref.at[pl.ds(pl.multiple_of(r * ROW_S, ROW_S), ROW_S), :]


def _store_rows(ref, val, row0=0):
    n = val.shape[0]
    for s in range(ROW_S):
        ref[pl.ds(row0 * ROW_S + s, n, stride=ROW_S), :] = val[:, s * 128:(s + 1) * 128]


def _load_rows(ref, n):
    return jnp.concatenate([ref[pl.ds(s, n, stride=ROW_S), :] for s in range(ROW_S)], axis=1)


def _mod_kernel(c_ref, w_ref, b_ref, o_ref):
    cnd = c_ref[...]
    s = cnd * jax.nn.sigmoid(cnd)
    o_ref[...] = jnp.dot(s, w_ref[...], precision=HI, preferred_element_type=F32) + b_ref[...]


def _modulation(cond8, mod_w, mod_b):
    tn = 1024
    n = mod_w.shape[-1]
    return pl.pallas_call(
        _mod_kernel,
        grid=(DEPTH, n // tn),
        in_specs=[
            pl.BlockSpec((8, D), lambda l, j: (0, 0)),
            pl.BlockSpec((None, D, tn), lambda l, j: (l, 0, j)),
            pl.BlockSpec((None, 1, tn), lambda l, j: (l, 0, j)),
        ],
        out_specs=pl.BlockSpec((None, 8, tn), lambda l, j: (l, 0, j)),
        out_shape=jax.ShapeDtypeStruct((DEPTH, 8, n), F32),
        compiler_params=_cparams(("parallel", "parallel"), VMEM_LIMIT),
        name="modulation",
    )(cond8, mod_w, mod_b.reshape(DEPTH, 1, n))


def _inproj_kernel(*refs, nw):
    x_ref, g_ref, sh_ref, sc_ref = refs[:4]
    w_refs, o_refs = refs[4:4 + nw], refs[4 + nw:]
    y = _rms(x_ref[...]) * g_ref[...]
    h = (y * (1.0 + sc_ref[...]) + sh_ref[...]).astype(BF16)
    for w_ref, o_ref in zip(w_refs, o_refs):
        o_ref[...] = jnp.dot(h, w_ref[...], preferred_element_type=F32).astype(o_ref.dtype)


def _inproj(xs, g, modm, layer, ws_bf, out_dtype):
    tm = 256
    t = xs.shape[0]
    row = lambda i: layer * 8 + _mod_row(i, tm)
    return pl.pallas_call(
        functools.partial(_inproj_kernel, nw=len(ws_bf)),
        grid=(t // tm,),
        in_specs=[
            pl.BlockSpec((tm, D), lambda i: (i, 0)),
            pl.BlockSpec((1, D), lambda i: (0, 0)),
            pl.BlockSpec((None, 1, D), lambda i: (row(i), 0, 0)),
            pl.BlockSpec((None, 1, D), lambda i: (row(i), 0, 1)),
        ] + [pl.BlockSpec(w.shape, lambda i: (0, 0), pipeline_mode=pl.Buffered(1)) for w in ws_bf],
        out_specs=[pl.BlockSpec((tm, w.shape[1]), lambda i: (i, 0)) for w in ws_bf],
        out_shape=[jax.ShapeDtypeStruct((t, w.shape[1]), out_dtype) for w in ws_bf],
        compiler_params=_cparams(("parallel",), VMEM_LIMIT),
        name="inproj",
    )(xs, g.reshape(1, D), modm, modm, *ws_bf)


def _fnet_chan_kernel(p_ref, cs_ref, zc_ref, zs_ref):
    for g in range(FNET_GROUPS):
        sl = slice(g * FNET_GD, (g + 1) * FNET_GD)
        z = jnp.dot(p_ref[:, sl].astype(BF16), cs_ref[...], preferred_element_type=F32)
        zc_ref[:, sl] = z[:, :FNET_GD].astype(BF16)
        zs_ref[:, sl] = z[:, FNET_GD:].astype(BF16)


def _fnet_chan(p):
    tm = 512
    t = p.shape[0]
    k = np.arange(FNET_GD)
    ang = 2.0 * np.pi * ((k[:, None] * k[None, :]) % FNET_GD) / FNET_GD
    cs = np.concatenate([np.cos(ang), np.sin(ang)], axis=1) / np.sqrt(FNET_GD)
    cs = jnp.asarray(cs, F32).astype(BF16)
    return pl.pallas_call(
        _fnet_chan_kernel,
        grid=(t // tm,),
        in_specs=[
            pl.BlockSpec((tm, HALF), lambda i: (i, 0)),
            pl.BlockSpec((FNET_GD, 2 * FNET_GD), lambda i: (0, 0)),
        ],
        out_specs=[pl.BlockSpec((tm, HALF), lambda i: (i, 0))] * 2,
        out_shape=[jax.ShapeDtypeStruct((t, HALF), BF16)] * 2,
        compiler_params=_cparams(("parallel",), VMEM_LIMIT),
        name="fnet_chan",
    )(p, cs)


def _fnet_pos_kernel(bre_ref, bim_ref, cre_ref, cim_ref, rre_ref, rim_ref, zc_ref, zs_ref, o_ref, acc):
    nt = pl.program_id(2)
    br, bi = bre_ref[...], bim_ref[...]
    cr, ci = cre_ref[...], cim_ref[...]
    rr, ri = rre_ref[...], rim_ref[...]
    tr = br * cr - bi * ci
    ti = br * ci + bi * cr
    er = (tr * rr - ti * ri).astype(BF16)
    ei = (tr * ri + ti * rr).astype(BF16)
    part = (jnp.dot(er, zc_ref[...], preferred_element_type=F32)
            + jnp.dot(ei, zs_ref[...], preferred_element_type=F32))

    @pl.when(nt == 0)
    def _():
        acc[...] = part

    @pl.when(nt > 0)
    def _():
        acc[...] += part

    @pl.when(nt == pl.num_programs(2) - 1)
    def _():
        o_ref[...] = acc[...].astype(BF16)


def _fnet_pos(zc, zs, row0, seq, tk, tn):
    nkt, nnt = seq // tk, seq // tn
    th = 2.0 * np.pi / seq
    kk = np.arange(tk)[:, None]
    nn = np.arange(tn)[None, :]
    base = th * ((kk * nn) % seq)
    n0 = (np.arange(nnt) * tn)[:, None, None]
    col = th * ((np.arange(tk)[None, :, None] * n0) % seq)
    k0 = (np.arange(nkt) * tk)[:, None, None, None]
    nfull = (np.arange(nnt) * tn)[None, :, None, None] + np.arange(tn)[None, None, None, :]
    row = th * ((k0 * nfull) % seq)
    scale = 1.0 / np.sqrt(seq)
    f = lambda a: jnp.asarray(a, F32)
    tabs = (f(np.cos(base)), f(-np.sin(base)), f(np.cos(col)), f(-np.sin(col)),
            f(np.cos(row) * scale), f(-np.sin(row) * scale))
    rb = row0 // tn
    ob = row0 // tk
    return pl.pallas_call(
        _fnet_pos_kernel,
        grid=(B, nkt, nnt),
        in_specs=[
            pl.BlockSpec((tk, tn), lambda b, k, n: (0, 0)),
            pl.BlockSpec((tk, tn), lambda b, k, n: (0, 0)),
            pl.BlockSpec((None, tk, 1), lambda b, k, n: (n, 0, 0)),
            pl.BlockSpec((None, tk, 1), lambda b, k, n: (n, 0, 0)),
            pl.BlockSpec((None, None, 1, tn), lambda b, k, n: (k, n, 0, 0)),
            pl.BlockSpec((None, None, 1, tn), lambda b, k, n: (k, n, 0, 0)),
            pl.BlockSpec((tn, HALF), lambda b, k, n: (rb + b * nnt + n, 0)),
            pl.BlockSpec((tn, HALF), lambda b, k, n: (rb + b * nnt + n, 0)),
        ],
        out_specs=pl.BlockSpec((tk, HALF), lambda b, k, n: (b * nkt + k, 0)),
        out_shape=jax.ShapeDtypeStruct((B * seq, HALF), BF16),
        scratch_shapes=[pltpu.VMEM((tk, HALF), F32)],
        compiler_params=_cparams(("parallel", "parallel", "arbitrary"), VMEM_LIMIT),
        name=f"fnet_pos_{seq}",
    )(*tabs, zc, zs)


def _gmlp_kernel(u_ref, v_ref, vg_ref, ws_ref, bst_ref, o_ref, *, tm):
    zu = jax.nn.gelu(u_ref[...].astype(F32))
    zv = jax.nn.gelu(v_ref[...].astype(F32))
    mu = jnp.mean(zv, axis=-1, keepdims=True)
    dv = zv - mu
    var = jnp.mean(dv * dv, axis=-1, keepdims=True)
    vn = dv * lax.rsqrt(var + EPS) * vg_ref[...]
    hd = HALF // GMLP_HEADS
    for c in range(tm // GMLP_CHUNK):
        rs = slice(c * GMLP_CHUNK, (c + 1) * GMLP_CHUNK)
        for h in range(GMLP_HEADS):
            cs = slice(h * hd, (h + 1) * hd)
            s = jnp.dot(ws_ref[h], vn[rs, cs], precision=HI, preferred_element_type=F32)
            s = s + bst_ref[:, h:h + 1]
            o_ref[rs, cs] = (zu[rs, cs] * s).astype(BF16)


def _gmlp(p, v_g, w_s, b_s):
    tm = 256
    t = p.shape[0]
    return pl.pallas_call(
        functools.partial(_gmlp_kernel, tm=tm),
        grid=(t // tm,),
        in_specs=[
            pl.BlockSpec((tm, HALF), lambda i: (i, 1)),
            pl.BlockSpec((tm, HALF), lambda i: (i, 2)),
            pl.BlockSpec((1, HALF), lambda i: (0, 0)),
            pl.BlockSpec((GMLP_HEADS, GMLP_CHUNK, GMLP_CHUNK), lambda i: (0, 0, 0)),
            pl.BlockSpec((GMLP_CHUNK, GMLP_HEADS), lambda i: (0, 0)),
        ],
        out_specs=pl.BlockSpec((tm, HALF), lambda i: (i, 0)),
        out_shape=jax.ShapeDtypeStruct((t, HALF), BF16),
        compiler_params=_cparams(("parallel",), VMEM_LIMIT),
        name="gmlp",
    )(p, p, v_g.reshape(1, HALF), w_s, b_s.T)


def _outproj_kernel(ma_ref, mb_ref, w_ref, x_ref, g1_ref, g2_ref, ga1_ref, sh2_ref, sc2_ref,
                    rw_ref, rb_ref, xo_ref, h2_ref, e_ref, p_ref, rk_ref, cnt_ref, carry, rw2, *, tm):
    @pl.when(pl.program_id(0) == 0)
    def _():
        carry[...] = jnp.zeros_like(carry)
        rw = rw_ref[...]
        rw_hi = rw.astype(BF16)
        rw2[:, :N_EXPERTS] = rw_hi
        rw2[:, N_EXPERTS:] = (rw - rw_hi.astype(F32)).astype(BF16)

    sub = 128
    iota = lax.broadcasted_iota(jnp.int32, (sub, N_EXPERTS), 1).astype(F32)
    lane4 = lax.broadcasted_iota(jnp.int32, (sub, TOP_K), 1)
    r_i = lax.broadcasted_iota(jnp.int32, (sub, sub), 0)
    c_i = lax.broadcasted_iota(jnp.int32, (sub, sub), 1)
    tri = (c_i < r_i).astype(BF16)
    running = carry[...]
    for r0 in range(0, tm, sub):
        rs = slice(r0, r0 + sub)
        m = (jnp.dot(ma_ref[rs, :], w_ref[:HALF, :], preferred_element_type=F32)
             + jnp.dot(mb_ref[rs, :], w_ref[HALF:, :], preferred_element_type=F32))
        xn = x_ref[rs, :] + ga1_ref[...] * (_rms(m) * g1_ref[...])
        xo_ref[rs, :] = xn
        h2 = (_rms(xn) * g2_ref[...]) * (1.0 + sc2_ref[...]) + sh2_ref[...]
        _store_rows(h2_ref, h2, r0)

        h_hi = h2.astype(BF16)
        h_lo = (h2 - h_hi.astype(F32)).astype(BF16)
        t1 = jnp.dot(h_hi, rw2[...], preferred_element_type=F32)
        t2 = jnp.dot(h_lo, rw2[:, :N_EXPERTS], preferred_element_type=F32)
        lg = t1[:, :N_EXPERTS] + t1[:, N_EXPERTS:] + t2 + rb_ref[...]
        idxs, vals = [], []
        for _ in range(TOP_K):
            mx = jnp.max(lg, axis=-1, keepdims=True)
            ix = jnp.min(jnp.where(lg == mx, iota, float(N_EXPERTS)), axis=-1, keepdims=True)
            idxs.append(ix)
            vals.append(mx)
            lg = jnp.where(iota == ix, -jnp.inf, lg)
        exs = [jnp.exp(v - vals[0]) for v in vals]
        den = exs[0] + exs[1] + exs[2] + exs[3]

        onehot = [(iota == ix) for ix in idxs]
        cnt = jnp.zeros((sub, N_EXPERTS), F32)
        for oh in onehot:
            cnt = cnt + oh.astype(F32)
        prefix = jnp.dot(tri, cnt.astype(BF16), preferred_element_type=F32) + running

        e4 = jnp.zeros((sub, TOP_K), jnp.int32)
        p4 = jnp.zeros((sub, TOP_K), F32)
        r4 = jnp.zeros((sub, TOP_K), jnp.int32)
        for k in range(TOP_K):
            rk = jnp.sum(jnp.where(onehot[k], prefix, 0.0), axis=-1, keepdims=True)
            e4 = jnp.where(lane4 == k, idxs[k].astype(jnp.int32), e4)
            p4 = jnp.where(lane4 == k, exs[k] / den, p4)
            r4 = jnp.where(lane4 == k, rk.astype(jnp.int32), r4)
        e_ref[rs, :] = e4
        p_ref[rs, :] = p4
        rk_ref[rs, :] = r4
        running = running + jnp.sum(cnt, axis=0, keepdims=True)
    carry[...] = running
    cnt_ref[...] = running


def _outproj(ma, mb, w_bf, xs, t, g1, g2, modm, layer, rw, rb):
    tm = 256
    row = lambda i: layer * 8 + _mod_row(i, tm)
    modspec = lambda c: pl.BlockSpec((None, 1, D), lambda i: (row(i), 0, c))
    vec = pl.BlockSpec((1, D), lambda i: (0, 0))
    small = lambda dt: jax.ShapeDtypeStruct((t, TOP_K), dt)
    return pl.pallas_call(
        functools.partial(_outproj_kernel, tm=tm),
        grid=(t // tm,),
        in_specs=[
            pl.BlockSpec((tm, HALF), lambda i: (i, 0)),
            pl.BlockSpec((tm, HALF), lambda i: (i, 0)),
            pl.BlockSpec((D, D), lambda i: (0, 0)),
            pl.BlockSpec((tm, D), lambda i: (i, 0)),
            vec, vec, modspec(2), modspec(3), modspec(4),
            pl.BlockSpec((D, N_EXPERTS), lambda i: (0, 0)),
            pl.BlockSpec((1, N_EXPERTS), lambda i: (0, 0)),
        ],
        out_specs=[
            pl.BlockSpec((tm, D), lambda i: (i, 0)),
            pl.BlockSpec((tm * ROW_S, 128), lambda i: (i, 0)),
            pl.BlockSpec((tm, TOP_K), lambda i: (i, 0)),
            pl.BlockSpec((tm, TOP_K), lambda i: (i, 0)),
            pl.BlockSpec((tm, TOP_K), lambda i: (i, 0)),
            pl.BlockSpec((1, N_EXPERTS), lambda i: (0, 0)),
        ],
        out_shape=[
            jax.ShapeDtypeStruct((t, D), F32),
            jax.ShapeDtypeStruct((t * ROW_S, 128), F32),
            small(jnp.int32), small(F32), small(jnp.int32),
            jax.ShapeDtypeStruct((1, N_EXPERTS), F32),
        ],
        scratch_shapes=[pltpu.VMEM((1, N_EXPERTS), F32), pltpu.VMEM((D, 2 * N_EXPERTS), BF16)],
        compiler_params=_cparams(("arbitrary",), VMEM_LIMIT),
        name="outproj_router",
    )(ma, mb, w_bf, xs, g1.reshape(1, D), g2.reshape(1, D), modm, modm, modm,
      rw, rb.reshape(1, N_EXPERTS))


def _dispatch_kernel(pstart_ref, padded_ref, nused_ref, dest_ref, h_ref, xs_ref, zbuf, sem, *, tm, nblk):
    i = pl.program_id(0)
    blk_rows = MOE_TM * ROW_S

    @pl.when(i == 0)
    def _():
        zbuf[...] = jnp.zeros_like(zbuf)

        def zero_block(slot):
            start = pl.multiple_of(slot * ROW_S, blk_rows)
            cp = pltpu.make_async_copy(zbuf, xs_ref.at[pl.ds(start, blk_rows), :], sem)
            cp.start()
            cp.wait()

        def per_expert(e, c):
            @pl.when(padded_ref[e] > 0)
            def _():
                zero_block(pstart_ref[e] + padded_ref[e] - MOE_TM)
            return c

        lax.fori_loop(0, N_EXPERTS, per_expert, 0)

        def per_tail(b, c):
            zero_block(b * MOE_TM)
            return c

        lax.fori_loop(nused_ref[0], nblk, per_tail, 0)

    def row_copy(r, k):
        return pltpu.make_async_copy(_row_slab(h_ref, r), _row_slab(xs_ref, dest_ref[r * TOP_K + k]), sem)

    def issue(r, c):
        for k in range(TOP_K):
            row_copy(r, k).start()
        return c

    def drain(r, c):
        for k in range(TOP_K):
            row_copy(r, k).wait()
        return c

    lax.fori_loop(0, tm, issue, 0)
    lax.fori_loop(0, tm, drain, 0)


def _dispatch(h2, dest, pstart, padded, nused, nblk):
    tm = 256
    t = h2.shape[0] // ROW_S
    return pl.pallas_call(
        functools.partial(_dispatch_kernel, tm=tm, nblk=nblk),
        grid_spec=pltpu.PrefetchScalarGridSpec(
            num_scalar_prefetch=3,
            grid=(t // tm,),
            in_specs=[
                pl.BlockSpec((tm * TOP_K,), lambda i, *_: (i,), memory_space=pltpu.SMEM),
                pl.BlockSpec((tm * ROW_S, 128), lambda i, *_: (i, 0)),
            ],
            out_specs=pl.BlockSpec(memory_space=pl.ANY),
            scratch_shapes=[pltpu.VMEM((MOE_TM * ROW_S, 128), F32), pltpu.SemaphoreType.DMA(())],
        ),
        out_shape=jax.ShapeDtypeStruct((nblk * MOE_TM * ROW_S, 128), F32),
        compiler_params=_cparams(("arbitrary",), VMEM_LIMIT),
        name="moe_dispatch",
    )(pstart, padded, nused, dest, h2)


def _expert_kernel(be_ref, nused_ref, x_ref, wgu_ref, bgu_ref, wd_ref, bd_ref, o_ref):
    i = pl.program_id(0)

    @pl.when(i < nused_ref[0])
    def _():
        x = _load_rows(x_ref, MOE_TM).astype(BF16)
        gu = jnp.dot(x, wgu_ref[...], preferred_element_type=F32) + bgu_ref[...]
        gate = jnp.minimum(gu[:, :D_EXPERT], SWIGLU_LIMIT)
        up = jnp.clip(gu[:, D_EXPERT:], -SWIGLU_LIMIT, SWIGLU_LIMIT)
        act = gate * jax.nn.sigmoid(SWIGLU_ALPHA * gate) * (up + 1.0)
        _store_rows(o_ref, jnp.dot(act.astype(BF16), wd_ref[...], preferred_element_type=F32) + bd_ref[...])

    @pl.when(i >= nused_ref[0])
    def _():
        o_ref[...] = jnp.zeros_like(o_ref)


def _experts(xs, block_e, nused, wgu_bf, bgu, wd_bf, bd, nblk):
    live = lambda i, nu: jnp.minimum(i, nu[0] - 1)
    return pl.pallas_call(
        _expert_kernel,
        grid_spec=pltpu.PrefetchScalarGridSpec(
            num_scalar_prefetch=2,
            grid=(nblk,),
            in_specs=[
                pl.BlockSpec((MOE_TM * ROW_S, 128), lambda i, be, nu: (live(i, nu), 0)),
                pl.BlockSpec((None, D, 2 * D_EXPERT), lambda i, be, nu: (be[live(i, nu)], 0, 0)),
                pl.BlockSpec((None, 1, 2 * D_EXPERT), lambda i, be, nu: (be[live(i, nu)], 0, 0)),
                pl.BlockSpec((None, D_EXPERT, D), lambda i, be, nu: (be[live(i, nu)], 0, 0)),
                pl.BlockSpec((None, 1, D), lambda i, be, nu: (be[live(i, nu)], 0, 0)),
            ],
            out_specs=pl.BlockSpec((MOE_TM * ROW_S, 128), lambda i, be, nu: (i, 0)),
        ),
        out_shape=jax.ShapeDtypeStruct((nblk * MOE_TM * ROW_S, 128), F32),
        compiler_params=_cparams(("arbitrary",), VMEM_LIMIT),
        name="moe_experts",
    )(block_e, nused, xs, wgu_bf, bgu.reshape(N_EXPERTS, 1, -1), wd_bf, bd.reshape(N_EXPERTS, 1, -1))


def _combine_kernel(dcur_ref, dnext_ref, p_ref, x_ref, g3_ref, ga2_ref, yb_ref, xo_ref, buf, sem, *, tm):
    i = pl.program_id(0)
    cur = lax.rem(i, 2)

    def row_copy(slot, d_ref, r, k):
        return pltpu.make_async_copy(_row_slab(yb_ref, d_ref[r * TOP_K + k]),
                                     _row_slab(buf.at[slot, k], r), sem.at[slot])

    def issue(slot, d_ref):
        def body(r, c):
            for k in range(TOP_K):
                row_copy(slot, d_ref, r, k).start()
            return c
        lax.fori_loop(0, tm, body, 0)

    @pl.when(i == 0)
    def _():
        issue(0, dcur_ref)

    @pl.when(i + 1 < pl.num_programs(0))
    def _():
        issue(1 - cur, dnext_ref)

    def drain(r, c):
        for k in range(TOP_K):
            row_copy(cur, dcur_ref, r, k).wait()
        return c

    lax.fori_loop(0, tm, drain, 0)
    p = p_ref[...]
    y = _load_rows(buf.at[cur, 0], tm) * p[:, 0:1]
    for k in range(1, TOP_K):
        y = y + _load_rows(buf.at[cur, k], tm) * p[:, k:k + 1]
    xo_ref[...] = x_ref[...] + ga2_ref[...] * (_rms(y) * g3_ref[...])


def _combine(yb, dest, p4, xs, g3, modm, layer):
    tm = 128
    t = xs.shape[0]
    nt = t // tm
    row = lambda i: layer * 8 + _mod_row(i, tm)
    return pl.pallas_call(
        functools.partial(_combine_kernel, tm=tm),
        grid=(nt,),
        in_specs=[
            pl.BlockSpec((tm * TOP_K,), lambda i: (i,), memory_space=pltpu.SMEM),
            pl.BlockSpec((tm * TOP_K,), lambda i: (jnp.minimum(i + 1, nt - 1),), memory_space=pltpu.SMEM),
            pl.BlockSpec((tm, TOP_K), lambda i: (i, 0)),
            pl.BlockSpec((tm, D), lambda i: (i, 0)),
            pl.BlockSpec((1, D), lambda i: (0, 0)),
            pl.BlockSpec((None, 1, D), lambda i: (row(i), 0, 5)),
            pl.BlockSpec(memory_space=pl.ANY),
        ],
        out_specs=pl.BlockSpec((tm, D), lambda i: (i, 0)),
        out_shape=jax.ShapeDtypeStruct((t, D), F32),
        scratch_shapes=[pltpu.VMEM((2, TOP_K, tm * ROW_S, 128), F32), pltpu.SemaphoreType.DMA((2,))],
        compiler_params=_cparams(("arbitrary",), VMEM_LIMIT),
        name="moe_combine",
    )(dest, dest, p4, xs, g3.reshape(1, D), modm, yb)


def _moe(h2, e4, p4, r4, counts, xs, g3, modm, layer, wgu_bf, bgu, wd_bf, bd):
    t = e4.shape[0]
    nblk = -(-(t * TOP_K + N_EXPERTS * (MOE_TM - 1)) // MOE_TM)
    cnt = counts.reshape(N_EXPERTS).astype(jnp.int32)
    padded = (cnt + MOE_TM - 1) // MOE_TM * MOE_TM
    pend = jnp.cumsum(padded)
    pstart = pend - padded
    dest = (pstart[e4] + r4).reshape(-1).astype(jnp.int32)
    nused = (pend[-1:] // MOE_TM).astype(jnp.int32)
    blk_start = jnp.arange(nblk, dtype=jnp.int32) * MOE_TM
    block_e = jnp.minimum(jnp.sum((pend[None, :] <= blk_start[:, None]).astype(jnp.int32), axis=1),
                          N_EXPERTS - 1).astype(jnp.int32)
    xs_sorted = _dispatch(h2, dest, pstart.astype(jnp.int32), padded.astype(jnp.int32), nused, nblk)
    yb = _experts(xs_sorted, block_e, nused, wgu_bf, bgu, wd_bf, bd, nblk)
    return _combine(yb, dest, p4, xs, g3, modm, layer)


N_CTX_BLK = LC // HGRN_R
N_LAT_BLK = L // HGRN_R
N_SEQ_BLK = N_CTX_BLK + N_LAT_BLK


def _scan_block(b, s, rev):
    if rev:
        return jnp.where(s < N_CTX_BLK, T_LAT // HGRN_R + N_CTX_BLK * b + (N_CTX_BLK - 1 - s),
                         N_LAT_BLK * b + (N_SEQ_BLK - 1 - s))
    return jnp.where(s < N_CTX_BLK, T_LAT // HGRN_R + N_CTX_BLK * b + s, N_LAT_BLK * b + (s - N_CTX_BLK))


def _hgrn_kernel(*refs, rev):
    if rev:
        q_ref, f_ref, v_ref, lb_ref, of_ref, g_ref, on_ref, o_ref, st = refs
    else:
        q_ref, f_ref, v_ref, lb_ref, o_ref, st = refs

    @pl.when(pl.program_id(1) == 0)
    def _():
        st[...] = jnp.zeros_like(st)

    c = HGRN_C
    r_i = lax.broadcasted_iota(jnp.int32, (c, c), 0)
    c_i = lax.broadcasted_iota(jnp.int32, (c, c), 1)
    mask = (c_i >= r_i) if rev else (c_i <= r_i)
    tri = mask.astype(F32)
    order = range(HGRN_R // c - 1, -1, -1) if rev else range(HGRN_R // c)
    for h in range(HGRN_HEADS):
        hs = slice(h * 128, (h + 1) * 128)
        lb = lb_ref[:, hs]
        sg = jax.nn.sigmoid(f_ref[:, hs])
        logf = jnp.log(lb + (1.0 - lb) * sg)
        kk = (1.0 - lb) * (1.0 - sg)
        q = q_ref[:, hs]
        qs = q * jax.nn.sigmoid(q)
        v = v_ref[:, hs]
        state = st[h]
        outs = [None] * (HGRN_R // c)
        for ci in order:
            rs = slice(ci * c, (ci + 1) * c)
            cum = jnp.dot(tri, logf[rs], precision=HI, preferred_element_type=F32)
            total = cum[0:1] if rev else cum[c - 1:c]
            mid = cum[c // 2:c // 2 + 1]
            q_in = qs[rs] * jnp.exp(cum)
            q_d = qs[rs] * jnp.exp(cum - mid)
            k_i = kk[rs] * jnp.exp(mid - cum)
            k_e = kk[rs] * jnp.exp(total - cum)
            sc = lax.dot_general(q_d.astype(BF16), k_i.astype(BF16), (((1,), (1,)), ((), ())),
                                 preferred_element_type=F32)
            sc = jnp.where(mask, sc, 0.0)
            vb = v[rs].astype(BF16)
            intra = jnp.dot(sc.astype(BF16), vb, preferred_element_type=F32)
            inter = lax.dot_general(q_in.astype(BF16), state.astype(BF16), (((1,), (1,)), ((), ())),
                                    preferred_element_type=F32)
            outs[ci] = intra + inter
            state = state * jnp.exp(total) + lax.dot_general(
                vb, k_e.astype(BF16), (((0,), (0,)), ((), ())), preferred_element_type=F32)
        st[h] = state
        o = jnp.concatenate(outs, axis=0)
        if rev:
            o = o + of_ref[:, hs]
            g = g_ref[:, hs]
            o_ref[:, hs] = (_rms(o) * on_ref[:, hs] * (g * jax.nn.sigmoid(g))).astype(BF16)
        else:
            o_ref[:, hs] = o


def _hgrn(pa, pb, lb, rev, of=None, onorm=None):
    r = HGRN_R
    blk = lambda b, s: _scan_block(b, s, rev)
    col = lambda c: pl.BlockSpec((r, HALF), lambda b, s: (blk(b, s), c))
    vec = pl.BlockSpec((1, HALF), lambda b, s: (0, 0))
    in_specs = [col(0), col(2 if rev else 1), col(3), vec]
    args = [pa, pa, pa, lb.reshape(1, HALF)]
    if rev:
        in_specs += [col(0), col(0), vec]
        args += [of, pb, onorm.reshape(1, HALF)]
    return pl.pallas_call(
        functools.partial(_hgrn_kernel, rev=rev),
        grid=(B, N_SEQ_BLK),
        in_specs=in_specs,
        out_specs=col(0),
        out_shape=jax.ShapeDtypeStruct((T_ALL, HALF), BF16 if rev else F32),
        scratch_shapes=[pltpu.VMEM((HGRN_HEADS, 128, 128), F32)],
        compiler_params=_cparams(("parallel", "arbitrary"), VMEM_LIMIT),
        name="hgrn_bwd" if rev else "hgrn_fwd",
    )(*args)


def _lru_kernel(*refs, rev):
    if rev:
        (x_ref, xp_ref, xn_ref, cw_ref, cb_ref, wr_ref, br_ref, wi_ref, bi_ref, lam_ref,
         hf_ref, gt_ref, o_ref, xcat, a_s, u_s, h_s, carry) = refs
    else:
        (x_ref, xp_ref, xn_ref, cw_ref, cb_ref, wr_ref, br_ref, wi_ref, bi_ref, lam_ref,
         o_ref, xcat, a_s, u_s, h_s, carry) = refs
    s = pl.program_id(1)
    r = LRU_R

    @pl.when(s == 0)
    def _():
        carry[...] = jnp.zeros_like(carry)

    is_ctx = s < N_CTX_BLK
    if rev:
        j = jnp.where(is_ctx, N_CTX_BLK - 1 - s, N_SEQ_BLK - 1 - s)
    else:
        j = jnp.where(is_ctx, s, s - N_CTX_BLK)
    nb = jnp.where(is_ctx, N_CTX_BLK, N_LAT_BLK)
    xcat[0:8, :] = jnp.where(j == 0, 0.0, xp_ref[...])
    xcat[8:8 + r, :] = x_ref[...]
    xcat[8 + r:16 + r, :] = jnp.where(j == nb - 1, 0.0, xn_ref[...])
    xc = cb_ref[...] + xcat[pl.ds(6, r), :] * cw_ref[0:1, :]
    for t in range(1, 4):
        xc = xc + xcat[pl.ds(6 + t, r), :] * cw_ref[t:t + 1, :]

    lam = lam_ref[...]
    sp = jnp.maximum(-lam, 0.0) + jnp.log1p(jnp.exp(-jnp.abs(lam)))
    hd = HALF // LRU_HEADS
    for h in range(LRU_HEADS):
        cs = slice(h * hd, (h + 1) * hd)
        xh = xc[:, cs]
        rg = jax.nn.sigmoid(_bdot(xh, wr_ref[h]) + br_ref[:, cs])
        ig = jax.nn.sigmoid(_bdot(xh, wi_ref[h]) + bi_ref[:, cs])
        log_a = -LRU_C * rg * sp[:, cs]
        a = jnp.exp(log_a)
        a_s[:, cs] = a
        u_s[:, cs] = jnp.sqrt(-jnp.tanh(log_a) * (a * a + 1.0)) * (ig * xh)

    def step(t, h):
        tt = (r - 1 - t) if rev else t
        h = a_s[pl.ds(tt, 1), :] * h + u_s[pl.ds(tt, 1), :]
        h_s[pl.ds(tt, 1), :] = h
        return h

    carry[...] = lax.fori_loop(0, r, step, carry[...])
    if rev:
        o_ref[...] = ((h_s[...] + hf_ref[...]) * jax.nn.gelu(gt_ref[...])).astype(BF16)
    else:
        o_ref[...] = h_s[...]


def _lru(p, conv_w, conv_b, w_r, b_r, w_i, b_i, lam, rev, hf=None):
    r = LRU_R
    d = 1 if rev else 0
    blk = lambda b, s: _scan_block(b, s, rev)
    xcol = 2
    n8 = T_ALL // 8
    vec = lambda: pl.BlockSpec((None, 1, HALF), lambda b, s: (d, 0, 0))
    wspec = lambda: pl.BlockSpec((None, LRU_HEADS, 128, 128), lambda b, s: (d, 0, 0, 0))
    in_specs = [
        pl.BlockSpec((r, HALF), lambda b, s: (blk(b, s), xcol)),
        pl.BlockSpec((8, HALF), lambda b, s: (jnp.maximum(blk(b, s) * (r // 8) - 1, 0), xcol)),
        pl.BlockSpec((8, HALF), lambda b, s: (jnp.minimum((blk(b, s) + 1) * (r // 8), n8 - 1), xcol)),
        pl.BlockSpec((4, HALF), lambda b, s: (0, 0)),
        pl.BlockSpec((1, HALF), lambda b, s: (0, 0)),
        wspec(), vec(), wspec(), vec(), vec(),
    ]
    args = [p, p, p, conv_w, conv_b.reshape(1, HALF), w_r, b_r.reshape(2, 1, HALF), w_i,
            b_i.reshape(2, 1, HALF), lam.reshape(2, 1, HALF)]
    if rev:
        in_specs += [pl.BlockSpec((r, HALF), lambda b, s: (blk(b, s), 0)),
                     pl.BlockSpec((r, HALF), lambda b, s: (blk(b, s), xcol - 1))]
        args += [hf, p]
    return pl.pallas_call(
        functools.partial(_lru_kernel, rev=rev),
        grid=(B, N_SEQ_BLK),
        in_specs=in_specs,
        out_specs=pl.BlockSpec((r, HALF), lambda b, s: (blk(b, s), 0)),
        out_shape=jax.ShapeDtypeStruct((T_ALL, HALF), BF16 if rev else F32),
        scratch_shapes=[pltpu.VMEM((r + 16, HALF), F32), pltpu.VMEM((r, HALF), F32),
                        pltpu.VMEM((r, HALF), F32), pltpu.VMEM((r, HALF), F32),
                        pltpu.VMEM((1, HALF), F32)],
        compiler_params=_cparams(("parallel", "arbitrary"), VMEM_LIMIT),
        name="lru_bwd" if rev else "lru_fwd",
    )(*args)


def kernel(x, c, ctx, c_ctx, mod_w, mod_b, norm_g, ab_w_in, ab_w_out, gmlp_v_g, gmlp_ws, gmlp_bs, cd_w_in, cd_w_out, hgrn_lb, hgrn_onorm_g, lru_conv_w, lru_conv_b, lru_wr, lru_br, lru_wi, lru_bi, lru_lambda, router_w, router_b, exp_w_gu, exp_b_gu, exp_w_down, exp_b_down):
    cond8 = jnp.concatenate([c, c_ctx[None, :], jnp.zeros((5, D), F32)], axis=0)
    modm = _modulation(cond8, mod_w, mod_b).reshape(DEPTH * 8, 1, 6 * D)
    xs = jnp.concatenate([x.reshape(T_LAT, D), ctx.reshape(T_CTX, D)], axis=0)

    lb_soft = jax.nn.softmax(hgrn_lb.astype(F32), axis=0)
    lb_all = jnp.cumsum(lb_soft, axis=0) - lb_soft[0]

    (p0,) = _inproj(xs, norm_g[0, 0], modm, 0, [ab_w_in[0].astype(BF16)], BF16)
    zc, zs = _fnet_chan(p0)
    mix_a = jnp.concatenate([_fnet_pos(zc, zs, 0, L, 512, 1024),
                             _fnet_pos(zc, zs, T_LAT, LC, LC, LC)], axis=0)
    mix_g = _gmlp(p0, gmlp_v_g[0], gmlp_ws[0], gmlp_bs[0])
    xs, h2, e4, p4, r4, counts = _outproj(mix_a, mix_g, ab_w_out[0].astype(BF16), xs, T_ALL, norm_g[0, 1],
                                          norm_g[0, 2], modm, 0, router_w[0], router_b[0])
    xs = _moe(h2, e4, p4, r4, counts, xs, norm_g[0, 3], modm, 0,
              exp_w_gu[0].astype(BF16), exp_b_gu[0], exp_w_down[0].astype(BF16), exp_b_down[0])

    w_cd = cd_w_in[0].astype(BF16)
    p1a, p1b = _inproj(xs, norm_g[1, 0], modm, 1, [w_cd[:, :4 * HALF], w_cd[:, 4 * HALF:]], F32)
    o_f = _hgrn(p1a, p1b, lb_all[1], False)
    hg = _hgrn(p1a, p1b, lb_all[1], True, o_f, hgrn_onorm_g[0])
    lru_args = (lru_conv_w[0], lru_conv_b[0], lru_wr[0], lru_br[0], lru_wi[0], lru_bi[0], lru_lambda[0])
    h_f = _lru(p1b, *lru_args, False)
    lr = _lru(p1b, *lru_args, True, h_f)
    xl, h2, e4, p4, r4, counts = _outproj(hg, lr, cd_w_out[0].astype(BF16), xs, T_LAT, norm_g[1, 1],
                                          norm_g[1, 2], modm, 1, router_w[1], router_b[1])
    xl = _moe(h2, e4, p4, r4, counts, xl, norm_g[1, 3], modm, 1,
              exp_w_gu[1].astype(BF16), exp_b_gu[1], exp_w_down[1].astype(BF16), exp_b_down[1])
    return xl.reshape(B, L, D)
```

```python
import functools

import numpy as np
import jax
import jax.numpy as jnp
from jax import lax
from jax.experimental import pallas as pl
from jax.experimental.pallas import tpu as pltpu

F32 = jnp.float32
BF16 = jnp.bfloat16
HI = lax.Precision.HIGHEST

D = 2048
B = 2
L = 4096
LC = 256
T_LAT = B * L
T_CTX = B * LC
T_ALL = T_LAT + T_CTX
DEPTH = 2
EPS = 1e-6

HALF = D // 2
N_EXPERTS = 32
TOP_K = 4
D_EXPERT = D // 2
SWIGLU_LIMIT = 7.0
SWIGLU_ALPHA = 1.702
MOE_TM = 256

FNET_GROUPS = 4
FNET_GD = HALF // FNET_GROUPS
GMLP_HEADS = 8
GMLP_CHUNK = 128
HGRN_HEADS = 8
HGRN_R = 128
HGRN_C = 64
LRU_HEADS = 8
LRU_R = 128
LRU_C = 8.0

VMEM_LIMIT = 56 * 1024 * 1024


def _cparams(sem, vmem=None):
    return pltpu.CompilerParams(dimension_semantics=sem, vmem_limit_bytes=vmem)


def _mod_row(i, tm):
    return jnp.where(i < L // tm, 0, jnp.where(i < 2 * L // tm, 1, 2))


def _rms(x):
    return x * lax.rsqrt(jnp.mean(x * x, axis=-1, keepdims=True) + EPS)


def _bdot(a, b):
    return jnp.dot(a.astype(BF16), b.astype(BF16), preferred_element_type=F32)


ROW_S = D // 128


def _row_slab(ref, r):
    return ref.at[pl.ds(pl.multiple_of(r * ROW_S, ROW_S), ROW_S), :]


def _store_rows(ref, val, row0=0):
    n = val.shape[0]
    for s in range(ROW_S):
        ref[pl.ds(row0 * ROW_S + s, n, stride=ROW_S), :] = val[:, s * 128:(s + 1) * 128]


def _load_rows(ref, n):
    return jnp.concatenate([ref[pl.ds(s, n, stride=ROW_S), :] for s in range(ROW_S)], axis=1)


def _mod_kernel(c_ref, w_ref, b_ref, o_ref):
    cnd = c_ref[...]
    s = cnd * jax.nn.sigmoid(cnd)
    o_ref[...] = jnp.dot(s, w_ref[...], precision=HI, preferred_element_type=F32) + b_ref[...]


def _modulation(cond8, mod_w, mod_b):
    tn = 1024
    n = mod_w.shape[-1]
    return pl.pallas_call(
        _mod_kernel,
        grid=(DEPTH, n // tn),
        in_specs=[
            pl.BlockSpec((8, D), lambda l, j: (0, 0)),
            pl.BlockSpec((None, D, tn), lambda l, j: (l, 0, j)),
            pl.BlockSpec((None, 1, tn), lambda l, j: (l, 0, j)),
        ],
        out_specs=pl.BlockSpec((None, 8, tn), lambda l, j: (l, 0, j)),
        out_shape=jax.ShapeDtypeStruct((DEPTH, 8, n), F32),
        compiler_params=_cparams(("parallel", "parallel"), VMEM_LIMIT),
        name="modulation",
    )(cond8, mod_w, mod_b.reshape(DEPTH, 1, n))


def _inproj_kernel(*refs, nw):
    x_ref, g_ref, sh_ref, sc_ref = refs[:4]
    w_refs, o_refs = refs[4:4 + nw], refs[4 + nw:]
    y = _rms(x_ref[...]) * g_ref[...]
    h = (y * (1.0 + sc_ref[...]) + sh_ref[...]).astype(BF16)
    for w_ref, o_ref in zip(w_refs, o_refs):
        o_ref[...] = jnp.dot(h, w_ref[...], preferred_element_type=F32).astype(o_ref.dtype)


def _inproj(xs, g, modm, layer, ws_bf, out_dtype):
    tm = 256
    t = xs.shape[0]
    row = lambda i: layer * 8 + _mod_row(i, tm)
    return pl.pallas_call(
        functools.partial(_inproj_kernel, nw=len(ws_bf)),
        grid=(t // tm,),
        in_specs=[
            pl.BlockSpec((tm, D), lambda i: (i, 0)),
            pl.BlockSpec((1, D), lambda i: (0, 0)),
            pl.BlockSpec((None, 1, D), lambda i: (row(i), 0, 0)),
            pl.BlockSpec((None, 1, D), lambda i: (row(i), 0, 1)),
        ] + [pl.BlockSpec(w.shape, lambda i: (0, 0), pipeline_mode=pl.Buffered(1)) for w in ws_bf],
        out_specs=[pl.BlockSpec((tm, w.shape[1]), lambda i: (i, 0)) for w in ws_bf],
        out_shape=[jax.ShapeDtypeStruct((t, w.shape[1]), out_dtype) for w in ws_bf],
        compiler_params=_cparams(("parallel",), VMEM_LIMIT),
        name="inproj",
    )(xs, g.reshape(1, D), modm, modm, *ws_bf)


def _fnet_chan_kernel(p_ref, cs_ref, zc_ref, zs_ref):
    for g in range(FNET_GROUPS):
        sl = slice(g * FNET_GD, (g + 1) * FNET_GD)
        z = jnp.dot(p_ref[:, sl].astype(BF16), cs_ref[...], preferred_element_type=F32)
        zc_ref[:, sl] = z[:, :FNET_GD].astype(BF16)
        zs_ref[:, sl] = z[:, FNET_GD:].astype(BF16)


def _fnet_chan(p):
    tm = 512
    t = p.shape[0]
    k = np.arange(FNET_GD)
    ang = 2.0 * np.pi * ((k[:, None] * k[None, :]) % FNET_GD) / FNET_GD
    cs = np.concatenate([np.cos(ang), np.sin(ang)], axis=1) / np.sqrt(FNET_GD)
    cs = jnp.asarray(cs, F32).astype(BF16)
    return pl.pallas_call(
        _fnet_chan_kernel,
        grid=(t // tm,),
        in_specs=[
            pl.BlockSpec((tm, HALF), lambda i: (i, 0)),
            pl.BlockSpec((FNET_GD, 2 * FNET_GD), lambda i: (0, 0)),
        ],
        out_specs=[pl.BlockSpec((tm, HALF), lambda i: (i, 0))] * 2,
        out_shape=[jax.ShapeDtypeStruct((t, HALF), BF16)] * 2,
        compiler_params=_cparams(("parallel",), VMEM_LIMIT),
        name="fnet_chan",
    )(p, cs)


def _fnet_pos_kernel(bre_ref, bim_ref, cre_ref, cim_ref, rre_ref, rim_ref, zc_ref, zs_ref, o_ref, acc):
    nt = pl.program_id(2)
    br, bi = bre_ref[...], bim_ref[...]
    cr, ci = cre_ref[...], cim_ref[...]
    rr, ri = rre_ref[...], rim_ref[...]
    tr = br * cr - bi * ci
    ti = br * ci + bi * cr
    er = (tr * rr - ti * ri).astype(BF16)
    ei = (tr * ri + ti * rr).astype(BF16)
    part = (jnp.dot(er, zc_ref[...], preferred_element_type=F32)
            + jnp.dot(ei, zs_ref[...], preferred_element_type=F32))

    @pl.when(nt == 0)
    def _():
        acc[...] = part

    @pl.when(nt > 0)
    def _():
        acc[...] += part

    @pl.when(nt == pl.num_programs(2) - 1)
    def _():
        o_ref[...] = acc[...].astype(BF16)


def _fnet_pos(zc, zs, row0, seq, tk, tn):
    nkt, nnt = seq // tk, seq // tn
    th = 2.0 * np.pi / seq
    kk = np.arange(tk)[:, None]
    nn = np.arange(tn)[None, :]
    base = th * ((kk * nn) % seq)
    n0 = (np.arange(nnt) * tn)[:, None, None]
    col = th * ((np.arange(tk)[None, :, None] * n0) % seq)
    k0 = (np.arange(nkt) * tk)[:, None, None, None]
    nfull = (np.arange(nnt) * tn)[None, :, None, None] + np.arange(tn)[None, None, None, :]
    row = th * ((k0 * nfull) % seq)
    scale = 1.0 / np.sqrt(seq)
    f = lambda a: jnp.asarray(a, F32)
    tabs = (f(np.cos(base)), f(-np.sin(base)), f(np.cos(col)), f(-np.sin(col)),
            f(np.cos(row) * scale), f(-np.sin(row) * scale))
    rb = row0 // tn
    ob = row0 // tk
    return pl.pallas_call(
        _fnet_pos_kernel,
        grid=(B, nkt, nnt),
        in_specs=[
            pl.BlockSpec((tk, tn), lambda b, k, n: (0, 0)),
            pl.BlockSpec((tk, tn), lambda b, k, n: (0, 0)),
            pl.BlockSpec((None, tk, 1), lambda b, k, n: (n, 0, 0)),
            pl.BlockSpec((None, tk, 1), lambda b, k, n: (n, 0, 0)),
            pl.BlockSpec((None, None, 1, tn), lambda b, k, n: (k, n, 0, 0)),
            pl.BlockSpec((None, None, 1, tn), lambda b, k, n: (k, n, 0, 0)),
            pl.BlockSpec((tn, HALF), lambda b, k, n: (rb + b * nnt + n, 0)),
            pl.BlockSpec((tn, HALF), lambda b, k, n: (rb + b * nnt + n, 0)),
        ],
        out_specs=pl.BlockSpec((tk, HALF), lambda b, k, n: (b * nkt + k, 0)),
        out_shape=jax.ShapeDtypeStruct((B * seq, HALF), BF16),
        scratch_shapes=[pltpu.VMEM((tk, HALF), F32)],
        compiler_params=_cparams(("parallel", "parallel", "arbitrary"), VMEM_LIMIT),
        name=f"fnet_pos_{seq}",
    )(*tabs, zc, zs)


def _gmlp_kernel(u_ref, v_ref, vg_ref, ws_ref, bst_ref, o_ref, *, tm):
    zu = jax.nn.gelu(u_ref[...].astype(F32))
    zv = jax.nn.gelu(v_ref[...].astype(F32))
    mu = jnp.mean(zv, axis=-1, keepdims=True)
    dv = zv - mu
    var = jnp.mean(dv * dv, axis=-1, keepdims=True)
    vn = dv * lax.rsqrt(var + EPS) * vg_ref[...]
    hd = HALF // GMLP_HEADS
    for c in range(tm // GMLP_CHUNK):
        rs = slice(c * GMLP_CHUNK, (c + 1) * GMLP_CHUNK)
        for h in range(GMLP_HEADS):
            cs = slice(h * hd, (h + 1) * hd)
            s = jnp.dot(ws_ref[h], vn[rs, cs], precision=HI, preferred_element_type=F32)
            s = s + bst_ref[:, h:h + 1]
            o_ref[rs, cs] = (zu[rs, cs] * s).astype(BF16)


def _gmlp(p, v_g, w_s, b_s):
    tm = 256
    t = p.shape[0]
    return pl.pallas_call(
        functools.partial(_gmlp_kernel, tm=tm),
        grid=(t // tm,),
        in_specs=[
            pl.BlockSpec((tm, HALF), lambda i: (i, 1)),
            pl.BlockSpec((tm, HALF), lambda i: (i, 2)),
            pl.BlockSpec((1, HALF), lambda i: (0, 0)),
            pl.BlockSpec((GMLP_HEADS, GMLP_CHUNK, GMLP_CHUNK), lambda i: (0, 0, 0)),
            pl.BlockSpec((GMLP_CHUNK, GMLP_HEADS), lambda i: (0, 0)),
        ],
        out_specs=pl.BlockSpec((tm, HALF), lambda i: (i, 0)),
        out_shape=jax.ShapeDtypeStruct((t, HALF), BF16),
        compiler_params=_cparams(("parallel",), VMEM_LIMIT),
        name="gmlp",
    )(p, p, v_g.reshape(1, HALF), w_s, b_s.T)


def _outproj_kernel(ma_ref, mb_ref, w_ref, x_ref, g1_ref, g2_ref, ga1_ref, sh2_ref, sc2_ref,
                    rw_ref, rb_ref, xo_ref, h2_ref, e_ref, p_ref, rk_ref, cnt_ref, carry, rw2, *, tm):
    @pl.when(pl.program_id(0) == 0)
    def _():
        carry[...] = jnp.zeros_like(carry)
        rw = rw_ref[...]
        rw_hi = rw.astype(BF16)
        rw2[:, :N_EXPERTS] = rw_hi
        rw2[:, N_EXPERTS:] = (rw - rw_hi.astype(F32)).astype(BF16)

    sub = 128
    iota = lax.broadcasted_iota(jnp.int32, (sub, N_EXPERTS), 1).astype(F32)
    lane4 = lax.broadcasted_iota(jnp.int32, (sub, TOP_K), 1)
    r_i = lax.broadcasted_iota(jnp.int32, (sub, sub), 0)
    c_i = lax.broadcasted_iota(jnp.int32, (sub, sub), 1)
    tri = (c_i < r_i).astype(BF16)
    running = carry[...]
    for r0 in range(0, tm, sub):
        rs = slice(r0, r0 + sub)
        m = (jnp.dot(ma_ref[rs, :], w_ref[:HALF, :], preferred_element_type=F32)
             + jnp.dot(mb_ref[rs, :], w_ref[HALF:, :], preferred_element_type=F32))
        xn = x_ref[rs, :] + ga1_ref[...] * (_rms(m) * g1_ref[...])
        xo_ref[rs, :] = xn
        h2 = (_rms(xn) * g2_ref[...]) * (1.0 + sc2_ref[...]) + sh2_ref[...]
        _store_rows(h2_ref, h2, r0)

        h_hi = h2.astype(BF16)
        h_lo = (h2 - h_hi.astype(F32)).astype(BF16)
        t1 = jnp.dot(h_hi, rw2[...], preferred_element_type=F32)
        t2 = jnp.dot(h_lo, rw2[:, :N_EXPERTS], preferred_element_type=F32)
        lg = t1[:, :N_EXPERTS] + t1[:, N_EXPERTS:] + t2 + rb_ref[...]
        idxs, vals = [], []
        for _ in range(TOP_K):
            mx = jnp.max(lg, axis=-1, keepdims=True)
            ix = jnp.min(jnp.where(lg == mx, iota, float(N_EXPERTS)), axis=-1, keepdims=True)
            idxs.append(ix)
            vals.append(mx)
            lg = jnp.where(iota == ix, -jnp.inf, lg)
        exs = [jnp.exp(v - vals[0]) for v in vals]
        den = exs[0] + exs[1] + exs[2] + exs[3]

        onehot = [(iota == ix) for ix in idxs]
        cnt = jnp.zeros((sub, N_EXPERTS), F32)
        for oh in onehot:
            cnt = cnt + oh.astype(F32)
        prefix = jnp.dot(tri, cnt.astype(BF16), preferred_element_type=F32) + running

        e4 = jnp.zeros((sub, TOP_K), jnp.int32)
        p4 = jnp.zeros((sub, TOP_K), F32)
        r4 = jnp.zeros((sub, TOP_K), jnp.int32)
        for k in range(TOP_K):
            rk = jnp.sum(jnp.where(onehot[k], prefix, 0.0), axis=-1, keepdims=True)
            e4 = jnp.where(lane4 == k, idxs[k].astype(jnp.int32), e4)
            p4 = jnp.where(lane4 == k, exs[k] / den, p4)
            r4 = jnp.where(lane4 == k, rk.astype(jnp.int32), r4)
        e_ref[rs, :] = e4
        p_ref[rs, :] = p4
        rk_ref[rs, :] = r4
        running = running + jnp.sum(cnt, axis=0, keepdims=True)
    carry[...] = running
    cnt_ref[...] = running


def _outproj(ma, mb, w_bf, xs, t, g1, g2, modm, layer, rw, rb):
    tm = 256
    row = lambda i: layer * 8 + _mod_row(i, tm)
    modspec = lambda c: pl.BlockSpec((None, 1, D), lambda i: (row(i), 0, c))
    vec = pl.BlockSpec((1, D), lambda i: (0, 0))
    small = lambda dt: jax.ShapeDtypeStruct((t, TOP_K), dt)
    return pl.pallas_call(
        functools.partial(_outproj_kernel, tm=tm),
        grid=(t // tm,),
        in_specs=[
            pl.BlockSpec((tm, HALF), lambda i: (i, 0)),
            pl.BlockSpec((tm, HALF), lambda i: (i, 0)),
            pl.BlockSpec((D, D), lambda i: (0, 0)),
            pl.BlockSpec((tm, D), lambda i: (i, 0)),
            vec, vec, modspec(2), modspec(3), modspec(4),
            pl.BlockSpec((D, N_EXPERTS), lambda i: (0, 0)),
            pl.BlockSpec((1, N_EXPERTS), lambda i: (0, 0)),
        ],
        out_specs=[
            pl.BlockSpec((tm, D), lambda i: (i, 0)),
            pl.BlockSpec((tm * ROW_S, 128), lambda i: (i, 0)),
            pl.BlockSpec((tm, TOP_K), lambda i: (i, 0)),
            pl.BlockSpec((tm, TOP_K), lambda i: (i, 0)),
            pl.BlockSpec((tm, TOP_K), lambda i: (i, 0)),
            pl.BlockSpec((1, N_EXPERTS), lambda i: (0, 0)),
        ],
        out_shape=[
            jax.ShapeDtypeStruct((t, D), F32),
            jax.ShapeDtypeStruct((t * ROW_S, 128), F32),
            small(jnp.int32), small(F32), small(jnp.int32),
            jax.ShapeDtypeStruct((1, N_EXPERTS), F32),
        ],
        scratch_shapes=[pltpu.VMEM((1, N_EXPERTS), F32), pltpu.VMEM((D, 2 * N_EXPERTS), BF16)],
        compiler_params=_cparams(("arbitrary",), VMEM_LIMIT),
        name="outproj_router",
    )(ma, mb, w_bf, xs, g1.reshape(1, D), g2.reshape(1, D), modm, modm, modm,
      rw, rb.reshape(1, N_EXPERTS))


def _dispatch_kernel(pstart_ref, padded_ref, nused_ref, dest_ref, h_ref, xs_ref, zbuf, sem, *, tm, nblk):
    i = pl.program_id(0)
    blk_rows = MOE_TM * ROW_S

    @pl.when(i == 0)
    def _():
        zbuf[...] = jnp.zeros_like(zbuf)

        def zero_block(slot):
            start = pl.multiple_of(slot * ROW_S, blk_rows)
            cp = pltpu.make_async_copy(zbuf, xs_ref.at[pl.ds(start, blk_rows), :], sem)
            cp.start()
            cp.wait()

        def per_expert(e, c):
            @pl.when(padded_ref[e] > 0)
            def _():
                zero_block(pstart_ref[e] + padded_ref[e] - MOE_TM)
            return c

        lax.fori_loop(0, N_EXPERTS, per_expert, 0)

        def per_tail(b, c):
            zero_block(b * MOE_TM)
            return c

        lax.fori_loop(nused_ref[0], nblk, per_tail, 0)

    def row_copy(r, k):
        return pltpu.make_async_copy(_row_slab(h_ref, r), _row_slab(xs_ref, dest_ref[r * TOP_K + k]), sem)

    def issue(r, c):
        for k in range(TOP_K):
            row_copy(r, k).start()
        return c

    def drain(r, c):
        for k in range(TOP_K):
            row_copy(r, k).wait()
        return c

    lax.fori_loop(0, tm, issue, 0)
    lax.fori_loop(0, tm, drain, 0)


def _dispatch(h2, dest, pstart, padded, nused, nblk):
    tm = 256
    t = h2.shape[0] // ROW_S
    return pl.pallas_call(
        functools.partial(_dispatch_kernel, tm=tm, nblk=nblk),
        grid_spec=pltpu.PrefetchScalarGridSpec(
            num_scalar_prefetch=3,
            grid=(t // tm,),
            in_specs=[
                pl.BlockSpec((tm * TOP_K,), lambda i, *_: (i,), memory_space=pltpu.SMEM),
                pl.BlockSpec((tm * ROW_S, 128), lambda i, *_: (i, 0)),
            ],
            out_specs=pl.BlockSpec(memory_space=pl.ANY),
            scratch_shapes=[pltpu.VMEM((MOE_TM * ROW_S, 128), F32), pltpu.SemaphoreType.DMA(())],
        ),
        out_shape=jax.ShapeDtypeStruct((nblk * MOE_TM * ROW_S, 128), F32),
        compiler_params=_cparams(("arbitrary",), VMEM_LIMIT),
        name="moe_dispatch",
    )(pstart, padded, nused, dest, h2)


W_CH = 256
W_NCH_GU = D // W_CH
W_NCH = W_NCH_GU + D_EXPERT // W_CH
W_STAGE = 4


def _expert_kernel(be_ref, nused_ref, first_ref, nxt_ref, lo_ref, hi_ref, slot_ref,
                   x_ref, bgu_ref, bd_ref, wgu_all, wd_all, o_ref, wgu_buf, wd_buf, stage, sem, *, layer):
    i = pl.program_id(0)
    wgu_hbm = wgu_all.at[layer]
    wd_hbm = wd_all.at[layer]

    def start_chunk(e, c):
        s = lax.rem(c, W_STAGE)

        @pl.when(c < W_NCH_GU)
        def _():
            r0 = pl.multiple_of(c * W_CH, W_CH)
            pltpu.make_async_copy(wgu_hbm.at[e, pl.ds(r0, W_CH), :], stage.at[s], sem.at[s]).start()

        @pl.when(c >= W_NCH_GU)
        def _():
            r0 = pl.multiple_of((c - W_NCH_GU) * W_CH, W_CH)
            pltpu.make_async_copy(wd_hbm.at[e, pl.ds(r0, W_CH), :], stage.at[s], sem.at[s]).start()

    def finish_chunk(c, dst):
        s = lax.rem(c, W_STAGE)
        pltpu.make_async_copy(wgu_hbm.at[0, pl.ds(0, W_CH), :], stage.at[s], sem.at[s]).wait()
        w = stage[s].astype(BF16)

        @pl.when(c < W_NCH_GU)
        def _():
            wgu_buf[dst, pl.ds(pl.multiple_of(c * W_CH, W_CH), W_CH), :] = w

        @pl.when(c >= W_NCH_GU)
        def _():
            wd_buf[dst, pl.ds(pl.multiple_of((c - W_NCH_GU) * W_CH, W_CH), W_CH), :] = w

    def stream(e, dst, c_lo, c_hi):
        def body(c, carry):
            finish_chunk(c, dst)

            @pl.when(c + W_STAGE < W_NCH)
            def _():
                start_chunk(e, c + W_STAGE)
            return carry
        lax.fori_loop(c_lo, c_hi, body, 0)

    def prime(e):
        for c in range(W_STAGE):
            start_chunk(e, c)

    @pl.when(i == 0)
    def _():
        prime(be_ref[0])
        stream(be_ref[0], slot_ref[0], 0, W_NCH)

    @pl.when(i < nused_ref[0])
    def _():
        e_next = nxt_ref[i]
        cur = slot_ref[i]

        @pl.when((first_ref[i] == 1) & (e_next >= 0))
        def _():
            prime(e_next)

        x = _load_rows(x_ref, MOE_TM).astype(BF16)
        gu = jnp.dot(x, wgu_buf[cur], preferred_element_type=F32) + bgu_ref[...]
        gate = jnp.minimum(gu[:, :D_EXPERT], SWIGLU_LIMIT)
        up = jnp.clip(gu[:, D_EXPERT:], -SWIGLU_LIMIT, SWIGLU_LIMIT)
        act = gate * jax.nn.sigmoid(SWIGLU_ALPHA * gate) * (up + 1.0)
        _store_rows(o_ref, jnp.dot(act.astype(BF16), wd_buf[cur], preferred_element_type=F32) + bd_ref[...])

        @pl.when(e_next >= 0)
        def _():
            stream(e_next, 1 - cur, lo_ref[i], hi_ref[i])

    @pl.when(i >= nused_ref[0])
    def _():
        o_ref[...] = jnp.zeros_like(o_ref)


def _experts(xs, block_e, nused, pstart, padded, layer, wgu, bgu, wd, bd, nblk):
    nb_e = padded // MOE_TM
    pos = jnp.arange(nblk, dtype=jnp.int32) - (pstart // MOE_TM)[block_e]
    nb = jnp.maximum(nb_e[block_e], 1)
    lo = (pos * W_NCH) // nb
    hi = ((pos + 1) * W_NCH) // nb
    first = (pos == 0).astype(jnp.int32)
    eid = jnp.arange(N_EXPERTS, dtype=jnp.int32)
    later = jnp.where(nb_e > 0, eid, N_EXPERTS)
    nxt_e = jnp.concatenate([lax.cummin(later, reverse=True)[1:], jnp.full((1,), N_EXPERTS, jnp.int32)])
    nxt_e = jnp.where(nxt_e >= N_EXPERTS, -1, nxt_e)
    slot_e = (jnp.cumsum((nb_e > 0).astype(jnp.int32)) - 1) % 2
    i32 = lambda a: a.astype(jnp.int32)
    live = lambda i, nu: jnp.maximum(jnp.minimum(i, nu[0] - 1), 0)
    blk = lambda f: (lambda i, be, nu, *_: f(i, be, nu))
    return pl.pallas_call(
        functools.partial(_expert_kernel, layer=layer),
        grid_spec=pltpu.PrefetchScalarGridSpec(
            num_scalar_prefetch=7,
            grid=(nblk,),
            in_specs=[
                pl.BlockSpec((MOE_TM * ROW_S, 128), blk(lambda i, be, nu: (live(i, nu), 0))),
                pl.BlockSpec((None, 1, 2 * D_EXPERT), blk(lambda i, be, nu: (be[live(i, nu)], 0, 0))),
                pl.BlockSpec((None, 1, D), blk(lambda i, be, nu: (be[live(i, nu)], 0, 0))),
                pl.BlockSpec(memory_space=pl.ANY),
                pl.BlockSpec(memory_space=pl.ANY),
            ],
            out_specs=pl.BlockSpec((MOE_TM * ROW_S, 128), blk(lambda i, be, nu: (i, 0))),
            scratch_shapes=[
                pltpu.VMEM((2, D, 2 * D_EXPERT), BF16),
                pltpu.VMEM((2, D_EXPERT, D), BF16),
                pltpu.VMEM((W_STAGE, W_CH, D), F32),
                pltpu.SemaphoreType.DMA((W_STAGE,)),
            ],
        ),
        out_shape=jax.ShapeDtypeStruct((nblk * MOE_TM * ROW_S, 128), F32),
        compiler_params=_cparams(("arbitrary",), VMEM_LIMIT),
        name="moe_experts",
    )(block_e, nused, i32(first), i32(nxt_e[block_e]), i32(lo), i32(hi), i32(slot_e[block_e]),
      xs, bgu.reshape(N_EXPERTS, 1, -1), bd.reshape(N_EXPERTS, 1, -1), wgu, wd)


def _combine_kernel(dcur_ref, dnext_ref, p_ref, x_ref, g3_ref, ga2_ref, yb_ref, xo_ref, buf, sem, *, tm):
    i = pl.program_id(0)
    cur = lax.rem(i, 2)

    def row_copy(slot, d_ref, r, k):
        return pltpu.make_async_copy(_row_slab(yb_ref, d_ref[r * TOP_K + k]),
                                     _row_slab(buf.at[slot, k], r), sem.at[slot])

    def issue(slot, d_ref):
        def body(r, c):
            for k in range(TOP_K):
                row_copy(slot, d_ref, r, k).start()
            return c
        lax.fori_loop(0, tm, body, 0)

    @pl.when(i == 0)
    def _():
        issue(0, dcur_ref)

    @pl.when(i + 1 < pl.num_programs(0))
    def _():
        issue(1 - cur, dnext_ref)

    def drain(r, c):
        for k in range(TOP_K):
            row_copy(cur, dcur_ref, r, k).wait()
        return c

    lax.fori_loop(0, tm, drain, 0)
    p = p_ref[...]
    y = _load_rows(buf.at[cur, 0], tm) * p[:, 0:1]
    for k in range(1, TOP_K):
        y = y + _load_rows(buf.at[cur, k], tm) * p[:, k:k + 1]
    xo_ref[...] = x_ref[...] + ga2_ref[...] * (_rms(y) * g3_ref[...])


def _combine(yb, dest, p4, xs, g3, modm, layer):
    tm = 128
    t = xs.shape[0]
    nt = t // tm
    row = lambda i: layer * 8 + _mod_row(i, tm)
    return pl.pallas_call(
        functools.partial(_combine_kernel, tm=tm),
        grid=(nt,),
        in_specs=[
            pl.BlockSpec((tm * TOP_K,), lambda i: (i,), memory_space=pltpu.SMEM),
            pl.BlockSpec((tm * TOP_K,), lambda i: (jnp.minimum(i + 1, nt - 1),), memory_space=pltpu.SMEM),
            pl.BlockSpec((tm, TOP_K), lambda i: (i, 0)),
            pl.BlockSpec((tm, D), lambda i: (i, 0)),
            pl.BlockSpec((1, D), lambda i: (0, 0)),
            pl.BlockSpec((None, 1, D), lambda i: (row(i), 0, 5)),
            pl.BlockSpec(memory_space=pl.ANY),
        ],
        out_specs=pl.BlockSpec((tm, D), lambda i: (i, 0)),
        out_shape=jax.ShapeDtypeStruct((t, D), F32),
        scratch_shapes=[pltpu.VMEM((2, TOP_K, tm * ROW_S, 128), F32), pltpu.SemaphoreType.DMA((2,))],
        compiler_params=_cparams(("arbitrary",), VMEM_LIMIT),
        name="moe_combine",
    )(dest, dest, p4, xs, g3.reshape(1, D), modm, yb)


def _moe(h2, e4, p4, r4, counts, xs, g3, modm, layer, wgu, bgu, wd, bd):
    t = e4.shape[0]
    nblk = -(-(t * TOP_K + N_EXPERTS * (MOE_TM - 1)) // MOE_TM)
    cnt = counts.reshape(N_EXPERTS).astype(jnp.int32)
    padded = (cnt + MOE_TM - 1) // MOE_TM * MOE_TM
    pend = jnp.cumsum(padded)
    pstart = pend - padded
    dest = (pstart[e4] + r4).reshape(-1).astype(jnp.int32)
    nused = (pend[-1:] // MOE_TM).astype(jnp.int32)
    blk_start = jnp.arange(nblk, dtype=jnp.int32) * MOE_TM
    block_e = jnp.minimum(jnp.sum((pend[None, :] <= blk_start[:, None]).astype(jnp.int32), axis=1),
                          N_EXPERTS - 1).astype(jnp.int32)
    xs_sorted = _dispatch(h2, dest, pstart.astype(jnp.int32), padded.astype(jnp.int32), nused, nblk)
    yb = _experts(xs_sorted, block_e, nused, pstart, padded, layer, wgu, bgu, wd, bd, nblk)
    return _combine(yb, dest, p4, xs, g3, modm, layer)


N_CTX_BLK = LC // HGRN_R
N_LAT_BLK = L // HGRN_R
N_SEQ_BLK = N_CTX_BLK + N_LAT_BLK


def _scan_block(b, s, rev):
    if rev:
        return jnp.where(s < N_CTX_BLK, T_LAT // HGRN_R + N_CTX_BLK * b + (N_CTX_BLK - 1 - s),
                         N_LAT_BLK * b + (N_SEQ_BLK - 1 - s))
    return jnp.where(s < N_CTX_BLK, T_LAT // HGRN_R + N_CTX_BLK * b + s, N_LAT_BLK * b + (s - N_CTX_BLK))


def _hgrn_kernel(*refs, rev):
    if rev:
        q_ref, f_ref, v_ref, lb_ref, of_ref, g_ref, on_ref, o_ref, st, qin_s, qd_s, ki_s, ke_s, v_s = refs
    else:
        q_ref, f_ref, v_ref, lb_ref, o_ref, st, qin_s, qd_s, ki_s, ke_s, v_s = refs

    @pl.when(pl.program_id(1) == 0)
    def _():
        st[...] = jnp.zeros_like(st)

    c = HGRN_C
    nc = HGRN_R // c
    r_i = lax.broadcasted_iota(jnp.int32, (c, c), 0)
    c_i = lax.broadcasted_iota(jnp.int32, (c, c), 1)
    mask = (c_i >= r_i) if rev else (c_i <= r_i)
    tri = mask.astype(F32)

    lb = lb_ref[...]
    decay = [None] * nc
    v_s[...] = v_ref[...].astype(BF16)
    for ci in range(nc):
        rs = slice(ci * c, (ci + 1) * c)
        sg = jax.nn.sigmoid(f_ref[rs, :])
        logf = jnp.log(lb + (1.0 - lb) * sg)
        kk = (1.0 - lb) * (1.0 - sg)
        q = q_ref[rs, :]
        qs = q * jax.nn.sigmoid(q)
        cum = jnp.dot(tri, logf, precision=HI, preferred_element_type=F32)
        total = cum[0:1] if rev else cum[c - 1:c]
        mid = cum[c // 2:c // 2 + 1]
        qin_s[rs, :] = (qs * jnp.exp(cum)).astype(BF16)
        qd_s[rs, :] = (qs * jnp.exp(cum - mid)).astype(BF16)
        ki_s[rs, :] = (kk * jnp.exp(mid - cum)).astype(BF16)
        ke_s[rs, :] = (kk * jnp.exp(total - cum)).astype(BF16)
        decay[ci] = jnp.exp(total)

    order = range(nc - 1, -1, -1) if rev else range(nc)
    for ci in order:
        rs = slice(ci * c, (ci + 1) * c)
        for h in range(HGRN_HEADS):
            hs = slice(h * 128, (h + 1) * 128)
            state = st[h]
            vb = v_s[rs, hs]
            sc = lax.dot_general(qd_s[rs, hs], ki_s[rs, hs], (((1,), (1,)), ((), ())),
                                 preferred_element_type=F32)
            sc = jnp.where(mask, sc, 0.0)
            intra = jnp.dot(sc.astype(BF16), vb, preferred_element_type=F32)
            inter = lax.dot_general(qin_s[rs, hs], state.astype(BF16), (((1,), (1,)), ((), ())),
                                    preferred_element_type=F32)
            st[h] = state * decay[ci][:, hs] + lax.dot_general(
                vb, ke_s[rs, hs], (((0,), (0,)), ((), ())), preferred_element_type=F32)
            o = intra + inter
            if rev:
                o = o + of_ref[rs, hs]
                g = g_ref[rs, hs]
                o_ref[rs, hs] = (_rms(o) * on_ref[:, hs] * (g * jax.nn.sigmoid(g))).astype(BF16)
            else:
                o_ref[rs, hs] = o


def _hgrn(pa, pb, lb, rev, of=None, onorm=None):
    r = HGRN_R
    blk = lambda b, s: _scan_block(b, s, rev)
    col = lambda c: pl.BlockSpec((r, HALF), lambda b, s: (blk(b, s), c))
    vec = pl.BlockSpec((1, HALF), lambda b, s: (0, 0))
    in_specs = [col(0), col(2 if rev else 1), col(3), vec]
    args = [pa, pa, pa, lb.reshape(1, HALF)]
    if rev:
        in_specs += [col(0), col(0), vec]
        args += [of, pb, onorm.reshape(1, HALF)]
    return pl.pallas_call(
        functools.partial(_hgrn_kernel, rev=rev),
        grid=(B, N_SEQ_BLK),
        in_specs=in_specs,
        out_specs=col(0),
        out_shape=jax.ShapeDtypeStruct((T_ALL, HALF), BF16 if rev else F32),
        scratch_shapes=[pltpu.VMEM((HGRN_HEADS, 128, 128), F32)] + [pltpu.VMEM((r, HALF), BF16)] * 5,
        compiler_params=_cparams(("parallel", "arbitrary"), VMEM_LIMIT),
        name="hgrn_bwd" if rev else "hgrn_fwd",
    )(*args)


def _lru_kernel(*refs, rev):
    if rev:
        (x_ref, xp_ref, xn_ref, cw_ref, cb_ref, wr_ref, br_ref, wi_ref, bi_ref, lam_ref,
         hf_ref, gt_ref, o_ref, xcat, a_s, u_s, h_s, carry) = refs
    else:
        (x_ref, xp_ref, xn_ref, cw_ref, cb_ref, wr_ref, br_ref, wi_ref, bi_ref, lam_ref,
         o_ref, xcat, a_s, u_s, h_s, carry) = refs
    s = pl.program_id(1)
    r = LRU_R

    @pl.when(s == 0)
    def _():
        carry[...] = jnp.zeros_like(carry)

    is_ctx = s < N_CTX_BLK
    if rev:
        j = jnp.where(is_ctx, N_CTX_BLK - 1 - s, N_SEQ_BLK - 1 - s)
    else:
        j = jnp.where(is_ctx, s, s - N_CTX_BLK)
    nb = jnp.where(is_ctx, N_CTX_BLK, N_LAT_BLK)
    xcat[0:8, :] = jnp.where(j == 0, 0.0, xp_ref[...])
    xcat[8:8 + r, :] = x_ref[...]
    xcat[8 + r:16 + r, :] = jnp.where(j == nb - 1, 0.0, xn_ref[...])
    xc = cb_ref[...] + xcat[pl.ds(6, r), :] * cw_ref[0:1, :]
    for t in range(1, 4):
        xc = xc + xcat[pl.ds(6 + t, r), :] * cw_ref[t:t + 1, :]

    lam = lam_ref[...]
    sp = jnp.maximum(-lam, 0.0) + jnp.log1p(jnp.exp(-jnp.abs(lam)))
    hd = HALF // LRU_HEADS
    for h in range(LRU_HEADS):
        cs = slice(h * hd, (h + 1) * hd)
        xh = xc[:, cs]
        rg = jax.nn.sigmoid(_bdot(xh, wr_ref[h]) + br_ref[:, cs])
        ig = jax.nn.sigmoid(_bdot(xh, wi_ref[h]) + bi_ref[:, cs])
        log_a = -LRU_C * rg * sp[:, cs]
        a = jnp.exp(log_a)
        a_s[:, cs] = a
        u_s[:, cs] = jnp.sqrt(-jnp.tanh(log_a) * (a * a + 1.0)) * (ig * xh)

    def step(t, h):
        tt = (r - 1 - t) if rev else t
        h = a_s[pl.ds(tt, 1), :] * h + u_s[pl.ds(tt, 1), :]
        h_s[pl.ds(tt, 1), :] = h
        return h

    carry[...] = lax.fori_loop(0, r, step, carry[...])
    if rev:
        o_ref[...] = ((h_s[...] + hf_ref[...]) * jax.nn.gelu(gt_ref[...])).astype(BF16)
    else:
        o_ref[...] = h_s[...]


def _lru(p, conv_w, conv_b, w_r, b_r, w_i, b_i, lam, rev, hf=None):
    r = LRU_R
    d = 1 if rev else 0
    blk = lambda b, s: _scan_block(b, s, rev)
    xcol = 2
    n8 = T_ALL // 8
    vec = lambda: pl.BlockSpec((None, 1, HALF), lambda b, s: (d, 0, 0))
    wspec = lambda: pl.BlockSpec((None, LRU_HEADS, 128, 128), lambda b, s: (d, 0, 0, 0))
    in_specs = [
        pl.BlockSpec((r, HALF), lambda b, s: (blk(b, s), xcol)),
        pl.BlockSpec((8, HALF), lambda b, s: (jnp.maximum(blk(b, s) * (r // 8) - 1, 0), xcol)),
        pl.BlockSpec((8, HALF), lambda b, s: (jnp.minimum((blk(b, s) + 1) * (r // 8), n8 - 1), xcol)),
        pl.BlockSpec((4, HALF), lambda b, s: (0, 0)),
        pl.BlockSpec((1, HALF), lambda b, s: (0, 0)),
        wspec(), vec(), wspec(), vec(), vec(),
    ]
    args = [p, p, p, conv_w, conv_b.reshape(1, HALF), w_r, b_r.reshape(2, 1, HALF), w_i,
            b_i.reshape(2, 1, HALF), lam.reshape(2, 1, HALF)]
    if rev:
        in_specs += [pl.BlockSpec((r, HALF), lambda b, s: (blk(b, s), 0)),
                     pl.BlockSpec((r, HALF), lambda b, s: (blk(b, s), xcol - 1))]
        args += [hf, p]
    return pl.pallas_call(
        functools.partial(_lru_kernel, rev=rev),
        grid=(B, N_SEQ_BLK),
        in_specs=in_specs,
        out_specs=pl.BlockSpec((r, HALF), lambda b, s: (blk(b, s), 0)),
        out_shape=jax.ShapeDtypeStruct((T_ALL, HALF), BF16 if rev else F32),
        scratch_shapes=[pltpu.VMEM((r + 16, HALF), F32), pltpu.VMEM((r, HALF), F32),
                        pltpu.VMEM((r, HALF), F32), pltpu.VMEM((r, HALF), F32),
                        pltpu.VMEM((1, HALF), F32)],
        compiler_params=_cparams(("parallel", "arbitrary"), VMEM_LIMIT),
        name="lru_bwd" if rev else "lru_fwd",
    )(*args)


def kernel(x, c, ctx, c_ctx, mod_w, mod_b, norm_g, ab_w_in, ab_w_out, gmlp_v_g, gmlp_ws, gmlp_bs, cd_w_in, cd_w_out, hgrn_lb, hgrn_onorm_g, lru_conv_w, lru_conv_b, lru_wr, lru_br, lru_wi, lru_bi, lru_lambda, router_w, router_b, exp_w_gu, exp_b_gu, exp_w_down, exp_b_down):
    cond8 = jnp.concatenate([c, c_ctx[None, :], jnp.zeros((5, D), F32)], axis=0)
    modm = _modulation(cond8, mod_w, mod_b).reshape(DEPTH * 8, 1, 6 * D)
    xs = jnp.concatenate([x.reshape(T_LAT, D), ctx.reshape(T_CTX, D)], axis=0)

    lb_soft = jax.nn.softmax(hgrn_lb.astype(F32), axis=0)
    lb_all = jnp.cumsum(lb_soft, axis=0) - lb_soft[0]

    (p0,) = _inproj(xs, norm_g[0, 0], modm, 0, [ab_w_in[0].astype(BF16)], BF16)
    zc, zs = _fnet_chan(p0)
    mix_a = jnp.concatenate([_fnet_pos(zc, zs, 0, L, 512, 1024),
                             _fnet_pos(zc, zs, T_LAT, LC, LC, LC)], axis=0)
    mix_g = _gmlp(p0, gmlp_v_g[0], gmlp_ws[0], gmlp_bs[0])
    xs, h2, e4, p4, r4, counts = _outproj(mix_a, mix_g, ab_w_out[0].astype(BF16), xs, T_ALL, norm_g[0, 1],
                                          norm_g[0, 2], modm, 0, router_w[0], router_b[0])
    xs = _moe(h2, e4, p4, r4, counts, xs, norm_g[0, 3], modm, 0,
              exp_w_gu, exp_b_gu[0], exp_w_down, exp_b_down[0])

    w_cd = cd_w_in[0].astype(BF16)
    p1a, p1b = _inproj(xs, norm_g[1, 0], modm, 1, [w_cd[:, :4 * HALF], w_cd[:, 4 * HALF:]], F32)
    o_f = _hgrn(p1a, p1b, lb_all[1], False)
    hg = _hgrn(p1a, p1b, lb_all[1], True, o_f, hgrn_onorm_g[0])
    lru_args = (lru_conv_w[0], lru_conv_b[0], lru_wr[0], lru_br[0], lru_wi[0], lru_bi[0], lru_lambda[0])
    h_f = _lru(p1b, *lru_args, False)
    lr = _lru(p1b, *lru_args, True, h_f)
    xl, h2, e4, p4, r4, counts = _outproj(hg, lr, cd_w_out[0].astype(BF16), xs, T_LAT, norm_g[1, 1],
                                          norm_g[1, 2], modm, 1, router_w[1], router_b[1])
    xl = _moe(h2, e4, p4, r4, counts, xl, norm_g[1, 3], modm, 1,
              exp_w_gu, exp_b_gu[1], exp_w_down, exp_b_down[1])
    return xl.reshape(B, L, D)
```

```python
import functools

import numpy as np
import jax
import jax.numpy as jnp
from jax import lax
from jax.experimental import pallas as pl
from jax.experimental.pallas import tpu as pltpu

F32 = jnp.float32
BF16 = jnp.bfloat16
HI = lax.Precision.HIGHEST

D = 2048
B = 2
L = 4096
LC = 256
T_LAT = B * L
T_CTX = B * LC
T_ALL = T_LAT + T_CTX
DEPTH = 2
EPS = 1e-6

HALF = D // 2
N_EXPERTS = 32
TOP_K = 4
D_EXPERT = D // 2
SWIGLU_LIMIT = 7.0
SWIGLU_ALPHA = 1.702
MOE_TM = 256

FNET_GROUPS = 4
FNET_GD = HALF // FNET_GROUPS
GMLP_HEADS = 8
GMLP_CHUNK = 128
HGRN_HEADS = 8
HGRN_R = 128
HGRN_C = 64
LRU_HEADS = 8
LRU_R = 128
LRU_C = 8.0

VMEM_LIMIT = 56 * 1024 * 1024


def _cparams(sem, vmem=None):
    return pltpu.CompilerParams(dimension_semantics=sem, vmem_limit_bytes=vmem)


def _mod_row(i, tm):
    return jnp.where(i < L // tm, 0, jnp.where(i < 2 * L // tm, 1, 2))


def _rms(x):
    return x * lax.rsqrt(jnp.mean(x * x, axis=-1, keepdims=True) + EPS)


def _bdot(a, b):
    return jnp.dot(a.astype(BF16), b.astype(BF16), preferred_element_type=F32)


ROW_S = D // 128


def _row_slab(ref, r):
    return ref.at[pl.ds(pl.multiple_of(r * ROW_S, ROW_S), ROW_S), :]


def _store_rows(ref, val, row0=0):
    n = val.shape[0]
    for s in range(ROW_S):
        ref[pl.ds(row0 * ROW_S + s, n, stride=ROW_S), :] = val[:, s * 128:(s + 1) * 128]


def _load_rows(ref, n):
    return jnp.concatenate([ref[pl.ds(s, n, stride=ROW_S), :] for s in range(ROW_S)], axis=1)


def _mod_kernel(c_ref, w_ref, b_ref, o_ref):
    cnd = c_ref[...]
    s = cnd * jax.nn.sigmoid(cnd)
    o_ref[...] = jnp.dot(s, w_ref[...], precision=HI, preferred_element_type=F32) + b_ref[...]


def _modulation(cond8, mod_w, mod_b):
    tn = 1024
    n = mod_w.shape[-1]
    return pl.pallas_call(
        _mod_kernel,
        grid=(DEPTH, n // tn),
        in_specs=[
            pl.BlockSpec((8, D), lambda l, j: (0, 0)),
            pl.BlockSpec((None, D, tn), lambda l, j: (l, 0, j)),
            pl.BlockSpec((None, 1, tn), lambda l, j: (l, 0, j)),
        ],
        out_specs=pl.BlockSpec((None, 8, tn), lambda l, j: (l, 0, j)),
        out_shape=jax.ShapeDtypeStruct((DEPTH, 8, n), F32),
        compiler_params=_cparams(("parallel", "parallel"), VMEM_LIMIT),
        name="modulation",
    )(cond8, mod_w, mod_b.reshape(DEPTH, 1, n))


def _inproj_kernel(*refs, nw):
    x_ref, g_ref, sh_ref, sc_ref = refs[:4]
    w_refs, o_refs = refs[4:4 + nw], refs[4 + nw:]
    y = _rms(x_ref[...]) * g_ref[...]
    h = (y * (1.0 + sc_ref[...]) + sh_ref[...]).astype(BF16)
    for w_ref, o_ref in zip(w_refs, o_refs):
        o_ref[...] = jnp.dot(h, w_ref[...], preferred_element_type=F32).astype(o_ref.dtype)


def _inproj(xs, g, modm, layer, ws_bf, out_dtype):
    tm = 256
    t = xs.shape[0]
    row = lambda i: layer * 8 + _mod_row(i, tm)
    return pl.pallas_call(
        functools.partial(_inproj_kernel, nw=len(ws_bf)),
        grid=(t // tm,),
        in_specs=[
            pl.BlockSpec((tm, D), lambda i: (i, 0)),
            pl.BlockSpec((1, D), lambda i: (0, 0)),
            pl.BlockSpec((None, 1, D), lambda i: (row(i), 0, 0)),
            pl.BlockSpec((None, 1, D), lambda i: (row(i), 0, 1)),
        ] + [pl.BlockSpec(w.shape, lambda i: (0, 0), pipeline_mode=pl.Buffered(1)) for w in ws_bf],
        out_specs=[pl.BlockSpec((tm, w.shape[1]), lambda i: (i, 0)) for w in ws_bf],
        out_shape=[jax.ShapeDtypeStruct((t, w.shape[1]), out_dtype) for w in ws_bf],
        compiler_params=_cparams(("parallel",), VMEM_LIMIT),
        name="inproj",
    )(xs, g.reshape(1, D), modm, modm, *ws_bf)


def _fnet_chan_kernel(p_ref, cs_ref, zc_ref, zs_ref):
    for g in range(FNET_GROUPS):
        sl = slice(g * FNET_GD, (g + 1) * FNET_GD)
        z = jnp.dot(p_ref[:, sl].astype(BF16), cs_ref[...], preferred_element_type=F32)
        zc_ref[:, sl] = z[:, :FNET_GD].astype(BF16)
        zs_ref[:, sl] = z[:, FNET_GD:].astype(BF16)


def _fnet_chan(p):
    tm = 512
    t = p.shape[0]
    k = np.arange(FNET_GD)
    ang = 2.0 * np.pi * ((k[:, None] * k[None, :]) % FNET_GD) / FNET_GD
    cs = np.concatenate([np.cos(ang), np.sin(ang)], axis=1) / np.sqrt(FNET_GD)
    cs = jnp.asarray(cs, F32).astype(BF16)
    return pl.pallas_call(
        _fnet_chan_kernel,
        grid=(t // tm,),
        in_specs=[
            pl.BlockSpec((tm, HALF), lambda i: (i, 0)),
            pl.BlockSpec((FNET_GD, 2 * FNET_GD), lambda i: (0, 0)),
        ],
        out_specs=[pl.BlockSpec((tm, HALF), lambda i: (i, 0))] * 2,
        out_shape=[jax.ShapeDtypeStruct((t, HALF), BF16)] * 2,
        compiler_params=_cparams(("parallel",), VMEM_LIMIT),
        name="fnet_chan",
    )(p, cs)


def _fnet_pos_kernel(bre_ref, bim_ref, cre_ref, cim_ref, rre_ref, rim_ref, zc_ref, zs_ref, o_ref, acc):
    nt = pl.program_id(2)
    br, bi = bre_ref[...], bim_ref[...]
    cr, ci = cre_ref[...], cim_ref[...]
    rr, ri = rre_ref[...], rim_ref[...]
    tr = br * cr - bi * ci
    ti = br * ci + bi * cr
    er = (tr * rr - ti * ri).astype(BF16)
    ei = (tr * ri + ti * rr).astype(BF16)
    part = (jnp.dot(er, zc_ref[...], preferred_element_type=F32)
            + jnp.dot(ei, zs_ref[...], preferred_element_type=F32))

    @pl.when(nt == 0)
    def _():
        acc[...] = part

    @pl.when(nt > 0)
    def _():
        acc[...] += part

    @pl.when(nt == pl.num_programs(2) - 1)
    def _():
        o_ref[...] = acc[...].astype(BF16)


def _fnet_pos(zc, zs, row0, seq, tk, tn):
    nkt, nnt = seq // tk, seq // tn
    th = 2.0 * np.pi / seq
    kk = np.arange(tk)[:, None]
    nn = np.arange(tn)[None, :]
    base = th * ((kk * nn) % seq)
    n0 = (np.arange(nnt) * tn)[:, None, None]
    col = th * ((np.arange(tk)[None, :, None] * n0) % seq)
    k0 = (np.arange(nkt) * tk)[:, None, None, None]
    nfull = (np.arange(nnt) * tn)[None, :, None, None] + np.arange(tn)[None, None, None, :]
    row = th * ((k0 * nfull) % seq)
    scale = 1.0 / np.sqrt(seq)
    f = lambda a: jnp.asarray(a, F32)
    tabs = (f(np.cos(base)), f(-np.sin(base)), f(np.cos(col)), f(-np.sin(col)),
            f(np.cos(row) * scale), f(-np.sin(row) * scale))
    rb = row0 // tn
    ob = row0 // tk
    return pl.pallas_call(
        _fnet_pos_kernel,
        grid=(B, nkt, nnt),
        in_specs=[
            pl.BlockSpec((tk, tn), lambda b, k, n: (0, 0)),
            pl.BlockSpec((tk, tn), lambda b, k, n: (0, 0)),
            pl.BlockSpec((None, tk, 1), lambda b, k, n: (n, 0, 0)),
            pl.BlockSpec((None, tk, 1), lambda b, k, n: (n, 0, 0)),
            pl.BlockSpec((None, None, 1, tn), lambda b, k, n: (k, n, 0, 0)),
            pl.BlockSpec((None, None, 1, tn), lambda b, k, n: (k, n, 0, 0)),
            pl.BlockSpec((tn, HALF), lambda b, k, n: (rb + b * nnt + n, 0)),
            pl.BlockSpec((tn, HALF), lambda b, k, n: (rb + b * nnt + n, 0)),
        ],
        out_specs=pl.BlockSpec((tk, HALF), lambda b, k, n: (b * nkt + k, 0)),
        out_shape=jax.ShapeDtypeStruct((B * seq, HALF), BF16),
        scratch_shapes=[pltpu.VMEM((tk, HALF), F32)],
        compiler_params=_cparams(("parallel", "parallel", "arbitrary"), VMEM_LIMIT),
        name=f"fnet_pos_{seq}",
    )(*tabs, zc, zs)


def _gmlp_kernel(u_ref, v_ref, vg_ref, ws_ref, bst_ref, o_ref, *, tm):
    zu = jax.nn.gelu(u_ref[...].astype(F32))
    zv = jax.nn.gelu(v_ref[...].astype(F32))
    mu = jnp.mean(zv, axis=-1, keepdims=True)
    dv = zv - mu
    var = jnp.mean(dv * dv, axis=-1, keepdims=True)
    vn = dv * lax.rsqrt(var + EPS) * vg_ref[...]
    hd = HALF // GMLP_HEADS
    for c in range(tm // GMLP_CHUNK):
        rs = slice(c * GMLP_CHUNK, (c + 1) * GMLP_CHUNK)
        for h in range(GMLP_HEADS):
            cs = slice(h * hd, (h + 1) * hd)
            s = jnp.dot(ws_ref[h], vn[rs, cs], precision=HI, preferred_element_type=F32)
            s = s + bst_ref[:, h:h + 1]
            o_ref[rs, cs] = (zu[rs, cs] * s).astype(BF16)


def _gmlp(p, v_g, w_s, b_s):
    tm = 256
    t = p.shape[0]
    return pl.pallas_call(
        functools.partial(_gmlp_kernel, tm=tm),
        grid=(t // tm,),
        in_specs=[
            pl.BlockSpec((tm, HALF), lambda i: (i, 1)),
            pl.BlockSpec((tm, HALF), lambda i: (i, 2)),
            pl.BlockSpec((1, HALF), lambda i: (0, 0)),
            pl.BlockSpec((GMLP_HEADS, GMLP_CHUNK, GMLP_CHUNK), lambda i: (0, 0, 0)),
            pl.BlockSpec((GMLP_CHUNK, GMLP_HEADS), lambda i: (0, 0)),
        ],
        out_specs=pl.BlockSpec((tm, HALF), lambda i: (i, 0)),
        out_shape=jax.ShapeDtypeStruct((t, HALF), BF16),
        compiler_params=_cparams(("parallel",), VMEM_LIMIT),
        name="gmlp",
    )(p, p, v_g.reshape(1, HALF), w_s, b_s.T)


def _outproj_kernel(ma_ref, mb_ref, w_ref, x_ref, g1_ref, g2_ref, ga1_ref, sh2_ref, sc2_ref,
                    rw_ref, rb_ref, xo_ref, h2_ref, e_ref, p_ref, rk_ref, cnt_ref, carry, rw2, *, tm):
    @pl.when(pl.program_id(0) == 0)
    def _():
        carry[...] = jnp.zeros_like(carry)
        rw = rw_ref[...]
        rw_hi = rw.astype(BF16)
        rw2[:, :N_EXPERTS] = rw_hi
        rw2[:, N_EXPERTS:] = (rw - rw_hi.astype(F32)).astype(BF16)

    sub = 128
    iota = lax.broadcasted_iota(jnp.int32, (sub, N_EXPERTS), 1).astype(F32)
    lane4 = lax.broadcasted_iota(jnp.int32, (sub, TOP_K), 1)
    r_i = lax.broadcasted_iota(jnp.int32, (sub, sub), 0)
    c_i = lax.broadcasted_iota(jnp.int32, (sub, sub), 1)
    tri = (c_i < r_i).astype(BF16)
    running = carry[...]
    for r0 in range(0, tm, sub):
        rs = slice(r0, r0 + sub)
        m = (jnp.dot(ma_ref[rs, :], w_ref[:HALF, :], preferred_element_type=F32)
             + jnp.dot(mb_ref[rs, :], w_ref[HALF:, :], preferred_element_type=F32))
        xn = x_ref[rs, :] + ga1_ref[...] * (_rms(m) * g1_ref[...])
        xo_ref[rs, :] = xn
        h2 = (_rms(xn) * g2_ref[...]) * (1.0 + sc2_ref[...]) + sh2_ref[...]
        _store_rows(h2_ref, h2, r0)

        h_hi = h2.astype(BF16)
        h_lo = (h2 - h_hi.astype(F32)).astype(BF16)
        t1 = jnp.dot(h_hi, rw2[...], preferred_element_type=F32)
        t2 = jnp.dot(h_lo, rw2[:, :N_EXPERTS], preferred_element_type=F32)
        lg = t1[:, :N_EXPERTS] + t1[:, N_EXPERTS:] + t2 + rb_ref[...]
        idxs, vals = [], []
        for _ in range(TOP_K):
            mx = jnp.max(lg, axis=-1, keepdims=True)
            ix = jnp.min(jnp.where(lg == mx, iota, float(N_EXPERTS)), axis=-1, keepdims=True)
            idxs.append(ix)
            vals.append(mx)
            lg = jnp.where(iota == ix, -jnp.inf, lg)
        exs = [jnp.exp(v - vals[0]) for v in vals]
        den = exs[0] + exs[1] + exs[2] + exs[3]

        onehot = [(iota == ix) for ix in idxs]
        cnt = jnp.zeros((sub, N_EXPERTS), F32)
        for oh in onehot:
            cnt = cnt + oh.astype(F32)
        prefix = jnp.dot(tri, cnt.astype(BF16), preferred_element_type=F32) + running

        e4 = jnp.zeros((sub, TOP_K), jnp.int32)
        p4 = jnp.zeros((sub, TOP_K), F32)
        r4 = jnp.zeros((sub, TOP_K), jnp.int32)
        for k in range(TOP_K):
            rk = jnp.sum(jnp.where(onehot[k], prefix, 0.0), axis=-1, keepdims=True)
            e4 = jnp.where(lane4 == k, idxs[k].astype(jnp.int32), e4)
            p4 = jnp.where(lane4 == k, exs[k] / den, p4)
            r4 = jnp.where(lane4 == k, rk.astype(jnp.int32), r4)
        e_ref[rs, :] = e4
        p_ref[rs, :] = p4
        rk_ref[rs, :] = r4
        running = running + jnp.sum(cnt, axis=0, keepdims=True)
    carry[...] = running
    cnt_ref[...] = running


def _outproj(ma, mb, w_bf, xs, t, g1, g2, modm, layer, rw, rb):
    tm = 256
    row = lambda i: layer * 8 + _mod_row(i, tm)
    modspec = lambda c: pl.BlockSpec((None, 1, D), lambda i: (row(i), 0, c))
    vec = pl.BlockSpec((1, D), lambda i: (0, 0))
    small = lambda dt: jax.ShapeDtypeStruct((t, TOP_K), dt)
    return pl.pallas_call(
        functools.partial(_outproj_kernel, tm=tm),
        grid=(t // tm,),
        in_specs=[
            pl.BlockSpec((tm, HALF), lambda i: (i, 0)),
            pl.BlockSpec((tm, HALF), lambda i: (i, 0)),
            pl.BlockSpec((D, D), lambda i: (0, 0)),
            pl.BlockSpec((tm, D), lambda i: (i, 0)),
            vec, vec, modspec(2), modspec(3), modspec(4),
            pl.BlockSpec((D, N_EXPERTS), lambda i: (0, 0)),
            pl.BlockSpec((1, N_EXPERTS), lambda i: (0, 0)),
        ],
        out_specs=[
            pl.BlockSpec((tm, D), lambda i: (i, 0)),
            pl.BlockSpec((tm * ROW_S, 128), lambda i: (i, 0)),
            pl.BlockSpec((tm, TOP_K), lambda i: (i, 0)),
            pl.BlockSpec((tm, TOP_K), lambda i: (i, 0)),
            pl.BlockSpec((tm, TOP_K), lambda i: (i, 0)),
            pl.BlockSpec((1, N_EXPERTS), lambda i: (0, 0)),
        ],
        out_shape=[
            jax.ShapeDtypeStruct((t, D), F32),
            jax.ShapeDtypeStruct((t * ROW_S, 128), F32),
            small(jnp.int32), small(F32), small(jnp.int32),
            jax.ShapeDtypeStruct((1, N_EXPERTS), F32),
        ],
        scratch_shapes=[pltpu.VMEM((1, N_EXPERTS), F32), pltpu.VMEM((D, 2 * N_EXPERTS), BF16)],
        compiler_params=_cparams(("arbitrary",), VMEM_LIMIT),
        name="outproj_router",
    )(ma, mb, w_bf, xs, g1.reshape(1, D), g2.reshape(1, D), modm, modm, modm,
      rw, rb.reshape(1, N_EXPERTS))


def _invmap_kernel(pstart_ref, cnt_ref, padded_ref, dest_ref, inv_ref, *, n_pairs, n_slots):
    def unused(lo, hi):
        def body(s, c):
            inv_ref[s] = n_pairs + lax.rem(s, MOE_TM)
            return c
        lax.fori_loop(lo, hi, body, 0)

    def per_expert(e, c):
        unused(pstart_ref[e] + cnt_ref[e], pstart_ref[e] + padded_ref[e])
        return c

    lax.fori_loop(0, N_EXPERTS, per_expert, 0)
    unused(pstart_ref[N_EXPERTS - 1] + padded_ref[N_EXPERTS - 1], n_slots)

    def body(i, c):
        inv_ref[dest_ref[i]] = i
        return c

    lax.fori_loop(0, n_pairs, body, 0, unroll=8)


def _invmap(dest, pstart, cnt, padded, n_slots):
    n_pairs = dest.shape[0]
    return pl.pallas_call(
        functools.partial(_invmap_kernel, n_pairs=n_pairs, n_slots=n_slots),
        grid_spec=pltpu.PrefetchScalarGridSpec(
            num_scalar_prefetch=3,
            grid=(1,),
            in_specs=[pl.BlockSpec(memory_space=pltpu.SMEM)],
            out_specs=pl.BlockSpec(memory_space=pltpu.SMEM),
        ),
        out_shape=jax.ShapeDtypeStruct((n_slots,), jnp.int32),
        compiler_params=_cparams(("arbitrary",)),
        name="moe_invmap",
    )(pstart, cnt, padded, dest)


W_CH = 256
W_NCH_GU = D // W_CH
W_NCH = W_NCH_GU + D_EXPERT // W_CH
W_STAGE = 4


def _expert_kernel(be_ref, nused_ref, first_ref, nxt_ref, lo_ref, hi_ref, slot_ref,
                   inv_cur, inv_next, inv_prev, bgu_ref, bd_ref, h_hbm, wgu_all, wd_all, y_hbm,
                   xbuf, obuf, x_s, act_s, wgu_buf, wd_buf, stage, gsem, ssem, sem, *, layer, n_tok):
    i = pl.program_id(0)
    n_live = nused_ref[0]
    par = lax.rem(i, 2)
    wgu_hbm = wgu_all.at[layer]
    wd_hbm = wd_all.at[layer]

    def gather_copy(inv_ref, r, slot):
        tok = jnp.minimum(lax.shift_right_logical(inv_ref[r], 2), n_tok - 1)
        return pltpu.make_async_copy(_row_slab(h_hbm, tok), _row_slab(xbuf.at[slot], r), gsem.at[slot])

    def scatter_copy(inv_ref, r, slot):
        return pltpu.make_async_copy(_row_slab(obuf.at[slot], r), _row_slab(y_hbm, inv_ref[r]), ssem)

    def looped(fn):
        def body(r, c):
            fn(r)
            return c
        lax.fori_loop(0, MOE_TM, body, 0)

    def start_chunk(e, c):
        s = lax.rem(c, W_STAGE)

        @pl.when(c < W_NCH_GU)
        def _():
            r0 = pl.multiple_of(c * W_CH, W_CH)
            pltpu.make_async_copy(wgu_hbm.at[e, pl.ds(r0, W_CH), :], stage.at[s], sem.at[s]).start()

        @pl.when(c >= W_NCH_GU)
        def _():
            r0 = pl.multiple_of((c - W_NCH_GU) * W_CH, W_CH)
            pltpu.make_async_copy(wd_hbm.at[e, pl.ds(r0, W_CH), :], stage.at[s], sem.at[s]).start()

    def finish_chunk(c, dst):
        s = lax.rem(c, W_STAGE)
        pltpu.make_async_copy(wgu_hbm.at[0, pl.ds(0, W_CH), :], stage.at[s], sem.at[s]).wait()
        w = stage[s].astype(BF16)

        @pl.when(c < W_NCH_GU)
        def _():
            wgu_buf[dst, pl.ds(pl.multiple_of(c * W_CH, W_CH), W_CH), :] = w

        @pl.when(c >= W_NCH_GU)
        def _():
            wd_buf[dst, pl.ds(pl.multiple_of((c - W_NCH_GU) * W_CH, W_CH), W_CH), :] = w

    def stream(e, dst, c_lo, c_hi):
        def body(c, carry):
            finish_chunk(c, dst)

            @pl.when(c + W_STAGE < W_NCH)
            def _():
                start_chunk(e, c + W_STAGE)
            return carry
        lax.fori_loop(c_lo, c_hi, body, 0)

    def prime(e):
        for c in range(W_STAGE):
            start_chunk(e, c)

    @pl.when(i == 0)
    def _():
        looped(lambda r: gather_copy(inv_cur, r, 0).start())
        obuf[1] = jnp.zeros(obuf.shape[1:], F32)
        dump = y_hbm.at[pl.ds(n_tok * TOP_K * ROW_S, MOE_TM * ROW_S), :]
        zero_dump = pltpu.make_async_copy(obuf.at[1], dump, ssem)
        zero_dump.start()
        zero_dump.wait()
        prime(be_ref[0])
        stream(be_ref[0], slot_ref[0], 0, W_NCH)

    @pl.when(i < n_live)
    def _():
        e_next = nxt_ref[i]
        cur = slot_ref[i]

        @pl.when((first_ref[i] == 1) & (e_next >= 0))
        def _():
            prime(e_next)

        for r in range(MOE_TM):
            gather_copy(inv_cur, r, par).wait()
        for s in range(ROW_S):
            x_s[:, s * 128:(s + 1) * 128] = xbuf[par, pl.ds(s, MOE_TM, stride=ROW_S), :].astype(BF16)
        nc = 256

        def down_proj(c_lo, c_hi):
            for c0 in range(c_lo, c_hi, nc):
                y = (jnp.dot(act_s[...], wd_buf[cur, :, c0:c0 + nc], preferred_element_type=F32)
                     + bd_ref[:, c0:c0 + nc])
                for j in range(nc // 128):
                    s = c0 // 128 + j
                    obuf[par, pl.ds(s, MOE_TM, stride=ROW_S), :] = y[:, j * 128:(j + 1) * 128]

        for r in range(MOE_TM):
            gather_copy(inv_next, r, 1 - par).start()
            scatter_copy(inv_prev, r, 1 - par).start()
        for c0 in range(0, D_EXPERT, nc):
            g = (jnp.dot(x_s[...], wgu_buf[cur, :, c0:c0 + nc], preferred_element_type=F32)
                 + bgu_ref[:, c0:c0 + nc])
            u = (jnp.dot(x_s[...], wgu_buf[cur, :, D_EXPERT + c0:D_EXPERT + c0 + nc],
                         preferred_element_type=F32) + bgu_ref[:, D_EXPERT + c0:D_EXPERT + c0 + nc])
            gate = jnp.minimum(g, SWIGLU_LIMIT)
            up = jnp.clip(u, -SWIGLU_LIMIT, SWIGLU_LIMIT)
            act_s[:, c0:c0 + nc] = (gate * jax.nn.sigmoid(SWIGLU_ALPHA * gate) * (up + 1.0)).astype(BF16)

        down_proj(0, D)
        for r in range(MOE_TM):
            scatter_copy(inv_prev, r, 1 - par).wait()

        @pl.when(e_next >= 0)
        def _():
            stream(e_next, 1 - cur, lo_ref[i], hi_ref[i])

        @pl.when(i == n_live - 1)
        def _():
            looped(lambda r: gather_copy(inv_next, r, 1 - par).wait())
            looped(lambda r: scatter_copy(inv_cur, r, par).start())
            looped(lambda r: scatter_copy(inv_cur, r, par).wait())


def _take(table, idx):
    ids = jnp.arange(table.shape[0], dtype=jnp.int32)
    return jnp.sum(jnp.where(idx[..., None] == ids, table, 0), axis=-1)


def _experts(h2, inv, block_e, nused, pstart, padded, layer, wgu, bgu, wd, bd, nblk, n_tok):
    nb_e = padded // MOE_TM
    pos = jnp.arange(nblk, dtype=jnp.int32) - _take(pstart // MOE_TM, block_e)
    nb = jnp.maximum(_take(nb_e, block_e), 1)
    lo = (pos * W_NCH) // nb
    hi = ((pos + 1) * W_NCH) // nb
    first = (pos == 0).astype(jnp.int32)
    eid = jnp.arange(N_EXPERTS, dtype=jnp.int32)
    later = jnp.where(nb_e > 0, eid, N_EXPERTS)
    nxt_e = jnp.concatenate([lax.cummin(later, reverse=True)[1:], jnp.full((1,), N_EXPERTS, jnp.int32)])
    nxt_e = jnp.where(nxt_e >= N_EXPERTS, -1, nxt_e)
    slot_e = (jnp.cumsum((nb_e > 0).astype(jnp.int32)) - 1) % 2
    i32 = lambda a: a.astype(jnp.int32)
    live = lambda i, nu: jnp.clip(i, 0, nu[0] - 1)
    sblk = lambda off: pl.BlockSpec((MOE_TM,), lambda i, be, nu, *_: (live(i + off, nu),),
                                    memory_space=pltpu.SMEM)
    bias = lambda n: pl.BlockSpec((None, 1, n), lambda i, be, nu, *_: (be[live(i, nu)], 0, 0))
    hbm = pl.BlockSpec(memory_space=pl.ANY)
    n_slabs = n_tok * TOP_K + MOE_TM
    return pl.pallas_call(
        functools.partial(_expert_kernel, layer=layer, n_tok=n_tok),
        grid_spec=pltpu.PrefetchScalarGridSpec(
            num_scalar_prefetch=7,
            grid=(nblk,),
            in_specs=[sblk(0), sblk(1), sblk(-1), bias(2 * D_EXPERT), bias(D), hbm, hbm, hbm],
            out_specs=hbm,
            scratch_shapes=[
                pltpu.VMEM((2, MOE_TM * ROW_S, 128), F32),
                pltpu.VMEM((2, MOE_TM * ROW_S, 128), F32),
                pltpu.VMEM((MOE_TM, D), BF16),
                pltpu.VMEM((MOE_TM, D_EXPERT), BF16),
                pltpu.VMEM((2, D, 2 * D_EXPERT), BF16),
                pltpu.VMEM((2, D_EXPERT, D), BF16),
                pltpu.VMEM((W_STAGE, W_CH, D), F32),
                pltpu.SemaphoreType.DMA((2,)),
                pltpu.SemaphoreType.DMA(()),
                pltpu.SemaphoreType.DMA((W_STAGE,)),
            ],
        ),
        out_shape=jax.ShapeDtypeStruct((n_slabs * ROW_S, 128), F32),
        compiler_params=_cparams(("arbitrary",), VMEM_LIMIT),
        name="moe_experts",
    )(block_e, nused, i32(first), i32(_take(nxt_e, block_e)), i32(lo), i32(hi), i32(_take(slot_e, block_e)),
      inv, inv, inv, bgu.reshape(N_EXPERTS, 1, -1), bd.reshape(N_EXPERTS, 1, -1), h2, wgu, wd)


def _combine_kernel(p_ref, x_ref, g3_ref, ga2_ref, y_ref, xo_ref, *, tm):
    p = p_ref[...]
    pair = TOP_K * ROW_S
    cols = []
    for s in range(ROW_S):
        acc = y_ref[pl.ds(s, tm, stride=pair), :] * p[:, 0:1]
        for k in range(1, TOP_K):
            acc = acc + y_ref[pl.ds(k * ROW_S + s, tm, stride=pair), :] * p[:, k:k + 1]
        cols.append(acc)
    y = jnp.concatenate(cols, axis=1)
    xo_ref[...] = x_ref[...] + ga2_ref[...] * (_rms(y) * g3_ref[...])


def _combine(y4, p4, xs, g3, modm, layer):
    tm = 128
    t = xs.shape[0]
    row = lambda i: layer * 8 + _mod_row(i, tm)
    return pl.pallas_call(
        functools.partial(_combine_kernel, tm=tm),
        grid=(t // tm,),
        in_specs=[
            pl.BlockSpec((tm, TOP_K), lambda i: (i, 0)),
            pl.BlockSpec((tm, D), lambda i: (i, 0)),
            pl.BlockSpec((1, D), lambda i: (0, 0)),
            pl.BlockSpec((None, 1, D), lambda i: (row(i), 0, 5)),
            pl.BlockSpec((tm * TOP_K * ROW_S, 128), lambda i: (i, 0)),
        ],
        out_specs=pl.BlockSpec((tm, D), lambda i: (i, 0)),
        out_shape=jax.ShapeDtypeStruct((t, D), F32),
        compiler_params=_cparams(("parallel",), VMEM_LIMIT),
        name="moe_combine",
    )(p4, xs, g3.reshape(1, D), modm, y4)


def _moe(h2, e4, p4, r4, counts, xs, g3, modm, layer, wgu, bgu, wd, bd):
    t = e4.shape[0]
    nblk = -(-(t * TOP_K + N_EXPERTS * (MOE_TM - 1)) // MOE_TM)
    cnt = counts.reshape(N_EXPERTS).astype(jnp.int32)
    padded = (cnt + MOE_TM - 1) // MOE_TM * MOE_TM
    pend = jnp.cumsum(padded)
    pstart = pend - padded
    dest = (_take(pstart, e4) + r4).reshape(-1).astype(jnp.int32)
    nused = (pend[-1:] // MOE_TM).astype(jnp.int32)
    blk_start = jnp.arange(nblk, dtype=jnp.int32) * MOE_TM
    block_e = jnp.minimum(jnp.sum((pend[None, :] <= blk_start[:, None]).astype(jnp.int32), axis=1),
                          N_EXPERTS - 1).astype(jnp.int32)
    inv = _invmap(dest, pstart.astype(jnp.int32), cnt, padded.astype(jnp.int32), nblk * MOE_TM)
    y4 = _experts(h2, inv, block_e, nused, pstart, padded, layer, wgu, bgu, wd, bd, nblk, t)
    return _combine(y4, p4, xs, g3, modm, layer)


N_CTX_BLK = LC // HGRN_R
N_LAT_BLK = L // HGRN_R
N_SEQ_BLK = N_CTX_BLK + N_LAT_BLK


def _scan_block(b, s, rev):
    if rev:
        return jnp.where(s < N_CTX_BLK, T_LAT // HGRN_R + N_CTX_BLK * b + (N_CTX_BLK - 1 - s),
                         N_LAT_BLK * b + (N_SEQ_BLK - 1 - s))
    return jnp.where(s < N_CTX_BLK, T_LAT // HGRN_R + N_CTX_BLK * b + s, N_LAT_BLK * b + (s - N_CTX_BLK))


def _hgrn_kernel(*refs, rev):
    if rev:
        q_ref, f_ref, v_ref, lb_ref, of_ref, g_ref, on_ref, o_ref, st, qin_s, qd_s, ki_s, ke_s, v_s = refs
    else:
        q_ref, f_ref, v_ref, lb_ref, o_ref, st, qin_s, qd_s, ki_s, ke_s, v_s = refs

    @pl.when(pl.program_id(1) == 0)
    def _():
        st[...] = jnp.zeros_like(st)

    c = HGRN_C
    nc = HGRN_R // c
    r_i = lax.broadcasted_iota(jnp.int32, (c, c), 0)
    c_i = lax.broadcasted_iota(jnp.int32, (c, c), 1)
    mask = (c_i >= r_i) if rev else (c_i <= r_i)
    tri = mask.astype(F32)

    lb = lb_ref[...]
    decay = [None] * nc
    v_s[...] = v_ref[...].astype(BF16)
    for ci in range(nc):
        rs = slice(ci * c, (ci + 1) * c)
        sg = jax.nn.sigmoid(f_ref[rs, :])
        logf = jnp.log(lb + (1.0 - lb) * sg)
        kk = (1.0 - lb) * (1.0 - sg)
        q = q_ref[rs, :]
        qs = q * jax.nn.sigmoid(q)
        cum = jnp.dot(tri, logf, precision=HI, preferred_element_type=F32)
        total = cum[0:1] if rev else cum[c - 1:c]
        mid = cum[c // 2:c // 2 + 1]
        qin_s[rs, :] = (qs * jnp.exp(cum)).astype(BF16)
        qd_s[rs, :] = (qs * jnp.exp(cum - mid)).astype(BF16)
        ki_s[rs, :] = (kk * jnp.exp(mid - cum)).astype(BF16)
        ke_s[rs, :] = (kk * jnp.exp(total - cum)).astype(BF16)
        decay[ci] = jnp.exp(total)

    order = range(nc - 1, -1, -1) if rev else range(nc)
    for ci in order:
        rs = slice(ci * c, (ci + 1) * c)
        for h in range(HGRN_HEADS):
            hs = slice(h * 128, (h + 1) * 128)
            state = st[h]
            vb = v_s[rs, hs]
            sc = lax.dot_general(qd_s[rs, hs], ki_s[rs, hs], (((1,), (1,)), ((), ())),
                                 preferred_element_type=F32)
            sc = jnp.where(mask, sc, 0.0)
            intra = jnp.dot(sc.astype(BF16), vb, preferred_element_type=F32)
            inter = lax.dot_general(qin_s[rs, hs], state.astype(BF16), (((1,), (1,)), ((), ())),
                                    preferred_element_type=F32)
            st[h] = state * decay[ci][:, hs] + lax.dot_general(
                vb, ke_s[rs, hs], (((0,), (0,)), ((), ())), preferred_element_type=F32)
            o = intra + inter
            if rev:
                o = o + of_ref[rs, hs]
                g = g_ref[rs, hs]
                o_ref[rs, hs] = (_rms(o) * on_ref[:, hs] * (g * jax.nn.sigmoid(g))).astype(BF16)
            else:
                o_ref[rs, hs] = o


def _hgrn(pa, pb, lb, rev, of=None, onorm=None):
    r = HGRN_R
    blk = lambda b, s: _scan_block(b, s, rev)
    col = lambda c: pl.BlockSpec((r, HALF), lambda b, s: (blk(b, s), c))
    vec = pl.BlockSpec((1, HALF), lambda b, s: (0, 0))
    in_specs = [col(0), col(2 if rev else 1), col(3), vec]
    args = [pa, pa, pa, lb.reshape(1, HALF)]
    if rev:
        in_specs += [col(0), col(0), vec]
        args += [of, pb, onorm.reshape(1, HALF)]
    return pl.pallas_call(
        functools.partial(_hgrn_kernel, rev=rev),
        grid=(B, N_SEQ_BLK),
        in_specs=in_specs,
        out_specs=col(0),
        out_shape=jax.ShapeDtypeStruct((T_ALL, HALF), BF16 if rev else F32),
        scratch_shapes=[pltpu.VMEM((HGRN_HEADS, 128, 128), F32)] + [pltpu.VMEM((r, HALF), BF16)] * 5,
        compiler_params=_cparams(("parallel", "arbitrary"), VMEM_LIMIT),
        name="hgrn_bwd" if rev else "hgrn_fwd",
    )(*args)


def _lru_kernel(*refs, rev):
    if rev:
        (x_ref, xp_ref, xn_ref, cw_ref, cb_ref, wr_ref, br_ref, wi_ref, bi_ref, lam_ref,
         hf_ref, gt_ref, o_ref, xcat, a_s, u_s, h_s, carry) = refs
    else:
        (x_ref, xp_ref, xn_ref, cw_ref, cb_ref, wr_ref, br_ref, wi_ref, bi_ref, lam_ref,
         o_ref, xcat, a_s, u_s, h_s, carry) = refs
    s = pl.program_id(1)
    r = LRU_R

    @pl.when(s == 0)
    def _():
        carry[...] = jnp.zeros_like(carry)

    is_ctx = s < N_CTX_BLK
    if rev:
        j = jnp.where(is_ctx, N_CTX_BLK - 1 - s, N_SEQ_BLK - 1 - s)
    else:
        j = jnp.where(is_ctx, s, s - N_CTX_BLK)
    nb = jnp.where(is_ctx, N_CTX_BLK, N_LAT_BLK)
    xcat[0:8, :] = jnp.where(j == 0, 0.0, xp_ref[...])
    xcat[8:8 + r, :] = x_ref[...]
    xcat[8 + r:16 + r, :] = jnp.where(j == nb - 1, 0.0, xn_ref[...])
    xc = cb_ref[...] + xcat[pl.ds(6, r), :] * cw_ref[0:1, :]
    for t in range(1, 4):
        xc = xc + xcat[pl.ds(6 + t, r), :] * cw_ref[t:t + 1, :]

    lam = lam_ref[...]
    sp = jnp.maximum(-lam, 0.0) + jnp.log1p(jnp.exp(-jnp.abs(lam)))
    hd = HALF // LRU_HEADS
    for h in range(LRU_HEADS):
        cs = slice(h * hd, (h + 1) * hd)
        xh = xc[:, cs]
        rg = jax.nn.sigmoid(_bdot(xh, wr_ref[h]) + br_ref[:, cs])
        ig = jax.nn.sigmoid(_bdot(xh, wi_ref[h]) + bi_ref[:, cs])
        log_a = -LRU_C * rg * sp[:, cs]
        a = jnp.exp(log_a)
        a_s[:, cs] = a
        u_s[:, cs] = jnp.sqrt(-jnp.tanh(log_a) * (a * a + 1.0)) * (ig * xh)

    def step(t, h):
        tt = (r - 1 - t) if rev else t
        h = a_s[pl.ds(tt, 1), :] * h + u_s[pl.ds(tt, 1), :]
        h_s[pl.ds(tt, 1), :] = h
        return h

    carry[...] = lax.fori_loop(0, r, step, carry[...], unroll=8)
    if rev:
        o_ref[...] = ((h_s[...] + hf_ref[...]) * jax.nn.gelu(gt_ref[...])).astype(BF16)
    else:
        o_ref[...] = h_s[...]


def _lru(p, conv_w, conv_b, w_r, b_r, w_i, b_i, lam, rev, hf=None):
    r = LRU_R
    d = 1 if rev else 0
    blk = lambda b, s: _scan_block(b, s, rev)
    xcol = 2
    n8 = T_ALL // 8
    vec = lambda: pl.BlockSpec((None, 1, HALF), lambda b, s: (d, 0, 0))
    wspec = lambda: pl.BlockSpec((None, LRU_HEADS, 128, 128), lambda b, s: (d, 0, 0, 0))
    in_specs = [
        pl.BlockSpec((r, HALF), lambda b, s: (blk(b, s), xcol)),
        pl.BlockSpec((8, HALF), lambda b, s: (jnp.maximum(blk(b, s) * (r // 8) - 1, 0), xcol)),
        pl.BlockSpec((8, HALF), lambda b, s: (jnp.minimum((blk(b, s) + 1) * (r // 8), n8 - 1), xcol)),
        pl.BlockSpec((4, HALF), lambda b, s: (0, 0)),
        pl.BlockSpec((1, HALF), lambda b, s: (0, 0)),
        wspec(), vec(), wspec(), vec(), vec(),
    ]
    args = [p, p, p, conv_w, conv_b.reshape(1, HALF), w_r, b_r.reshape(2, 1, HALF), w_i,
            b_i.reshape(2, 1, HALF), lam.reshape(2, 1, HALF)]
    if rev:
        in_specs += [pl.BlockSpec((r, HALF), lambda b, s: (blk(b, s), 0)),
                     pl.BlockSpec((r, HALF), lambda b, s: (blk(b, s), xcol - 1))]
        args += [hf, p]
    return pl.pallas_call(
        functools.partial(_lru_kernel, rev=rev),
        grid=(B, N_SEQ_BLK),
        in_specs=in_specs,
        out_specs=pl.BlockSpec((r, HALF), lambda b, s: (blk(b, s), 0)),
        out_shape=jax.ShapeDtypeStruct((T_ALL, HALF), BF16 if rev else F32),
        scratch_shapes=[pltpu.VMEM((r + 16, HALF), F32), pltpu.VMEM((r, HALF), F32),
                        pltpu.VMEM((r, HALF), F32), pltpu.VMEM((r, HALF), F32),
                        pltpu.VMEM((1, HALF), F32)],
        compiler_params=_cparams(("parallel", "arbitrary"), VMEM_LIMIT),
        name="lru_bwd" if rev else "lru_fwd",
    )(*args)


def kernel(x, c, ctx, c_ctx, mod_w, mod_b, norm_g, ab_w_in, ab_w_out, gmlp_v_g, gmlp_ws, gmlp_bs, cd_w_in, cd_w_out, hgrn_lb, hgrn_onorm_g, lru_conv_w, lru_conv_b, lru_wr, lru_br, lru_wi, lru_bi, lru_lambda, router_w, router_b, exp_w_gu, exp_b_gu, exp_w_down, exp_b_down):
    cond8 = jnp.concatenate([c, c_ctx[None, :], jnp.zeros((5, D), F32)], axis=0)
    modm = _modulation(cond8, mod_w, mod_b).reshape(DEPTH * 8, 1, 6 * D)
    xs = jnp.concatenate([x.reshape(T_LAT, D), ctx.reshape(T_CTX, D)], axis=0)

    lb_soft = jax.nn.softmax(hgrn_lb.astype(F32), axis=0)
    lb_all = jnp.cumsum(lb_soft, axis=0) - lb_soft[0]

    (p0,) = _inproj(xs, norm_g[0, 0], modm, 0, [ab_w_in[0].astype(BF16)], BF16)
    zc, zs = _fnet_chan(p0)
    mix_a = jnp.concatenate([_fnet_pos(zc, zs, 0, L, 512, 1024),
                             _fnet_pos(zc, zs, T_LAT, LC, LC, LC)], axis=0)
    mix_g = _gmlp(p0, gmlp_v_g[0], gmlp_ws[0], gmlp_bs[0])
    xs, h2, e4, p4, r4, counts = _outproj(mix_a, mix_g, ab_w_out[0].astype(BF16), xs, T_ALL, norm_g[0, 1],
                                          norm_g[0, 2], modm, 0, router_w[0], router_b[0])
    xs = _moe(h2, e4, p4, r4, counts, xs, norm_g[0, 3], modm, 0,
              exp_w_gu, exp_b_gu[0], exp_w_down, exp_b_down[0])

    w_cd = cd_w_in[0].astype(BF16)
    p1a, p1b = _inproj(xs, norm_g[1, 0], modm, 1, [w_cd[:, :4 * HALF], w_cd[:, 4 * HALF:]], F32)
    o_f = _hgrn(p1a, p1b, lb_all[1], False)
    hg = _hgrn(p1a, p1b, lb_all[1], True, o_f, hgrn_onorm_g[0])
    lru_args = (lru_conv_w[0], lru_conv_b[0], lru_wr[0], lru_br[0], lru_wi[0], lru_bi[0], lru_lambda[0])
    h_f = _lru(p1b, *lru_args, False)
    lr = _lru(p1b, *lru_args, True, h_f)
    xl, h2, e4, p4, r4, counts = _outproj(hg, lr, cd_w_out[0].astype(BF16), xs, T_LAT, norm_g[1, 1],
                                          norm_g[1, 2], modm, 1, router_w[1], router_b[1])
    xl = _moe(h2, e4, p4, r4, counts, xl, norm_g[1, 3], modm, 1,
              exp_w_gu, exp_b_gu[1], exp_w_down, exp_b_down[1])
    return xl.reshape(B, L, D)
```

```python
import functools

import numpy as np
import jax
import jax.numpy as jnp
from jax import lax
from jax.experimental import pallas as pl
from jax.experimental.pallas import tpu as pltpu

F32 = jnp.float32
BF16 = jnp.bfloat16
HI = lax.Precision.HIGHEST

D = 2048
B = 2
L = 4096
LC = 256
T_LAT = B * L
T_CTX = B * LC
T_ALL = T_LAT + T_CTX
DEPTH = 2
EPS = 1e-6

HALF = D // 2
N_EXPERTS = 32
TOP_K = 4
D_EXPERT = D // 2
SWIGLU_LIMIT = 7.0
SWIGLU_ALPHA = 1.702
MOE_TM = 256

FNET_GROUPS = 4
FNET_GD = HALF // FNET_GROUPS
GMLP_HEADS = 8
GMLP_CHUNK = 128
HGRN_HEADS = 8
HGRN_R = 128
HGRN_C = 64
LRU_HEADS = 8
LRU_R = 128
LRU_C = 8.0

VMEM_LIMIT = 56 * 1024 * 1024


def _cparams(sem, vmem=None):
    return pltpu.CompilerParams(dimension_semantics=sem, vmem_limit_bytes=vmem)


def _mod_row(i, tm):
    return jnp.where(i < L // tm, 0, jnp.where(i < 2 * L // tm, 1, 2))


def _rms(x):
    return x * lax.rsqrt(jnp.mean(x * x, axis=-1, keepdims=True) + EPS)


def _bdot(a, b):
    return jnp.dot(a.astype(BF16), b.astype(BF16), preferred_element_type=F32)


ROW_S = D // 128


def _row_slab(ref, r):
    return ref.at[pl.ds(pl.multiple_of(r * ROW_S, ROW_S), ROW_S), :]


def _store_rows(ref, val, row0=0):
    n = val.shape[0]
    for s in range(ROW_S):
        ref[pl.ds(row0 * ROW_S + s, n, stride=ROW_S), :] = val[:, s * 128:(s + 1) * 128]


def _load_rows(ref, n):
    return jnp.concatenate([ref[pl.ds(s, n, stride=ROW_S), :] for s in range(ROW_S)], axis=1)


def _mod_kernel(c_ref, w_ref, b_ref, o_ref):
    cnd = c_ref[...]
    s = cnd * jax.nn.sigmoid(cnd)
    o_ref[...] = jnp.dot(s, w_ref[...], precision=HI, preferred_element_type=F32) + b_ref[...]


def _modulation(cond8, mod_w, mod_b):
    tn = 1024
    n = mod_w.shape[-1]
    return pl.pallas_call(
        _mod_kernel,
        grid=(DEPTH, n // tn),
        in_specs=[
            pl.BlockSpec((8, D), lambda l, j: (0, 0)),
            pl.BlockSpec((None, D, tn), lambda l, j: (l, 0, j)),
            pl.BlockSpec((None, 1, tn), lambda l, j: (l, 0, j)),
        ],
        out_specs=pl.BlockSpec((None, 8, tn), lambda l, j: (l, 0, j)),
        out_shape=jax.ShapeDtypeStruct((DEPTH, 8, n), F32),
        compiler_params=_cparams(("parallel", "parallel"), VMEM_LIMIT),
        name="modulation",
    )(cond8, mod_w, mod_b.reshape(DEPTH, 1, n))


def _inproj_kernel(*refs, nw):
    x_ref, g_ref, sh_ref, sc_ref = refs[:4]
    w_refs, o_refs = refs[4:4 + nw], refs[4 + nw:]
    y = _rms(x_ref[...]) * g_ref[...]
    h = (y * (1.0 + sc_ref[...]) + sh_ref[...]).astype(BF16)
    for w_ref, o_ref in zip(w_refs, o_refs):
        o_ref[...] = jnp.dot(h, w_ref[...], preferred_element_type=F32).astype(o_ref.dtype)


def _inproj(xs, g, modm, layer, ws_bf, out_dtype):
    tm = 256
    t = xs.shape[0]
    row = lambda i: layer * 8 + _mod_row(i, tm)
    return pl.pallas_call(
        functools.partial(_inproj_kernel, nw=len(ws_bf)),
        grid=(t // tm,),
        in_specs=[
            pl.BlockSpec((tm, D), lambda i: (i, 0)),
            pl.BlockSpec((1, D), lambda i: (0, 0)),
            pl.BlockSpec((None, 1, D), lambda i: (row(i), 0, 0)),
            pl.BlockSpec((None, 1, D), lambda i: (row(i), 0, 1)),
        ] + [pl.BlockSpec(w.shape, lambda i: (0, 0), pipeline_mode=pl.Buffered(1)) for w in ws_bf],
        out_specs=[pl.BlockSpec((tm, w.shape[1]), lambda i: (i, 0)) for w in ws_bf],
        out_shape=[jax.ShapeDtypeStruct((t, w.shape[1]), out_dtype) for w in ws_bf],
        compiler_params=_cparams(("parallel",), VMEM_LIMIT),
        name="inproj",
    )(xs, g.reshape(1, D), modm, modm, *ws_bf)


def _fnet_chan_kernel(p_ref, cs_ref, zc_ref, zs_ref):
    for g in range(FNET_GROUPS):
        sl = slice(g * FNET_GD, (g + 1) * FNET_GD)
        z = jnp.dot(p_ref[:, sl].astype(BF16), cs_ref[...], preferred_element_type=F32)
        zc_ref[:, sl] = z[:, :FNET_GD].astype(BF16)
        zs_ref[:, sl] = z[:, FNET_GD:].astype(BF16)


def _fnet_chan(p):
    tm = 512
    t = p.shape[0]
    k = np.arange(FNET_GD)
    ang = 2.0 * np.pi * ((k[:, None] * k[None, :]) % FNET_GD) / FNET_GD
    cs = np.concatenate([np.cos(ang), np.sin(ang)], axis=1) / np.sqrt(FNET_GD)
    cs = jnp.asarray(cs, F32).astype(BF16)
    return pl.pallas_call(
        _fnet_chan_kernel,
        grid=(t // tm,),
        in_specs=[
            pl.BlockSpec((tm, HALF), lambda i: (i, 0)),
            pl.BlockSpec((FNET_GD, 2 * FNET_GD), lambda i: (0, 0)),
        ],
        out_specs=[pl.BlockSpec((tm, HALF), lambda i: (i, 0))] * 2,
        out_shape=[jax.ShapeDtypeStruct((t, HALF), BF16)] * 2,
        compiler_params=_cparams(("parallel",), VMEM_LIMIT),
        name="fnet_chan",
    )(p, cs)


def _fnet_pos_kernel(bre_ref, bim_ref, cre_ref, cim_ref, rre_ref, rim_ref, zc_ref, zs_ref, o_ref, acc):
    nt = pl.program_id(2)
    br, bi = bre_ref[...], bim_ref[...]
    cr, ci = cre_ref[...], cim_ref[...]
    rr, ri = rre_ref[...], rim_ref[...]
    tr = br * cr - bi * ci
    ti = br * ci + bi * cr
    er = (tr * rr - ti * ri).astype(BF16)
    ei = (tr * ri + ti * rr).astype(BF16)
    part = (jnp.dot(er, zc_ref[...], preferred_element_type=F32)
            + jnp.dot(ei, zs_ref[...], preferred_element_type=F32))

    @pl.when(nt == 0)
    def _():
        acc[...] = part

    @pl.when(nt > 0)
    def _():
        acc[...] += part

    @pl.when(nt == pl.num_programs(2) - 1)
    def _():
        o_ref[...] = acc[...].astype(BF16)


def _fnet_pos(zc, zs, row0, seq, tk, tn):
    nkt, nnt = seq // tk, seq // tn
    th = 2.0 * np.pi / seq
    kk = np.arange(tk)[:, None]
    nn = np.arange(tn)[None, :]
    base = th * ((kk * nn) % seq)
    n0 = (np.arange(nnt) * tn)[:, None, None]
    col = th * ((np.arange(tk)[None, :, None] * n0) % seq)
    k0 = (np.arange(nkt) * tk)[:, None, None, None]
    nfull = (np.arange(nnt) * tn)[None, :, None, None] + np.arange(tn)[None, None, None, :]
    row = th * ((k0 * nfull) % seq)
    scale = 1.0 / np.sqrt(seq)
    f = lambda a: jnp.asarray(a, F32)
    tabs = (f(np.cos(base)), f(-np.sin(base)), f(np.cos(col)), f(-np.sin(col)),
            f(np.cos(row) * scale), f(-np.sin(row) * scale))
    rb = row0 // tn
    ob = row0 // tk
    return pl.pallas_call(
        _fnet_pos_kernel,
        grid=(B, nkt, nnt),
        in_specs=[
            pl.BlockSpec((tk, tn), lambda b, k, n: (0, 0)),
            pl.BlockSpec((tk, tn), lambda b, k, n: (0, 0)),
            pl.BlockSpec((None, tk, 1), lambda b, k, n: (n, 0, 0)),
            pl.BlockSpec((None, tk, 1), lambda b, k, n: (n, 0, 0)),
            pl.BlockSpec((None, None, 1, tn), lambda b, k, n: (k, n, 0, 0)),
            pl.BlockSpec((None, None, 1, tn), lambda b, k, n: (k, n, 0, 0)),
            pl.BlockSpec((tn, HALF), lambda b, k, n: (rb + b * nnt + n, 0)),
            pl.BlockSpec((tn, HALF), lambda b, k, n: (rb + b * nnt + n, 0)),
        ],
        out_specs=pl.BlockSpec((tk, HALF), lambda b, k, n: (b * nkt + k, 0)),
        out_shape=jax.ShapeDtypeStruct((B * seq, HALF), BF16),
        scratch_shapes=[pltpu.VMEM((tk, HALF), F32)],
        compiler_params=_cparams(("parallel", "parallel", "arbitrary"), VMEM_LIMIT),
        name=f"fnet_pos_{seq}",
    )(*tabs, zc, zs)


def _gmlp_kernel(u_ref, v_ref, vg_ref, ws_ref, bst_ref, o_ref, *, tm):
    zu = jax.nn.gelu(u_ref[...].astype(F32))
    zv = jax.nn.gelu(v_ref[...].astype(F32))
    mu = jnp.mean(zv, axis=-1, keepdims=True)
    dv = zv - mu
    var = jnp.mean(dv * dv, axis=-1, keepdims=True)
    vn = dv * lax.rsqrt(var + EPS) * vg_ref[...]
    hd = HALF // GMLP_HEADS
    for c in range(tm // GMLP_CHUNK):
        rs = slice(c * GMLP_CHUNK, (c + 1) * GMLP_CHUNK)
        for h in range(GMLP_HEADS):
            cs = slice(h * hd, (h + 1) * hd)
            s = jnp.dot(ws_ref[h], vn[rs, cs], precision=HI, preferred_element_type=F32)
            s = s + bst_ref[:, h:h + 1]
            o_ref[rs, cs] = (zu[rs, cs] * s).astype(BF16)


def _gmlp(p, v_g, w_s, b_s):
    tm = 256
    t = p.shape[0]
    return pl.pallas_call(
        functools.partial(_gmlp_kernel, tm=tm),
        grid=(t // tm,),
        in_specs=[
            pl.BlockSpec((tm, HALF), lambda i: (i, 1)),
            pl.BlockSpec((tm, HALF), lambda i: (i, 2)),
            pl.BlockSpec((1, HALF), lambda i: (0, 0)),
            pl.BlockSpec((GMLP_HEADS, GMLP_CHUNK, GMLP_CHUNK), lambda i: (0, 0, 0)),
            pl.BlockSpec((GMLP_CHUNK, GMLP_HEADS), lambda i: (0, 0)),
        ],
        out_specs=pl.BlockSpec((tm, HALF), lambda i: (i, 0)),
        out_shape=jax.ShapeDtypeStruct((t, HALF), BF16),
        compiler_params=_cparams(("parallel",), VMEM_LIMIT),
        name="gmlp",
    )(p, p, v_g.reshape(1, HALF), w_s, b_s.T)


def _outproj_kernel(ma_ref, mb_ref, w_ref, x_ref, g1_ref, g2_ref, ga1_ref, sh2_ref, sc2_ref,
                    rw_ref, rb_ref, xo_ref, h2_ref, e_ref, p_ref, rk_ref, cnt_ref, carry, rw2, *, tm):
    @pl.when(pl.program_id(0) == 0)
    def _():
        carry[...] = jnp.zeros_like(carry)
        rw = rw_ref[...]
        rw_hi = rw.astype(BF16)
        rw2[:, :N_EXPERTS] = rw_hi
        rw2[:, N_EXPERTS:] = (rw - rw_hi.astype(F32)).astype(BF16)

    sub = 128
    iota = lax.broadcasted_iota(jnp.int32, (sub, N_EXPERTS), 1).astype(F32)
    lane4 = lax.broadcasted_iota(jnp.int32, (sub, TOP_K), 1)
    r_i = lax.broadcasted_iota(jnp.int32, (sub, sub), 0)
    c_i = lax.broadcasted_iota(jnp.int32, (sub, sub), 1)
    tri = (c_i < r_i).astype(BF16)
    running = carry[...]
    for r0 in range(0, tm, sub):
        rs = slice(r0, r0 + sub)
        m = (jnp.dot(ma_ref[rs, :], w_ref[:HALF, :], preferred_element_type=F32)
             + jnp.dot(mb_ref[rs, :], w_ref[HALF:, :], preferred_element_type=F32))
        xn = x_ref[rs, :] + ga1_ref[...] * (_rms(m) * g1_ref[...])
        xo_ref[rs, :] = xn
        h2 = (_rms(xn) * g2_ref[...]) * (1.0 + sc2_ref[...]) + sh2_ref[...]
        _store_rows(h2_ref, h2, r0)

        h_hi = h2.astype(BF16)
        h_lo = (h2 - h_hi.astype(F32)).astype(BF16)
        t1 = jnp.dot(h_hi, rw2[...], preferred_element_type=F32)
        t2 = jnp.dot(h_lo, rw2[:, :N_EXPERTS], preferred_element_type=F32)
        lg = t1[:, :N_EXPERTS] + t1[:, N_EXPERTS:] + t2 + rb_ref[...]
        idxs, vals = [], []
        for _ in range(TOP_K):
            mx = jnp.max(lg, axis=-1, keepdims=True)
            ix = jnp.min(jnp.where(lg == mx, iota, float(N_EXPERTS)), axis=-1, keepdims=True)
            idxs.append(ix)
            vals.append(mx)
            lg = jnp.where(iota == ix, -jnp.inf, lg)
        exs = [jnp.exp(v - vals[0]) for v in vals]
        den = exs[0] + exs[1] + exs[2] + exs[3]

        onehot = [(iota == ix) for ix in idxs]
        cnt = jnp.zeros((sub, N_EXPERTS), F32)
        for oh in onehot:
            cnt = cnt + oh.astype(F32)
        prefix = jnp.dot(tri, cnt.astype(BF16), preferred_element_type=F32) + running

        e4 = jnp.zeros((sub, TOP_K), jnp.int32)
        p4 = jnp.zeros((sub, TOP_K), F32)
        r4 = jnp.zeros((sub, TOP_K), jnp.int32)
        for k in range(TOP_K):
            rk = jnp.sum(jnp.where(onehot[k], prefix, 0.0), axis=-1, keepdims=True)
            e4 = jnp.where(lane4 == k, idxs[k].astype(jnp.int32), e4)
            p4 = jnp.where(lane4 == k, exs[k] / den, p4)
            r4 = jnp.where(lane4 == k, rk.astype(jnp.int32), r4)
        e_ref[rs, :] = e4
        p_ref[rs, :] = p4
        rk_ref[rs, :] = r4
        running = running + jnp.sum(cnt, axis=0, keepdims=True)
    carry[...] = running
    cnt_ref[...] = running


def _outproj(ma, mb, w_bf, xs, t, g1, g2, modm, layer, rw, rb):
    tm = 256
    row = lambda i: layer * 8 + _mod_row(i, tm)
    modspec = lambda c: pl.BlockSpec((None, 1, D), lambda i: (row(i), 0, c))
    vec = pl.BlockSpec((1, D), lambda i: (0, 0))
    small = lambda dt: jax.ShapeDtypeStruct((t, TOP_K), dt)
    return pl.pallas_call(
        functools.partial(_outproj_kernel, tm=tm),
        grid=(t // tm,),
        in_specs=[
            pl.BlockSpec((tm, HALF), lambda i: (i, 0)),
            pl.BlockSpec((tm, HALF), lambda i: (i, 0)),
            pl.BlockSpec((D, D), lambda i: (0, 0)),
            pl.BlockSpec((tm, D), lambda i: (i, 0)),
            vec, vec, modspec(2), modspec(3), modspec(4),
            pl.BlockSpec((D, N_EXPERTS), lambda i: (0, 0)),
            pl.BlockSpec((1, N_EXPERTS), lambda i: (0, 0)),
        ],
        out_specs=[
            pl.BlockSpec((tm, D), lambda i: (i, 0)),
            pl.BlockSpec((tm * ROW_S, 128), lambda i: (i, 0)),
            pl.BlockSpec((tm, TOP_K), lambda i: (i, 0)),
            pl.BlockSpec((tm, TOP_K), lambda i: (i, 0)),
            pl.BlockSpec((tm, TOP_K), lambda i: (i, 0)),
            pl.BlockSpec((1, N_EXPERTS), lambda i: (0, 0)),
        ],
        out_shape=[
            jax.ShapeDtypeStruct((t, D), F32),
            jax.ShapeDtypeStruct((t * ROW_S, 128), F32),
            small(jnp.int32), small(F32), small(jnp.int32),
            jax.ShapeDtypeStruct((1, N_EXPERTS), F32),
        ],
        scratch_shapes=[pltpu.VMEM((1, N_EXPERTS), F32), pltpu.VMEM((D, 2 * N_EXPERTS), BF16)],
        compiler_params=_cparams(("arbitrary",), VMEM_LIMIT),
        name="outproj_router",
    )(ma, mb, w_bf, xs, g1.reshape(1, D), g2.reshape(1, D), modm, modm, modm,
      rw, rb.reshape(1, N_EXPERTS))


def _invmap_kernel(pstart_ref, cnt_ref, padded_ref, dest_ref, inv_ref, *, n_pairs, n_slots):
    def unused(lo, hi):
        def body(s, c):
            inv_ref[s] = n_pairs + jnp.bitwise_and(s, 2 * MOE_TM - 1)
            return c
        lax.fori_loop(lo, hi, body, 0)

    def per_expert(e, c):
        unused(pstart_ref[e] + cnt_ref[e], pstart_ref[e] + padded_ref[e])
        return c

    lax.fori_loop(0, N_EXPERTS, per_expert, 0)
    unused(pstart_ref[N_EXPERTS - 1] + padded_ref[N_EXPERTS - 1], n_slots)

    def body(i, c):
        inv_ref[dest_ref[i]] = i
        return c

    lax.fori_loop(0, n_pairs, body, 0, unroll=8)


def _invmap(dest, pstart, cnt, padded, n_slots):
    n_pairs = dest.shape[0]
    return pl.pallas_call(
        functools.partial(_invmap_kernel, n_pairs=n_pairs, n_slots=n_slots),
        grid_spec=pltpu.PrefetchScalarGridSpec(
            num_scalar_prefetch=3,
            grid=(1,),
            in_specs=[pl.BlockSpec(memory_space=pltpu.SMEM)],
            out_specs=pl.BlockSpec(memory_space=pltpu.SMEM),
        ),
        out_shape=jax.ShapeDtypeStruct((n_slots,), jnp.int32),
        compiler_params=_cparams(("arbitrary",)),
        name="moe_invmap",
    )(pstart, cnt, padded, dest)


W_CH = 256
W_NCH_GU = D // W_CH
W_NCH = W_NCH_GU + D_EXPERT // W_CH
W_STAGE = 4


def _expert_kernel(be_ref, nused_ref, first_ref, nxt_ref, lo_ref, hi_ref, slot_ref,
                   inv_cur, inv_next, inv_next2, inv_prev, bgu_ref, bd_ref, h_hbm, wgu_all, wd_all, y_hbm,
                   xbuf, obuf, x_s, act_s, wgu_buf, wd_buf, stage, gsem, ssem, sem, *, layer, n_tok):
    i = pl.program_id(0)
    n_live = nused_ref[0]
    s_cur = lax.rem(i, 3)
    s_p1 = lax.rem(i + 1, 3)
    s_p2 = lax.rem(i + 2, 3)
    wgu_hbm = wgu_all.at[layer]
    wd_hbm = wd_all.at[layer]

    def gather_copy(inv_ref, r, slot):
        tok = jnp.minimum(lax.shift_right_logical(inv_ref[r], 2), n_tok - 1)
        return pltpu.make_async_copy(_row_slab(h_hbm, tok), _row_slab(xbuf.at[slot], r), gsem.at[slot])

    n_pairs = n_tok * TOP_K

    def scatter_copy(inv_ref, r, slot, first_step=False):
        dst = inv_ref[r]
        if first_step:
            dst = jnp.where(i == 0, n_pairs + 2 * MOE_TM + r, dst)
        return pltpu.make_async_copy(_row_slab(obuf.at[slot], r), _row_slab(y_hbm, dst), ssem.at[slot])

    def looped(fn):
        def body(r, c):
            fn(r)
            return c
        lax.fori_loop(0, MOE_TM, body, 0)

    def start_chunk(e, c):
        s = lax.rem(c, W_STAGE)

        @pl.when(c < W_NCH_GU)
        def _():
            r0 = pl.multiple_of(c * W_CH, W_CH)
            pltpu.make_async_copy(wgu_hbm.at[e, pl.ds(r0, W_CH), :], stage.at[s], sem.at[s]).start()

        @pl.when(c >= W_NCH_GU)
        def _():
            r0 = pl.multiple_of((c - W_NCH_GU) * W_CH, W_CH)
            pltpu.make_async_copy(wd_hbm.at[e, pl.ds(r0, W_CH), :], stage.at[s], sem.at[s]).start()

    def finish_chunk(c, dst):
        s = lax.rem(c, W_STAGE)
        pltpu.make_async_copy(wgu_hbm.at[0, pl.ds(0, W_CH), :], stage.at[s], sem.at[s]).wait()
        w = stage[s].astype(BF16)

        @pl.when(c < W_NCH_GU)
        def _():
            wgu_buf[dst, pl.ds(pl.multiple_of(c * W_CH, W_CH), W_CH), :] = w

        @pl.when(c >= W_NCH_GU)
        def _():
            wd_buf[dst, pl.ds(pl.multiple_of((c - W_NCH_GU) * W_CH, W_CH), W_CH), :] = w

    def stream(e, dst, c_lo, c_hi):
        def body(c, carry):
            finish_chunk(c, dst)

            @pl.when(c + W_STAGE < W_NCH)
            def _():
                start_chunk(e, c + W_STAGE)
            return carry
        lax.fori_loop(c_lo, c_hi, body, 0)

    def prime(e):
        for c in range(W_STAGE):
            start_chunk(e, c)

    @pl.when(i == 0)
    def _():
        looped(lambda r: gather_copy(inv_cur, r, 0).start())
        looped(lambda r: gather_copy(inv_next, r, 1).start())
        obuf[2] = jnp.zeros(obuf.shape[1:], F32)
        for region in range(2):
            dump = y_hbm.at[pl.ds((n_pairs + region * MOE_TM) * ROW_S, MOE_TM * ROW_S), :]
            zero_dump = pltpu.make_async_copy(obuf.at[2], dump, ssem.at[2])
            zero_dump.start()
            zero_dump.wait()
        prime(be_ref[0])
        stream(be_ref[0], slot_ref[0], 0, W_NCH)

    @pl.when(i < n_live)
    def _():
        e_next = nxt_ref[i]
        cur = slot_ref[i]

        @pl.when((first_ref[i] == 1) & (e_next >= 0))
        def _():
            prime(e_next)

        for r in range(MOE_TM):
            gather_copy(inv_cur, r, s_cur).wait()
        for s in range(ROW_S):
            x_s[:, s * 128:(s + 1) * 128] = xbuf[s_cur, pl.ds(s, MOE_TM, stride=ROW_S), :].astype(BF16)
        nc = 256
        for r in range(MOE_TM):
            gather_copy(inv_next2, r, s_p2).start()
            scatter_copy(inv_prev, r, s_p2, first_step=True).start()
        for c0 in range(0, D_EXPERT, nc):
            g = (jnp.dot(x_s[...], wgu_buf[cur, :, c0:c0 + nc], preferred_element_type=F32)
                 + bgu_ref[:, c0:c0 + nc])
            u = (jnp.dot(x_s[...], wgu_buf[cur, :, D_EXPERT + c0:D_EXPERT + c0 + nc],
                         preferred_element_type=F32) + bgu_ref[:, D_EXPERT + c0:D_EXPERT + c0 + nc])
            gate = jnp.minimum(g, SWIGLU_LIMIT)
            up = jnp.clip(u, -SWIGLU_LIMIT, SWIGLU_LIMIT)
            act_s[:, c0:c0 + nc] = (gate * jax.nn.sigmoid(SWIGLU_ALPHA * gate) * (up + 1.0)).astype(BF16)

        for c0 in range(0, D, nc):
            y = (jnp.dot(act_s[...], wd_buf[cur, :, c0:c0 + nc], preferred_element_type=F32)
                 + bd_ref[:, c0:c0 + nc])
            for j in range(nc // 128):
                s = c0 // 128 + j
                obuf[s_cur, pl.ds(s, MOE_TM, stride=ROW_S), :] = y[:, j * 128:(j + 1) * 128]

        @pl.when(i >= 1)
        def _():
            for r in range(MOE_TM):
                scatter_copy(inv_prev, r, s_p1).wait()

        @pl.when(e_next >= 0)
        def _():
            stream(e_next, 1 - cur, lo_ref[i], hi_ref[i])

        @pl.when(i == n_live - 1)
        def _():
            looped(lambda r: scatter_copy(inv_prev, r, s_p2).wait())
            looped(lambda r: scatter_copy(inv_cur, r, s_cur).start())
            looped(lambda r: scatter_copy(inv_cur, r, s_cur).wait())
            looped(lambda r: gather_copy(inv_next, r, s_p1).wait())
            looped(lambda r: gather_copy(inv_next2, r, s_p2).wait())


def _take(table, idx):
    ids = jnp.arange(table.shape[0], dtype=jnp.int32)
    return jnp.sum(jnp.where(idx[..., None] == ids, table, 0), axis=-1)


def _experts(h2, inv, block_e, nused, pstart, padded, layer, wgu, bgu, wd, bd, nblk, n_tok):
    nb_e = padded // MOE_TM
    pos = jnp.arange(nblk, dtype=jnp.int32) - _take(pstart // MOE_TM, block_e)
    nb = jnp.maximum(_take(nb_e, block_e), 1)
    lo = (pos * W_NCH) // nb
    hi = ((pos + 1) * W_NCH) // nb
    first = (pos == 0).astype(jnp.int32)
    eid = jnp.arange(N_EXPERTS, dtype=jnp.int32)
    later = jnp.where(nb_e > 0, eid, N_EXPERTS)
    nxt_e = jnp.concatenate([lax.cummin(later, reverse=True)[1:], jnp.full((1,), N_EXPERTS, jnp.int32)])
    nxt_e = jnp.where(nxt_e >= N_EXPERTS, -1, nxt_e)
    slot_e = (jnp.cumsum((nb_e > 0).astype(jnp.int32)) - 1) % 2
    i32 = lambda a: a.astype(jnp.int32)
    live = lambda i, nu: jnp.clip(i, 0, nu[0] - 1)
    sblk = lambda off: pl.BlockSpec((MOE_TM,), lambda i, be, nu, *_: (live(i + off, nu),),
                                    memory_space=pltpu.SMEM)
    bias = lambda n: pl.BlockSpec((None, 1, n), lambda i, be, nu, *_: (be[live(i, nu)], 0, 0))
    hbm = pl.BlockSpec(memory_space=pl.ANY)
    n_slabs = n_tok * TOP_K + 3 * MOE_TM
    return pl.pallas_call(
        functools.partial(_expert_kernel, layer=layer, n_tok=n_tok),
        grid_spec=pltpu.PrefetchScalarGridSpec(
            num_scalar_prefetch=7,
            grid=(nblk,),
            in_specs=[sblk(0), sblk(1), sblk(2), sblk(-1), bias(2 * D_EXPERT), bias(D), hbm, hbm, hbm],
            out_specs=hbm,
            scratch_shapes=[
                pltpu.VMEM((3, MOE_TM * ROW_S, 128), F32),
                pltpu.VMEM((3, MOE_TM * ROW_S, 128), F32),
                pltpu.VMEM((MOE_TM, D), BF16),
                pltpu.VMEM((MOE_TM, D_EXPERT), BF16),
                pltpu.VMEM((2, D, 2 * D_EXPERT), BF16),
                pltpu.VMEM((2, D_EXPERT, D), BF16),
                pltpu.VMEM((W_STAGE, W_CH, D), F32),
                pltpu.SemaphoreType.DMA((3,)),
                pltpu.SemaphoreType.DMA((3,)),
                pltpu.SemaphoreType.DMA((W_STAGE,)),
            ],
        ),
        out_shape=jax.ShapeDtypeStruct((n_slabs * ROW_S, 128), F32),
        compiler_params=_cparams(("arbitrary",), VMEM_LIMIT),
        name="moe_experts",
    )(block_e, nused, i32(first), i32(_take(nxt_e, block_e)), i32(lo), i32(hi), i32(_take(slot_e, block_e)),
      inv, inv, inv, inv, bgu.reshape(N_EXPERTS, 1, -1), bd.reshape(N_EXPERTS, 1, -1), h2, wgu, wd)


def _combine_kernel(p_ref, x_ref, g3_ref, ga2_ref, y_ref, xo_ref, *, tm):
    p = p_ref[...]
    pair = TOP_K * ROW_S
    cols = []
    for s in range(ROW_S):
        acc = y_ref[pl.ds(s, tm, stride=pair), :] * p[:, 0:1]
        for k in range(1, TOP_K):
            acc = acc + y_ref[pl.ds(k * ROW_S + s, tm, stride=pair), :] * p[:, k:k + 1]
        cols.append(acc)
    y = jnp.concatenate(cols, axis=1)
    xo_ref[...] = x_ref[...] + ga2_ref[...] * (_rms(y) * g3_ref[...])


def _combine(y4, p4, xs, g3, modm, layer):
    tm = 128
    t = xs.shape[0]
    row = lambda i: layer * 8 + _mod_row(i, tm)
    return pl.pallas_call(
        functools.partial(_combine_kernel, tm=tm),
        grid=(t // tm,),
        in_specs=[
            pl.BlockSpec((tm, TOP_K), lambda i: (i, 0)),
            pl.BlockSpec((tm, D), lambda i: (i, 0)),
            pl.BlockSpec((1, D), lambda i: (0, 0)),
            pl.BlockSpec((None, 1, D), lambda i: (row(i), 0, 5)),
            pl.BlockSpec((tm * TOP_K * ROW_S, 128), lambda i: (i, 0)),
        ],
        out_specs=pl.BlockSpec((tm, D), lambda i: (i, 0)),
        out_shape=jax.ShapeDtypeStruct((t, D), F32),
        compiler_params=_cparams(("parallel",), VMEM_LIMIT),
        name="moe_combine",
    )(p4, xs, g3.reshape(1, D), modm, y4)


def _moe(h2, e4, p4, r4, counts, xs, g3, modm, layer, wgu, bgu, wd, bd):
    t = e4.shape[0]
    nblk = -(-(t * TOP_K + N_EXPERTS * (MOE_TM - 1)) // MOE_TM)
    cnt = counts.reshape(N_EXPERTS).astype(jnp.int32)
    padded = (cnt + MOE_TM - 1) // MOE_TM * MOE_TM
    pend = jnp.cumsum(padded)
    pstart = pend - padded
    dest = (_take(pstart, e4) + r4).reshape(-1).astype(jnp.int32)
    nused = (pend[-1:] // MOE_TM).astype(jnp.int32)
    blk_start = jnp.arange(nblk, dtype=jnp.int32) * MOE_TM
    block_e = jnp.minimum(jnp.sum((pend[None, :] <= blk_start[:, None]).astype(jnp.int32), axis=1),
                          N_EXPERTS - 1).astype(jnp.int32)
    inv = _invmap(dest, pstart.astype(jnp.int32), cnt, padded.astype(jnp.int32), nblk * MOE_TM)
    y4 = _experts(h2, inv, block_e, nused, pstart, padded, layer, wgu, bgu, wd, bd, nblk, t)
    return _combine(y4, p4, xs, g3, modm, layer)


N_CTX_BLK = LC // HGRN_R
N_LAT_BLK = L // HGRN_R
N_SEQ_BLK = N_CTX_BLK + N_LAT_BLK


def _scan_block(b, s, rev):
    if rev:
        return jnp.where(s < N_CTX_BLK, T_LAT // HGRN_R + N_CTX_BLK * b + (N_CTX_BLK - 1 - s),
                         N_LAT_BLK * b + (N_SEQ_BLK - 1 - s))
    return jnp.where(s < N_CTX_BLK, T_LAT // HGRN_R + N_CTX_BLK * b + s, N_LAT_BLK * b + (s - N_CTX_BLK))


def _hgrn_kernel(*refs, rev):
    if rev:
        q_ref, f_ref, v_ref, lb_ref, of_ref, g_ref, on_ref, o_ref, st, qin_s, qd_s, ki_s, ke_s, v_s = refs
    else:
        q_ref, f_ref, v_ref, lb_ref, o_ref, st, qin_s, qd_s, ki_s, ke_s, v_s = refs

    @pl.when(pl.program_id(1) == 0)
    def _():
        st[...] = jnp.zeros_like(st)

    c = HGRN_C
    nc = HGRN_R // c
    r_i = lax.broadcasted_iota(jnp.int32, (c, c), 0)
    c_i = lax.broadcasted_iota(jnp.int32, (c, c), 1)
    mask = (c_i >= r_i) if rev else (c_i <= r_i)
    tri = mask.astype(F32)

    lb = lb_ref[...]
    decay = [None] * nc
    v_s[...] = v_ref[...].astype(BF16)
    for ci in range(nc):
        rs = slice(ci * c, (ci + 1) * c)
        sg = jax.nn.sigmoid(f_ref[rs, :])
        logf = jnp.log(lb + (1.0 - lb) * sg)
        kk = (1.0 - lb) * (1.0 - sg)
        q = q_ref[rs, :]
        qs = q * jax.nn.sigmoid(q)
        cum = jnp.dot(tri, logf, precision=HI, preferred_element_type=F32)
        total = cum[0:1] if rev else cum[c - 1:c]
        mid = cum[c // 2:c // 2 + 1]
        qin_s[rs, :] = (qs * jnp.exp(cum)).astype(BF16)
        qd_s[rs, :] = (qs * jnp.exp(cum - mid)).astype(BF16)
        ki_s[rs, :] = (kk * jnp.exp(mid - cum)).astype(BF16)
        ke_s[rs, :] = (kk * jnp.exp(total - cum)).astype(BF16)
        decay[ci] = jnp.exp(total)

    order = range(nc - 1, -1, -1) if rev else range(nc)
    for ci in order:
        rs = slice(ci * c, (ci + 1) * c)
        for h in range(HGRN_HEADS):
            hs = slice(h * 128, (h + 1) * 128)
            state = st[h]
            vb = v_s[rs, hs]
            sc = lax.dot_general(qd_s[rs, hs], ki_s[rs, hs], (((1,), (1,)), ((), ())),
                                 preferred_element_type=F32)
            sc = jnp.where(mask, sc, 0.0)
            intra = jnp.dot(sc.astype(BF16), vb, preferred_element_type=F32)
            inter = lax.dot_general(qin_s[rs, hs], state.astype(BF16), (((1,), (1,)), ((), ())),
                                    preferred_element_type=F32)
            st[h] = state * decay[ci][:, hs] + lax.dot_general(
                vb, ke_s[rs, hs], (((0,), (0,)), ((), ())), preferred_element_type=F32)
            o = intra + inter
            if rev:
                o = o + of_ref[rs, hs]
                g = g_ref[rs, hs]
                o_ref[rs, hs] = (_rms(o) * on_ref[:, hs] * (g * jax.nn.sigmoid(g))).astype(BF16)
            else:
                o_ref[rs, hs] = o


def _hgrn(pa, pb, lb, rev, of=None, onorm=None):
    r = HGRN_R
    blk = lambda b, s: _scan_block(b, s, rev)
    col = lambda c: pl.BlockSpec((r, HALF), lambda b, s: (blk(b, s), c))
    vec = pl.BlockSpec((1, HALF), lambda b, s: (0, 0))
    in_specs = [col(0), col(2 if rev else 1), col(3), vec]
    args = [pa, pa, pa, lb.reshape(1, HALF)]
    if rev:
        in_specs += [col(0), col(0), vec]
        args += [of, pb, onorm.reshape(1, HALF)]
    return pl.pallas_call(
        functools.partial(_hgrn_kernel, rev=rev),
        grid=(B, N_SEQ_BLK),
        in_specs=in_specs,
        out_specs=col(0),
        out_shape=jax.ShapeDtypeStruct((T_ALL, HALF), BF16 if rev else F32),
        scratch_shapes=[pltpu.VMEM((HGRN_HEADS, 128, 128), F32)] + [pltpu.VMEM((r, HALF), BF16)] * 5,
        compiler_params=_cparams(("parallel", "arbitrary"), VMEM_LIMIT),
        name="hgrn_bwd" if rev else "hgrn_fwd",
    )(*args)


def _lru_kernel(*refs, rev):
    if rev:
        (x_ref, xp_ref, xn_ref, cw_ref, cb_ref, wr_ref, br_ref, wi_ref, bi_ref, lam_ref,
         hf_ref, gt_ref, o_ref, xcat, a_s, u_s, h_s, carry) = refs
    else:
        (x_ref, xp_ref, xn_ref, cw_ref, cb_ref, wr_ref, br_ref, wi_ref, bi_ref, lam_ref,
         o_ref, xcat, a_s, u_s, h_s, carry) = refs
    s = pl.program_id(1)
    r = LRU_R

    @pl.when(s == 0)
    def _():
        carry[...] = jnp.zeros_like(carry)

    is_ctx = s < N_CTX_BLK
    if rev:
        j = jnp.where(is_ctx, N_CTX_BLK - 1 - s, N_SEQ_BLK - 1 - s)
    else:
        j = jnp.where(is_ctx, s, s - N_CTX_BLK)
    nb = jnp.where(is_ctx, N_CTX_BLK, N_LAT_BLK)
    xcat[0:8, :] = jnp.where(j == 0, 0.0, xp_ref[...])
    xcat[8:8 + r, :] = x_ref[...]
    xcat[8 + r:16 + r, :] = jnp.where(j == nb - 1, 0.0, xn_ref[...])
    xc = cb_ref[...] + xcat[pl.ds(6, r), :] * cw_ref[0:1, :]
    for t in range(1, 4):
        xc = xc + xcat[pl.ds(6 + t, r), :] * cw_ref[t:t + 1, :]

    lam = lam_ref[...]
    sp = jnp.maximum(-lam, 0.0) + jnp.log1p(jnp.exp(-jnp.abs(lam)))
    hd = HALF // LRU_HEADS
    for h in range(LRU_HEADS):
        cs = slice(h * hd, (h + 1) * hd)
        xh = xc[:, cs]
        rg = jax.nn.sigmoid(_bdot(xh, wr_ref[h]) + br_ref[:, cs])
        ig = jax.nn.sigmoid(_bdot(xh, wi_ref[h]) + bi_ref[:, cs])
        log_a = -LRU_C * rg * sp[:, cs]
        a = jnp.exp(log_a)
        a_s[:, cs] = a
        u_s[:, cs] = jnp.sqrt(-jnp.tanh(log_a) * (a * a + 1.0)) * (ig * xh)

    def step(t, h):
        tt = (r - 1 - t) if rev else t
        h = a_s[pl.ds(tt, 1), :] * h + u_s[pl.ds(tt, 1), :]
        h_s[pl.ds(tt, 1), :] = h
        return h

    carry[...] = lax.fori_loop(0, r, step, carry[...], unroll=8)
    if rev:
        o_ref[...] = ((h_s[...] + hf_ref[...]) * jax.nn.gelu(gt_ref[...])).astype(BF16)
    else:
        o_ref[...] = h_s[...]


def _lru(p, conv_w, conv_b, w_r, b_r, w_i, b_i, lam, rev, hf=None):
    r = LRU_R
    d = 1 if rev else 0
    blk = lambda b, s: _scan_block(b, s, rev)
    xcol = 2
    n8 = T_ALL // 8
    vec = lambda: pl.BlockSpec((None, 1, HALF), lambda b, s: (d, 0, 0))
    wspec = lambda: pl.BlockSpec((None, LRU_HEADS, 128, 128), lambda b, s: (d, 0, 0, 0))
    in_specs = [
        pl.BlockSpec((r, HALF), lambda b, s: (blk(b, s), xcol)),
        pl.BlockSpec((8, HALF), lambda b, s: (jnp.maximum(blk(b, s) * (r // 8) - 1, 0), xcol)),
        pl.BlockSpec((8, HALF), lambda b, s: (jnp.minimum((blk(b, s) + 1) * (r // 8), n8 - 1), xcol)),
        pl.BlockSpec((4, HALF), lambda b, s: (0, 0)),
        pl.BlockSpec((1, HALF), lambda b, s: (0, 0)),
        wspec(), vec(), wspec(), vec(), vec(),
    ]
    args = [p, p, p, conv_w, conv_b.reshape(1, HALF), w_r, b_r.reshape(2, 1, HALF), w_i,
            b_i.reshape(2, 1, HALF), lam.reshape(2, 1, HALF)]
    if rev:
        in_specs += [pl.BlockSpec((r, HALF), lambda b, s: (blk(b, s), 0)),
                     pl.BlockSpec((r, HALF), lambda b, s: (blk(b, s), xcol - 1))]
        args += [hf, p]
    return pl.pallas_call(
        functools.partial(_lru_kernel, rev=rev),
        grid=(B, N_SEQ_BLK),
        in_specs=in_specs,
        out_specs=pl.BlockSpec((r, HALF), lambda b, s: (blk(b, s), 0)),
        out_shape=jax.ShapeDtypeStruct((T_ALL, HALF), BF16 if rev else F32),
        scratch_shapes=[pltpu.VMEM((r + 16, HALF), F32), pltpu.VMEM((r, HALF), F32),
                        pltpu.VMEM((r, HALF), F32), pltpu.VMEM((r, HALF), F32),
                        pltpu.VMEM((1, HALF), F32)],
        compiler_params=_cparams(("parallel", "arbitrary"), VMEM_LIMIT),
        name="lru_bwd" if rev else "lru_fwd",
    )(*args)


def kernel(x, c, ctx, c_ctx, mod_w, mod_b, norm_g, ab_w_in, ab_w_out, gmlp_v_g, gmlp_ws, gmlp_bs, cd_w_in, cd_w_out, hgrn_lb, hgrn_onorm_g, lru_conv_w, lru_conv_b, lru_wr, lru_br, lru_wi, lru_bi, lru_lambda, router_w, router_b, exp_w_gu, exp_b_gu, exp_w_down, exp_b_down):
    cond8 = jnp.concatenate([c, c_ctx[None, :], jnp.zeros((5, D), F32)], axis=0)
    modm = _modulation(cond8, mod_w, mod_b).reshape(DEPTH * 8, 1, 6 * D)
    xs = jnp.concatenate([x.reshape(T_LAT, D), ctx.reshape(T_CTX, D)], axis=0)

    lb_soft = jax.nn.softmax(hgrn_lb.astype(F32), axis=0)
    lb_all = jnp.cumsum(lb_soft, axis=0) - lb_soft[0]

    (p0,) = _inproj(xs, norm_g[0, 0], modm, 0, [ab_w_in[0].astype(BF16)], BF16)
    zc, zs = _fnet_chan(p0)
    mix_a = jnp.concatenate([_fnet_pos(zc, zs, 0, L, 512, 1024),
                             _fnet_pos(zc, zs, T_LAT, LC, LC, LC)], axis=0)
    mix_g = _gmlp(p0, gmlp_v_g[0], gmlp_ws[0], gmlp_bs[0])
    xs, h2, e4, p4, r4, counts = _outproj(mix_a, mix_g, ab_w_out[0].astype(BF16), xs, T_ALL, norm_g[0, 1],
                                          norm_g[0, 2], modm, 0, router_w[0], router_b[0])
    xs = _moe(h2, e4, p4, r4, counts, xs, norm_g[0, 3], modm, 0,
              exp_w_gu, exp_b_gu[0], exp_w_down, exp_b_down[0])

    w_cd = cd_w_in[0].astype(BF16)
    p1a, p1b = _inproj(xs, norm_g[1, 0], modm, 1, [w_cd[:, :4 * HALF], w_cd[:, 4 * HALF:]], F32)
    o_f = _hgrn(p1a, p1b, lb_all[1], False)
    hg = _hgrn(p1a, p1b, lb_all[1], True, o_f, hgrn_onorm_g[0])
    lru_args = (lru_conv_w[0], lru_conv_b[0], lru_wr[0], lru_br[0], lru_wi[0], lru_bi[0], lru_lambda[0])
    h_f = _lru(p1b, *lru_args, False)
    lr = _lru(p1b, *lru_args, True, h_f)
    xl, h2, e4, p4, r4, counts = _outproj(hg, lr, cd_w_out[0].astype(BF16), xs, T_LAT, norm_g[1, 1],
                                          norm_g[1, 2], modm, 1, router_w[1], router_b[1])
    xl = _moe(h2, e4, p4, r4, counts, xl, norm_g[1, 3], modm, 1,
              exp_w_gu, exp_b_gu[1], exp_w_down, exp_b_down[1])
    return xl.reshape(B, L, D)
```

```python
import functools

import numpy as np
import jax
import jax.numpy as jnp
from jax import lax
from jax.experimental import pallas as pl
from jax.experimental.pallas import tpu as pltpu

F32 = jnp.float32
BF16 = jnp.bfloat16
HI = lax.Precision.HIGHEST

D = 2048
B = 2
L = 4096
LC = 256
T_LAT = B * L
T_CTX = B * LC
T_ALL = T_LAT + T_CTX
DEPTH = 2
EPS = 1e-6

HALF = D // 2
N_EXPERTS = 32
TOP_K = 4
D_EXPERT = D // 2
SWIGLU_LIMIT = 7.0
SWIGLU_ALPHA = 1.702
MOE_TM = 256

FNET_GROUPS = 4
FNET_GD = HALF // FNET_GROUPS
GMLP_HEADS = 8
GMLP_CHUNK = 128
HGRN_HEADS = 8
HGRN_R = 128
HGRN_C = 64
LRU_HEADS = 8
LRU_R = 128
LRU_C = 8.0

VMEM_LIMIT = 56 * 1024 * 1024


def _cparams(sem, vmem=None):
    return pltpu.CompilerParams(dimension_semantics=sem, vmem_limit_bytes=vmem)


def _mod_row(i, tm):
    return jnp.where(i < L // tm, 0, jnp.where(i < 2 * L // tm, 1, 2))


def _rms(x):
    return x * lax.rsqrt(jnp.mean(x * x, axis=-1, keepdims=True) + EPS)


def _bdot(a, b):
    return jnp.dot(a.astype(BF16), b.astype(BF16), preferred_element_type=F32)


ROW_S = D // 128


def _row_slab(ref, r):
    return ref.at[pl.ds(pl.multiple_of(r * ROW_S, ROW_S), ROW_S), :]


def _store_rows(ref, val, row0=0):
    n = val.shape[0]
    for s in range(ROW_S):
        ref[pl.ds(row0 * ROW_S + s, n, stride=ROW_S), :] = val[:, s * 128:(s + 1) * 128]


def _load_rows(ref, n):
    return jnp.concatenate([ref[pl.ds(s, n, stride=ROW_S), :] for s in range(ROW_S)], axis=1)


def _mod_kernel(c_ref, w_ref, b_ref, o_ref):
    cnd = c_ref[...]
    s = cnd * jax.nn.sigmoid(cnd)
    o_ref[...] = _bdot(s, w_ref[...]) + b_ref[...]


def _modulation(cond8, mod_w, mod_b):
    tn = 1024
    n = mod_w.shape[-1]
    return pl.pallas_call(
        _mod_kernel,
        grid=(DEPTH, n // tn),
        in_specs=[
            pl.BlockSpec((8, D), lambda l, j: (0, 0)),
            pl.BlockSpec((None, D, tn), lambda l, j: (l, 0, j)),
            pl.BlockSpec((None, 1, tn), lambda l, j: (l, 0, j)),
        ],
        out_specs=pl.BlockSpec((None, 8, tn), lambda l, j: (l, 0, j)),
        out_shape=jax.ShapeDtypeStruct((DEPTH, 8, n), F32),
        compiler_params=_cparams(("parallel", "parallel"), VMEM_LIMIT),
        name="modulation",
    )(cond8, mod_w, mod_b.reshape(DEPTH, 1, n))


def _inproj_kernel(*refs, nw):
    x_ref, g_ref, sh_ref, sc_ref = refs[:4]
    w_refs, o_refs = refs[4:4 + nw], refs[4 + nw:]
    y = _rms(x_ref[...]) * g_ref[...]
    h = (y * (1.0 + sc_ref[...]) + sh_ref[...]).astype(BF16)
    for w_ref, o_ref in zip(w_refs, o_refs):
        o_ref[...] = jnp.dot(h, w_ref[...], preferred_element_type=F32).astype(o_ref.dtype)


def _inproj(xs, g, modm, layer, ws_bf, out_dtype):
    tm = 256
    t = xs.shape[0]
    row = lambda i: layer * 8 + _mod_row(i, tm)
    return pl.pallas_call(
        functools.partial(_inproj_kernel, nw=len(ws_bf)),
        grid=(t // tm,),
        in_specs=[
            pl.BlockSpec((tm, D), lambda i: (i, 0)),
            pl.BlockSpec((1, D), lambda i: (0, 0)),
            pl.BlockSpec((None, 1, D), lambda i: (row(i), 0, 0)),
            pl.BlockSpec((None, 1, D), lambda i: (row(i), 0, 1)),
        ] + [pl.BlockSpec(w.shape, lambda i: (0, 0), pipeline_mode=pl.Buffered(1)) for w in ws_bf],
        out_specs=[pl.BlockSpec((tm, w.shape[1]), lambda i: (i, 0)) for w in ws_bf],
        out_shape=[jax.ShapeDtypeStruct((t, w.shape[1]), out_dtype) for w in ws_bf],
        compiler_params=_cparams(("parallel",), VMEM_LIMIT),
        name="inproj",
    )(xs, g.reshape(1, D), modm, modm, *ws_bf)


def _fnet_chan_kernel(p_ref, cs_ref, zc_ref, zs_ref):
    for g in range(FNET_GROUPS):
        sl = slice(g * FNET_GD, (g + 1) * FNET_GD)
        z = jnp.dot(p_ref[:, sl].astype(BF16), cs_ref[...], preferred_element_type=F32)
        zc_ref[:, sl] = z[:, :FNET_GD].astype(BF16)
        zs_ref[:, sl] = z[:, FNET_GD:].astype(BF16)


def _fnet_chan(p):
    tm = 512
    t = p.shape[0]
    k = np.arange(FNET_GD)
    ang = 2.0 * np.pi * ((k[:, None] * k[None, :]) % FNET_GD) / FNET_GD
    cs = np.concatenate([np.cos(ang), np.sin(ang)], axis=1) / np.sqrt(FNET_GD)
    cs = jnp.asarray(cs, F32).astype(BF16)
    return pl.pallas_call(
        _fnet_chan_kernel,
        grid=(t // tm,),
        in_specs=[
            pl.BlockSpec((tm, HALF), lambda i: (i, 0)),
            pl.BlockSpec((FNET_GD, 2 * FNET_GD), lambda i: (0, 0)),
        ],
        out_specs=[pl.BlockSpec((tm, HALF), lambda i: (i, 0))] * 2,
        out_shape=[jax.ShapeDtypeStruct((t, HALF), BF16)] * 2,
        compiler_params=_cparams(("parallel",), VMEM_LIMIT),
        name="fnet_chan",
    )(p, cs)


def _fnet_pos_kernel(bre_ref, bim_ref, cre_ref, cim_ref, rre_ref, rim_ref, zc_ref, zs_ref, o_ref, acc):
    nt = pl.program_id(2)
    br, bi = bre_ref[...], bim_ref[...]
    cr, ci = cre_ref[...], cim_ref[...]
    rr, ri = rre_ref[...], rim_ref[...]
    tr = br * cr - bi * ci
    ti = br * ci + bi * cr
    er = (tr * rr - ti * ri).astype(BF16)
    ei = (tr * ri + ti * rr).astype(BF16)
    part = (jnp.dot(er, zc_ref[...], preferred_element_type=F32)
            + jnp.dot(ei, zs_ref[...], preferred_element_type=F32))

    @pl.when(nt == 0)
    def _():
        acc[...] = part

    @pl.when(nt > 0)
    def _():
        acc[...] += part

    @pl.when(nt == pl.num_programs(2) - 1)
    def _():
        o_ref[...] = acc[...].astype(BF16)


def _fnet_pos(zc, zs, row0, seq, tk, tn):
    nkt, nnt = seq // tk, seq // tn
    th = 2.0 * np.pi / seq
    kk = np.arange(tk)[:, None]
    nn = np.arange(tn)[None, :]
    base = th * ((kk * nn) % seq)
    n0 = (np.arange(nnt) * tn)[:, None, None]
    col = th * ((np.arange(tk)[None, :, None] * n0) % seq)
    k0 = (np.arange(nkt) * tk)[:, None, None, None]
    nfull = (np.arange(nnt) * tn)[None, :, None, None] + np.arange(tn)[None, None, None, :]
    row = th * ((k0 * nfull) % seq)
    scale = 1.0 / np.sqrt(seq)
    f = lambda a: jnp.asarray(a, F32)
    tabs = (f(np.cos(base)), f(-np.sin(base)), f(np.cos(col)), f(-np.sin(col)),
            f(np.cos(row) * scale), f(-np.sin(row) * scale))
    rb = row0 // tn
    ob = row0 // tk
    return pl.pallas_call(
        _fnet_pos_kernel,
        grid=(B, nkt, nnt),
        in_specs=[
            pl.BlockSpec((tk, tn), lambda b, k, n: (0, 0)),
            pl.BlockSpec((tk, tn), lambda b, k, n: (0, 0)),
            pl.BlockSpec((None, tk, 1), lambda b, k, n: (n, 0, 0)),
            pl.BlockSpec((None, tk, 1), lambda b, k, n: (n, 0, 0)),
            pl.BlockSpec((None, None, 1, tn), lambda b, k, n: (k, n, 0, 0)),
            pl.BlockSpec((None, None, 1, tn), lambda b, k, n: (k, n, 0, 0)),
            pl.BlockSpec((tn, HALF), lambda b, k, n: (rb + b * nnt + n, 0)),
            pl.BlockSpec((tn, HALF), lambda b, k, n: (rb + b * nnt + n, 0)),
        ],
        out_specs=pl.BlockSpec((tk, HALF), lambda b, k, n: (b * nkt + k, 0)),
        out_shape=jax.ShapeDtypeStruct((B * seq, HALF), BF16),
        scratch_shapes=[pltpu.VMEM((tk, HALF), F32)],
        compiler_params=_cparams(("parallel", "parallel", "arbitrary"), VMEM_LIMIT),
        name=f"fnet_pos_{seq}",
    )(*tabs, zc, zs)


def _gmlp_kernel(u_ref, v_ref, vg_ref, ws_ref, bst_ref, o_ref, *, tm):
    zu = jax.nn.gelu(u_ref[...].astype(F32))
    zv = jax.nn.gelu(v_ref[...].astype(F32))
    mu = jnp.mean(zv, axis=-1, keepdims=True)
    dv = zv - mu
    var = jnp.mean(dv * dv, axis=-1, keepdims=True)
    vn = dv * lax.rsqrt(var + EPS) * vg_ref[...]
    hd = HALF // GMLP_HEADS
    for c in range(tm // GMLP_CHUNK):
        rs = slice(c * GMLP_CHUNK, (c + 1) * GMLP_CHUNK)
        for h in range(GMLP_HEADS):
            cs = slice(h * hd, (h + 1) * hd)
            s = jnp.dot(ws_ref[h], vn[rs, cs], precision=HI, preferred_element_type=F32)
            s = s + bst_ref[:, h:h + 1]
            o_ref[rs, cs] = (zu[rs, cs] * s).astype(BF16)


def _gmlp(p, v_g, w_s, b_s):
    tm = 256
    t = p.shape[0]
    return pl.pallas_call(
        functools.partial(_gmlp_kernel, tm=tm),
        grid=(t // tm,),
        in_specs=[
            pl.BlockSpec((tm, HALF), lambda i: (i, 1)),
            pl.BlockSpec((tm, HALF), lambda i: (i, 2)),
            pl.BlockSpec((1, HALF), lambda i: (0, 0)),
            pl.BlockSpec((GMLP_HEADS, GMLP_CHUNK, GMLP_CHUNK), lambda i: (0, 0, 0)),
            pl.BlockSpec((GMLP_CHUNK, GMLP_HEADS), lambda i: (0, 0)),
        ],
        out_specs=pl.BlockSpec((tm, HALF), lambda i: (i, 0)),
        out_shape=jax.ShapeDtypeStruct((t, HALF), BF16),
        compiler_params=_cparams(("parallel",), VMEM_LIMIT),
        name="gmlp",
    )(p, p, v_g.reshape(1, HALF), w_s, b_s.T)


def _outproj_kernel(ma_ref, mb_ref, w_ref, x_ref, g1_ref, g2_ref, ga1_ref, sh2_ref, sc2_ref,
                    rw_ref, rb_ref, xo_ref, h2_ref, e_ref, p_ref, rk_ref, cnt_ref, carry, rw2, *, tm):
    @pl.when(pl.program_id(0) == 0)
    def _():
        carry[...] = jnp.zeros_like(carry)
        rw = rw_ref[...]
        rw_hi = rw.astype(BF16)
        rw2[:, :N_EXPERTS] = rw_hi
        rw2[:, N_EXPERTS:] = (rw - rw_hi.astype(F32)).astype(BF16)

    sub = 128
    iota = lax.broadcasted_iota(jnp.int32, (sub, N_EXPERTS), 1).astype(F32)
    lane4 = lax.broadcasted_iota(jnp.int32, (sub, TOP_K), 1)
    r_i = lax.broadcasted_iota(jnp.int32, (sub, sub), 0)
    c_i = lax.broadcasted_iota(jnp.int32, (sub, sub), 1)
    tri = (c_i < r_i).astype(BF16)
    running = carry[...]
    for r0 in range(0, tm, sub):
        rs = slice(r0, r0 + sub)
        m = (jnp.dot(ma_ref[rs, :], w_ref[:HALF, :], preferred_element_type=F32)
             + jnp.dot(mb_ref[rs, :], w_ref[HALF:, :], preferred_element_type=F32))
        xn = x_ref[rs, :] + ga1_ref[...] * (_rms(m) * g1_ref[...])
        xo_ref[rs, :] = xn
        h2 = (_rms(xn) * g2_ref[...]) * (1.0 + sc2_ref[...]) + sh2_ref[...]
        _store_rows(h2_ref, h2, r0)

        h_hi = h2.astype(BF16)
        h_lo = (h2 - h_hi.astype(F32)).astype(BF16)
        t1 = jnp.dot(h_hi, rw2[...], preferred_element_type=F32)
        t2 = jnp.dot(h_lo, rw2[:, :N_EXPERTS], preferred_element_type=F32)
        lg = t1[:, :N_EXPERTS] + t1[:, N_EXPERTS:] + t2 + rb_ref[...]
        idxs, vals = [], []
        for _ in range(TOP_K):
            mx = jnp.max(lg, axis=-1, keepdims=True)
            ix = jnp.min(jnp.where(lg == mx, iota, float(N_EXPERTS)), axis=-1, keepdims=True)
            idxs.append(ix)
            vals.append(mx)
            lg = jnp.where(iota == ix, -jnp.inf, lg)
        exs = [jnp.exp(v - vals[0]) for v in vals]
        den = exs[0] + exs[1] + exs[2] + exs[3]

        onehot = [(iota == ix) for ix in idxs]
        cnt = jnp.zeros((sub, N_EXPERTS), F32)
        for oh in onehot:
            cnt = cnt + oh.astype(F32)
        prefix = jnp.dot(tri, cnt.astype(BF16), preferred_element_type=F32) + running

        e4 = jnp.zeros((sub, TOP_K), jnp.int32)
        p4 = jnp.zeros((sub, TOP_K), F32)
        r4 = jnp.zeros((sub, TOP_K), jnp.int32)
        for k in range(TOP_K):
            rk = jnp.sum(jnp.where(onehot[k], prefix, 0.0), axis=-1, keepdims=True)
            e4 = jnp.where(lane4 == k, idxs[k].astype(jnp.int32), e4)
            p4 = jnp.where(lane4 == k, exs[k] / den, p4)
            r4 = jnp.where(lane4 == k, rk.astype(jnp.int32), r4)
        e_ref[rs, :] = e4
        p_ref[rs, :] = p4
        rk_ref[rs, :] = r4
        running = running + jnp.sum(cnt, axis=0, keepdims=True)
    carry[...] = running
    cnt_ref[...] = running


def _outproj(ma, mb, w_bf, xs, t, g1, g2, modm, layer, rw, rb):
    tm = 256
    row = lambda i: layer * 8 + _mod_row(i, tm)
    modspec = lambda c: pl.BlockSpec((None, 1, D), lambda i: (row(i), 0, c))
    vec = pl.BlockSpec((1, D), lambda i: (0, 0))
    small = lambda dt: jax.ShapeDtypeStruct((t, TOP_K), dt)
    return pl.pallas_call(
        functools.partial(_outproj_kernel, tm=tm),
        grid=(t // tm,),
        in_specs=[
            pl.BlockSpec((tm, HALF), lambda i: (i, 0)),
            pl.BlockSpec((tm, HALF), lambda i: (i, 0)),
            pl.BlockSpec((D, D), lambda i: (0, 0)),
            pl.BlockSpec((tm, D), lambda i: (i, 0)),
            vec, vec, modspec(2), modspec(3), modspec(4),
            pl.BlockSpec((D, N_EXPERTS), lambda i: (0, 0)),
            pl.BlockSpec((1, N_EXPERTS), lambda i: (0, 0)),
        ],
        out_specs=[
            pl.BlockSpec((tm, D), lambda i: (i, 0)),
            pl.BlockSpec((tm * ROW_S, 128), lambda i: (i, 0)),
            pl.BlockSpec((tm, TOP_K), lambda i: (i, 0)),
            pl.BlockSpec((tm, TOP_K), lambda i: (i, 0)),
            pl.BlockSpec((tm, TOP_K), lambda i: (i, 0)),
            pl.BlockSpec((1, N_EXPERTS), lambda i: (0, 0)),
        ],
        out_shape=[
            jax.ShapeDtypeStruct((t, D), F32),
            jax.ShapeDtypeStruct((t * ROW_S, 128), F32),
            small(jnp.int32), small(F32), small(jnp.int32),
            jax.ShapeDtypeStruct((1, N_EXPERTS), F32),
        ],
        scratch_shapes=[pltpu.VMEM((1, N_EXPERTS), F32), pltpu.VMEM((D, 2 * N_EXPERTS), BF16)],
        compiler_params=_cparams(("arbitrary",), VMEM_LIMIT),
        name="outproj_router",
    )(ma, mb, w_bf, xs, g1.reshape(1, D), g2.reshape(1, D), modm, modm, modm,
      rw, rb.reshape(1, N_EXPERTS))


def _invmap_kernel(pstart_ref, cnt_ref, padded_ref, dest_ref, inv_ref, *, n_pairs, n_slots):
    def unused(lo, hi):
        def body(s, c):
            inv_ref[s] = n_pairs + jnp.bitwise_and(s, 2 * MOE_TM - 1)
            return c
        lax.fori_loop(lo, hi, body, 0)

    def per_expert(e, c):
        unused(pstart_ref[e] + cnt_ref[e], pstart_ref[e] + padded_ref[e])
        return c

    lax.fori_loop(0, N_EXPERTS, per_expert, 0)
    unused(pstart_ref[N_EXPERTS - 1] + padded_ref[N_EXPERTS - 1], n_slots)

    def body(i, c):
        inv_ref[dest_ref[i]] = i
        return c

    lax.fori_loop(0, n_pairs, body, 0, unroll=8)


def _invmap(dest, pstart, cnt, padded, n_slots):
    n_pairs = dest.shape[0]
    return pl.pallas_call(
        functools.partial(_invmap_kernel, n_pairs=n_pairs, n_slots=n_slots),
        grid_spec=pltpu.PrefetchScalarGridSpec(
            num_scalar_prefetch=3,
            grid=(1,),
            in_specs=[pl.BlockSpec(memory_space=pltpu.SMEM)],
            out_specs=pl.BlockSpec(memory_space=pltpu.SMEM),
        ),
        out_shape=jax.ShapeDtypeStruct((n_slots,), jnp.int32),
        compiler_params=_cparams(("arbitrary",)),
        name="moe_invmap",
    )(pstart, cnt, padded, dest)


W_CH = 256
W_NCH_GU = D // W_CH
W_NCH = W_NCH_GU + D_EXPERT // W_CH
W_STAGE = 4


def _expert_kernel(be_ref, nused_ref, first_ref, nxt_ref, lo_ref, hi_ref, slot_ref,
                   inv_cur, inv_next, inv_next2, inv_prev, bgu_ref, bd_ref, h_hbm, wgu_all, wd_all, y_hbm,
                   xbuf, obuf, x_s, act_s, wgu_buf, wd_buf, stage, gsem, ssem, sem, *, layer, n_tok):
    i = pl.program_id(0)
    n_live = nused_ref[0]
    s_cur = lax.rem(i, 3)
    s_p1 = lax.rem(i + 1, 3)
    s_p2 = lax.rem(i + 2, 3)
    wgu_hbm = wgu_all.at[layer]
    wd_hbm = wd_all.at[layer]

    def gather_copy(inv_ref, r, slot):
        tok = jnp.minimum(lax.shift_right_logical(inv_ref[r], 2), n_tok - 1)
        return pltpu.make_async_copy(_row_slab(h_hbm, tok), _row_slab(xbuf.at[slot], r), gsem.at[slot])

    n_pairs = n_tok * TOP_K

    def scatter_copy(inv_ref, r, slot, first_step=False):
        dst = inv_ref[r]
        if first_step:
            dst = jnp.where(i == 0, n_pairs + 2 * MOE_TM + r, dst)
        return pltpu.make_async_copy(_row_slab(obuf.at[slot], r), _row_slab(y_hbm, dst), ssem.at[slot])

    def looped(fn):
        def body(r, c):
            fn(r)
            return c
        lax.fori_loop(0, MOE_TM, body, 0)

    def start_chunk(e, c):
        s = lax.rem(c, W_STAGE)

        @pl.when(c < W_NCH_GU)
        def _():
            r0 = pl.multiple_of(c * W_CH, W_CH)
            pltpu.make_async_copy(wgu_hbm.at[e, pl.ds(r0, W_CH), :], stage.at[s], sem.at[s]).start()

        @pl.when(c >= W_NCH_GU)
        def _():
            r0 = pl.multiple_of((c - W_NCH_GU) * W_CH, W_CH)
            pltpu.make_async_copy(wd_hbm.at[e, pl.ds(r0, W_CH), :], stage.at[s], sem.at[s]).start()

    def finish_chunk(c, dst):
        s = lax.rem(c, W_STAGE)
        pltpu.make_async_copy(wgu_hbm.at[0, pl.ds(0, W_CH), :], stage.at[s], sem.at[s]).wait()
        w = stage[s].astype(BF16)

        @pl.when(c < W_NCH_GU)
        def _():
            wgu_buf[dst, pl.ds(pl.multiple_of(c * W_CH, W_CH), W_CH), :] = w

        @pl.when(c >= W_NCH_GU)
        def _():
            wd_buf[dst, pl.ds(pl.multiple_of((c - W_NCH_GU) * W_CH, W_CH), W_CH), :] = w

    def stream(e, dst, c_lo, c_hi):
        def body(c, carry):
            finish_chunk(c, dst)

            @pl.when(c + W_STAGE < W_NCH)
            def _():
                start_chunk(e, c + W_STAGE)
            return carry
        lax.fori_loop(c_lo, c_hi, body, 0)

    def prime(e):
        for c in range(W_STAGE):
            start_chunk(e, c)

    @pl.when(i == 0)
    def _():
        looped(lambda r: gather_copy(inv_cur, r, 0).start())
        looped(lambda r: gather_copy(inv_next, r, 1).start())
        obuf[2] = jnp.zeros(obuf.shape[1:], F32)
        for region in range(2):
            dump = y_hbm.at[pl.ds((n_pairs + region * MOE_TM) * ROW_S, MOE_TM * ROW_S), :]
            zero_dump = pltpu.make_async_copy(obuf.at[2], dump, ssem.at[2])
            zero_dump.start()
            zero_dump.wait()
        prime(be_ref[0])
        stream(be_ref[0], slot_ref[0], 0, W_NCH)

    @pl.when(i < n_live)
    def _():
        e_next = nxt_ref[i]
        cur = slot_ref[i]

        @pl.when((first_ref[i] == 1) & (e_next >= 0))
        def _():
            prime(e_next)

        def block_body(c_cur, c_p1, c_p2):
            for r in range(MOE_TM):
                gather_copy(inv_cur, r, c_cur).wait()
            for s in range(ROW_S):
                x_s[:, s * 128:(s + 1) * 128] = xbuf[c_cur, pl.ds(s, MOE_TM, stride=ROW_S), :].astype(BF16)
            nc = 256
            for r in range(MOE_TM):
                gather_copy(inv_next2, r, c_p2).start()
                scatter_copy(inv_prev, r, c_p2, first_step=True).start()
            for c0 in range(0, D_EXPERT, nc):
                g = (jnp.dot(x_s[...], wgu_buf[cur, :, c0:c0 + nc], preferred_element_type=F32)
                     + bgu_ref[:, c0:c0 + nc])
                u = (jnp.dot(x_s[...], wgu_buf[cur, :, D_EXPERT + c0:D_EXPERT + c0 + nc],
                             preferred_element_type=F32) + bgu_ref[:, D_EXPERT + c0:D_EXPERT + c0 + nc])
                gate = jnp.minimum(g, SWIGLU_LIMIT)
                up = jnp.clip(u, -SWIGLU_LIMIT, SWIGLU_LIMIT)
                act_s[:, c0:c0 + nc] = (gate * jax.nn.sigmoid(SWIGLU_ALPHA * gate) * (up + 1.0)).astype(BF16)

            for c0 in range(0, D, nc):
                y = (jnp.dot(act_s[...], wd_buf[cur, :, c0:c0 + nc], preferred_element_type=F32)
                     + bd_ref[:, c0:c0 + nc])
                for j in range(nc // 128):
                    s = c0 // 128 + j
                    obuf[c_cur, pl.ds(s, MOE_TM, stride=ROW_S), :] = y[:, j * 128:(j + 1) * 128]

            @pl.when(i >= 1)
            def _():
                for r in range(MOE_TM):
                    scatter_copy(inv_prev, r, c_p1).wait()

        for ring in range(3):
            @pl.when(s_cur == ring)
            def _():
                block_body(ring, (ring + 1) % 3, (ring + 2) % 3)

        @pl.when(e_next >= 0)
        def _():
            stream(e_next, 1 - cur, lo_ref[i], hi_ref[i])

        @pl.when(i == n_live - 1)
        def _():
            looped(lambda r: scatter_copy(inv_prev, r, s_p2).wait())
            looped(lambda r: scatter_copy(inv_cur, r, s_cur).start())
            looped(lambda r: scatter_copy(inv_cur, r, s_cur).wait())
            looped(lambda r: gather_copy(inv_next, r, s_p1).wait())
            looped(lambda r: gather_copy(inv_next2, r, s_p2).wait())


def _take(table, idx):
    ids = jnp.arange(table.shape[0], dtype=jnp.int32)
    return jnp.sum(jnp.where(idx[..., None] == ids, table, 0), axis=-1)


def _experts(h2, inv, block_e, nused, pstart, padded, layer, wgu, bgu, wd, bd, nblk, n_tok):
    nb_e = padded // MOE_TM
    pos = jnp.arange(nblk, dtype=jnp.int32) - _take(pstart // MOE_TM, block_e)
    nb = jnp.maximum(_take(nb_e, block_e), 1)
    lo = (pos * W_NCH) // nb
    hi = ((pos + 1) * W_NCH) // nb
    first = (pos == 0).astype(jnp.int32)
    eid = jnp.arange(N_EXPERTS, dtype=jnp.int32)
    later = jnp.where(nb_e > 0, eid, N_EXPERTS)
    nxt_e = jnp.concatenate([lax.cummin(later, reverse=True)[1:], jnp.full((1,), N_EXPERTS, jnp.int32)])
    nxt_e = jnp.where(nxt_e >= N_EXPERTS, -1, nxt_e)
    slot_e = (jnp.cumsum((nb_e > 0).astype(jnp.int32)) - 1) % 2
    i32 = lambda a: a.astype(jnp.int32)
    live = lambda i, nu: jnp.clip(i, 0, nu[0] - 1)
    sblk = lambda off: pl.BlockSpec((MOE_TM,), lambda i, be, nu, *_: (live(i + off, nu),),
                                    memory_space=pltpu.SMEM)
    bias = lambda n: pl.BlockSpec((None, 1, n), lambda i, be, nu, *_: (be[live(i, nu)], 0, 0))
    hbm = pl.BlockSpec(memory_space=pl.ANY)
    n_slabs = n_tok * TOP_K + 3 * MOE_TM
    return pl.pallas_call(
        functools.partial(_expert_kernel, layer=layer, n_tok=n_tok),
        grid_spec=pltpu.PrefetchScalarGridSpec(
            num_scalar_prefetch=7,
            grid=(nblk,),
            in_specs=[sblk(0), sblk(1), sblk(2), sblk(-1), bias(2 * D_EXPERT), bias(D), hbm, hbm, hbm],
            out_specs=hbm,
            scratch_shapes=[
                pltpu.VMEM((3, MOE_TM * ROW_S, 128), F32),
                pltpu.VMEM((3, MOE_TM * ROW_S, 128), F32),
                pltpu.VMEM((MOE_TM, D), BF16),
                pltpu.VMEM((MOE_TM, D_EXPERT), BF16),
                pltpu.VMEM((2, D, 2 * D_EXPERT), BF16),
                pltpu.VMEM((2, D_EXPERT, D), BF16),
                pltpu.VMEM((W_STAGE, W_CH, D), F32),
                pltpu.SemaphoreType.DMA((3,)),
                pltpu.SemaphoreType.DMA((3,)),
                pltpu.SemaphoreType.DMA((W_STAGE,)),
            ],
        ),
        out_shape=jax.ShapeDtypeStruct((n_slabs * ROW_S, 128), F32),
        compiler_params=_cparams(("arbitrary",), VMEM_LIMIT),
        name="moe_experts",
    )(block_e, nused, i32(first), i32(_take(nxt_e, block_e)), i32(lo), i32(hi), i32(_take(slot_e, block_e)),
      inv, inv, inv, inv, bgu.reshape(N_EXPERTS, 1, -1), bd.reshape(N_EXPERTS, 1, -1), h2, wgu, wd)


def _combine_kernel(p_ref, x_ref, g3_ref, ga2_ref, y_ref, xo_ref, acc, *, tm):
    pair = TOP_K * ROW_S
    for r in range(tm):
        a = y_ref[pl.ds(r * pair, ROW_S), :] * p_ref[r * TOP_K]
        for k in range(1, TOP_K):
            a = a + y_ref[pl.ds(r * pair + k * ROW_S, ROW_S), :] * p_ref[r * TOP_K + k]
        acc[pl.ds(r * ROW_S, ROW_S), :] = a
    y = _load_rows(acc, tm)
    xo_ref[...] = x_ref[...] + ga2_ref[...] * (_rms(y) * g3_ref[...])


def _combine(y4, p4, xs, g3, modm, layer):
    tm = 128
    t = xs.shape[0]
    row = lambda i: layer * 8 + _mod_row(i, tm)
    return pl.pallas_call(
        functools.partial(_combine_kernel, tm=tm),
        grid=(t // tm,),
        in_specs=[
            pl.BlockSpec((tm * TOP_K,), lambda i: (i,), memory_space=pltpu.SMEM),
            pl.BlockSpec((tm, D), lambda i: (i, 0)),
            pl.BlockSpec((1, D), lambda i: (0, 0)),
            pl.BlockSpec((None, 1, D), lambda i: (row(i), 0, 5)),
            pl.BlockSpec((tm * TOP_K * ROW_S, 128), lambda i: (i, 0)),
        ],
        out_specs=pl.BlockSpec((tm, D), lambda i: (i, 0)),
        out_shape=jax.ShapeDtypeStruct((t, D), F32),
        scratch_shapes=[pltpu.VMEM((tm * ROW_S, 128), F32)],
        compiler_params=_cparams(("parallel",), VMEM_LIMIT),
        name="moe_combine",
    )(p4.reshape(-1), xs, g3.reshape(1, D), modm, y4)


def _moe(h2, e4, p4, r4, counts, xs, g3, modm, layer, wgu, bgu, wd, bd):
    t = e4.shape[0]
    nblk = -(-(t * TOP_K + N_EXPERTS * (MOE_TM - 1)) // MOE_TM)
    cnt = counts.reshape(N_EXPERTS).astype(jnp.int32)
    padded = (cnt + MOE_TM - 1) // MOE_TM * MOE_TM
    pend = jnp.cumsum(padded)
    pstart = pend - padded
    dest = (_take(pstart, e4) + r4).reshape(-1).astype(jnp.int32)
    nused = (pend[-1:] // MOE_TM).astype(jnp.int32)
    blk_start = jnp.arange(nblk, dtype=jnp.int32) * MOE_TM
    block_e = jnp.minimum(jnp.sum((pend[None, :] <= blk_start[:, None]).astype(jnp.int32), axis=1),
                          N_EXPERTS - 1).astype(jnp.int32)
    inv = _invmap(dest, pstart.astype(jnp.int32), cnt, padded.astype(jnp.int32), nblk * MOE_TM)
    y4 = _experts(h2, inv, block_e, nused, pstart, padded, layer, wgu, bgu, wd, bd, nblk, t)
    return _combine(y4, p4, xs, g3, modm, layer)


N_CTX_BLK = LC // HGRN_R
N_LAT_BLK = L // HGRN_R
N_SEQ_BLK = N_CTX_BLK + N_LAT_BLK


def _scan_block(b, s, rev):
    if rev:
        return jnp.where(s < N_CTX_BLK, T_LAT // HGRN_R + N_CTX_BLK * b + (N_CTX_BLK - 1 - s),
                         N_LAT_BLK * b + (N_SEQ_BLK - 1 - s))
    return jnp.where(s < N_CTX_BLK, T_LAT // HGRN_R + N_CTX_BLK * b + s, N_LAT_BLK * b + (s - N_CTX_BLK))


def _hgrn_kernel(*refs, rev):
    if rev:
        q_ref, f_ref, v_ref, lb_ref, of_ref, g_ref, on_ref, o_ref, st, qin_s, qd_s, ki_s, ke_s, v_s = refs
    else:
        q_ref, f_ref, v_ref, lb_ref, o_ref, st, qin_s, qd_s, ki_s, ke_s, v_s = refs

    @pl.when(pl.program_id(1) == 0)
    def _():
        st[...] = jnp.zeros_like(st)

    c = HGRN_C
    nc = HGRN_R // c
    r_i = lax.broadcasted_iota(jnp.int32, (c, c), 0)
    c_i = lax.broadcasted_iota(jnp.int32, (c, c), 1)
    mask = (c_i >= r_i) if rev else (c_i <= r_i)
    tri = mask.astype(F32)

    lb = lb_ref[...]
    decay = [None] * nc
    v_s[...] = v_ref[...].astype(BF16)
    for ci in range(nc):
        rs = slice(ci * c, (ci + 1) * c)
        sg = jax.nn.sigmoid(f_ref[rs, :])
        logf = jnp.log(lb + (1.0 - lb) * sg)
        kk = (1.0 - lb) * (1.0 - sg)
        q = q_ref[rs, :]
        qs = q * jax.nn.sigmoid(q)
        cum = jnp.dot(tri, logf, precision=HI, preferred_element_type=F32)
        total = cum[0:1] if rev else cum[c - 1:c]
        mid = cum[c // 2:c // 2 + 1]
        qin_s[rs, :] = (qs * jnp.exp(cum)).astype(BF16)
        qd_s[rs, :] = (qs * jnp.exp(cum - mid)).astype(BF16)
        ki_s[rs, :] = (kk * jnp.exp(mid - cum)).astype(BF16)
        ke_s[rs, :] = (kk * jnp.exp(total - cum)).astype(BF16)
        decay[ci] = jnp.exp(total)

    order = range(nc - 1, -1, -1) if rev else range(nc)
    for ci in order:
        rs = slice(ci * c, (ci + 1) * c)
        for h in range(HGRN_HEADS):
            hs = slice(h * 128, (h + 1) * 128)
            state = st[h]
            vb = v_s[rs, hs]
            sc = lax.dot_general(qd_s[rs, hs], ki_s[rs, hs], (((1,), (1,)), ((), ())),
                                 preferred_element_type=F32)
            sc = jnp.where(mask, sc, 0.0)
            intra = jnp.dot(sc.astype(BF16), vb, preferred_element_type=F32)
            inter = lax.dot_general(qin_s[rs, hs], state.astype(BF16), (((1,), (1,)), ((), ())),
                                    preferred_element_type=F32)
            st[h] = state * decay[ci][:, hs] + lax.dot_general(
                vb, ke_s[rs, hs], (((0,), (0,)), ((), ())), preferred_element_type=F32)
            o = intra + inter
            if rev:
                o = o + of_ref[rs, hs]
                g = g_ref[rs, hs]
                o_ref[rs, hs] = (_rms(o) * on_ref[:, hs] * (g * jax.nn.sigmoid(g))).astype(BF16)
            else:
                o_ref[rs, hs] = o


def _hgrn(pa, pb, lb, rev, of=None, onorm=None):
    r = HGRN_R
    blk = lambda b, s: _scan_block(b, s, rev)
    col = lambda c: pl.BlockSpec((r, HALF), lambda b, s: (blk(b, s), c))
    vec = pl.BlockSpec((1, HALF), lambda b, s: (0, 0))
    in_specs = [col(0), col(2 if rev else 1), col(3), vec]
    args = [pa, pa, pa, lb.reshape(1, HALF)]
    if rev:
        in_specs += [col(0), col(0), vec]
        args += [of, pb, onorm.reshape(1, HALF)]
    return pl.pallas_call(
        functools.partial(_hgrn_kernel, rev=rev),
        grid=(B, N_SEQ_BLK),
        in_specs=in_specs,
        out_specs=col(0),
        out_shape=jax.ShapeDtypeStruct((T_ALL, HALF), BF16 if rev else F32),
        scratch_shapes=[pltpu.VMEM((HGRN_HEADS, 128, 128), F32)] + [pltpu.VMEM((r, HALF), BF16)] * 5,
        compiler_params=_cparams(("parallel", "arbitrary"), VMEM_LIMIT),
        name="hgrn_bwd" if rev else "hgrn_fwd",
    )(*args)


def _lru_kernel(*refs, rev):
    if rev:
        (x_ref, xp_ref, xn_ref, cw_ref, cb_ref, wr_ref, br_ref, wi_ref, bi_ref, lam_ref,
         hf_ref, gt_ref, o_ref, xcat, a_s, u_s, h_s, carry) = refs
    else:
        (x_ref, xp_ref, xn_ref, cw_ref, cb_ref, wr_ref, br_ref, wi_ref, bi_ref, lam_ref,
         o_ref, xcat, a_s, u_s, h_s, carry) = refs
    s = pl.program_id(1)
    r = LRU_R

    @pl.when(s == 0)
    def _():
        carry[...] = jnp.zeros_like(carry)

    is_ctx = s < N_CTX_BLK
    if rev:
        j = jnp.where(is_ctx, N_CTX_BLK - 1 - s, N_SEQ_BLK - 1 - s)
    else:
        j = jnp.where(is_ctx, s, s - N_CTX_BLK)
    nb = jnp.where(is_ctx, N_CTX_BLK, N_LAT_BLK)
    xcat[0:8, :] = jnp.where(j == 0, 0.0, xp_ref[...])
    xcat[8:8 + r, :] = x_ref[...]
    xcat[8 + r:16 + r, :] = jnp.where(j == nb - 1, 0.0, xn_ref[...])
    xc = cb_ref[...] + xcat[pl.ds(6, r), :] * cw_ref[0:1, :]
    for t in range(1, 4):
        xc = xc + xcat[pl.ds(6 + t, r), :] * cw_ref[t:t + 1, :]

    lam = lam_ref[...]
    sp = jnp.maximum(-lam, 0.0) + jnp.log1p(jnp.exp(-jnp.abs(lam)))
    hd = HALF // LRU_HEADS
    for h in range(LRU_HEADS):
        cs = slice(h * hd, (h + 1) * hd)
        xh = xc[:, cs]
        rg = jax.nn.sigmoid(_bdot(xh, wr_ref[h]) + br_ref[:, cs])
        ig = jax.nn.sigmoid(_bdot(xh, wi_ref[h]) + bi_ref[:, cs])
        log_a = -LRU_C * rg * sp[:, cs]
        a = jnp.exp(log_a)
        a_s[:, cs] = a
        u_s[:, cs] = jnp.sqrt(-jnp.tanh(log_a) * (a * a + 1.0)) * (ig * xh)

    def step(t, h):
        tt = (r - 1 - t) if rev else t
        h = a_s[pl.ds(tt, 1), :] * h + u_s[pl.ds(tt, 1), :]
        h_s[pl.ds(tt, 1), :] = h
        return h

    carry[...] = lax.fori_loop(0, r, step, carry[...], unroll=8)
    if rev:
        o_ref[...] = ((h_s[...] + hf_ref[...]) * jax.nn.gelu(gt_ref[...])).astype(BF16)
    else:
        o_ref[...] = h_s[...]


def _lru(p, conv_w, conv_b, w_r, b_r, w_i, b_i, lam, rev, hf=None):
    r = LRU_R
    d = 1 if rev else 0
    blk = lambda b, s: _scan_block(b, s, rev)
    xcol = 2
    n8 = T_ALL // 8
    vec = lambda: pl.BlockSpec((None, 1, HALF), lambda b, s: (d, 0, 0))
    wspec = lambda: pl.BlockSpec((None, LRU_HEADS, 128, 128), lambda b, s: (d, 0, 0, 0))
    in_specs = [
        pl.BlockSpec((r, HALF), lambda b, s: (blk(b, s), xcol)),
        pl.BlockSpec((8, HALF), lambda b, s: (jnp.maximum(blk(b, s) * (r // 8) - 1, 0), xcol)),
        pl.BlockSpec((8, HALF), lambda b, s: (jnp.minimum((blk(b, s) + 1) * (r // 8), n8 - 1), xcol)),
        pl.BlockSpec((4, HALF), lambda b, s: (0, 0)),
        pl.BlockSpec((1, HALF), lambda b, s: (0, 0)),
        wspec(), vec(), wspec(), vec(), vec(),
    ]
    args = [p, p, p, conv_w, conv_b.reshape(1, HALF), w_r, b_r.reshape(2, 1, HALF), w_i,
            b_i.reshape(2, 1, HALF), lam.reshape(2, 1, HALF)]
    if rev:
        in_specs += [pl.BlockSpec((r, HALF), lambda b, s: (blk(b, s), 0)),
                     pl.BlockSpec((r, HALF), lambda b, s: (blk(b, s), xcol - 1))]
        args += [hf, p]
    return pl.pallas_call(
        functools.partial(_lru_kernel, rev=rev),
        grid=(B, N_SEQ_BLK),
        in_specs=in_specs,
        out_specs=pl.BlockSpec((r, HALF), lambda b, s: (blk(b, s), 0)),
        out_shape=jax.ShapeDtypeStruct((T_ALL, HALF), BF16 if rev else F32),
        scratch_shapes=[pltpu.VMEM((r + 16, HALF), F32), pltpu.VMEM((r, HALF), F32),
                        pltpu.VMEM((r, HALF), F32), pltpu.VMEM((r, HALF), F32),
                        pltpu.VMEM((1, HALF), F32)],
        compiler_params=_cparams(("parallel", "arbitrary"), VMEM_LIMIT),
        name="lru_bwd" if rev else "lru_fwd",
    )(*args)


def kernel(x, c, ctx, c_ctx, mod_w, mod_b, norm_g, ab_w_in, ab_w_out, gmlp_v_g, gmlp_ws, gmlp_bs, cd_w_in, cd_w_out, hgrn_lb, hgrn_onorm_g, lru_conv_w, lru_conv_b, lru_wr, lru_br, lru_wi, lru_bi, lru_lambda, router_w, router_b, exp_w_gu, exp_b_gu, exp_w_down, exp_b_down):
    cond8 = jnp.concatenate([c, c_ctx[None, :], jnp.zeros((5, D), F32)], axis=0)
    modm = _modulation(cond8, mod_w, mod_b).reshape(DEPTH * 8, 1, 6 * D)
    xs = jnp.concatenate([x.reshape(T_LAT, D), ctx.reshape(T_CTX, D)], axis=0)

    lb_soft = jax.nn.softmax(hgrn_lb.astype(F32), axis=0)
    lb_all = jnp.cumsum(lb_soft, axis=0) - lb_soft[0]

    (p0,) = _inproj(xs, norm_g[0, 0], modm, 0, [ab_w_in[0].astype(BF16)], BF16)
    zc, zs = _fnet_chan(p0)
    mix_a = jnp.concatenate([_fnet_pos(zc, zs, 0, L, 512, 1024),
                             _fnet_pos(zc, zs, T_LAT, LC, LC, LC)], axis=0)
    mix_g = _gmlp(p0, gmlp_v_g[0], gmlp_ws[0], gmlp_bs[0])
    xs, h2, e4, p4, r4, counts = _outproj(mix_a, mix_g, ab_w_out[0].astype(BF16), xs, T_ALL, norm_g[0, 1],
                                          norm_g[0, 2], modm, 0, router_w[0], router_b[0])
    xs = _moe(h2, e4, p4, r4, counts, xs, norm_g[0, 3], modm, 0,
              exp_w_gu, exp_b_gu[0], exp_w_down, exp_b_down[0])

    w_cd = cd_w_in[0].astype(BF16)
    p1a, p1b = _inproj(xs, norm_g[1, 0], modm, 1, [w_cd[:, :4 * HALF], w_cd[:, 4 * HALF:]], F32)
    o_f = _hgrn(p1a, p1b, lb_all[1], False)
    hg = _hgrn(p1a, p1b, lb_all[1], True, o_f, hgrn_onorm_g[0])
    lru_args = (lru_conv_w[0], lru_conv_b[0], lru_wr[0], lru_br[0], lru_wi[0], lru_bi[0], lru_lambda[0])
    h_f = _lru(p1b, *lru_args, False)
    lr = _lru(p1b, *lru_args, True, h_f)
    xl, h2, e4, p4, r4, counts = _outproj(hg, lr, cd_w_out[0].astype(BF16), xs, T_LAT, norm_g[1, 1],
                                          norm_g[1, 2], modm, 1, router_w[1], router_b[1])
    xl = _moe(h2, e4, p4, r4, counts, xl, norm_g[1, 3], modm, 1,
              exp_w_gu, exp_b_gu[1], exp_w_down, exp_b_down[1])
    return xl.reshape(B, L, D)
```

```python
import functools

import numpy as np
import jax
import jax.numpy as jnp
from jax import lax
from jax.experimental import pallas as pl
from jax.experimental.pallas import tpu as pltpu

F32 = jnp.float32
BF16 = jnp.bfloat16
HI = lax.Precision.HIGHEST

D = 2048
B = 2
L = 4096
LC = 256
T_LAT = B * L
T_CTX = B * LC
T_ALL = T_LAT + T_CTX
DEPTH = 2
EPS = 1e-6

HALF = D // 2
N_EXPERTS = 32
TOP_K = 4
D_EXPERT = D // 2
SWIGLU_LIMIT = 7.0
SWIGLU_ALPHA = 1.702
MOE_TM = 256

FNET_GROUPS = 4
FNET_GD = HALF // FNET_GROUPS
GMLP_HEADS = 8
GMLP_CHUNK = 128
HGRN_HEADS = 8
HGRN_R = 128
HGRN_C = 64
LRU_HEADS = 8
LRU_R = 128
LRU_C = 8.0

VMEM_LIMIT = 56 * 1024 * 1024


def _cparams(sem, vmem=None):
    return pltpu.CompilerParams(dimension_semantics=sem, vmem_limit_bytes=vmem)


def _mod_row(i, tm):
    return jnp.where(i < L // tm, 0, jnp.where(i < 2 * L // tm, 1, 2))


def _rms(x):
    return x * lax.rsqrt(jnp.mean(x * x, axis=-1, keepdims=True) + EPS)


def _bdot(a, b):
    return jnp.dot(a.astype(BF16), b.astype(BF16), preferred_element_type=F32)


ROW_S = D // 256
U32 = jnp.uint32
HI_MASK = 0xFFFF0000


def _row_slab(ref, r):
    return ref.at[pl.ds(pl.multiple_of(r * ROW_S, ROW_S), ROW_S), :]


def _pack_pieces(hi, lo):
    hb = pltpu.bitcast(hi.astype(BF16).astype(F32), U32)
    lb = pltpu.bitcast(lo.astype(BF16).astype(F32), U32)
    return jnp.bitwise_or(jnp.bitwise_and(hb, U32(HI_MASK)), jnp.right_shift(lb, U32(16)))


def _unpack_piece(w):
    hi = pltpu.bitcast(jnp.bitwise_and(w, U32(HI_MASK)), F32)
    lo = pltpu.bitcast(jnp.left_shift(w, U32(16)), F32)
    return hi, lo


def _store_rows(ref, val, row0=0):
    n = val.shape[0]
    for s in range(ROW_S):
        ref[pl.ds(row0 * ROW_S + s, n, stride=ROW_S), :] = _pack_pieces(
            val[:, s * 128:(s + 1) * 128], val[:, (s + ROW_S) * 128:(s + ROW_S + 1) * 128])


def _mod_kernel(c_ref, w_ref, b_ref, o_ref):
    cnd = c_ref[...]
    s = cnd * jax.nn.sigmoid(cnd)
    o_ref[...] = _bdot(s, w_ref[...]) + b_ref[...]


def _modulation(cond8, mod_w, mod_b):
    tn = 1024
    n = mod_w.shape[-1]
    return pl.pallas_call(
        _mod_kernel,
        grid=(DEPTH, n // tn),
        in_specs=[
            pl.BlockSpec((8, D), lambda l, j: (0, 0)),
            pl.BlockSpec((None, D, tn), lambda l, j: (l, 0, j)),
            pl.BlockSpec((None, 1, tn), lambda l, j: (l, 0, j)),
        ],
        out_specs=pl.BlockSpec((None, 8, tn), lambda l, j: (l, 0, j)),
        out_shape=jax.ShapeDtypeStruct((DEPTH, 8, n), F32),
        compiler_params=_cparams(("parallel", "parallel"), VMEM_LIMIT),
        name="modulation",
    )(cond8, mod_w, mod_b.reshape(DEPTH, 1, n))


def _inproj_kernel(*refs, nw):
    x_ref, g_ref, sh_ref, sc_ref = refs[:4]
    w_refs, o_refs = refs[4:4 + nw], refs[4 + nw:]
    y = _rms(x_ref[...]) * g_ref[...]
    h = (y * (1.0 + sc_ref[...]) + sh_ref[...]).astype(BF16)
    for w_ref, o_ref in zip(w_refs, o_refs):
        o_ref[...] = jnp.dot(h, w_ref[...], preferred_element_type=F32).astype(o_ref.dtype)


def _inproj(xs, g, modm, layer, ws_bf, out_dtype):
    tm = 256
    t = xs.shape[0]
    row = lambda i: layer * 8 + _mod_row(i, tm)
    return pl.pallas_call(
        functools.partial(_inproj_kernel, nw=len(ws_bf)),
        grid=(t // tm,),
        in_specs=[
            pl.BlockSpec((tm, D), lambda i: (i, 0)),
            pl.BlockSpec((1, D), lambda i: (0, 0)),
            pl.BlockSpec((None, 1, D), lambda i: (row(i), 0, 0)),
            pl.BlockSpec((None, 1, D), lambda i: (row(i), 0, 1)),
        ] + [pl.BlockSpec(w.shape, lambda i: (0, 0), pipeline_mode=pl.Buffered(1)) for w in ws_bf],
        out_specs=[pl.BlockSpec((tm, w.shape[1]), lambda i: (i, 0)) for w in ws_bf],
        out_shape=[jax.ShapeDtypeStruct((t, w.shape[1]), out_dtype) for w in ws_bf],
        compiler_params=_cparams(("parallel",), VMEM_LIMIT),
        name="inproj",
    )(xs, g.reshape(1, D), modm, modm, *ws_bf)


def _fnet_chan_kernel(p_ref, cs_ref, zc_ref, zs_ref):
    for g in range(FNET_GROUPS):
        sl = slice(g * FNET_GD, (g + 1) * FNET_GD)
        z = jnp.dot(p_ref[:, sl].astype(BF16), cs_ref[...], preferred_element_type=F32)
        zc_ref[:, sl] = z[:, :FNET_GD].astype(BF16)
        zs_ref[:, sl] = z[:, FNET_GD:].astype(BF16)


def _fnet_chan(p):
    tm = 512
    t = p.shape[0]
    k = np.arange(FNET_GD)
    ang = 2.0 * np.pi * ((k[:, None] * k[None, :]) % FNET_GD) / FNET_GD
    cs = np.concatenate([np.cos(ang), np.sin(ang)], axis=1) / np.sqrt(FNET_GD)
    cs = jnp.asarray(cs, F32).astype(BF16)
    return pl.pallas_call(
        _fnet_chan_kernel,
        grid=(t // tm,),
        in_specs=[
            pl.BlockSpec((tm, HALF), lambda i: (i, 0)),
            pl.BlockSpec((FNET_GD, 2 * FNET_GD), lambda i: (0, 0)),
        ],
        out_specs=[pl.BlockSpec((tm, HALF), lambda i: (i, 0))] * 2,
        out_shape=[jax.ShapeDtypeStruct((t, HALF), BF16)] * 2,
        compiler_params=_cparams(("parallel",), VMEM_LIMIT),
        name="fnet_chan",
    )(p, cs)


def _fnet_pos_kernel(bre_ref, bim_ref, cre_ref, cim_ref, rre_ref, rim_ref, zc_ref, zs_ref, o_ref, acc):
    nt = pl.program_id(2)
    br, bi = bre_ref[...], bim_ref[...]
    cr, ci = cre_ref[...], cim_ref[...]
    rr, ri = rre_ref[...], rim_ref[...]
    tr = br * cr - bi * ci
    ti = br * ci + bi * cr
    er = (tr * rr - ti * ri).astype(BF16)
    ei = (tr * ri + ti * rr).astype(BF16)
    part = (jnp.dot(er, zc_ref[...], preferred_element_type=F32)
            + jnp.dot(ei, zs_ref[...], preferred_element_type=F32))

    @pl.when(nt == 0)
    def _():
        acc[...] = part

    @pl.when(nt > 0)
    def _():
        acc[...] += part

    @pl.when(nt == pl.num_programs(2) - 1)
    def _():
        o_ref[...] = acc[...].astype(BF16)


def _fnet_pos(zc, zs, row0, seq, tk, tn):
    nkt, nnt = seq // tk, seq // tn
    th = 2.0 * np.pi / seq
    kk = np.arange(tk)[:, None]
    nn = np.arange(tn)[None, :]
    base = th * ((kk * nn) % seq)
    n0 = (np.arange(nnt) * tn)[:, None, None]
    col = th * ((np.arange(tk)[None, :, None] * n0) % seq)
    k0 = (np.arange(nkt) * tk)[:, None, None, None]
    nfull = (np.arange(nnt) * tn)[None, :, None, None] + np.arange(tn)[None, None, None, :]
    row = th * ((k0 * nfull) % seq)
    scale = 1.0 / np.sqrt(seq)
    f = lambda a: jnp.asarray(a, F32)
    tabs = (f(np.cos(base)), f(-np.sin(base)), f(np.cos(col)), f(-np.sin(col)),
            f(np.cos(row) * scale), f(-np.sin(row) * scale))
    rb = row0 // tn
    ob = row0 // tk
    return pl.pallas_call(
        _fnet_pos_kernel,
        grid=(B, nkt, nnt),
        in_specs=[
            pl.BlockSpec((tk, tn), lambda b, k, n: (0, 0)),
            pl.BlockSpec((tk, tn), lambda b, k, n: (0, 0)),
            pl.BlockSpec((None, tk, 1), lambda b, k, n: (n, 0, 0)),
            pl.BlockSpec((None, tk, 1), lambda b, k, n: (n, 0, 0)),
            pl.BlockSpec((None, None, 1, tn), lambda b, k, n: (k, n, 0, 0)),
            pl.BlockSpec((None, None, 1, tn), lambda b, k, n: (k, n, 0, 0)),
            pl.BlockSpec((tn, HALF), lambda b, k, n: (rb + b * nnt + n, 0)),
            pl.BlockSpec((tn, HALF), lambda b, k, n: (rb + b * nnt + n, 0)),
        ],
        out_specs=pl.BlockSpec((tk, HALF), lambda b, k, n: (b * nkt + k, 0)),
        out_shape=jax.ShapeDtypeStruct((B * seq, HALF), BF16),
        scratch_shapes=[pltpu.VMEM((tk, HALF), F32)],
        compiler_params=_cparams(("parallel", "parallel", "arbitrary"), VMEM_LIMIT),
        name=f"fnet_pos_{seq}",
    )(*tabs, zc, zs)


def _gmlp_kernel(u_ref, v_ref, vg_ref, ws_ref, bst_ref, o_ref, *, tm):
    zu = jax.nn.gelu(u_ref[...].astype(F32))
    zv = jax.nn.gelu(v_ref[...].astype(F32))
    mu = jnp.mean(zv, axis=-1, keepdims=True)
    dv = zv - mu
    var = jnp.mean(dv * dv, axis=-1, keepdims=True)
    vn = dv * lax.rsqrt(var + EPS) * vg_ref[...]
    hd = HALF // GMLP_HEADS
    for c in range(tm // GMLP_CHUNK):
        rs = slice(c * GMLP_CHUNK, (c + 1) * GMLP_CHUNK)
        for h in range(GMLP_HEADS):
            cs = slice(h * hd, (h + 1) * hd)
            s = jnp.dot(ws_ref[h], vn[rs, cs], precision=HI, preferred_element_type=F32)
            s = s + bst_ref[:, h:h + 1]
            o_ref[rs, cs] = (zu[rs, cs] * s).astype(BF16)


def _gmlp(p, v_g, w_s, b_s):
    tm = 256
    t = p.shape[0]
    return pl.pallas_call(
        functools.partial(_gmlp_kernel, tm=tm),
        grid=(t // tm,),
        in_specs=[
            pl.BlockSpec((tm, HALF), lambda i: (i, 1)),
            pl.BlockSpec((tm, HALF), lambda i: (i, 2)),
            pl.BlockSpec((1, HALF), lambda i: (0, 0)),
            pl.BlockSpec((GMLP_HEADS, GMLP_CHUNK, GMLP_CHUNK), lambda i: (0, 0, 0)),
            pl.BlockSpec((GMLP_CHUNK, GMLP_HEADS), lambda i: (0, 0)),
        ],
        out_specs=pl.BlockSpec((tm, HALF), lambda i: (i, 0)),
        out_shape=jax.ShapeDtypeStruct((t, HALF), BF16),
        compiler_params=_cparams(("parallel",), VMEM_LIMIT),
        name="gmlp",
    )(p, p, v_g.reshape(1, HALF), w_s, b_s.T)


def _outproj_kernel(ma_ref, mb_ref, w_ref, x_ref, g1_ref, g2_ref, ga1_ref, sh2_ref, sc2_ref,
                    rw_ref, rb_ref, xo_ref, h2_ref, e_ref, p_ref, rk_ref, cnt_ref, carry, rw2, *, tm):
    @pl.when(pl.program_id(0) == 0)
    def _():
        carry[...] = jnp.zeros_like(carry)
        rw = rw_ref[...]
        rw_hi = rw.astype(BF16)
        rw2[:, :N_EXPERTS] = rw_hi
        rw2[:, N_EXPERTS:] = (rw - rw_hi.astype(F32)).astype(BF16)

    sub = 128
    iota = lax.broadcasted_iota(jnp.int32, (sub, N_EXPERTS), 1).astype(F32)
    lane4 = lax.broadcasted_iota(jnp.int32, (sub, TOP_K), 1)
    r_i = lax.broadcasted_iota(jnp.int32, (sub, sub), 0)
    c_i = lax.broadcasted_iota(jnp.int32, (sub, sub), 1)
    tri = (c_i < r_i).astype(BF16)
    running = carry[...]
    for r0 in range(0, tm, sub):
        rs = slice(r0, r0 + sub)
        m = (jnp.dot(ma_ref[rs, :], w_ref[:HALF, :], preferred_element_type=F32)
             + jnp.dot(mb_ref[rs, :], w_ref[HALF:, :], preferred_element_type=F32))
        xn = x_ref[rs, :] + ga1_ref[...] * (_rms(m) * g1_ref[...])
        xo_ref[rs, :] = xn
        h2 = (_rms(xn) * g2_ref[...]) * (1.0 + sc2_ref[...]) + sh2_ref[...]
        _store_rows(h2_ref, h2, r0)

        h_hi = h2.astype(BF16)
        h_lo = (h2 - h_hi.astype(F32)).astype(BF16)
        t1 = jnp.dot(h_hi, rw2[...], preferred_element_type=F32)
        t2 = jnp.dot(h_lo, rw2[:, :N_EXPERTS], preferred_element_type=F32)
        lg = t1[:, :N_EXPERTS] + t1[:, N_EXPERTS:] + t2 + rb_ref[...]
        idxs, vals = [], []
        for _ in range(TOP_K):
            mx = jnp.max(lg, axis=-1, keepdims=True)
            ix = jnp.min(jnp.where(lg == mx, iota, float(N_EXPERTS)), axis=-1, keepdims=True)
            idxs.append(ix)
            vals.append(mx)
            lg = jnp.where(iota == ix, -jnp.inf, lg)
        exs = [jnp.exp(v - vals[0]) for v in vals]
        den = exs[0] + exs[1] + exs[2] + exs[3]

        onehot = [(iota == ix) for ix in idxs]
        cnt = jnp.zeros((sub, N_EXPERTS), F32)
        for oh in onehot:
            cnt = cnt + oh.astype(F32)
        prefix = jnp.dot(tri, cnt.astype(BF16), preferred_element_type=F32) + running

        e4 = jnp.zeros((sub, TOP_K), jnp.int32)
        p4 = jnp.zeros((sub, TOP_K), F32)
        r4 = jnp.zeros((sub, TOP_K), jnp.int32)
        for k in range(TOP_K):
            rk = jnp.sum(jnp.where(onehot[k], prefix, 0.0), axis=-1, keepdims=True)
            e4 = jnp.where(lane4 == k, idxs[k].astype(jnp.int32), e4)
            p4 = jnp.where(lane4 == k, exs[k] / den, p4)
            r4 = jnp.where(lane4 == k, rk.astype(jnp.int32), r4)
        e_ref[rs, :] = e4
        p_ref[rs, :] = p4
        rk_ref[rs, :] = r4
        running = running + jnp.sum(cnt, axis=0, keepdims=True)
    carry[...] = running
    cnt_ref[...] = running


def _outproj(ma, mb, w_bf, xs, t, g1, g2, modm, layer, rw, rb):
    tm = 256
    row = lambda i: layer * 8 + _mod_row(i, tm)
    modspec = lambda c: pl.BlockSpec((None, 1, D), lambda i: (row(i), 0, c))
    vec = pl.BlockSpec((1, D), lambda i: (0, 0))
    small = lambda dt: jax.ShapeDtypeStruct((t, TOP_K), dt)
    return pl.pallas_call(
        functools.partial(_outproj_kernel, tm=tm),
        grid=(t // tm,),
        in_specs=[
            pl.BlockSpec((tm, HALF), lambda i: (i, 0)),
            pl.BlockSpec((tm, HALF), lambda i: (i, 0)),
            pl.BlockSpec((D, D), lambda i: (0, 0)),
            pl.BlockSpec((tm, D), lambda i: (i, 0)),
            vec, vec, modspec(2), modspec(3), modspec(4),
            pl.BlockSpec((D, N_EXPERTS), lambda i: (0, 0)),
            pl.BlockSpec((1, N_EXPERTS), lambda i: (0, 0)),
        ],
        out_specs=[
            pl.BlockSpec((tm, D), lambda i: (i, 0)),
            pl.BlockSpec((tm * ROW_S, 128), lambda i: (i, 0)),
            pl.BlockSpec((tm, TOP_K), lambda i: (i, 0)),
            pl.BlockSpec((tm, TOP_K), lambda i: (i, 0)),
            pl.BlockSpec((tm, TOP_K), lambda i: (i, 0)),
            pl.BlockSpec((1, N_EXPERTS), lambda i: (0, 0)),
        ],
        out_shape=[
            jax.ShapeDtypeStruct((t, D), F32),
            jax.ShapeDtypeStruct((t * ROW_S, 128), U32),
            small(jnp.int32), small(F32), small(jnp.int32),
            jax.ShapeDtypeStruct((1, N_EXPERTS), F32),
        ],
        scratch_shapes=[pltpu.VMEM((1, N_EXPERTS), F32), pltpu.VMEM((D, 2 * N_EXPERTS), BF16)],
        compiler_params=_cparams(("arbitrary",), VMEM_LIMIT),
        name="outproj_router",
    )(ma, mb, w_bf, xs, g1.reshape(1, D), g2.reshape(1, D), modm, modm, modm,
      rw, rb.reshape(1, N_EXPERTS))


def _invmap_kernel(pstart_ref, cnt_ref, padded_ref, dest_ref, inv_ref, *, n_pairs, n_slots):
    def unused(lo, hi):
        def body(s, c):
            inv_ref[s] = n_pairs + jnp.bitwise_and(s, 2 * MOE_TM - 1)
            return c
        lax.fori_loop(lo, hi, body, 0)

    def per_expert(e, c):
        unused(pstart_ref[e] + cnt_ref[e], pstart_ref[e] + padded_ref[e])
        return c

    lax.fori_loop(0, N_EXPERTS, per_expert, 0)
    unused(pstart_ref[N_EXPERTS - 1] + padded_ref[N_EXPERTS - 1], n_slots)

    def body(i, c):
        inv_ref[dest_ref[i]] = i
        return c

    lax.fori_loop(0, n_pairs, body, 0, unroll=8)


def _invmap(dest, pstart, cnt, padded, n_slots):
    n_pairs = dest.shape[0]
    return pl.pallas_call(
        functools.partial(_invmap_kernel, n_pairs=n_pairs, n_slots=n_slots),
        grid_spec=pltpu.PrefetchScalarGridSpec(
            num_scalar_prefetch=3,
            grid=(1,),
            in_specs=[pl.BlockSpec(memory_space=pltpu.SMEM)],
            out_specs=pl.BlockSpec(memory_space=pltpu.SMEM),
        ),
        out_shape=jax.ShapeDtypeStruct((n_slots,), jnp.int32),
        compiler_params=_cparams(("arbitrary",)),
        name="moe_invmap",
    )(pstart, cnt, padded, dest)


W_CH = 256
W_NCH_GU = D // W_CH
W_NCH = W_NCH_GU + D_EXPERT // W_CH
W_STAGE = 4


def _expert_kernel(be_ref, nused_ref, first_ref, nxt_ref, lo_ref, hi_ref, slot_ref,
                   inv_cur, inv_next, inv_next2, inv_prev, bgu_ref, bd_ref, h_hbm, wgu_all, wd_all, y_hbm,
                   xbuf, obuf, x_s, act_s, wgu_buf, wd_buf, stage, gsem, ssem, sem, *, layer, n_tok):
    i = pl.program_id(0)
    n_live = nused_ref[0]
    s_cur = lax.rem(i, 3)
    s_p1 = lax.rem(i + 1, 3)
    s_p2 = lax.rem(i + 2, 3)
    wgu_hbm = wgu_all.at[layer]
    wd_hbm = wd_all.at[layer]

    def gather_copy(inv_ref, r, slot):
        tok = jnp.minimum(lax.shift_right_logical(inv_ref[r], 2), n_tok - 1)
        return pltpu.make_async_copy(_row_slab(h_hbm, tok), _row_slab(xbuf.at[slot], r), gsem.at[slot])

    n_pairs = n_tok * TOP_K

    def scatter_copy(inv_ref, r, slot, first_step=False):
        dst = inv_ref[r]
        if first_step:
            dst = jnp.where(i == 0, n_pairs + 2 * MOE_TM + r, dst)
        return pltpu.make_async_copy(_row_slab(obuf.at[slot], r), _row_slab(y_hbm, dst), ssem.at[slot])

    def looped(fn):
        def body(r, c):
            fn(r)
            return c
        lax.fori_loop(0, MOE_TM, body, 0)

    def start_chunk(e, c):
        s = lax.rem(c, W_STAGE)

        @pl.when(c < W_NCH_GU)
        def _():
            r0 = pl.multiple_of(c * W_CH, W_CH)
            pltpu.make_async_copy(wgu_hbm.at[e, pl.ds(r0, W_CH), :], stage.at[s], sem.at[s]).start()

        @pl.when(c >= W_NCH_GU)
        def _():
            r0 = pl.multiple_of((c - W_NCH_GU) * W_CH, W_CH)
            pltpu.make_async_copy(wd_hbm.at[e, pl.ds(r0, W_CH), :], stage.at[s], sem.at[s]).start()

    def finish_chunk(c, dst):
        s = lax.rem(c, W_STAGE)
        pltpu.make_async_copy(wgu_hbm.at[0, pl.ds(0, W_CH), :], stage.at[s], sem.at[s]).wait()
        w = stage[s].astype(BF16)

        @pl.when(c < W_NCH_GU)
        def _():
            wgu_buf[dst, pl.ds(pl.multiple_of(c * W_CH, W_CH), W_CH), :] = w

        @pl.when(c >= W_NCH_GU)
        def _():
            wd_buf[dst, pl.ds(pl.multiple_of((c - W_NCH_GU) * W_CH, W_CH), W_CH), :] = w

    def stream(e, dst, c_lo, c_hi):
        def body(c, carry):
            finish_chunk(c, dst)

            @pl.when(c + W_STAGE < W_NCH)
            def _():
                start_chunk(e, c + W_STAGE)
            return carry
        lax.fori_loop(c_lo, c_hi, body, 0)

    def prime(e):
        for c in range(W_STAGE):
            start_chunk(e, c)

    @pl.when(i == 0)
    def _():
        looped(lambda r: gather_copy(inv_cur, r, 0).start())
        looped(lambda r: gather_copy(inv_next, r, 1).start())
        obuf[2] = jnp.zeros(obuf.shape[1:], U32)
        for region in range(2):
            dump = y_hbm.at[pl.ds((n_pairs + region * MOE_TM) * ROW_S, MOE_TM * ROW_S), :]
            zero_dump = pltpu.make_async_copy(obuf.at[2], dump, ssem.at[2])
            zero_dump.start()
            zero_dump.wait()
        prime(be_ref[0])
        stream(be_ref[0], slot_ref[0], 0, W_NCH)

    @pl.when(i < n_live)
    def _():
        e_next = nxt_ref[i]
        cur = slot_ref[i]

        @pl.when((first_ref[i] == 1) & (e_next >= 0))
        def _():
            prime(e_next)

        def block_body(c_cur, c_p1, c_p2):
            for r in range(MOE_TM):
                gather_copy(inv_cur, r, c_cur).wait()
            for s in range(ROW_S):
                hi, lo = _unpack_piece(xbuf[c_cur, pl.ds(s, MOE_TM, stride=ROW_S), :])
                x_s[:, s * 128:(s + 1) * 128] = hi.astype(BF16)
                x_s[:, (s + ROW_S) * 128:(s + ROW_S + 1) * 128] = lo.astype(BF16)
            nc = 256
            for r in range(MOE_TM):
                gather_copy(inv_next2, r, c_p2).start()
                scatter_copy(inv_prev, r, c_p2, first_step=True).start()
            for c0 in range(0, D_EXPERT, nc):
                g = (jnp.dot(x_s[...], wgu_buf[cur, :, c0:c0 + nc], preferred_element_type=F32)
                     + bgu_ref[:, c0:c0 + nc])
                u = (jnp.dot(x_s[...], wgu_buf[cur, :, D_EXPERT + c0:D_EXPERT + c0 + nc],
                             preferred_element_type=F32) + bgu_ref[:, D_EXPERT + c0:D_EXPERT + c0 + nc])
                gate = jnp.minimum(g, SWIGLU_LIMIT)
                up = jnp.clip(u, -SWIGLU_LIMIT, SWIGLU_LIMIT)
                act_s[:, c0:c0 + nc] = (gate * jax.nn.sigmoid(SWIGLU_ALPHA * gate) * (up + 1.0)).astype(BF16)

            for c0 in range(0, D // 2, nc):
                y_hi = (jnp.dot(act_s[...], wd_buf[cur, :, c0:c0 + nc], preferred_element_type=F32)
                        + bd_ref[:, c0:c0 + nc])
                y_lo = (jnp.dot(act_s[...], wd_buf[cur, :, D // 2 + c0:D // 2 + c0 + nc],
                                preferred_element_type=F32) + bd_ref[:, D // 2 + c0:D // 2 + c0 + nc])
                for j in range(nc // 128):
                    s = c0 // 128 + j
                    obuf[c_cur, pl.ds(s, MOE_TM, stride=ROW_S), :] = _pack_pieces(
                        y_hi[:, j * 128:(j + 1) * 128], y_lo[:, j * 128:(j + 1) * 128])

            @pl.when(i >= 1)
            def _():
                for r in range(MOE_TM):
                    scatter_copy(inv_prev, r, c_p1).wait()

        for ring in range(3):
            @pl.when(s_cur == ring)
            def _():
                block_body(ring, (ring + 1) % 3, (ring + 2) % 3)

        @pl.when(e_next >= 0)
        def _():
            stream(e_next, 1 - cur, lo_ref[i], hi_ref[i])

        @pl.when(i == n_live - 1)
        def _():
            looped(lambda r: scatter_copy(inv_prev, r, s_p2).wait())
            looped(lambda r: scatter_copy(inv_cur, r, s_cur).start())
            looped(lambda r: scatter_copy(inv_cur, r, s_cur).wait())
            looped(lambda r: gather_copy(inv_next, r, s_p1).wait())
            looped(lambda r: gather_copy(inv_next2, r, s_p2).wait())


def _take(table, idx):
    ids = jnp.arange(table.shape[0], dtype=jnp.int32)
    return jnp.sum(jnp.where(idx[..., None] == ids, table, 0), axis=-1)


def _experts(h2, inv, block_e, nused, pstart, padded, layer, wgu, bgu, wd, bd, nblk, n_tok):
    nb_e = padded // MOE_TM
    pos = jnp.arange(nblk, dtype=jnp.int32) - _take(pstart // MOE_TM, block_e)
    nb = jnp.maximum(_take(nb_e, block_e), 1)
    lo = (pos * W_NCH) // nb
    hi = ((pos + 1) * W_NCH) // nb
    first = (pos == 0).astype(jnp.int32)
    eid = jnp.arange(N_EXPERTS, dtype=jnp.int32)
    later = jnp.where(nb_e > 0, eid, N_EXPERTS)
    nxt_e = jnp.concatenate([lax.cummin(later, reverse=True)[1:], jnp.full((1,), N_EXPERTS, jnp.int32)])
    nxt_e = jnp.where(nxt_e >= N_EXPERTS, -1, nxt_e)
    slot_e = (jnp.cumsum((nb_e > 0).astype(jnp.int32)) - 1) % 2
    i32 = lambda a: a.astype(jnp.int32)
    live = lambda i, nu: jnp.clip(i, 0, nu[0] - 1)
    sblk = lambda off: pl.BlockSpec((MOE_TM,), lambda i, be, nu, *_: (live(i + off, nu),),
                                    memory_space=pltpu.SMEM)
    bias = lambda n: pl.BlockSpec((None, 1, n), lambda i, be, nu, *_: (be[live(i, nu)], 0, 0))
    hbm = pl.BlockSpec(memory_space=pl.ANY)
    n_slabs = n_tok * TOP_K + 3 * MOE_TM
    return pl.pallas_call(
        functools.partial(_expert_kernel, layer=layer, n_tok=n_tok),
        grid_spec=pltpu.PrefetchScalarGridSpec(
            num_scalar_prefetch=7,
            grid=(nblk,),
            in_specs=[sblk(0), sblk(1), sblk(2), sblk(-1), bias(2 * D_EXPERT), bias(D), hbm, hbm, hbm],
            out_specs=hbm,
            scratch_shapes=[
                pltpu.VMEM((3, MOE_TM * ROW_S, 128), U32),
                pltpu.VMEM((3, MOE_TM * ROW_S, 128), U32),
                pltpu.VMEM((MOE_TM, D), BF16),
                pltpu.VMEM((MOE_TM, D_EXPERT), BF16),
                pltpu.VMEM((2, D, 2 * D_EXPERT), BF16),
                pltpu.VMEM((2, D_EXPERT, D), BF16),
                pltpu.VMEM((W_STAGE, W_CH, D), F32),
                pltpu.SemaphoreType.DMA((3,)),
                pltpu.SemaphoreType.DMA((3,)),
                pltpu.SemaphoreType.DMA((W_STAGE,)),
            ],
        ),
        out_shape=jax.ShapeDtypeStruct((n_slabs * ROW_S, 128), U32),
        compiler_params=_cparams(("arbitrary",), VMEM_LIMIT),
        name="moe_experts",
    )(block_e, nused, i32(first), i32(_take(nxt_e, block_e)), i32(lo), i32(hi), i32(_take(slot_e, block_e)),
      inv, inv, inv, inv, bgu.reshape(N_EXPERTS, 1, -1), bd.reshape(N_EXPERTS, 1, -1), h2, wgu, wd)


def _combine_kernel(p_ref, x_ref, g3_ref, ga2_ref, y_ref, xo_ref, acc_hi, acc_lo, *, tm):
    pair = TOP_K * ROW_S
    for r in range(tm):
        a_hi = a_lo = None
        for k in range(TOP_K):
            hi, lo = _unpack_piece(y_ref[pl.ds(r * pair + k * ROW_S, ROW_S), :])
            w = p_ref[r * TOP_K + k]
            a_hi = hi * w if k == 0 else a_hi + hi * w
            a_lo = lo * w if k == 0 else a_lo + lo * w
        acc_hi[pl.ds(r * ROW_S, ROW_S), :] = a_hi
        acc_lo[pl.ds(r * ROW_S, ROW_S), :] = a_lo
    y = jnp.concatenate([acc_hi[pl.ds(s, tm, stride=ROW_S), :] for s in range(ROW_S)]
                        + [acc_lo[pl.ds(s, tm, stride=ROW_S), :] for s in range(ROW_S)], axis=1)
    xo_ref[...] = x_ref[...] + ga2_ref[...] * (_rms(y) * g3_ref[...])


def _combine(y4, p4, xs, g3, modm, layer):
    tm = 128
    t = xs.shape[0]
    row = lambda i: layer * 8 + _mod_row(i, tm)
    return pl.pallas_call(
        functools.partial(_combine_kernel, tm=tm),
        grid=(t // tm,),
        in_specs=[
            pl.BlockSpec((tm * TOP_K,), lambda i: (i,), memory_space=pltpu.SMEM),
            pl.BlockSpec((tm, D), lambda i: (i, 0)),
            pl.BlockSpec((1, D), lambda i: (0, 0)),
            pl.BlockSpec((None, 1, D), lambda i: (row(i), 0, 5)),
            pl.BlockSpec((tm * TOP_K * ROW_S, 128), lambda i: (i, 0)),
        ],
        out_specs=pl.BlockSpec((tm, D), lambda i: (i, 0)),
        out_shape=jax.ShapeDtypeStruct((t, D), F32),
        scratch_shapes=[pltpu.VMEM((tm * ROW_S, 128), F32)] * 2,
        compiler_params=_cparams(("parallel",), VMEM_LIMIT),
        name="moe_combine",
    )(p4.reshape(-1), xs, g3.reshape(1, D), modm, y4)


def _moe(h2, e4, p4, r4, counts, xs, g3, modm, layer, wgu, bgu, wd, bd):
    t = e4.shape[0]
    nblk = -(-(t * TOP_K + N_EXPERTS * (MOE_TM - 1)) // MOE_TM)
    cnt = counts.reshape(N_EXPERTS).astype(jnp.int32)
    padded = (cnt + MOE_TM - 1) // MOE_TM * MOE_TM
    pend = jnp.cumsum(padded)
    pstart = pend - padded
    dest = (_take(pstart, e4) + r4).reshape(-1).astype(jnp.int32)
    nused = (pend[-1:] // MOE_TM).astype(jnp.int32)
    blk_start = jnp.arange(nblk, dtype=jnp.int32) * MOE_TM
    block_e = jnp.minimum(jnp.sum((pend[None, :] <= blk_start[:, None]).astype(jnp.int32), axis=1),
                          N_EXPERTS - 1).astype(jnp.int32)
    inv = _invmap(dest, pstart.astype(jnp.int32), cnt, padded.astype(jnp.int32), nblk * MOE_TM)
    y4 = _experts(h2, inv, block_e, nused, pstart, padded, layer, wgu, bgu, wd, bd, nblk, t)
    return _combine(y4, p4, xs, g3, modm, layer)


N_CTX_BLK = LC // HGRN_R
N_LAT_BLK = L // HGRN_R
N_SEQ_BLK = N_CTX_BLK + N_LAT_BLK


def _scan_block(b, s, rev):
    if rev:
        return jnp.where(s < N_CTX_BLK, T_LAT // HGRN_R + N_CTX_BLK * b + (N_CTX_BLK - 1 - s),
                         N_LAT_BLK * b + (N_SEQ_BLK - 1 - s))
    return jnp.where(s < N_CTX_BLK, T_LAT // HGRN_R + N_CTX_BLK * b + s, N_LAT_BLK * b + (s - N_CTX_BLK))


def _hgrn_kernel(*refs, rev):
    if rev:
        q_ref, f_ref, v_ref, lb_ref, of_ref, g_ref, on_ref, o_ref, st, qin_s, qd_s, ki_s, ke_s, v_s = refs
    else:
        q_ref, f_ref, v_ref, lb_ref, o_ref, st, qin_s, qd_s, ki_s, ke_s, v_s = refs

    @pl.when(pl.program_id(1) == 0)
    def _():
        st[...] = jnp.zeros_like(st)

    c = HGRN_C
    nc = HGRN_R // c
    r_i = lax.broadcasted_iota(jnp.int32, (c, c), 0)
    c_i = lax.broadcasted_iota(jnp.int32, (c, c), 1)
    mask = (c_i >= r_i) if rev else (c_i <= r_i)
    tri = mask.astype(F32)

    lb = lb_ref[...]
    decay = [None] * nc
    v_s[...] = v_ref[...].astype(BF16)
    for ci in range(nc):
        rs = slice(ci * c, (ci + 1) * c)
        sg = jax.nn.sigmoid(f_ref[rs, :])
        logf = jnp.log(lb + (1.0 - lb) * sg)
        kk = (1.0 - lb) * (1.0 - sg)
        q = q_ref[rs, :]
        qs = q * jax.nn.sigmoid(q)
        cum = jnp.dot(tri, logf, precision=HI, preferred_element_type=F32)
        total = cum[0:1] if rev else cum[c - 1:c]
        mid = cum[c // 2:c // 2 + 1]
        qin_s[rs, :] = (qs * jnp.exp(cum)).astype(BF16)
        qd_s[rs, :] = (qs * jnp.exp(cum - mid)).astype(BF16)
        ki_s[rs, :] = (kk * jnp.exp(mid - cum)).astype(BF16)
        ke_s[rs, :] = (kk * jnp.exp(total - cum)).astype(BF16)
        decay[ci] = jnp.exp(total)

    order = range(nc - 1, -1, -1) if rev else range(nc)
    for ci in order:
        rs = slice(ci * c, (ci + 1) * c)
        for h in range(HGRN_HEADS):
            hs = slice(h * 128, (h + 1) * 128)
            state = st[h]
            vb = v_s[rs, hs]
            sc = lax.dot_general(qd_s[rs, hs], ki_s[rs, hs], (((1,), (1,)), ((), ())),
                                 preferred_element_type=F32)
            sc = jnp.where(mask, sc, 0.0)
            intra = jnp.dot(sc.astype(BF16), vb, preferred_element_type=F32)
            inter = lax.dot_general(qin_s[rs, hs], state.astype(BF16), (((1,), (1,)), ((), ())),
                                    preferred_element_type=F32)
            st[h] = state * decay[ci][:, hs] + lax.dot_general(
                vb, ke_s[rs, hs], (((0,), (0,)), ((), ())), preferred_element_type=F32)
            o = intra + inter
            if rev:
                o = o + of_ref[rs, hs]
                g = g_ref[rs, hs]
                o_ref[rs, hs] = (_rms(o) * on_ref[:, hs] * (g * jax.nn.sigmoid(g))).astype(BF16)
            else:
                o_ref[rs, hs] = o


def _hgrn(pa, pb, lb, rev, of=None, onorm=None):
    r = HGRN_R
    blk = lambda b, s: _scan_block(b, s, rev)
    col = lambda c: pl.BlockSpec((r, HALF), lambda b, s: (blk(b, s), c))
    vec = pl.BlockSpec((1, HALF), lambda b, s: (0, 0))
    in_specs = [col(0), col(2 if rev else 1), col(3), vec]
    args = [pa, pa, pa, lb.reshape(1, HALF)]
    if rev:
        in_specs += [col(0), col(0), vec]
        args += [of, pb, onorm.reshape(1, HALF)]
    return pl.pallas_call(
        functools.partial(_hgrn_kernel, rev=rev),
        grid=(B, N_SEQ_BLK),
        in_specs=in_specs,
        out_specs=col(0),
        out_shape=jax.ShapeDtypeStruct((T_ALL, HALF), BF16 if rev else F32),
        scratch_shapes=[pltpu.VMEM((HGRN_HEADS, 128, 128), F32)] + [pltpu.VMEM((r, HALF), BF16)] * 5,
        compiler_params=_cparams(("parallel", "arbitrary"), VMEM_LIMIT),
        name="hgrn_bwd" if rev else "hgrn_fwd",
    )(*args)


def _lru_kernel(*refs, rev):
    if rev:
        (x_ref, xp_ref, xn_ref, cw_ref, cb_ref, wr_ref, br_ref, wi_ref, bi_ref, lam_ref,
         hf_ref, gt_ref, o_ref, xcat, a_s, u_s, h_s, carry) = refs
    else:
        (x_ref, xp_ref, xn_ref, cw_ref, cb_ref, wr_ref, br_ref, wi_ref, bi_ref, lam_ref,
         o_ref, xcat, a_s, u_s, h_s, carry) = refs
    s = pl.program_id(1)
    r = LRU_R

    @pl.when(s == 0)
    def _():
        carry[...] = jnp.zeros_like(carry)

    is_ctx = s < N_CTX_BLK
    if rev:
        j = jnp.where(is_ctx, N_CTX_BLK - 1 - s, N_SEQ_BLK - 1 - s)
    else:
        j = jnp.where(is_ctx, s, s - N_CTX_BLK)
    nb = jnp.where(is_ctx, N_CTX_BLK, N_LAT_BLK)
    xcat[0:8, :] = jnp.where(j == 0, 0.0, xp_ref[...])
    xcat[8:8 + r, :] = x_ref[...]
    xcat[8 + r:16 + r, :] = jnp.where(j == nb - 1, 0.0, xn_ref[...])
    xc = cb_ref[...] + xcat[pl.ds(6, r), :] * cw_ref[0:1, :]
    for t in range(1, 4):
        xc = xc + xcat[pl.ds(6 + t, r), :] * cw_ref[t:t + 1, :]

    lam = lam_ref[...]
    sp = jnp.maximum(-lam, 0.0) + jnp.log1p(jnp.exp(-jnp.abs(lam)))
    hd = HALF // LRU_HEADS
    for h in range(LRU_HEADS):
        cs = slice(h * hd, (h + 1) * hd)
        xh = xc[:, cs]
        rg = jax.nn.sigmoid(_bdot(xh, wr_ref[h]) + br_ref[:, cs])
        ig = jax.nn.sigmoid(_bdot(xh, wi_ref[h]) + bi_ref[:, cs])
        log_a = -LRU_C * rg * sp[:, cs]
        a = jnp.exp(log_a)
        a_s[:, cs] = a
        u_s[:, cs] = jnp.sqrt(-jnp.tanh(log_a) * (a * a + 1.0)) * (ig * xh)

    def step(t, h):
        tt = (r - 1 - t) if rev else t
        h = a_s[pl.ds(tt, 1), :] * h + u_s[pl.ds(tt, 1), :]
        h_s[pl.ds(tt, 1), :] = h
        return h

    carry[...] = lax.fori_loop(0, r, step, carry[...], unroll=8)
    if rev:
        o_ref[...] = ((h_s[...] + hf_ref[...]) * jax.nn.gelu(gt_ref[...])).astype(BF16)
    else:
        o_ref[...] = h_s[...]


def _lru(p, conv_w, conv_b, w_r, b_r, w_i, b_i, lam, rev, hf=None):
    r = LRU_R
    d = 1 if rev else 0
    blk = lambda b, s: _scan_block(b, s, rev)
    xcol = 2
    n8 = T_ALL // 8
    vec = lambda: pl.BlockSpec((None, 1, HALF), lambda b, s: (d, 0, 0))
    wspec = lambda: pl.BlockSpec((None, LRU_HEADS, 128, 128), lambda b, s: (d, 0, 0, 0))
    in_specs = [
        pl.BlockSpec((r, HALF), lambda b, s: (blk(b, s), xcol)),
        pl.BlockSpec((8, HALF), lambda b, s: (jnp.maximum(blk(b, s) * (r // 8) - 1, 0), xcol)),
        pl.BlockSpec((8, HALF), lambda b, s: (jnp.minimum((blk(b, s) + 1) * (r // 8), n8 - 1), xcol)),
        pl.BlockSpec((4, HALF), lambda b, s: (0, 0)),
        pl.BlockSpec((1, HALF), lambda b, s: (0, 0)),
        wspec(), vec(), wspec(), vec(), vec(),
    ]
    args = [p, p, p, conv_w, conv_b.reshape(1, HALF), w_r, b_r.reshape(2, 1, HALF), w_i,
            b_i.reshape(2, 1, HALF), lam.reshape(2, 1, HALF)]
    if rev:
        in_specs += [pl.BlockSpec((r, HALF), lambda b, s: (blk(b, s), 0)),
                     pl.BlockSpec((r, HALF), lambda b, s: (blk(b, s), xcol - 1))]
        args += [hf, p]
    return pl.pallas_call(
        functools.partial(_lru_kernel, rev=rev),
        grid=(B, N_SEQ_BLK),
        in_specs=in_specs,
        out_specs=pl.BlockSpec((r, HALF), lambda b, s: (blk(b, s), 0)),
        out_shape=jax.ShapeDtypeStruct((T_ALL, HALF), BF16 if rev else F32),
        scratch_shapes=[pltpu.VMEM((r + 16, HALF), F32), pltpu.VMEM((r, HALF), F32),
                        pltpu.VMEM((r, HALF), F32), pltpu.VMEM((r, HALF), F32),
                        pltpu.VMEM((1, HALF), F32)],
        compiler_params=_cparams(("parallel", "arbitrary"), VMEM_LIMIT),
        name="lru_bwd" if rev else "lru_fwd",
    )(*args)


def kernel(x, c, ctx, c_ctx, mod_w, mod_b, norm_g, ab_w_in, ab_w_out, gmlp_v_g, gmlp_ws, gmlp_bs, cd_w_in, cd_w_out, hgrn_lb, hgrn_onorm_g, lru_conv_w, lru_conv_b, lru_wr, lru_br, lru_wi, lru_bi, lru_lambda, router_w, router_b, exp_w_gu, exp_b_gu, exp_w_down, exp_b_down):
    cond8 = jnp.concatenate([c, c_ctx[None, :], jnp.zeros((5, D), F32)], axis=0)
    modm = _modulation(cond8, mod_w, mod_b).reshape(DEPTH * 8, 1, 6 * D)
    xs = jnp.concatenate([x.reshape(T_LAT, D), ctx.reshape(T_CTX, D)], axis=0)

    lb_soft = jax.nn.softmax(hgrn_lb.astype(F32), axis=0)
    lb_all = jnp.cumsum(lb_soft, axis=0) - lb_soft[0]

    (p0,) = _inproj(xs, norm_g[0, 0], modm, 0, [ab_w_in[0].astype(BF16)], BF16)
    zc, zs = _fnet_chan(p0)
    mix_a = jnp.concatenate([_fnet_pos(zc, zs, 0, L, 512, 1024),
                             _fnet_pos(zc, zs, T_LAT, LC, LC, LC)], axis=0)
    mix_g = _gmlp(p0, gmlp_v_g[0], gmlp_ws[0], gmlp_bs[0])
    xs, h2, e4, p4, r4, counts = _outproj(mix_a, mix_g, ab_w_out[0].astype(BF16), xs, T_ALL, norm_g[0, 1],
                                          norm_g[0, 2], modm, 0, router_w[0], router_b[0])
    xs = _moe(h2, e4, p4, r4, counts, xs, norm_g[0, 3], modm, 0,
              exp_w_gu, exp_b_gu[0], exp_w_down, exp_b_down[0])

    w_cd = cd_w_in[0].astype(BF16)
    p1a, p1b = _inproj(xs, norm_g[1, 0], modm, 1, [w_cd[:, :4 * HALF], w_cd[:, 4 * HALF:]], F32)
    o_f = _hgrn(p1a, p1b, lb_all[1], False)
    hg = _hgrn(p1a, p1b, lb_all[1], True, o_f, hgrn_onorm_g[0])
    lru_args = (lru_conv_w[0], lru_conv_b[0], lru_wr[0], lru_br[0], lru_wi[0], lru_bi[0], lru_lambda[0])
    h_f = _lru(p1b, *lru_args, False)
    lr = _lru(p1b, *lru_args, True, h_f)
    xl, h2, e4, p4, r4, counts = _outproj(hg, lr, cd_w_out[0].astype(BF16), xs, T_LAT, norm_g[1, 1],
                                          norm_g[1, 2], modm, 1, router_w[1], router_b[1])
    xl = _moe(h2, e4, p4, r4, counts, xl, norm_g[1, 3], modm, 1,
              exp_w_gu, exp_b_gu[1], exp_w_down, exp_b_down[1])
    return xl.reshape(B, L, D)
```

```python
import functools

import numpy as np
import jax
import jax.numpy as jnp
from jax import lax
from jax.experimental import pallas as pl
from jax.experimental.pallas import tpu as pltpu

F32 = jnp.float32
BF16 = jnp.bfloat16
HI = lax.Precision.HIGHEST

D = 2048
B = 2
L = 4096
LC = 256
T_LAT = B * L
T_CTX = B * LC
T_ALL = T_LAT + T_CTX
DEPTH = 2
EPS = 1e-6

HALF = D // 2
N_EXPERTS = 32
TOP_K = 4
D_EXPERT = D // 2
SWIGLU_LIMIT = 7.0
SWIGLU_ALPHA = 1.702
MOE_TM = 256

FNET_GROUPS = 4
FNET_GD = HALF // FNET_GROUPS
GMLP_HEADS = 8
GMLP_CHUNK = 128
HGRN_HEADS = 8
HGRN_R = 128
HGRN_C = 64
LRU_HEADS = 8
LRU_R = 128
LRU_C = 8.0

VMEM_LIMIT = 56 * 1024 * 1024


def _cparams(sem, vmem=None):
    return pltpu.CompilerParams(dimension_semantics=sem, vmem_limit_bytes=vmem)


def _mod_row(i, tm):
    return jnp.where(i < L // tm, 0, jnp.where(i < 2 * L // tm, 1, 2))


def _rms(x):
    return x * lax.rsqrt(jnp.mean(x * x, axis=-1, keepdims=True) + EPS)


def _bdot(a, b):
    return jnp.dot(a.astype(BF16), b.astype(BF16), preferred_element_type=F32)


ROW_S = D // 256
U32 = jnp.uint32
HI_MASK = 0xFFFF0000


def _row_slab(ref, r):
    return ref.at[pl.ds(pl.multiple_of(r * ROW_S, ROW_S), ROW_S), :]


def _pack_pieces(hi, lo):
    hb = pltpu.bitcast(hi.astype(BF16).astype(F32), U32)
    lb = pltpu.bitcast(lo.astype(BF16).astype(F32), U32)
    return jnp.bitwise_or(jnp.bitwise_and(hb, U32(HI_MASK)), jnp.right_shift(lb, U32(16)))


def _unpack_piece(w):
    hi = pltpu.bitcast(jnp.bitwise_and(w, U32(HI_MASK)), F32)
    lo = pltpu.bitcast(jnp.left_shift(w, U32(16)), F32)
    return hi, lo


def _store_rows(ref, val, row0=0):
    n = val.shape[0]
    for s in range(ROW_S):
        ref[pl.ds(row0 * ROW_S + s, n, stride=ROW_S), :] = _pack_pieces(
            val[:, s * 128:(s + 1) * 128], val[:, (s + ROW_S) * 128:(s + ROW_S + 1) * 128])


def _mod_kernel(c_ref, w_ref, b_ref, o_ref):
    cnd = c_ref[...]
    s = cnd * jax.nn.sigmoid(cnd)
    o_ref[...] = _bdot(s, w_ref[...]) + b_ref[...]


def _modulation(cond8, mod_w, mod_b):
    tn = 1024
    n = mod_w.shape[-1]
    return pl.pallas_call(
        _mod_kernel,
        grid=(DEPTH, n // tn),
        in_specs=[
            pl.BlockSpec((8, D), lambda l, j: (0, 0)),
            pl.BlockSpec((None, D, tn), lambda l, j: (l, 0, j)),
            pl.BlockSpec((None, 1, tn), lambda l, j: (l, 0, j)),
        ],
        out_specs=pl.BlockSpec((None, 8, tn), lambda l, j: (l, 0, j)),
        out_shape=jax.ShapeDtypeStruct((DEPTH, 8, n), F32),
        compiler_params=_cparams(("parallel", "parallel"), VMEM_LIMIT),
        name="modulation",
    )(cond8, mod_w, mod_b.reshape(DEPTH, 1, n))


def _inproj_kernel(*refs, nw):
    x_ref, g_ref, sh_ref, sc_ref = refs[:4]
    w_refs, o_refs = refs[4:4 + nw], refs[4 + nw:]
    y = _rms(x_ref[...]) * g_ref[...]
    h = (y * (1.0 + sc_ref[...]) + sh_ref[...]).astype(BF16)
    for w_ref, o_ref in zip(w_refs, o_refs):
        o_ref[...] = jnp.dot(h, w_ref[...], preferred_element_type=F32).astype(o_ref.dtype)


def _inproj(xs, g, modm, layer, ws_bf, out_dtype):
    tm = 256
    t = xs.shape[0]
    row = lambda i: layer * 8 + _mod_row(i, tm)
    return pl.pallas_call(
        functools.partial(_inproj_kernel, nw=len(ws_bf)),
        grid=(t // tm,),
        in_specs=[
            pl.BlockSpec((tm, D), lambda i: (i, 0)),
            pl.BlockSpec((1, D), lambda i: (0, 0)),
            pl.BlockSpec((None, 1, D), lambda i: (row(i), 0, 0)),
            pl.BlockSpec((None, 1, D), lambda i: (row(i), 0, 1)),
        ] + [pl.BlockSpec(w.shape, lambda i: (0, 0), pipeline_mode=pl.Buffered(1)) for w in ws_bf],
        out_specs=[pl.BlockSpec((tm, w.shape[1]), lambda i: (i, 0)) for w in ws_bf],
        out_shape=[jax.ShapeDtypeStruct((t, w.shape[1]), out_dtype) for w in ws_bf],
        compiler_params=_cparams(("parallel",), VMEM_LIMIT),
        name="inproj",
    )(xs, g.reshape(1, D), modm, modm, *ws_bf)


def _fnet_chan_kernel(p_ref, cs_ref, zc_ref, zs_ref):
    for g in range(FNET_GROUPS):
        sl = slice(g * FNET_GD, (g + 1) * FNET_GD)
        z = jnp.dot(p_ref[:, sl].astype(BF16), cs_ref[...], preferred_element_type=F32)
        zc_ref[:, sl] = z[:, :FNET_GD].astype(BF16)
        zs_ref[:, sl] = z[:, FNET_GD:].astype(BF16)


def _fnet_chan(p):
    tm = 512
    t = p.shape[0]
    k = np.arange(FNET_GD)
    ang = 2.0 * np.pi * ((k[:, None] * k[None, :]) % FNET_GD) / FNET_GD
    cs = np.concatenate([np.cos(ang), np.sin(ang)], axis=1) / np.sqrt(FNET_GD)
    cs = jnp.asarray(cs, F32).astype(BF16)
    return pl.pallas_call(
        _fnet_chan_kernel,
        grid=(t // tm,),
        in_specs=[
            pl.BlockSpec((tm, HALF), lambda i: (i, 0)),
            pl.BlockSpec((FNET_GD, 2 * FNET_GD), lambda i: (0, 0)),
        ],
        out_specs=[pl.BlockSpec((tm, HALF), lambda i: (i, 0))] * 2,
        out_shape=[jax.ShapeDtypeStruct((t, HALF), BF16)] * 2,
        compiler_params=_cparams(("parallel",), VMEM_LIMIT),
        name="fnet_chan",
    )(p, cs)


def _fnet_pos_kernel(bre_ref, bim_ref, cre_ref, cim_ref, rre_ref, rim_ref, zc_ref, zs_ref, o_ref, acc):
    nt = pl.program_id(2)
    br, bi = bre_ref[...], bim_ref[...]
    cr, ci = cre_ref[...], cim_ref[...]
    rr, ri = rre_ref[...], rim_ref[...]
    tr = br * cr - bi * ci
    ti = br * ci + bi * cr
    er = (tr * rr - ti * ri).astype(BF16)
    ei = (tr * ri + ti * rr).astype(BF16)
    part = (jnp.dot(er, zc_ref[...], preferred_element_type=F32)
            + jnp.dot(ei, zs_ref[...], preferred_element_type=F32))

    @pl.when(nt == 0)
    def _():
        acc[...] = part

    @pl.when(nt > 0)
    def _():
        acc[...] += part

    @pl.when(nt == pl.num_programs(2) - 1)
    def _():
        o_ref[...] = acc[...].astype(BF16)


def _fnet_pos(zc, zs, row0, seq, tk, tn):
    nkt, nnt = seq // tk, seq // tn
    th = 2.0 * np.pi / seq
    kk = np.arange(tk)[:, None]
    nn = np.arange(tn)[None, :]
    base = th * ((kk * nn) % seq)
    n0 = (np.arange(nnt) * tn)[:, None, None]
    col = th * ((np.arange(tk)[None, :, None] * n0) % seq)
    k0 = (np.arange(nkt) * tk)[:, None, None, None]
    nfull = (np.arange(nnt) * tn)[None, :, None, None] + np.arange(tn)[None, None, None, :]
    row = th * ((k0 * nfull) % seq)
    scale = 1.0 / np.sqrt(seq)
    f = lambda a: jnp.asarray(a, F32)
    tabs = (f(np.cos(base)), f(-np.sin(base)), f(np.cos(col)), f(-np.sin(col)),
            f(np.cos(row) * scale), f(-np.sin(row) * scale))
    rb = row0 // tn
    ob = row0 // tk
    return pl.pallas_call(
        _fnet_pos_kernel,
        grid=(B, nkt, nnt),
        in_specs=[
            pl.BlockSpec((tk, tn), lambda b, k, n: (0, 0)),
            pl.BlockSpec((tk, tn), lambda b, k, n: (0, 0)),
            pl.BlockSpec((None, tk, 1), lambda b, k, n: (n, 0, 0)),
            pl.BlockSpec((None, tk, 1), lambda b, k, n: (n, 0, 0)),
            pl.BlockSpec((None, None, 1, tn), lambda b, k, n: (k, n, 0, 0)),
            pl.BlockSpec((None, None, 1, tn), lambda b, k, n: (k, n, 0, 0)),
            pl.BlockSpec((tn, HALF), lambda b, k, n: (rb + b * nnt + n, 0)),
            pl.BlockSpec((tn, HALF), lambda b, k, n: (rb + b * nnt + n, 0)),
        ],
        out_specs=pl.BlockSpec((tk, HALF), lambda b, k, n: (b * nkt + k, 0)),
        out_shape=jax.ShapeDtypeStruct((B * seq, HALF), BF16),
        scratch_shapes=[pltpu.VMEM((tk, HALF), F32)],
        compiler_params=_cparams(("parallel", "parallel", "arbitrary"), VMEM_LIMIT),
        name=f"fnet_pos_{seq}",
    )(*tabs, zc, zs)


def _gmlp_kernel(u_ref, v_ref, vg_ref, ws_ref, bst_ref, o_ref, *, tm):
    zu = jax.nn.gelu(u_ref[...].astype(F32))
    zv = jax.nn.gelu(v_ref[...].astype(F32))
    mu = jnp.mean(zv, axis=-1, keepdims=True)
    dv = zv - mu
    var = jnp.mean(dv * dv, axis=-1, keepdims=True)
    vn = dv * lax.rsqrt(var + EPS) * vg_ref[...]
    hd = HALF // GMLP_HEADS
    for c in range(tm // GMLP_CHUNK):
        rs = slice(c * GMLP_CHUNK, (c + 1) * GMLP_CHUNK)
        for h in range(GMLP_HEADS):
            cs = slice(h * hd, (h + 1) * hd)
            s = jnp.dot(ws_ref[h], vn[rs, cs], precision=HI, preferred_element_type=F32)
            s = s + bst_ref[:, h:h + 1]
            o_ref[rs, cs] = (zu[rs, cs] * s).astype(BF16)


def _gmlp(p, v_g, w_s, b_s):
    tm = 256
    t = p.shape[0]
    return pl.pallas_call(
        functools.partial(_gmlp_kernel, tm=tm),
        grid=(t // tm,),
        in_specs=[
            pl.BlockSpec((tm, HALF), lambda i: (i, 1)),
            pl.BlockSpec((tm, HALF), lambda i: (i, 2)),
            pl.BlockSpec((1, HALF), lambda i: (0, 0)),
            pl.BlockSpec((GMLP_HEADS, GMLP_CHUNK, GMLP_CHUNK), lambda i: (0, 0, 0)),
            pl.BlockSpec((GMLP_CHUNK, GMLP_HEADS), lambda i: (0, 0)),
        ],
        out_specs=pl.BlockSpec((tm, HALF), lambda i: (i, 0)),
        out_shape=jax.ShapeDtypeStruct((t, HALF), BF16),
        compiler_params=_cparams(("parallel",), VMEM_LIMIT),
        name="gmlp",
    )(p, p, v_g.reshape(1, HALF), w_s, b_s.T)


def _outproj_kernel(ma_ref, mb_ref, w_ref, x_ref, g1_ref, g2_ref, ga1_ref, sh2_ref, sc2_ref,
                    rw_ref, rb_ref, xo_ref, h2_ref, e_ref, p_ref, rk_ref, cnt_ref, carry, rw2, *, tm):
    @pl.when(pl.program_id(0) == 0)
    def _():
        carry[...] = jnp.zeros_like(carry)
        rw = rw_ref[...]
        rw_hi = rw.astype(BF16)
        rw2[:, :N_EXPERTS] = rw_hi
        rw2[:, N_EXPERTS:] = (rw - rw_hi.astype(F32)).astype(BF16)

    sub = 128
    iota = lax.broadcasted_iota(jnp.int32, (sub, N_EXPERTS), 1).astype(F32)
    lane4 = lax.broadcasted_iota(jnp.int32, (sub, TOP_K), 1)
    r_i = lax.broadcasted_iota(jnp.int32, (sub, sub), 0)
    c_i = lax.broadcasted_iota(jnp.int32, (sub, sub), 1)
    tri = (c_i < r_i).astype(BF16)
    running = carry[...]
    for r0 in range(0, tm, sub):
        rs = slice(r0, r0 + sub)
        m = (jnp.dot(ma_ref[rs, :], w_ref[:HALF, :], preferred_element_type=F32)
             + jnp.dot(mb_ref[rs, :], w_ref[HALF:, :], preferred_element_type=F32))
        xn = x_ref[rs, :] + ga1_ref[...] * (_rms(m) * g1_ref[...])
        xo_ref[rs, :] = xn
        h2 = (_rms(xn) * g2_ref[...]) * (1.0 + sc2_ref[...]) + sh2_ref[...]
        _store_rows(h2_ref, h2, r0)

        h_hi = h2.astype(BF16)
        h_lo = (h2 - h_hi.astype(F32)).astype(BF16)
        t1 = jnp.dot(h_hi, rw2[...], preferred_element_type=F32)
        t2 = jnp.dot(h_lo, rw2[:, :N_EXPERTS], preferred_element_type=F32)
        lg = t1[:, :N_EXPERTS] + t1[:, N_EXPERTS:] + t2 + rb_ref[...]
        idxs, vals = [], []
        for _ in range(TOP_K):
            mx = jnp.max(lg, axis=-1, keepdims=True)
            ix = jnp.min(jnp.where(lg == mx, iota, float(N_EXPERTS)), axis=-1, keepdims=True)
            idxs.append(ix)
            vals.append(mx)
            lg = jnp.where(iota == ix, -jnp.inf, lg)
        exs = [jnp.exp(v - vals[0]) for v in vals]
        den = exs[0] + exs[1] + exs[2] + exs[3]

        onehot = [(iota == ix) for ix in idxs]
        cnt = jnp.zeros((sub, N_EXPERTS), F32)
        for oh in onehot:
            cnt = cnt + oh.astype(F32)
        prefix = jnp.dot(tri, cnt.astype(BF16), preferred_element_type=F32) + running

        e4 = jnp.zeros((sub, TOP_K), jnp.int32)
        p4 = jnp.zeros((sub, TOP_K), F32)
        r4 = jnp.zeros((sub, TOP_K), jnp.int32)
        for k in range(TOP_K):
            rk = jnp.sum(jnp.where(onehot[k], prefix, 0.0), axis=-1, keepdims=True)
            e4 = jnp.where(lane4 == k, idxs[k].astype(jnp.int32), e4)
            p4 = jnp.where(lane4 == k, exs[k] / den, p4)
            r4 = jnp.where(lane4 == k, rk.astype(jnp.int32), r4)
        e_ref[rs, :] = e4
        p_ref[rs, :] = p4
        rk_ref[rs, :] = r4
        running = running + jnp.sum(cnt, axis=0, keepdims=True)
    carry[...] = running
    cnt_ref[...] = running


def _outproj(ma, mb, w_bf, xs, t, g1, g2, modm, layer, rw, rb):
    tm = 256
    row = lambda i: layer * 8 + _mod_row(i, tm)
    modspec = lambda c: pl.BlockSpec((None, 1, D), lambda i: (row(i), 0, c))
    vec = pl.BlockSpec((1, D), lambda i: (0, 0))
    small = lambda dt: jax.ShapeDtypeStruct((t, TOP_K), dt)
    return pl.pallas_call(
        functools.partial(_outproj_kernel, tm=tm),
        grid=(t // tm,),
        in_specs=[
            pl.BlockSpec((tm, HALF), lambda i: (i, 0)),
            pl.BlockSpec((tm, HALF), lambda i: (i, 0)),
            pl.BlockSpec((D, D), lambda i: (0, 0)),
            pl.BlockSpec((tm, D), lambda i: (i, 0)),
            vec, vec, modspec(2), modspec(3), modspec(4),
            pl.BlockSpec((D, N_EXPERTS), lambda i: (0, 0)),
            pl.BlockSpec((1, N_EXPERTS), lambda i: (0, 0)),
        ],
        out_specs=[
            pl.BlockSpec((tm, D), lambda i: (i, 0)),
            pl.BlockSpec((tm * ROW_S, 128), lambda i: (i, 0)),
            pl.BlockSpec((tm, TOP_K), lambda i: (i, 0)),
            pl.BlockSpec((tm, TOP_K), lambda i: (i, 0)),
            pl.BlockSpec((tm, TOP_K), lambda i: (i, 0)),
            pl.BlockSpec((1, N_EXPERTS), lambda i: (0, 0)),
        ],
        out_shape=[
            jax.ShapeDtypeStruct((t, D), F32),
            jax.ShapeDtypeStruct((t * ROW_S, 128), U32),
            small(jnp.int32), small(F32), small(jnp.int32),
            jax.ShapeDtypeStruct((1, N_EXPERTS), F32),
        ],
        scratch_shapes=[pltpu.VMEM((1, N_EXPERTS), F32), pltpu.VMEM((D, 2 * N_EXPERTS), BF16)],
        compiler_params=_cparams(("arbitrary",), VMEM_LIMIT),
        name="outproj_router",
    )(ma, mb, w_bf, xs, g1.reshape(1, D), g2.reshape(1, D), modm, modm, modm,
      rw, rb.reshape(1, N_EXPERTS))


def _invmap_kernel(pstart_ref, cnt_ref, padded_ref, dest_ref, inv_ref, *, n_pairs, n_slots):
    def unused(lo, hi):
        def body(s, c):
            inv_ref[s] = n_pairs + jnp.bitwise_and(s, 2 * MOE_TM - 1)
            return c
        lax.fori_loop(lo, hi, body, 0)

    def per_expert(e, c):
        unused(pstart_ref[e] + cnt_ref[e], pstart_ref[e] + padded_ref[e])
        return c

    lax.fori_loop(0, N_EXPERTS, per_expert, 0)
    unused(pstart_ref[N_EXPERTS - 1] + padded_ref[N_EXPERTS - 1], n_slots)

    def body(i, c):
        inv_ref[dest_ref[i]] = i
        return c

    lax.fori_loop(0, n_pairs, body, 0, unroll=8)


def _invmap(dest, pstart, cnt, padded, n_slots):
    n_pairs = dest.shape[0]
    return pl.pallas_call(
        functools.partial(_invmap_kernel, n_pairs=n_pairs, n_slots=n_slots),
        grid_spec=pltpu.PrefetchScalarGridSpec(
            num_scalar_prefetch=3,
            grid=(1,),
            in_specs=[pl.BlockSpec(memory_space=pltpu.SMEM)],
            out_specs=pl.BlockSpec(memory_space=pltpu.SMEM),
        ),
        out_shape=jax.ShapeDtypeStruct((n_slots,), jnp.int32),
        compiler_params=_cparams(("arbitrary",)),
        name="moe_invmap",
    )(pstart, cnt, padded, dest)


W_CH = 256
W_NCH_GU = D // W_CH
W_NCH = W_NCH_GU + D_EXPERT // W_CH
W_STAGE = 4


def _expert_kernel(be_ref, nused_ref, first_ref, nxt_ref, lo_ref, hi_ref, slot_ref,
                   inv_cur, inv_next, inv_next2, inv_prev, bgu_ref, bd_ref, h_hbm, wgu_all, wd_all, y_hbm,
                   xbuf, obuf, x_s, act_s, wgu_buf, wd_buf, stage, gsem, ssem, sem, *, layer, n_tok):
    i = pl.program_id(0)
    n_live = nused_ref[0]
    s_cur = lax.rem(i, 3)
    s_p1 = lax.rem(i + 1, 3)
    s_p2 = lax.rem(i + 2, 3)
    wgu_hbm = wgu_all.at[layer]
    wd_hbm = wd_all.at[layer]

    def gather_copy(inv_ref, r, slot):
        tok = jnp.minimum(lax.shift_right_logical(inv_ref[r], 2), n_tok - 1)
        return pltpu.make_async_copy(_row_slab(h_hbm, tok), _row_slab(xbuf.at[slot], r), gsem.at[slot])

    n_pairs = n_tok * TOP_K

    def scatter_copy(inv_ref, r, slot, first_step=False):
        dst = inv_ref[r]
        if first_step:
            dst = jnp.where(i == 0, n_pairs + 2 * MOE_TM + r, dst)
        return pltpu.make_async_copy(_row_slab(obuf.at[slot], r), _row_slab(y_hbm, dst), ssem.at[slot])

    def looped(fn):
        def body(r, c):
            fn(r)
            return c
        lax.fori_loop(0, MOE_TM, body, 0)

    def start_chunk(e, c):
        s = lax.rem(c, W_STAGE)

        @pl.when(c < W_NCH_GU)
        def _():
            r0 = pl.multiple_of(c * W_CH, W_CH)
            pltpu.make_async_copy(wgu_hbm.at[e, pl.ds(r0, W_CH), :], stage.at[s], sem.at[s]).start()

        @pl.when(c >= W_NCH_GU)
        def _():
            r0 = pl.multiple_of((c - W_NCH_GU) * W_CH, W_CH)
            pltpu.make_async_copy(wd_hbm.at[e, pl.ds(r0, W_CH), :], stage.at[s], sem.at[s]).start()

    def finish_chunk(c, dst):
        s = lax.rem(c, W_STAGE)
        pltpu.make_async_copy(wgu_hbm.at[0, pl.ds(0, W_CH), :], stage.at[s], sem.at[s]).wait()
        w = stage[s].astype(BF16)

        @pl.when(c < W_NCH_GU)
        def _():
            wgu_buf[dst, pl.ds(pl.multiple_of(c * W_CH, W_CH), W_CH), :] = w

        @pl.when(c >= W_NCH_GU)
        def _():
            wd_buf[dst, pl.ds(pl.multiple_of((c - W_NCH_GU) * W_CH, W_CH), W_CH), :] = w

    def stream(e, dst, c_lo, c_hi):
        def body(c, carry):
            finish_chunk(c, dst)

            @pl.when(c + W_STAGE < W_NCH)
            def _():
                start_chunk(e, c + W_STAGE)
            return carry
        lax.fori_loop(c_lo, c_hi, body, 0)

    def prime(e):
        for c in range(W_STAGE):
            start_chunk(e, c)

    @pl.when(i == 0)
    def _():
        looped(lambda r: gather_copy(inv_cur, r, 0).start())
        looped(lambda r: gather_copy(inv_next, r, 1).start())
        obuf[2] = jnp.zeros(obuf.shape[1:], U32)
        for region in range(2):
            dump = y_hbm.at[pl.ds((n_pairs + region * MOE_TM) * ROW_S, MOE_TM * ROW_S), :]
            zero_dump = pltpu.make_async_copy(obuf.at[2], dump, ssem.at[2])
            zero_dump.start()
            zero_dump.wait()
        prime(be_ref[0])
        stream(be_ref[0], slot_ref[0], 0, W_NCH)

    @pl.when(i < n_live)
    def _():
        e_next = nxt_ref[i]
        cur = slot_ref[i]

        @pl.when((first_ref[i] == 1) & (e_next >= 0))
        def _():
            prime(e_next)

        def block_body(c_cur, c_p1, c_p2):
            for r in range(MOE_TM):
                gather_copy(inv_cur, r, c_cur).wait()
            for s in range(ROW_S):
                hi, lo = _unpack_piece(xbuf[c_cur, pl.ds(s, MOE_TM, stride=ROW_S), :])
                x_s[:, s * 128:(s + 1) * 128] = hi.astype(BF16)
                x_s[:, (s + ROW_S) * 128:(s + ROW_S + 1) * 128] = lo.astype(BF16)
            nc = 256

            def gate_up(c_lo, c_hi):
                for c0 in range(c_lo, c_hi, nc):
                    g = (jnp.dot(x_s[...], wgu_buf[cur, :, c0:c0 + nc], preferred_element_type=F32)
                         + bgu_ref[:, c0:c0 + nc])
                    u = (jnp.dot(x_s[...], wgu_buf[cur, :, D_EXPERT + c0:D_EXPERT + c0 + nc],
                                 preferred_element_type=F32) + bgu_ref[:, D_EXPERT + c0:D_EXPERT + c0 + nc])
                    gate = jnp.minimum(g, SWIGLU_LIMIT)
                    up = jnp.clip(u, -SWIGLU_LIMIT, SWIGLU_LIMIT)
                    act_s[:, c0:c0 + nc] = (gate * jax.nn.sigmoid(SWIGLU_ALPHA * gate)
                                            * (up + 1.0)).astype(BF16)

            for r in range(MOE_TM):
                scatter_copy(inv_prev, r, c_p2, first_step=True).start()
            gate_up(0, D_EXPERT // 2)

            @pl.when(n_live > 0)
            def _():
                for r in range(MOE_TM // 2):
                    gather_copy(inv_next2, r, c_p2).start()
                gate_up(D_EXPERT // 2, D_EXPERT)

            @pl.when(first_ref[i] >= 0)
            def _():
                for r in range(MOE_TM // 2, MOE_TM):
                    gather_copy(inv_next2, r, c_p2).start()
                for c0 in range(0, D // 2, nc):
                    y_hi = (jnp.dot(act_s[...], wd_buf[cur, :, c0:c0 + nc], preferred_element_type=F32)
                            + bd_ref[:, c0:c0 + nc])
                    y_lo = (jnp.dot(act_s[...], wd_buf[cur, :, D // 2 + c0:D // 2 + c0 + nc],
                                    preferred_element_type=F32) + bd_ref[:, D // 2 + c0:D // 2 + c0 + nc])
                    for j in range(nc // 128):
                        s = c0 // 128 + j
                        obuf[c_cur, pl.ds(s, MOE_TM, stride=ROW_S), :] = _pack_pieces(
                            y_hi[:, j * 128:(j + 1) * 128], y_lo[:, j * 128:(j + 1) * 128])

            @pl.when(i >= 1)
            def _():
                for r in range(MOE_TM):
                    scatter_copy(inv_prev, r, c_p1).wait()

        for ring in range(3):
            @pl.when(s_cur == ring)
            def _():
                block_body(ring, (ring + 1) % 3, (ring + 2) % 3)

        @pl.when(e_next >= 0)
        def _():
            stream(e_next, 1 - cur, lo_ref[i], hi_ref[i])

        @pl.when(i == n_live - 1)
        def _():
            looped(lambda r: scatter_copy(inv_prev, r, s_p2).wait())
            looped(lambda r: scatter_copy(inv_cur, r, s_cur).start())
            looped(lambda r: scatter_copy(inv_cur, r, s_cur).wait())
            looped(lambda r: gather_copy(inv_next, r, s_p1).wait())
            looped(lambda r: gather_copy(inv_next2, r, s_p2).wait())


def _take(table, idx):
    ids = jnp.arange(table.shape[0], dtype=jnp.int32)
    return jnp.sum(jnp.where(idx[..., None] == ids, table, 0), axis=-1)


def _experts(h2, inv, block_e, nused, pstart, padded, layer, wgu, bgu, wd, bd, nblk, n_tok):
    nb_e = padded // MOE_TM
    pos = jnp.arange(nblk, dtype=jnp.int32) - _take(pstart // MOE_TM, block_e)
    nb = jnp.maximum(_take(nb_e, block_e), 1)
    lo = (pos * W_NCH) // nb
    hi = ((pos + 1) * W_NCH) // nb
    first = (pos == 0).astype(jnp.int32)
    eid = jnp.arange(N_EXPERTS, dtype=jnp.int32)
    later = jnp.where(nb_e > 0, eid, N_EXPERTS)
    nxt_e = jnp.concatenate([lax.cummin(later, reverse=True)[1:], jnp.full((1,), N_EXPERTS, jnp.int32)])
    nxt_e = jnp.where(nxt_e >= N_EXPERTS, -1, nxt_e)
    slot_e = (jnp.cumsum((nb_e > 0).astype(jnp.int32)) - 1) % 2
    i32 = lambda a: a.astype(jnp.int32)
    live = lambda i, nu: jnp.clip(i, 0, nu[0] - 1)
    sblk = lambda off: pl.BlockSpec((MOE_TM,), lambda i, be, nu, *_: (live(i + off, nu),),
                                    memory_space=pltpu.SMEM)
    bias = lambda n: pl.BlockSpec((None, 1, n), lambda i, be, nu, *_: (be[live(i, nu)], 0, 0))
    hbm = pl.BlockSpec(memory_space=pl.ANY)
    n_slabs = n_tok * TOP_K + 3 * MOE_TM
    return pl.pallas_call(
        functools.partial(_expert_kernel, layer=layer, n_tok=n_tok),
        grid_spec=pltpu.PrefetchScalarGridSpec(
            num_scalar_prefetch=7,
            grid=(nblk,),
            in_specs=[sblk(0), sblk(1), sblk(2), sblk(-1), bias(2 * D_EXPERT), bias(D), hbm, hbm, hbm],
            out_specs=hbm,
            scratch_shapes=[
                pltpu.VMEM((3, MOE_TM * ROW_S, 128), U32),
                pltpu.VMEM((3, MOE_TM * ROW_S, 128), U32),
                pltpu.VMEM((MOE_TM, D), BF16),
                pltpu.VMEM((MOE_TM, D_EXPERT), BF16),
                pltpu.VMEM((2, D, 2 * D_EXPERT), BF16),
                pltpu.VMEM((2, D_EXPERT, D), BF16),
                pltpu.VMEM((W_STAGE, W_CH, D), F32),
                pltpu.SemaphoreType.DMA((3,)),
                pltpu.SemaphoreType.DMA((3,)),
                pltpu.SemaphoreType.DMA((W_STAGE,)),
            ],
        ),
        out_shape=jax.ShapeDtypeStruct((n_slabs * ROW_S, 128), U32),
        compiler_params=_cparams(("arbitrary",), VMEM_LIMIT),
        name="moe_experts",
    )(block_e, nused, i32(first), i32(_take(nxt_e, block_e)), i32(lo), i32(hi), i32(_take(slot_e, block_e)),
      inv, inv, inv, inv, bgu.reshape(N_EXPERTS, 1, -1), bd.reshape(N_EXPERTS, 1, -1), h2, wgu, wd)


def _combine_kernel(p_ref, x_ref, g3_ref, ga2_ref, y_ref, xo_ref, acc_hi, acc_lo, *, tm):
    pair = TOP_K * ROW_S
    for r in range(tm):
        a_hi = a_lo = None
        for k in range(TOP_K):
            hi, lo = _unpack_piece(y_ref[pl.ds(r * pair + k * ROW_S, ROW_S), :])
            w = p_ref[r * TOP_K + k]
            a_hi = hi * w if k == 0 else a_hi + hi * w
            a_lo = lo * w if k == 0 else a_lo + lo * w
        acc_hi[pl.ds(r * ROW_S, ROW_S), :] = a_hi
        acc_lo[pl.ds(r * ROW_S, ROW_S), :] = a_lo
    y = jnp.concatenate([acc_hi[pl.ds(s, tm, stride=ROW_S), :] for s in range(ROW_S)]
                        + [acc_lo[pl.ds(s, tm, stride=ROW_S), :] for s in range(ROW_S)], axis=1)
    xo_ref[...] = x_ref[...] + ga2_ref[...] * (_rms(y) * g3_ref[...])


def _combine(y4, p4, xs, g3, modm, layer):
    tm = 128
    t = xs.shape[0]
    row = lambda i: layer * 8 + _mod_row(i, tm)
    return pl.pallas_call(
        functools.partial(_combine_kernel, tm=tm),
        grid=(t // tm,),
        in_specs=[
            pl.BlockSpec((tm * TOP_K,), lambda i: (i,), memory_space=pltpu.SMEM),
            pl.BlockSpec((tm, D), lambda i: (i, 0)),
            pl.BlockSpec((1, D), lambda i: (0, 0)),
            pl.BlockSpec((None, 1, D), lambda i: (row(i), 0, 5)),
            pl.BlockSpec((tm * TOP_K * ROW_S, 128), lambda i: (i, 0)),
        ],
        out_specs=pl.BlockSpec((tm, D), lambda i: (i, 0)),
        out_shape=jax.ShapeDtypeStruct((t, D), F32),
        scratch_shapes=[pltpu.VMEM((tm * ROW_S, 128), F32)] * 2,
        compiler_params=_cparams(("parallel",), VMEM_LIMIT),
        name="moe_combine",
    )(p4.reshape(-1), xs, g3.reshape(1, D), modm, y4)


def _moe(h2, e4, p4, r4, counts, xs, g3, modm, layer, wgu, bgu, wd, bd):
    t = e4.shape[0]
    nblk = -(-(t * TOP_K + N_EXPERTS * (MOE_TM - 1)) // MOE_TM)
    cnt = counts.reshape(N_EXPERTS).astype(jnp.int32)
    padded = (cnt + MOE_TM - 1) // MOE_TM * MOE_TM
    pend = jnp.cumsum(padded)
    pstart = pend - padded
    dest = (_take(pstart, e4) + r4).reshape(-1).astype(jnp.int32)
    nused = (pend[-1:] // MOE_TM).astype(jnp.int32)
    blk_start = jnp.arange(nblk, dtype=jnp.int32) * MOE_TM
    block_e = jnp.minimum(jnp.sum((pend[None, :] <= blk_start[:, None]).astype(jnp.int32), axis=1),
                          N_EXPERTS - 1).astype(jnp.int32)
    inv = _invmap(dest, pstart.astype(jnp.int32), cnt, padded.astype(jnp.int32), nblk * MOE_TM)
    y4 = _experts(h2, inv, block_e, nused, pstart, padded, layer, wgu, bgu, wd, bd, nblk, t)
    return _combine(y4, p4, xs, g3, modm, layer)


N_CTX_BLK = LC // HGRN_R
N_LAT_BLK = L // HGRN_R
N_SEQ_BLK = N_CTX_BLK + N_LAT_BLK


def _scan_block(b, s, rev):
    if rev:
        return jnp.where(s < N_CTX_BLK, T_LAT // HGRN_R + N_CTX_BLK * b + (N_CTX_BLK - 1 - s),
                         N_LAT_BLK * b + (N_SEQ_BLK - 1 - s))
    return jnp.where(s < N_CTX_BLK, T_LAT // HGRN_R + N_CTX_BLK * b + s, N_LAT_BLK * b + (s - N_CTX_BLK))


def _hgrn_kernel(*refs, rev):
    if rev:
        q_ref, f_ref, v_ref, lb_ref, of_ref, g_ref, on_ref, o_ref, st, qin_s, qd_s, ki_s, ke_s, v_s = refs
    else:
        q_ref, f_ref, v_ref, lb_ref, o_ref, st, qin_s, qd_s, ki_s, ke_s, v_s = refs

    @pl.when(pl.program_id(1) == 0)
    def _():
        st[...] = jnp.zeros_like(st)

    c = HGRN_C
    nc = HGRN_R // c
    r_i = lax.broadcasted_iota(jnp.int32, (c, c), 0)
    c_i = lax.broadcasted_iota(jnp.int32, (c, c), 1)
    mask = (c_i >= r_i) if rev else (c_i <= r_i)
    tri = mask.astype(F32)

    lb = lb_ref[...]
    decay = [None] * nc
    v_s[...] = v_ref[...].astype(BF16)
    for ci in range(nc):
        rs = slice(ci * c, (ci + 1) * c)
        sg = jax.nn.sigmoid(f_ref[rs, :])
        logf = jnp.log(lb + (1.0 - lb) * sg)
        kk = (1.0 - lb) * (1.0 - sg)
        q = q_ref[rs, :]
        qs = q * jax.nn.sigmoid(q)
        cum = jnp.dot(tri, logf, precision=HI, preferred_element_type=F32)
        total = cum[0:1] if rev else cum[c - 1:c]
        mid = cum[c // 2:c // 2 + 1]
        qin_s[rs, :] = (qs * jnp.exp(cum)).astype(BF16)
        qd_s[rs, :] = (qs * jnp.exp(cum - mid)).astype(BF16)
        ki_s[rs, :] = (kk * jnp.exp(mid - cum)).astype(BF16)
        ke_s[rs, :] = (kk * jnp.exp(total - cum)).astype(BF16)
        decay[ci] = jnp.exp(total)

    order = range(nc - 1, -1, -1) if rev else range(nc)
    for ci in order:
        rs = slice(ci * c, (ci + 1) * c)
        for h in range(HGRN_HEADS):
            hs = slice(h * 128, (h + 1) * 128)
            state = st[h]
            vb = v_s[rs, hs]
            sc = lax.dot_general(qd_s[rs, hs], ki_s[rs, hs], (((1,), (1,)), ((), ())),
                                 preferred_element_type=F32)
            sc = jnp.where(mask, sc, 0.0)
            intra = jnp.dot(sc.astype(BF16), vb, preferred_element_type=F32)
            inter = lax.dot_general(qin_s[rs, hs], state.astype(BF16), (((1,), (1,)), ((), ())),
                                    preferred_element_type=F32)
            st[h] = state * decay[ci][:, hs] + lax.dot_general(
                vb, ke_s[rs, hs], (((0,), (0,)), ((), ())), preferred_element_type=F32)
            o = intra + inter
            if rev:
                o = o + of_ref[rs, hs]
                g = g_ref[rs, hs]
                o_ref[rs, hs] = (_rms(o) * on_ref[:, hs] * (g * jax.nn.sigmoid(g))).astype(BF16)
            else:
                o_ref[rs, hs] = o


def _hgrn(pa, pb, lb, rev, of=None, onorm=None):
    r = HGRN_R
    blk = lambda b, s: _scan_block(b, s, rev)
    col = lambda c: pl.BlockSpec((r, HALF), lambda b, s: (blk(b, s), c))
    vec = pl.BlockSpec((1, HALF), lambda b, s: (0, 0))
    in_specs = [col(0), col(2 if rev else 1), col(3), vec]
    args = [pa, pa, pa, lb.reshape(1, HALF)]
    if rev:
        in_specs += [col(0), col(0), vec]
        args += [of, pb, onorm.reshape(1, HALF)]
    return pl.pallas_call(
        functools.partial(_hgrn_kernel, rev=rev),
        grid=(B, N_SEQ_BLK),
        in_specs=in_specs,
        out_specs=col(0),
        out_shape=jax.ShapeDtypeStruct((T_ALL, HALF), BF16 if rev else F32),
        scratch_shapes=[pltpu.VMEM((HGRN_HEADS, 128, 128), F32)] + [pltpu.VMEM((r, HALF), BF16)] * 5,
        compiler_params=_cparams(("parallel", "arbitrary"), VMEM_LIMIT),
        name="hgrn_bwd" if rev else "hgrn_fwd",
    )(*args)


def _lru_kernel(*refs, rev):
    if rev:
        (x_ref, xp_ref, xn_ref, cw_ref, cb_ref, wr_ref, br_ref, wi_ref, bi_ref, lam_ref,
         hf_ref, gt_ref, o_ref, xcat, a_s, u_s, h_s, carry) = refs
    else:
        (x_ref, xp_ref, xn_ref, cw_ref, cb_ref, wr_ref, br_ref, wi_ref, bi_ref, lam_ref,
         o_ref, xcat, a_s, u_s, h_s, carry) = refs
    s = pl.program_id(1)
    r = LRU_R

    @pl.when(s == 0)
    def _():
        carry[...] = jnp.zeros_like(carry)

    is_ctx = s < N_CTX_BLK
    if rev:
        j = jnp.where(is_ctx, N_CTX_BLK - 1 - s, N_SEQ_BLK - 1 - s)
    else:
        j = jnp.where(is_ctx, s, s - N_CTX_BLK)
    nb = jnp.where(is_ctx, N_CTX_BLK, N_LAT_BLK)
    xcat[0:8, :] = jnp.where(j == 0, 0.0, xp_ref[...])
    xcat[8:8 + r, :] = x_ref[...]
    xcat[8 + r:16 + r, :] = jnp.where(j == nb - 1, 0.0, xn_ref[...])
    xc = cb_ref[...] + xcat[pl.ds(6, r), :] * cw_ref[0:1, :]
    for t in range(1, 4):
        xc = xc + xcat[pl.ds(6 + t, r), :] * cw_ref[t:t + 1, :]

    lam = lam_ref[...]
    sp = jnp.maximum(-lam, 0.0) + jnp.log1p(jnp.exp(-jnp.abs(lam)))
    hd = HALF // LRU_HEADS
    for h in range(LRU_HEADS):
        cs = slice(h * hd, (h + 1) * hd)
        xh = xc[:, cs]
        rg = jax.nn.sigmoid(_bdot(xh, wr_ref[h]) + br_ref[:, cs])
        ig = jax.nn.sigmoid(_bdot(xh, wi_ref[h]) + bi_ref[:, cs])
        log_a = -LRU_C * rg * sp[:, cs]
        a = jnp.exp(log_a)
        a_s[:, cs] = a
        u_s[:, cs] = jnp.sqrt(-jnp.tanh(log_a) * (a * a + 1.0)) * (ig * xh)

    def step(t, h):
        tt = (r - 1 - t) if rev else t
        h = a_s[pl.ds(tt, 1), :] * h + u_s[pl.ds(tt, 1), :]
        h_s[pl.ds(tt, 1), :] = h
        return h

    carry[...] = lax.fori_loop(0, r, step, carry[...], unroll=8)
    if rev:
        o_ref[...] = ((h_s[...] + hf_ref[...]) * jax.nn.gelu(gt_ref[...])).astype(BF16)
    else:
        o_ref[...] = h_s[...]


def _lru(p, conv_w, conv_b, w_r, b_r, w_i, b_i, lam, rev, hf=None):
    r = LRU_R
    d = 1 if rev else 0
    blk = lambda b, s: _scan_block(b, s, rev)
    xcol = 2
    n8 = T_ALL // 8
    vec = lambda: pl.BlockSpec((None, 1, HALF), lambda b, s: (d, 0, 0))
    wspec = lambda: pl.BlockSpec((None, LRU_HEADS, 128, 128), lambda b, s: (d, 0, 0, 0))
    in_specs = [
        pl.BlockSpec((r, HALF), lambda b, s: (blk(b, s), xcol)),
        pl.BlockSpec((8, HALF), lambda b, s: (jnp.maximum(blk(b, s) * (r // 8) - 1, 0), xcol)),
        pl.BlockSpec((8, HALF), lambda b, s: (jnp.minimum((blk(b, s) + 1) * (r // 8), n8 - 1), xcol)),
        pl.BlockSpec((4, HALF), lambda b, s: (0, 0)),
        pl.BlockSpec((1, HALF), lambda b, s: (0, 0)),
        wspec(), vec(), wspec(), vec(), vec(),
    ]
    args = [p, p, p, conv_w, conv_b.reshape(1, HALF), w_r, b_r.reshape(2, 1, HALF), w_i,
            b_i.reshape(2, 1, HALF), lam.reshape(2, 1, HALF)]
    if rev:
        in_specs += [pl.BlockSpec((r, HALF), lambda b, s: (blk(b, s), 0)),
                     pl.BlockSpec((r, HALF), lambda b, s: (blk(b, s), xcol - 1))]
        args += [hf, p]
    return pl.pallas_call(
        functools.partial(_lru_kernel, rev=rev),
        grid=(B, N_SEQ_BLK),
        in_specs=in_specs,
        out_specs=pl.BlockSpec((r, HALF), lambda b, s: (blk(b, s), 0)),
        out_shape=jax.ShapeDtypeStruct((T_ALL, HALF), BF16 if rev else F32),
        scratch_shapes=[pltpu.VMEM((r + 16, HALF), F32), pltpu.VMEM((r, HALF), F32),
                        pltpu.VMEM((r, HALF), F32), pltpu.VMEM((r, HALF), F32),
                        pltpu.VMEM((1, HALF), F32)],
        compiler_params=_cparams(("parallel", "arbitrary"), VMEM_LIMIT),
        name="lru_bwd" if rev else "lru_fwd",
    )(*args)


def kernel(x, c, ctx, c_ctx, mod_w, mod_b, norm_g, ab_w_in, ab_w_out, gmlp_v_g, gmlp_ws, gmlp_bs, cd_w_in, cd_w_out, hgrn_lb, hgrn_onorm_g, lru_conv_w, lru_conv_b, lru_wr, lru_br, lru_wi, lru_bi, lru_lambda, router_w, router_b, exp_w_gu, exp_b_gu, exp_w_down, exp_b_down):
    cond8 = jnp.concatenate([c, c_ctx[None, :], jnp.zeros((5, D), F32)], axis=0)
    modm = _modulation(cond8, mod_w, mod_b).reshape(DEPTH * 8, 1, 6 * D)
    xs = jnp.concatenate([x.reshape(T_LAT, D), ctx.reshape(T_CTX, D)], axis=0)

    lb_soft = jax.nn.softmax(hgrn_lb.astype(F32), axis=0)
    lb_all = jnp.cumsum(lb_soft, axis=0) - lb_soft[0]

    (p0,) = _inproj(xs, norm_g[0, 0], modm, 0, [ab_w_in[0].astype(BF16)], BF16)
    zc, zs = _fnet_chan(p0)
    mix_a = jnp.concatenate([_fnet_pos(zc, zs, 0, L, 512, 1024),
                             _fnet_pos(zc, zs, T_LAT, LC, LC, LC)], axis=0)
    mix_g = _gmlp(p0, gmlp_v_g[0], gmlp_ws[0], gmlp_bs[0])
    xs, h2, e4, p4, r4, counts = _outproj(mix_a, mix_g, ab_w_out[0].astype(BF16), xs, T_ALL, norm_g[0, 1],
                                          norm_g[0, 2], modm, 0, router_w[0], router_b[0])
    xs = _moe(h2, e4, p4, r4, counts, xs, norm_g[0, 3], modm, 0,
              exp_w_gu, exp_b_gu[0], exp_w_down, exp_b_down[0])

    w_cd = cd_w_in[0].astype(BF16)
    p1a, p1b = _inproj(xs, norm_g[1, 0], modm, 1, [w_cd[:, :4 * HALF], w_cd[:, 4 * HALF:]], F32)
    o_f = _hgrn(p1a, p1b, lb_all[1], False)
    hg = _hgrn(p1a, p1b, lb_all[1], True, o_f, hgrn_onorm_g[0])
    lru_args = (lru_conv_w[0], lru_conv_b[0], lru_wr[0], lru_br[0], lru_wi[0], lru_bi[0], lru_lambda[0])
    h_f = _lru(p1b, *lru_args, False)
    lr = _lru(p1b, *lru_args, True, h_f)
    xl, h2, e4, p4, r4, counts = _outproj(hg, lr, cd_w_out[0].astype(BF16), xs, T_LAT, norm_g[1, 1],
                                          norm_g[1, 2], modm, 1, router_w[1], router_b[1])
    xl = _moe(h2, e4, p4, r4, counts, xl, norm_g[1, 3], modm, 1,
              exp_w_gu, exp_b_gu[1], exp_w_down, exp_b_down[1])
    return xl.reshape(B, L, D)
```

```python
import functools

import numpy as np
import jax
import jax.numpy as jnp
from jax import lax
from jax.experimental import pallas as pl
from jax.experimental.pallas import tpu as pltpu

F32 = jnp.float32
BF16 = jnp.bfloat16
HI = lax.Precision.HIGHEST

D = 2048
B = 2
L = 4096
LC = 256
T_LAT = B * L
T_CTX = B * LC
T_ALL = T_LAT + T_CTX
DEPTH = 2
EPS = 1e-6

HALF = D // 2
N_EXPERTS = 32
TOP_K = 4
D_EXPERT = D // 2
SWIGLU_LIMIT = 7.0
SWIGLU_ALPHA = 1.702
MOE_TM = 256

FNET_GROUPS = 4
FNET_GD = HALF // FNET_GROUPS
GMLP_HEADS = 8
GMLP_CHUNK = 128
HGRN_HEADS = 8
HGRN_R = 256
HGRN_C = 64
LRU_HEADS = 8
LRU_R = HGRN_R
LRU_C = 8.0

VMEM_LIMIT = 56 * 1024 * 1024


def _cparams(sem, vmem=None):
    return pltpu.CompilerParams(dimension_semantics=sem, vmem_limit_bytes=vmem)


def _mod_row(i, tm):
    return jnp.where(i < L // tm, 0, jnp.where(i < 2 * L // tm, 1, 2))


def _rms(x):
    return x * lax.rsqrt(jnp.mean(x * x, axis=-1, keepdims=True) + EPS)


def _bdot(a, b):
    return jnp.dot(a.astype(BF16), b.astype(BF16), preferred_element_type=F32)


ROW_S = D // 256
U32 = jnp.uint32
HI_MASK = 0xFFFF0000


def _row_slab(ref, r):
    return ref.at[pl.ds(pl.multiple_of(r * ROW_S, ROW_S), ROW_S), :]


def _pack_pieces(hi, lo):
    hb = pltpu.bitcast(hi.astype(BF16).astype(F32), U32)
    lb = pltpu.bitcast(lo.astype(BF16).astype(F32), U32)
    return jnp.bitwise_or(jnp.bitwise_and(hb, U32(HI_MASK)), jnp.right_shift(lb, U32(16)))


def _unpack_piece(w):
    hi = pltpu.bitcast(jnp.bitwise_and(w, U32(HI_MASK)), F32)
    lo = pltpu.bitcast(jnp.left_shift(w, U32(16)), F32)
    return hi, lo


def _store_rows(ref, val, row0=0):
    n = val.shape[0]
    for s in range(ROW_S):
        ref[pl.ds(row0 * ROW_S + s, n, stride=ROW_S), :] = _pack_pieces(
            val[:, s * 128:(s + 1) * 128], val[:, (s + ROW_S) * 128:(s + ROW_S + 1) * 128])


def _mod_kernel(c_ref, w_ref, b_ref, o_ref):
    cnd = c_ref[...]
    s = cnd * jax.nn.sigmoid(cnd)
    o_ref[...] = _bdot(s, w_ref[...]) + b_ref[...]


def _modulation(cond8, mod_w, mod_b):
    tn = 1024
    n = mod_w.shape[-1]
    return pl.pallas_call(
        _mod_kernel,
        grid=(DEPTH, n // tn),
        in_specs=[
            pl.BlockSpec((8, D), lambda l, j: (0, 0)),
            pl.BlockSpec((None, D, tn), lambda l, j: (l, 0, j)),
            pl.BlockSpec((None, 1, tn), lambda l, j: (l, 0, j)),
        ],
        out_specs=pl.BlockSpec((None, 8, tn), lambda l, j: (l, 0, j)),
        out_shape=jax.ShapeDtypeStruct((DEPTH, 8, n), F32),
        compiler_params=_cparams(("parallel", "parallel"), VMEM_LIMIT),
        name="modulation",
    )(cond8, mod_w, mod_b.reshape(DEPTH, 1, n))


def _inproj_kernel(*refs, nw):
    x_ref, g_ref, sh_ref, sc_ref = refs[:4]
    w_refs, o_refs = refs[4:4 + nw], refs[4 + nw:]
    y = _rms(x_ref[...]) * g_ref[...]
    h = (y * (1.0 + sc_ref[...]) + sh_ref[...]).astype(BF16)
    for w_ref, o_ref in zip(w_refs, o_refs):
        o_ref[...] = jnp.dot(h, w_ref[...], preferred_element_type=F32).astype(o_ref.dtype)


def _inproj(xs, g, modm, layer, ws_bf, out_dtype):
    tm = 256
    t = xs.shape[0]
    row = lambda i: layer * 8 + _mod_row(i, tm)
    return pl.pallas_call(
        functools.partial(_inproj_kernel, nw=len(ws_bf)),
        grid=(t // tm,),
        in_specs=[
            pl.BlockSpec((tm, D), lambda i: (i, 0)),
            pl.BlockSpec((1, D), lambda i: (0, 0)),
            pl.BlockSpec((None, 1, D), lambda i: (row(i), 0, 0)),
            pl.BlockSpec((None, 1, D), lambda i: (row(i), 0, 1)),
        ] + [pl.BlockSpec(w.shape, lambda i: (0, 0), pipeline_mode=pl.Buffered(1)) for w in ws_bf],
        out_specs=[pl.BlockSpec((tm, w.shape[1]), lambda i: (i, 0)) for w in ws_bf],
        out_shape=[jax.ShapeDtypeStruct((t, w.shape[1]), out_dtype) for w in ws_bf],
        compiler_params=_cparams(("parallel",), VMEM_LIMIT),
        name="inproj",
    )(xs, g.reshape(1, D), modm, modm, *ws_bf)


def _fnet_chan_kernel(p_ref, cs_ref, zc_ref, zs_ref):
    for g in range(FNET_GROUPS):
        sl = slice(g * FNET_GD, (g + 1) * FNET_GD)
        z = jnp.dot(p_ref[:, sl].astype(BF16), cs_ref[...], preferred_element_type=F32)
        zc_ref[:, sl] = z[:, :FNET_GD].astype(BF16)
        zs_ref[:, sl] = z[:, FNET_GD:].astype(BF16)


def _fnet_chan(p):
    tm = 512
    t = p.shape[0]
    k = np.arange(FNET_GD)
    ang = 2.0 * np.pi * ((k[:, None] * k[None, :]) % FNET_GD) / FNET_GD
    cs = np.concatenate([np.cos(ang), np.sin(ang)], axis=1) / np.sqrt(FNET_GD)
    cs = jnp.asarray(cs, F32).astype(BF16)
    return pl.pallas_call(
        _fnet_chan_kernel,
        grid=(t // tm,),
        in_specs=[
            pl.BlockSpec((tm, HALF), lambda i: (i, 0)),
            pl.BlockSpec((FNET_GD, 2 * FNET_GD), lambda i: (0, 0)),
        ],
        out_specs=[pl.BlockSpec((tm, HALF), lambda i: (i, 0))] * 2,
        out_shape=[jax.ShapeDtypeStruct((t, HALF), BF16)] * 2,
        compiler_params=_cparams(("parallel",), VMEM_LIMIT),
        name="fnet_chan",
    )(p, cs)


def _fnet_pos_kernel(bre_ref, bim_ref, cre_ref, cim_ref, rre_ref, rim_ref, zc_ref, zs_ref, o_ref, acc):
    nt = pl.program_id(2)
    br, bi = bre_ref[...], bim_ref[...]
    cr, ci = cre_ref[...], cim_ref[...]
    rr, ri = rre_ref[...], rim_ref[...]
    tr = br * cr - bi * ci
    ti = br * ci + bi * cr
    er = (tr * rr - ti * ri).astype(BF16)
    ei = (tr * ri + ti * rr).astype(BF16)
    part = (jnp.dot(er, zc_ref[...], preferred_element_type=F32)
            + jnp.dot(ei, zs_ref[...], preferred_element_type=F32))

    @pl.when(nt == 0)
    def _():
        acc[...] = part

    @pl.when(nt > 0)
    def _():
        acc[...] += part

    @pl.when(nt == pl.num_programs(2) - 1)
    def _():
        o_ref[...] = acc[...].astype(BF16)


def _fnet_pos(zc, zs, row0, seq, tk, tn):
    nkt, nnt = seq // tk, seq // tn
    th = 2.0 * np.pi / seq
    kk = np.arange(tk)[:, None]
    nn = np.arange(tn)[None, :]
    base = th * ((kk * nn) % seq)
    n0 = (np.arange(nnt) * tn)[:, None, None]
    col = th * ((np.arange(tk)[None, :, None] * n0) % seq)
    k0 = (np.arange(nkt) * tk)[:, None, None, None]
    nfull = (np.arange(nnt) * tn)[None, :, None, None] + np.arange(tn)[None, None, None, :]
    row = th * ((k0 * nfull) % seq)
    scale = 1.0 / np.sqrt(seq)
    f = lambda a: jnp.asarray(a, F32)
    tabs = (f(np.cos(base)), f(-np.sin(base)), f(np.cos(col)), f(-np.sin(col)),
            f(np.cos(row) * scale), f(-np.sin(row) * scale))
    rb = row0 // tn
    ob = row0 // tk
    return pl.pallas_call(
        _fnet_pos_kernel,
        grid=(B, nkt, nnt),
        in_specs=[
            pl.BlockSpec((tk, tn), lambda b, k, n: (0, 0)),
            pl.BlockSpec((tk, tn), lambda b, k, n: (0, 0)),
            pl.BlockSpec((None, tk, 1), lambda b, k, n: (n, 0, 0)),
            pl.BlockSpec((None, tk, 1), lambda b, k, n: (n, 0, 0)),
            pl.BlockSpec((None, None, 1, tn), lambda b, k, n: (k, n, 0, 0)),
            pl.BlockSpec((None, None, 1, tn), lambda b, k, n: (k, n, 0, 0)),
            pl.BlockSpec((tn, HALF), lambda b, k, n: (rb + b * nnt + n, 0)),
            pl.BlockSpec((tn, HALF), lambda b, k, n: (rb + b * nnt + n, 0)),
        ],
        out_specs=pl.BlockSpec((tk, HALF), lambda b, k, n: (b * nkt + k, 0)),
        out_shape=jax.ShapeDtypeStruct((B * seq, HALF), BF16),
        scratch_shapes=[pltpu.VMEM((tk, HALF), F32)],
        compiler_params=_cparams(("parallel", "parallel", "arbitrary"), VMEM_LIMIT),
        name=f"fnet_pos_{seq}",
    )(*tabs, zc, zs)


def _gmlp_kernel(u_ref, v_ref, vg_ref, ws_ref, bst_ref, o_ref, *, tm):
    zu = jax.nn.gelu(u_ref[...].astype(F32))
    zv = jax.nn.gelu(v_ref[...].astype(F32))
    mu = jnp.mean(zv, axis=-1, keepdims=True)
    dv = zv - mu
    var = jnp.mean(dv * dv, axis=-1, keepdims=True)
    vn = dv * lax.rsqrt(var + EPS) * vg_ref[...]
    hd = HALF // GMLP_HEADS
    for c in range(tm // GMLP_CHUNK):
        rs = slice(c * GMLP_CHUNK, (c + 1) * GMLP_CHUNK)
        for h in range(GMLP_HEADS):
            cs = slice(h * hd, (h + 1) * hd)
            s = _bdot(ws_ref[h], vn[rs, cs])
            s = s + bst_ref[:, h:h + 1]
            o_ref[rs, cs] = (zu[rs, cs] * s).astype(BF16)


def _gmlp(p, v_g, w_s, b_s):
    tm = 256
    t = p.shape[0]
    return pl.pallas_call(
        functools.partial(_gmlp_kernel, tm=tm),
        grid=(t // tm,),
        in_specs=[
            pl.BlockSpec((tm, HALF), lambda i: (i, 1)),
            pl.BlockSpec((tm, HALF), lambda i: (i, 2)),
            pl.BlockSpec((1, HALF), lambda i: (0, 0)),
            pl.BlockSpec((GMLP_HEADS, GMLP_CHUNK, GMLP_CHUNK), lambda i: (0, 0, 0)),
            pl.BlockSpec((GMLP_CHUNK, GMLP_HEADS), lambda i: (0, 0)),
        ],
        out_specs=pl.BlockSpec((tm, HALF), lambda i: (i, 0)),
        out_shape=jax.ShapeDtypeStruct((t, HALF), BF16),
        compiler_params=_cparams(("parallel",), VMEM_LIMIT),
        name="gmlp",
    )(p, p, v_g.reshape(1, HALF), w_s, b_s.T)


def _outproj_kernel(ma_ref, mb_ref, w_ref, x_ref, g1_ref, g2_ref, ga1_ref, sh2_ref, sc2_ref,
                    rw_ref, rb_ref, xo_ref, h2_ref, e_ref, p_ref, rk_ref, cnt_ref, carry, rw2, m_s, *, tm):
    @pl.when(pl.program_id(0) == 0)
    def _():
        carry[...] = jnp.zeros_like(carry)
        rw = rw_ref[...]
        rw_hi = rw.astype(BF16)
        rw2[:, :N_EXPERTS] = rw_hi
        rw2[:, N_EXPERTS:] = (rw - rw_hi.astype(F32)).astype(BF16)

    sub = 128
    iota = lax.broadcasted_iota(jnp.int32, (sub, N_EXPERTS), 1).astype(F32)
    lane4 = lax.broadcasted_iota(jnp.int32, (sub, TOP_K), 1)
    r_i = lax.broadcasted_iota(jnp.int32, (sub, sub), 0)
    c_i = lax.broadcasted_iota(jnp.int32, (sub, sub), 1)
    tri = (c_i < r_i).astype(BF16)
    running = carry[...]
    for r0 in range(0, tm, sub):
        rs = slice(r0, r0 + sub)
        m_s[rs, :] = (jnp.dot(ma_ref[rs, :], w_ref[:HALF, :], preferred_element_type=F32)
                      + jnp.dot(mb_ref[rs, :], w_ref[HALF:, :], preferred_element_type=F32))
    for r0 in range(0, tm, sub):
        rs = slice(r0, r0 + sub)
        m = m_s[rs, :]
        xn = x_ref[rs, :] + ga1_ref[...] * (_rms(m) * g1_ref[...])
        xo_ref[rs, :] = xn
        h2 = (_rms(xn) * g2_ref[...]) * (1.0 + sc2_ref[...]) + sh2_ref[...]
        _store_rows(h2_ref, h2, r0)

        h_hi = h2.astype(BF16)
        h_lo = (h2 - h_hi.astype(F32)).astype(BF16)
        t1 = jnp.dot(h_hi, rw2[...], preferred_element_type=F32)
        t2 = jnp.dot(h_lo, rw2[:, :N_EXPERTS], preferred_element_type=F32)
        lg = t1[:, :N_EXPERTS] + t1[:, N_EXPERTS:] + t2 + rb_ref[...]
        idxs, vals = [], []
        for _ in range(TOP_K):
            mx = jnp.max(lg, axis=-1, keepdims=True)
            ix = jnp.min(jnp.where(lg == mx, iota, float(N_EXPERTS)), axis=-1, keepdims=True)
            idxs.append(ix)
            vals.append(mx)
            lg = jnp.where(iota == ix, -jnp.inf, lg)
        exs = [jnp.exp(v - vals[0]) for v in vals]
        den = exs[0] + exs[1] + exs[2] + exs[3]

        onehot = [(iota == ix) for ix in idxs]
        cnt = jnp.zeros((sub, N_EXPERTS), F32)
        for oh in onehot:
            cnt = cnt + oh.astype(F32)
        prefix = jnp.dot(tri, cnt.astype(BF16), preferred_element_type=F32) + running

        e4 = jnp.zeros((sub, TOP_K), jnp.int32)
        p4 = jnp.zeros((sub, TOP_K), F32)
        r4 = jnp.zeros((sub, TOP_K), jnp.int32)
        for k in range(TOP_K):
            rk = jnp.sum(jnp.where(onehot[k], prefix, 0.0), axis=-1, keepdims=True)
            e4 = jnp.where(lane4 == k, idxs[k].astype(jnp.int32), e4)
            p4 = jnp.where(lane4 == k, exs[k] / den, p4)
            r4 = jnp.where(lane4 == k, rk.astype(jnp.int32), r4)
        e_ref[rs, :] = e4
        p_ref[rs, :] = p4
        rk_ref[rs, :] = r4
        running = running + jnp.sum(cnt, axis=0, keepdims=True)
    carry[...] = running
    cnt_ref[...] = running


def _outproj(ma, mb, w_bf, xs, t, g1, g2, modm, layer, rw, rb):
    tm = 512
    row = lambda i: layer * 8 + _mod_row(i, tm)
    modspec = lambda c: pl.BlockSpec((None, 1, D), lambda i: (row(i), 0, c))
    vec = pl.BlockSpec((1, D), lambda i: (0, 0))
    small = lambda dt: jax.ShapeDtypeStruct((t, TOP_K), dt)
    return pl.pallas_call(
        functools.partial(_outproj_kernel, tm=tm),
        grid=(t // tm,),
        in_specs=[
            pl.BlockSpec((tm, HALF), lambda i: (i, 0)),
            pl.BlockSpec((tm, HALF), lambda i: (i, 0)),
            pl.BlockSpec((D, D), lambda i: (0, 0)),
            pl.BlockSpec((tm, D), lambda i: (i, 0)),
            vec, vec, modspec(2), modspec(3), modspec(4),
            pl.BlockSpec((D, N_EXPERTS), lambda i: (0, 0)),
            pl.BlockSpec((1, N_EXPERTS), lambda i: (0, 0)),
        ],
        out_specs=[
            pl.BlockSpec((tm, D), lambda i: (i, 0)),
            pl.BlockSpec((tm * ROW_S, 128), lambda i: (i, 0)),
            pl.BlockSpec((tm, TOP_K), lambda i: (i, 0)),
            pl.BlockSpec((tm, TOP_K), lambda i: (i, 0)),
            pl.BlockSpec((tm, TOP_K), lambda i: (i, 0)),
            pl.BlockSpec((1, N_EXPERTS), lambda i: (0, 0)),
        ],
        out_shape=[
            jax.ShapeDtypeStruct((t, D), F32),
            jax.ShapeDtypeStruct((t * ROW_S, 128), U32),
            small(jnp.int32), small(F32), small(jnp.int32),
            jax.ShapeDtypeStruct((1, N_EXPERTS), F32),
        ],
        scratch_shapes=[pltpu.VMEM((1, N_EXPERTS), F32), pltpu.VMEM((D, 2 * N_EXPERTS), BF16),
                        pltpu.VMEM((tm, D), F32)],
        compiler_params=_cparams(("arbitrary",), VMEM_LIMIT),
        name="outproj_router",
    )(ma, mb, w_bf, xs, g1.reshape(1, D), g2.reshape(1, D), modm, modm, modm,
      rw, rb.reshape(1, N_EXPERTS))


def _invmap_kernel(pstart_ref, cnt_ref, padded_ref, dest_ref, inv_ref, *, n_pairs, n_slots):
    def unused(lo, hi):
        def body(s, c):
            inv_ref[s] = n_pairs + jnp.bitwise_and(s, 2 * MOE_TM - 1)
            return c
        lax.fori_loop(lo, hi, body, 0)

    def per_expert(e, c):
        unused(pstart_ref[e] + cnt_ref[e], pstart_ref[e] + padded_ref[e])
        return c

    lax.fori_loop(0, N_EXPERTS, per_expert, 0)
    unused(pstart_ref[N_EXPERTS - 1] + padded_ref[N_EXPERTS - 1], n_slots)

    def body(i, c):
        inv_ref[dest_ref[i]] = i
        return c

    lax.fori_loop(0, n_pairs, body, 0, unroll=16)


def _invmap(dest, pstart, cnt, padded, n_slots):
    n_pairs = dest.shape[0]
    return pl.pallas_call(
        functools.partial(_invmap_kernel, n_pairs=n_pairs, n_slots=n_slots),
        grid_spec=pltpu.PrefetchScalarGridSpec(
            num_scalar_prefetch=3,
            grid=(1,),
            in_specs=[pl.BlockSpec(memory_space=pltpu.SMEM)],
            out_specs=pl.BlockSpec(memory_space=pltpu.SMEM),
        ),
        out_shape=jax.ShapeDtypeStruct((n_slots,), jnp.int32),
        compiler_params=_cparams(("arbitrary",)),
        name="moe_invmap",
    )(pstart, cnt, padded, dest)


W_CH = 256
W_NCH_GU = D // W_CH
W_NCH = W_NCH_GU + D_EXPERT // W_CH
W_STAGE = 4


def _expert_kernel(be_ref, nused_ref, first_ref, nxt_ref, lo_ref, hi_ref, slot_ref,
                   inv_cur, inv_next, inv_next2, inv_prev, bgu_ref, bd_ref, h_hbm, wgu_all, wd_all, y_hbm,
                   xbuf, obuf, x_s, act_s, wgu_buf, wd_buf, stage, gsem, ssem, sem, *, layer, n_tok):
    i = pl.program_id(0)
    n_live = nused_ref[0]
    s_cur = lax.rem(i, 3)
    s_p1 = lax.rem(i + 1, 3)
    s_p2 = lax.rem(i + 2, 3)
    wgu_hbm = wgu_all.at[layer]
    wd_hbm = wd_all.at[layer]

    def gather_copy(inv_ref, r, slot):
        tok = jnp.minimum(lax.shift_right_logical(inv_ref[r], 2), n_tok - 1)
        return pltpu.make_async_copy(_row_slab(h_hbm, tok), _row_slab(xbuf.at[slot], r), gsem.at[slot])

    n_pairs = n_tok * TOP_K

    def scatter_copy(inv_ref, r, slot, first_step=False):
        dst = inv_ref[r]
        if first_step:
            dst = jnp.where(i == 0, n_pairs + 2 * MOE_TM + r, dst)
        return pltpu.make_async_copy(_row_slab(obuf.at[slot], r), _row_slab(y_hbm, dst), ssem.at[slot])

    def looped(fn):
        def body(r, c):
            fn(r)
            return c
        lax.fori_loop(0, MOE_TM, body, 0)

    def start_chunk(e, c):
        s = lax.rem(c, W_STAGE)

        @pl.when(c < W_NCH_GU)
        def _():
            r0 = pl.multiple_of(c * W_CH, W_CH)
            pltpu.make_async_copy(wgu_hbm.at[e, pl.ds(r0, W_CH), :], stage.at[s], sem.at[s]).start()

        @pl.when(c >= W_NCH_GU)
        def _():
            r0 = pl.multiple_of((c - W_NCH_GU) * W_CH, W_CH)
            pltpu.make_async_copy(wd_hbm.at[e, pl.ds(r0, W_CH), :], stage.at[s], sem.at[s]).start()

    def finish_chunk(c, dst):
        s = lax.rem(c, W_STAGE)
        pltpu.make_async_copy(wgu_hbm.at[0, pl.ds(0, W_CH), :], stage.at[s], sem.at[s]).wait()
        w = stage[s].astype(BF16)

        @pl.when(c < W_NCH_GU)
        def _():
            wgu_buf[dst, pl.ds(pl.multiple_of(c * W_CH, W_CH), W_CH), :] = w

        @pl.when(c >= W_NCH_GU)
        def _():
            wd_buf[dst, pl.ds(pl.multiple_of((c - W_NCH_GU) * W_CH, W_CH), W_CH), :] = w

    def stream(e, dst, c_lo, c_hi):
        def body(c, carry):
            finish_chunk(c, dst)

            @pl.when(c + W_STAGE < W_NCH)
            def _():
                start_chunk(e, c + W_STAGE)
            return carry
        lax.fori_loop(c_lo, c_hi, body, 0)

    def prime(e):
        for c in range(W_STAGE):
            start_chunk(e, c)

    @pl.when(i == 0)
    def _():
        looped(lambda r: gather_copy(inv_cur, r, 0).start())
        looped(lambda r: gather_copy(inv_next, r, 1).start())
        obuf[2] = jnp.zeros(obuf.shape[1:], U32)
        for region in range(2):
            dump = y_hbm.at[pl.ds((n_pairs + region * MOE_TM) * ROW_S, MOE_TM * ROW_S), :]
            zero_dump = pltpu.make_async_copy(obuf.at[2], dump, ssem.at[2])
            zero_dump.start()
            zero_dump.wait()
        prime(be_ref[0])
        stream(be_ref[0], slot_ref[0], 0, W_NCH)

    @pl.when(i < n_live)
    def _():
        e_next = nxt_ref[i]
        cur = slot_ref[i]

        @pl.when((first_ref[i] == 1) & (e_next >= 0))
        def _():
            prime(e_next)

        def block_body(c_cur, c_p1, c_p2):
            for r in range(MOE_TM):
                gather_copy(inv_cur, r, c_cur).wait()
            for s in range(ROW_S):
                hi, lo = _unpack_piece(xbuf[c_cur, pl.ds(s, MOE_TM, stride=ROW_S), :])
                x_s[:, s * 128:(s + 1) * 128] = hi.astype(BF16)
                x_s[:, (s + ROW_S) * 128:(s + ROW_S + 1) * 128] = lo.astype(BF16)
            nc = 256

            def gate_up(c_lo, c_hi):
                for c0 in range(c_lo, c_hi, nc):
                    g = (jnp.dot(x_s[...], wgu_buf[cur, :, c0:c0 + nc], preferred_element_type=F32)
                         + bgu_ref[:, c0:c0 + nc])
                    u = (jnp.dot(x_s[...], wgu_buf[cur, :, D_EXPERT + c0:D_EXPERT + c0 + nc],
                                 preferred_element_type=F32) + bgu_ref[:, D_EXPERT + c0:D_EXPERT + c0 + nc])
                    gate = jnp.minimum(g, SWIGLU_LIMIT)
                    up = jnp.clip(u, -SWIGLU_LIMIT, SWIGLU_LIMIT)
                    act_s[:, c0:c0 + nc] = (gate * jax.nn.sigmoid(SWIGLU_ALPHA * gate)
                                            * (up + 1.0)).astype(BF16)

            for r in range(MOE_TM):
                scatter_copy(inv_prev, r, c_p2, first_step=True).start()
            gate_up(0, D_EXPERT // 2)

            @pl.when(n_live > 0)
            def _():
                for r in range(MOE_TM // 2):
                    gather_copy(inv_next2, r, c_p2).start()
                gate_up(D_EXPERT // 2, D_EXPERT)

            @pl.when(first_ref[i] >= 0)
            def _():
                for r in range(MOE_TM // 2, MOE_TM):
                    gather_copy(inv_next2, r, c_p2).start()
                for c0 in range(0, D // 2, nc):
                    y_hi = (jnp.dot(act_s[...], wd_buf[cur, :, c0:c0 + nc], preferred_element_type=F32)
                            + bd_ref[:, c0:c0 + nc])
                    y_lo = (jnp.dot(act_s[...], wd_buf[cur, :, D // 2 + c0:D // 2 + c0 + nc],
                                    preferred_element_type=F32) + bd_ref[:, D // 2 + c0:D // 2 + c0 + nc])
                    for j in range(nc // 128):
                        s = c0 // 128 + j
                        obuf[c_cur, pl.ds(s, MOE_TM, stride=ROW_S), :] = _pack_pieces(
                            y_hi[:, j * 128:(j + 1) * 128], y_lo[:, j * 128:(j + 1) * 128])

            @pl.when(i >= 1)
            def _():
                for r in range(MOE_TM):
                    scatter_copy(inv_prev, r, c_p1).wait()

        for ring in range(3):
            @pl.when(s_cur == ring)
            def _():
                block_body(ring, (ring + 1) % 3, (ring + 2) % 3)

        @pl.when(e_next >= 0)
        def _():
            stream(e_next, 1 - cur, lo_ref[i], hi_ref[i])

        @pl.when(i == n_live - 1)
        def _():
            looped(lambda r: scatter_copy(inv_prev, r, s_p2).wait())
            looped(lambda r: scatter_copy(inv_cur, r, s_cur).start())
            looped(lambda r: scatter_copy(inv_cur, r, s_cur).wait())
            looped(lambda r: gather_copy(inv_next, r, s_p1).wait())
            looped(lambda r: gather_copy(inv_next2, r, s_p2).wait())


def _take(table, idx):
    ids = jnp.arange(table.shape[0], dtype=jnp.int32)
    return jnp.sum(jnp.where(idx[..., None] == ids, table, 0), axis=-1)


def _experts(h2, inv, block_e, nused, pstart, padded, layer, wgu, bgu, wd, bd, nblk, n_tok):
    nb_e = padded // MOE_TM
    pos = jnp.arange(nblk, dtype=jnp.int32) - _take(pstart // MOE_TM, block_e)
    nb = jnp.maximum(_take(nb_e, block_e), 1)
    lo = (pos * W_NCH) // nb
    hi = ((pos + 1) * W_NCH) // nb
    first = (pos == 0).astype(jnp.int32)
    eid = jnp.arange(N_EXPERTS, dtype=jnp.int32)
    later = jnp.where(nb_e > 0, eid, N_EXPERTS)
    nxt_e = jnp.concatenate([lax.cummin(later, reverse=True)[1:], jnp.full((1,), N_EXPERTS, jnp.int32)])
    nxt_e = jnp.where(nxt_e >= N_EXPERTS, -1, nxt_e)
    slot_e = (jnp.cumsum((nb_e > 0).astype(jnp.int32)) - 1) % 2
    i32 = lambda a: a.astype(jnp.int32)
    live = lambda i, nu: jnp.clip(i, 0, nu[0] - 1)
    sblk = lambda off: pl.BlockSpec((MOE_TM,), lambda i, be, nu, *_: (live(i + off, nu),),
                                    memory_space=pltpu.SMEM)
    bias = lambda n: pl.BlockSpec((None, 1, n), lambda i, be, nu, *_: (be[live(i, nu)], 0, 0))
    hbm = pl.BlockSpec(memory_space=pl.ANY)
    n_slabs = n_tok * TOP_K + 3 * MOE_TM
    return pl.pallas_call(
        functools.partial(_expert_kernel, layer=layer, n_tok=n_tok),
        grid_spec=pltpu.PrefetchScalarGridSpec(
            num_scalar_prefetch=7,
            grid=(nblk,),
            in_specs=[sblk(0), sblk(1), sblk(2), sblk(-1), bias(2 * D_EXPERT), bias(D), hbm, hbm, hbm],
            out_specs=hbm,
            scratch_shapes=[
                pltpu.VMEM((3, MOE_TM * ROW_S, 128), U32),
                pltpu.VMEM((3, MOE_TM * ROW_S, 128), U32),
                pltpu.VMEM((MOE_TM, D), BF16),
                pltpu.VMEM((MOE_TM, D_EXPERT), BF16),
                pltpu.VMEM((2, D, 2 * D_EXPERT), BF16),
                pltpu.VMEM((2, D_EXPERT, D), BF16),
                pltpu.VMEM((W_STAGE, W_CH, D), F32),
                pltpu.SemaphoreType.DMA((3,)),
                pltpu.SemaphoreType.DMA((3,)),
                pltpu.SemaphoreType.DMA((W_STAGE,)),
            ],
        ),
        out_shape=jax.ShapeDtypeStruct((n_slabs * ROW_S, 128), U32),
        compiler_params=_cparams(("arbitrary",), VMEM_LIMIT),
        name="moe_experts",
    )(block_e, nused, i32(first), i32(_take(nxt_e, block_e)), i32(lo), i32(hi), i32(_take(slot_e, block_e)),
      inv, inv, inv, inv, bgu.reshape(N_EXPERTS, 1, -1), bd.reshape(N_EXPERTS, 1, -1), h2, wgu, wd)


def _combine_kernel(p_ref, x_ref, g3_ref, ga2_ref, y_ref, xo_ref, acc_hi, acc_lo, *, tm):
    pair = TOP_K * ROW_S
    for r in range(tm):
        a_hi = a_lo = None
        for k in range(TOP_K):
            hi, lo = _unpack_piece(y_ref[pl.ds(r * pair + k * ROW_S, ROW_S), :])
            w = p_ref[r * TOP_K + k]
            a_hi = hi * w if k == 0 else a_hi + hi * w
            a_lo = lo * w if k == 0 else a_lo + lo * w
        acc_hi[pl.ds(r * ROW_S, ROW_S), :] = a_hi
        acc_lo[pl.ds(r * ROW_S, ROW_S), :] = a_lo
    y = jnp.concatenate([acc_hi[pl.ds(s, tm, stride=ROW_S), :] for s in range(ROW_S)]
                        + [acc_lo[pl.ds(s, tm, stride=ROW_S), :] for s in range(ROW_S)], axis=1)
    xo_ref[...] = x_ref[...] + ga2_ref[...] * (_rms(y) * g3_ref[...])


def _combine(y4, p4, xs, g3, modm, layer):
    tm = 128
    t = xs.shape[0]
    row = lambda i: layer * 8 + _mod_row(i, tm)
    return pl.pallas_call(
        functools.partial(_combine_kernel, tm=tm),
        grid=(t // tm,),
        in_specs=[
            pl.BlockSpec((tm * TOP_K,), lambda i: (i,), memory_space=pltpu.SMEM),
            pl.BlockSpec((tm, D), lambda i: (i, 0)),
            pl.BlockSpec((1, D), lambda i: (0, 0)),
            pl.BlockSpec((None, 1, D), lambda i: (row(i), 0, 5)),
            pl.BlockSpec((tm * TOP_K * ROW_S, 128), lambda i: (i, 0)),
        ],
        out_specs=pl.BlockSpec((tm, D), lambda i: (i, 0)),
        out_shape=jax.ShapeDtypeStruct((t, D), F32),
        scratch_shapes=[pltpu.VMEM((tm * ROW_S, 128), F32)] * 2,
        compiler_params=_cparams(("parallel",), VMEM_LIMIT),
        name="moe_combine",
    )(p4.reshape(-1), xs, g3.reshape(1, D), modm, y4)


def _moe(h2, e4, p4, r4, counts, xs, g3, modm, layer, wgu, bgu, wd, bd):
    t = e4.shape[0]
    nblk = -(-(t * TOP_K + N_EXPERTS * (MOE_TM - 1)) // MOE_TM)
    cnt = counts.reshape(N_EXPERTS).astype(jnp.int32)
    padded = (cnt + MOE_TM - 1) // MOE_TM * MOE_TM
    pend = jnp.cumsum(padded)
    pstart = pend - padded
    dest = (_take(pstart, e4) + r4).reshape(-1).astype(jnp.int32)
    nused = (pend[-1:] // MOE_TM).astype(jnp.int32)
    blk_start = jnp.arange(nblk, dtype=jnp.int32) * MOE_TM
    block_e = jnp.minimum(jnp.sum((pend[None, :] <= blk_start[:, None]).astype(jnp.int32), axis=1),
                          N_EXPERTS - 1).astype(jnp.int32)
    inv = _invmap(dest, pstart.astype(jnp.int32), cnt, padded.astype(jnp.int32), nblk * MOE_TM)
    y4 = _experts(h2, inv, block_e, nused, pstart, padded, layer, wgu, bgu, wd, bd, nblk, t)
    return _combine(y4, p4, xs, g3, modm, layer)


N_CTX_BLK = LC // HGRN_R
N_LAT_BLK = L // HGRN_R
N_SEQ_BLK = N_CTX_BLK + N_LAT_BLK


def _scan_block(b, s, rev):
    if rev:
        return jnp.where(s < N_CTX_BLK, T_LAT // HGRN_R + N_CTX_BLK * b + (N_CTX_BLK - 1 - s),
                         N_LAT_BLK * b + (N_SEQ_BLK - 1 - s))
    return jnp.where(s < N_CTX_BLK, T_LAT // HGRN_R + N_CTX_BLK * b + s, N_LAT_BLK * b + (s - N_CTX_BLK))


def _hgrn_kernel(*refs, rev):
    if rev:
        q_ref, f_ref, v_ref, lb_ref, of_ref, g_ref, on_ref, o_ref, st, qin_s, qd_s, ki_s, ke_s, v_s = refs
    else:
        q_ref, f_ref, v_ref, lb_ref, o_ref, st, qin_s, qd_s, ki_s, ke_s, v_s = refs

    @pl.when(pl.program_id(1) == 0)
    def _():
        st[...] = jnp.zeros_like(st)

    c = HGRN_C
    nc = HGRN_R // c
    r_i = lax.broadcasted_iota(jnp.int32, (c, c), 0)
    c_i = lax.broadcasted_iota(jnp.int32, (c, c), 1)
    mask = (c_i >= r_i) if rev else (c_i <= r_i)
    tri = mask.astype(F32)

    lb = lb_ref[...]
    decay = [None] * nc
    v_s[...] = v_ref[...].astype(BF16)
    for ci in range(nc):
        rs = slice(ci * c, (ci + 1) * c)
        sg = jax.nn.sigmoid(f_ref[rs, :])
        logf = jnp.log(lb + (1.0 - lb) * sg)
        kk = (1.0 - lb) * (1.0 - sg)
        q = q_ref[rs, :]
        qs = q * jax.nn.sigmoid(q)
        cum = jnp.dot(tri, logf, precision=HI, preferred_element_type=F32)
        total = cum[0:1] if rev else cum[c - 1:c]
        mid = cum[c // 2:c // 2 + 1]
        qin_s[rs, :] = (qs * jnp.exp(cum)).astype(BF16)
        qd_s[rs, :] = (qs * jnp.exp(cum - mid)).astype(BF16)
        ki_s[rs, :] = (kk * jnp.exp(mid - cum)).astype(BF16)
        ke_s[rs, :] = (kk * jnp.exp(total - cum)).astype(BF16)
        decay[ci] = jnp.exp(total)

    order = range(nc - 1, -1, -1) if rev else range(nc)
    for ci in order:
        rs = slice(ci * c, (ci + 1) * c)
        for h in range(HGRN_HEADS):
            hs = slice(h * 128, (h + 1) * 128)
            state = st[h]
            vb = v_s[rs, hs]
            sc = lax.dot_general(qd_s[rs, hs], ki_s[rs, hs], (((1,), (1,)), ((), ())),
                                 preferred_element_type=F32)
            sc = jnp.where(mask, sc, 0.0)
            intra = jnp.dot(sc.astype(BF16), vb, preferred_element_type=F32)
            inter = lax.dot_general(qin_s[rs, hs], state.astype(BF16), (((1,), (1,)), ((), ())),
                                    preferred_element_type=F32)
            st[h] = state * decay[ci][:, hs] + lax.dot_general(
                vb, ke_s[rs, hs], (((0,), (0,)), ((), ())), preferred_element_type=F32)
            o = intra + inter
            if rev:
                o = o + of_ref[rs, hs]
                g = g_ref[rs, hs]
                o_ref[rs, hs] = (_rms(o) * on_ref[:, hs] * (g * jax.nn.sigmoid(g))).astype(BF16)
            else:
                o_ref[rs, hs] = o


def _hgrn(pa, pb, lb, rev, of=None, onorm=None):
    r = HGRN_R
    blk = lambda b, s: _scan_block(b, s, rev)
    col = lambda c: pl.BlockSpec((r, HALF), lambda b, s: (blk(b, s), c))
    vec = pl.BlockSpec((1, HALF), lambda b, s: (0, 0))
    in_specs = [col(0), col(2 if rev else 1), col(3), vec]
    args = [pa, pa, pa, lb.reshape(1, HALF)]
    if rev:
        in_specs += [col(0), col(0), vec]
        args += [of, pb, onorm.reshape(1, HALF)]
    return pl.pallas_call(
        functools.partial(_hgrn_kernel, rev=rev),
        grid=(B, N_SEQ_BLK),
        in_specs=in_specs,
        out_specs=col(0),
        out_shape=jax.ShapeDtypeStruct((T_ALL, HALF), BF16 if rev else F32),
        scratch_shapes=[pltpu.VMEM((HGRN_HEADS, 128, 128), F32)] + [pltpu.VMEM((r, HALF), BF16)] * 5,
        compiler_params=_cparams(("parallel", "arbitrary"), VMEM_LIMIT),
        name="hgrn_bwd" if rev else "hgrn_fwd",
    )(*args)


def _lru_kernel(*refs, rev):
    if rev:
        (x_ref, xp_ref, xn_ref, cw_ref, cb_ref, wr_ref, br_ref, wi_ref, bi_ref, lam_ref,
         hf_ref, gt_ref, o_ref, xcat, a_s, u_s, h_s, carry) = refs
    else:
        (x_ref, xp_ref, xn_ref, cw_ref, cb_ref, wr_ref, br_ref, wi_ref, bi_ref, lam_ref,
         o_ref, xcat, a_s, u_s, h_s, carry) = refs
    s = pl.program_id(1)
    r = LRU_R

    @pl.when(s == 0)
    def _():
        carry[...] = jnp.zeros_like(carry)

    is_ctx = s < N_CTX_BLK
    if rev:
        j = jnp.where(is_ctx, N_CTX_BLK - 1 - s, N_SEQ_BLK - 1 - s)
    else:
        j = jnp.where(is_ctx, s, s - N_CTX_BLK)
    nb = jnp.where(is_ctx, N_CTX_BLK, N_LAT_BLK)
    xcat[0:8, :] = jnp.where(j == 0, 0.0, xp_ref[...])
    xcat[8:8 + r, :] = x_ref[...]
    xcat[8 + r:16 + r, :] = jnp.where(j == nb - 1, 0.0, xn_ref[...])
    xc = cb_ref[...] + xcat[pl.ds(6, r), :] * cw_ref[0:1, :]
    for t in range(1, 4):
        xc = xc + xcat[pl.ds(6 + t, r), :] * cw_ref[t:t + 1, :]

    lam = lam_ref[...]
    sp = jnp.maximum(-lam, 0.0) + jnp.log1p(jnp.exp(-jnp.abs(lam)))
    hd = HALF // LRU_HEADS
    for h in range(LRU_HEADS):
        cs = slice(h * hd, (h + 1) * hd)
        xh = xc[:, cs]
        rg = jax.nn.sigmoid(_bdot(xh, wr_ref[h]) + br_ref[:, cs])
        ig = jax.nn.sigmoid(_bdot(xh, wi_ref[h]) + bi_ref[:, cs])
        log_a = -LRU_C * rg * sp[:, cs]
        a = jnp.exp(log_a)
        a_s[:, cs] = a
        u_s[:, cs] = jnp.sqrt(-jnp.tanh(log_a) * (a * a + 1.0)) * (ig * xh)

    def step(t, h):
        tt = (r - 1 - t) if rev else t
        h = a_s[pl.ds(tt, 1), :] * h + u_s[pl.ds(tt, 1), :]
        h_s[pl.ds(tt, 1), :] = h
        return h

    carry[...] = lax.fori_loop(0, r, step, carry[...], unroll=8)
    if rev:
        o_ref[...] = ((h_s[...] + hf_ref[...]) * jax.nn.gelu(gt_ref[...])).astype(BF16)
    else:
        o_ref[...] = h_s[...]


def _lru(p, conv_w, conv_b, w_r, b_r, w_i, b_i, lam, rev, hf=None):
    r = LRU_R
    d = 1 if rev else 0
    blk = lambda b, s: _scan_block(b, s, rev)
    xcol = 2
    n8 = T_ALL // 8
    vec = lambda: pl.BlockSpec((None, 1, HALF), lambda b, s: (d, 0, 0))
    wspec = lambda: pl.BlockSpec((None, LRU_HEADS, 128, 128), lambda b, s: (d, 0, 0, 0))
    in_specs = [
        pl.BlockSpec((r, HALF), lambda b, s: (blk(b, s), xcol)),
        pl.BlockSpec((8, HALF), lambda b, s: (jnp.maximum(blk(b, s) * (r // 8) - 1, 0), xcol)),
        pl.BlockSpec((8, HALF), lambda b, s: (jnp.minimum((blk(b, s) + 1) * (r // 8), n8 - 1), xcol)),
        pl.BlockSpec((4, HALF), lambda b, s: (0, 0)),
        pl.BlockSpec((1, HALF), lambda b, s: (0, 0)),
        wspec(), vec(), wspec(), vec(), vec(),
    ]
    args = [p, p, p, conv_w, conv_b.reshape(1, HALF), w_r, b_r.reshape(2, 1, HALF), w_i,
            b_i.reshape(2, 1, HALF), lam.reshape(2, 1, HALF)]
    if rev:
        in_specs += [pl.BlockSpec((r, HALF), lambda b, s: (blk(b, s), 0)),
                     pl.BlockSpec((r, HALF), lambda b, s: (blk(b, s), xcol - 1))]
        args += [hf, p]
    return pl.pallas_call(
        functools.partial(_lru_kernel, rev=rev),
        grid=(B, N_SEQ_BLK),
        in_specs=in_specs,
        out_specs=pl.BlockSpec((r, HALF), lambda b, s: (blk(b, s), 0)),
        out_shape=jax.ShapeDtypeStruct((T_ALL, HALF), BF16 if rev else F32),
        scratch_shapes=[pltpu.VMEM((r + 16, HALF), F32), pltpu.VMEM((r, HALF), F32),
                        pltpu.VMEM((r, HALF), F32), pltpu.VMEM((r, HALF), F32),
                        pltpu.VMEM((1, HALF), F32)],
        compiler_params=_cparams(("parallel", "arbitrary"), VMEM_LIMIT),
        name="lru_bwd" if rev else "lru_fwd",
    )(*args)


def kernel(x, c, ctx, c_ctx, mod_w, mod_b, norm_g, ab_w_in, ab_w_out, gmlp_v_g, gmlp_ws, gmlp_bs, cd_w_in, cd_w_out, hgrn_lb, hgrn_onorm_g, lru_conv_w, lru_conv_b, lru_wr, lru_br, lru_wi, lru_bi, lru_lambda, router_w, router_b, exp_w_gu, exp_b_gu, exp_w_down, exp_b_down):
    cond8 = jnp.concatenate([c, c_ctx[None, :], jnp.zeros((5, D), F32)], axis=0)
    modm = _modulation(cond8, mod_w, mod_b).reshape(DEPTH * 8, 1, 6 * D)
    xs = jnp.concatenate([x.reshape(T_LAT, D), ctx.reshape(T_CTX, D)], axis=0)

    lb_soft = jax.nn.softmax(hgrn_lb.astype(F32), axis=0)
    lb_all = jnp.cumsum(lb_soft, axis=0) - lb_soft[0]

    (p0,) = _inproj(xs, norm_g[0, 0], modm, 0, [ab_w_in[0].astype(BF16)], BF16)
    zc, zs = _fnet_chan(p0)
    mix_a = jnp.concatenate([_fnet_pos(zc, zs, 0, L, 512, 1024),
                             _fnet_pos(zc, zs, T_LAT, LC, LC, LC)], axis=0)
    mix_g = _gmlp(p0, gmlp_v_g[0], gmlp_ws[0], gmlp_bs[0])
    xs, h2, e4, p4, r4, counts = _outproj(mix_a, mix_g, ab_w_out[0].astype(BF16), xs, T_ALL, norm_g[0, 1],
                                          norm_g[0, 2], modm, 0, router_w[0], router_b[0])
    xs = _moe(h2, e4, p4, r4, counts, xs, norm_g[0, 3], modm, 0,
              exp_w_gu, exp_b_gu[0], exp_w_down, exp_b_down[0])

    w_cd = cd_w_in[0].astype(BF16)
    p1a, p1b = _inproj(xs, norm_g[1, 0], modm, 1, [w_cd[:, :4 * HALF], w_cd[:, 4 * HALF:]], F32)
    o_f = _hgrn(p1a, p1b, lb_all[1], False)
    hg = _hgrn(p1a, p1b, lb_all[1], True, o_f, hgrn_onorm_g[0])
    lru_args = (lru_conv_w[0], lru_conv_b[0], lru_wr[0], lru_br[0], lru_wi[0], lru_bi[0], lru_lambda[0])
    h_f = _lru(p1b, *lru_args, False)
    lr = _lru(p1b, *lru_args, True, h_f)
    xl, h2, e4, p4, r4, counts = _outproj(hg, lr, cd_w_out[0].astype(BF16), xs, T_LAT, norm_g[1, 1],
                                          norm_g[1, 2], modm, 1, router_w[1], router_b[1])
    xl = _moe(h2, e4, p4, r4, counts, xl, norm_g[1, 3], modm, 1,
              exp_w_gu, exp_b_gu[1], exp_w_down, exp_b_down[1])
    return xl.reshape(B, L, D)
```

```python
import functools

import numpy as np
import jax
import jax.numpy as jnp
from jax import lax
from jax.experimental import pallas as pl
from jax.experimental.pallas import tpu as pltpu

F32 = jnp.float32
BF16 = jnp.bfloat16
HI = lax.Precision.HIGHEST

D = 2048
B = 2
L = 4096
LC = 256
T_LAT = B * L
T_CTX = B * LC
T_ALL = T_LAT + T_CTX
DEPTH = 2
EPS = 1e-6

HALF = D // 2
N_EXPERTS = 32
TOP_K = 4
D_EXPERT = D // 2
SWIGLU_LIMIT = 7.0
SWIGLU_ALPHA = 1.702
MOE_TM = 256

FNET_GROUPS = 4
FNET_GD = HALF // FNET_GROUPS
GMLP_HEADS = 8
GMLP_CHUNK = 128
HGRN_HEADS = 8
HGRN_R = 256
HGRN_C = 64
LRU_HEADS = 8
LRU_R = HGRN_R
LRU_C = 8.0

VMEM_LIMIT = 56 * 1024 * 1024


def _cparams(sem, vmem=None):
    return pltpu.CompilerParams(dimension_semantics=sem, vmem_limit_bytes=vmem)


def _mod_row(i, tm):
    return jnp.where(i < L // tm, 0, jnp.where(i < 2 * L // tm, 1, 2))


def _rms(x):
    return x * lax.rsqrt(jnp.mean(x * x, axis=-1, keepdims=True) + EPS)


def _bdot(a, b):
    return jnp.dot(a.astype(BF16), b.astype(BF16), preferred_element_type=F32)


ROW_S = D // 256
U32 = jnp.uint32
HI_MASK = 0xFFFF0000


def _row_slab(ref, r):
    return ref.at[pl.ds(pl.multiple_of(r * ROW_S, ROW_S), ROW_S), :]


def _pack_pieces(hi, lo):
    hb = pltpu.bitcast(hi.astype(BF16).astype(F32), U32)
    lb = pltpu.bitcast(lo.astype(BF16).astype(F32), U32)
    return jnp.bitwise_or(jnp.bitwise_and(hb, U32(HI_MASK)), jnp.right_shift(lb, U32(16)))


def _unpack_piece(w):
    hi = pltpu.bitcast(jnp.bitwise_and(w, U32(HI_MASK)), F32)
    lo = pltpu.bitcast(jnp.left_shift(w, U32(16)), F32)
    return hi, lo


def _store_rows(ref, val, row0=0):
    n = val.shape[0]
    for s in range(ROW_S):
        ref[pl.ds(row0 * ROW_S + s, n, stride=ROW_S), :] = _pack_pieces(
            val[:, s * 128:(s + 1) * 128], val[:, (s + ROW_S) * 128:(s + ROW_S + 1) * 128])


def _mod_kernel(c_ref, w_ref, b_ref, o_ref):
    cnd = c_ref[...]
    s = cnd * jax.nn.sigmoid(cnd)
    o_ref[...] = _bdot(s, w_ref[...]) + b_ref[...]


def _modulation(cond8, mod_w, mod_b):
    tn = 1024
    n = mod_w.shape[-1]
    return pl.pallas_call(
        _mod_kernel,
        grid=(DEPTH, n // tn),
        in_specs=[
            pl.BlockSpec((8, D), lambda l, j: (0, 0)),
            pl.BlockSpec((None, D, tn), lambda l, j: (l, 0, j)),
            pl.BlockSpec((None, 1, tn), lambda l, j: (l, 0, j)),
        ],
        out_specs=pl.BlockSpec((None, 8, tn), lambda l, j: (l, 0, j)),
        out_shape=jax.ShapeDtypeStruct((DEPTH, 8, n), F32),
        compiler_params=_cparams(("parallel", "parallel"), VMEM_LIMIT),
        name="modulation",
    )(cond8, mod_w, mod_b.reshape(DEPTH, 1, n))


def _inproj_kernel(*refs, nw):
    x_ref, g_ref, sh_ref, sc_ref = refs[:4]
    w_refs, o_refs = refs[4:4 + nw], refs[4 + nw:]
    y = _rms(x_ref[...]) * g_ref[...]
    h = (y * (1.0 + sc_ref[...]) + sh_ref[...]).astype(BF16)
    for w_ref, o_ref in zip(w_refs, o_refs):
        o_ref[...] = jnp.dot(h, w_ref[...], preferred_element_type=F32).astype(o_ref.dtype)


def _inproj(xs, g, modm, layer, ws_bf, out_dtype):
    tm = 256
    t = xs.shape[0]
    row = lambda i: layer * 8 + _mod_row(i, tm)
    return pl.pallas_call(
        functools.partial(_inproj_kernel, nw=len(ws_bf)),
        grid=(t // tm,),
        in_specs=[
            pl.BlockSpec((tm, D), lambda i: (i, 0)),
            pl.BlockSpec((1, D), lambda i: (0, 0)),
            pl.BlockSpec((None, 1, D), lambda i: (row(i), 0, 0)),
            pl.BlockSpec((None, 1, D), lambda i: (row(i), 0, 1)),
        ] + [pl.BlockSpec(w.shape, lambda i: (0, 0), pipeline_mode=pl.Buffered(1)) for w in ws_bf],
        out_specs=[pl.BlockSpec((tm, w.shape[1]), lambda i: (i, 0)) for w in ws_bf],
        out_shape=[jax.ShapeDtypeStruct((t, w.shape[1]), out_dtype) for w in ws_bf],
        compiler_params=_cparams(("parallel",), VMEM_LIMIT),
        name="inproj",
    )(xs, g.reshape(1, D), modm, modm, *ws_bf)


def _fnet_chan_kernel(p_ref, cs_ref, zc_ref, zs_ref):
    for g in range(FNET_GROUPS):
        sl = slice(g * FNET_GD, (g + 1) * FNET_GD)
        z = jnp.dot(p_ref[:, sl].astype(BF16), cs_ref[...], preferred_element_type=F32)
        zc_ref[:, sl] = z[:, :FNET_GD].astype(BF16)
        zs_ref[:, sl] = z[:, FNET_GD:].astype(BF16)


def _fnet_chan(p):
    tm = 512
    t = p.shape[0]
    k = np.arange(FNET_GD)
    ang = 2.0 * np.pi * ((k[:, None] * k[None, :]) % FNET_GD) / FNET_GD
    cs = np.concatenate([np.cos(ang), np.sin(ang)], axis=1) / np.sqrt(FNET_GD)
    cs = jnp.asarray(cs, F32).astype(BF16)
    return pl.pallas_call(
        _fnet_chan_kernel,
        grid=(t // tm,),
        in_specs=[
            pl.BlockSpec((tm, HALF), lambda i: (i, 0)),
            pl.BlockSpec((FNET_GD, 2 * FNET_GD), lambda i: (0, 0)),
        ],
        out_specs=[pl.BlockSpec((tm, HALF), lambda i: (i, 0))] * 2,
        out_shape=[jax.ShapeDtypeStruct((t, HALF), BF16)] * 2,
        compiler_params=_cparams(("parallel",), VMEM_LIMIT),
        name="fnet_chan",
    )(p, cs)


def _fnet_pos_kernel(bre_ref, bim_ref, cre_ref, cim_ref, rre_ref, rim_ref, zc_ref, zs_ref, o_ref, acc):
    nt = pl.program_id(2)
    br, bi = bre_ref[...], bim_ref[...]
    cr, ci = cre_ref[...], cim_ref[...]
    rr, ri = rre_ref[...], rim_ref[...]
    tr = br * cr - bi * ci
    ti = br * ci + bi * cr
    er = (tr * rr - ti * ri).astype(BF16)
    ei = (tr * ri + ti * rr).astype(BF16)
    part = (jnp.dot(er, zc_ref[...], preferred_element_type=F32)
            + jnp.dot(ei, zs_ref[...], preferred_element_type=F32))

    @pl.when(nt == 0)
    def _():
        acc[...] = part

    @pl.when(nt > 0)
    def _():
        acc[...] += part

    @pl.when(nt == pl.num_programs(2) - 1)
    def _():
        o_ref[...] = acc[...].astype(BF16)


def _fnet_pos(zc, zs, row0, seq, tk, tn):
    nkt, nnt = seq // tk, seq // tn
    th = 2.0 * np.pi / seq
    kk = np.arange(tk)[:, None]
    nn = np.arange(tn)[None, :]
    base = th * ((kk * nn) % seq)
    n0 = (np.arange(nnt) * tn)[:, None, None]
    col = th * ((np.arange(tk)[None, :, None] * n0) % seq)
    k0 = (np.arange(nkt) * tk)[:, None, None, None]
    nfull = (np.arange(nnt) * tn)[None, :, None, None] + np.arange(tn)[None, None, None, :]
    row = th * ((k0 * nfull) % seq)
    scale = 1.0 / np.sqrt(seq)
    f = lambda a: jnp.asarray(a, F32)
    tabs = (f(np.cos(base)), f(-np.sin(base)), f(np.cos(col)), f(-np.sin(col)),
            f(np.cos(row) * scale), f(-np.sin(row) * scale))
    rb = row0 // tn
    ob = row0 // tk
    return pl.pallas_call(
        _fnet_pos_kernel,
        grid=(B, nkt, nnt),
        in_specs=[
            pl.BlockSpec((tk, tn), lambda b, k, n: (0, 0)),
            pl.BlockSpec((tk, tn), lambda b, k, n: (0, 0)),
            pl.BlockSpec((None, tk, 1), lambda b, k, n: (n, 0, 0)),
            pl.BlockSpec((None, tk, 1), lambda b, k, n: (n, 0, 0)),
            pl.BlockSpec((None, None, 1, tn), lambda b, k, n: (k, n, 0, 0)),
            pl.BlockSpec((None, None, 1, tn), lambda b, k, n: (k, n, 0, 0)),
            pl.BlockSpec((tn, HALF), lambda b, k, n: (rb + b * nnt + n, 0)),
            pl.BlockSpec((tn, HALF), lambda b, k, n: (rb + b * nnt + n, 0)),
        ],
        out_specs=pl.BlockSpec((tk, HALF), lambda b, k, n: (b * nkt + k, 0)),
        out_shape=jax.ShapeDtypeStruct((B * seq, HALF), BF16),
        scratch_shapes=[pltpu.VMEM((tk, HALF), F32)],
        compiler_params=_cparams(("parallel", "parallel", "arbitrary"), VMEM_LIMIT),
        name=f"fnet_pos_{seq}",
    )(*tabs, zc, zs)


def _gmlp_kernel(u_ref, v_ref, vg_ref, ws_ref, bst_ref, o_ref, *, tm):
    zu = jax.nn.gelu(u_ref[...].astype(F32))
    zv = jax.nn.gelu(v_ref[...].astype(F32))
    mu = jnp.mean(zv, axis=-1, keepdims=True)
    dv = zv - mu
    var = jnp.mean(dv * dv, axis=-1, keepdims=True)
    vn = dv * lax.rsqrt(var + EPS) * vg_ref[...]
    hd = HALF // GMLP_HEADS
    for c in range(tm // GMLP_CHUNK):
        rs = slice(c * GMLP_CHUNK, (c + 1) * GMLP_CHUNK)
        for h in range(GMLP_HEADS):
            cs = slice(h * hd, (h + 1) * hd)
            s = _bdot(ws_ref[h], vn[rs, cs])
            s = s + bst_ref[:, h:h + 1]
            o_ref[rs, cs] = (zu[rs, cs] * s).astype(BF16)


def _gmlp(p, v_g, w_s, b_s):
    tm = 256
    t = p.shape[0]
    return pl.pallas_call(
        functools.partial(_gmlp_kernel, tm=tm),
        grid=(t // tm,),
        in_specs=[
            pl.BlockSpec((tm, HALF), lambda i: (i, 1)),
            pl.BlockSpec((tm, HALF), lambda i: (i, 2)),
            pl.BlockSpec((1, HALF), lambda i: (0, 0)),
            pl.BlockSpec((GMLP_HEADS, GMLP_CHUNK, GMLP_CHUNK), lambda i: (0, 0, 0)),
            pl.BlockSpec((GMLP_CHUNK, GMLP_HEADS), lambda i: (0, 0)),
        ],
        out_specs=pl.BlockSpec((tm, HALF), lambda i: (i, 0)),
        out_shape=jax.ShapeDtypeStruct((t, HALF), BF16),
        compiler_params=_cparams(("parallel",), VMEM_LIMIT),
        name="gmlp",
    )(p, p, v_g.reshape(1, HALF), w_s, b_s.T)


def _outproj_kernel(ma_ref, mb_ref, w_ref, x_ref, g1_ref, g2_ref, ga1_ref, sh2_ref, sc2_ref,
                    rw_ref, rb_ref, xo_ref, h2_ref, e_ref, p_ref, rk_ref, cnt_ref, carry, rw2, m_s, *, tm):
    @pl.when(pl.program_id(0) == 0)
    def _():
        carry[...] = jnp.zeros_like(carry)
        rw = rw_ref[...]
        rw_hi = rw.astype(BF16)
        rw2[:, :N_EXPERTS] = rw_hi
        rw2[:, N_EXPERTS:] = (rw - rw_hi.astype(F32)).astype(BF16)

    sub = 128
    iota = lax.broadcasted_iota(jnp.int32, (sub, N_EXPERTS), 1).astype(F32)
    lane4 = lax.broadcasted_iota(jnp.int32, (sub, TOP_K), 1)
    r_i = lax.broadcasted_iota(jnp.int32, (sub, sub), 0)
    c_i = lax.broadcasted_iota(jnp.int32, (sub, sub), 1)
    tri = (c_i < r_i).astype(BF16)
    running = carry[...]
    for r0 in range(0, tm, sub):
        rs = slice(r0, r0 + sub)
        m_s[rs, :] = (jnp.dot(ma_ref[rs, :], w_ref[:HALF, :], preferred_element_type=F32)
                      + jnp.dot(mb_ref[rs, :], w_ref[HALF:, :], preferred_element_type=F32))
    for r0 in range(0, tm, sub):
        rs = slice(r0, r0 + sub)
        m = m_s[rs, :]
        xn = x_ref[rs, :] + ga1_ref[...] * (_rms(m) * g1_ref[...])
        xo_ref[rs, :] = xn
        h2 = (_rms(xn) * g2_ref[...]) * (1.0 + sc2_ref[...]) + sh2_ref[...]
        _store_rows(h2_ref, h2, r0)

        h_hi = h2.astype(BF16)
        h_lo = (h2 - h_hi.astype(F32)).astype(BF16)
        t1 = jnp.dot(h_hi, rw2[...], preferred_element_type=F32)
        t2 = jnp.dot(h_lo, rw2[:, :N_EXPERTS], preferred_element_type=F32)
        lg = t1[:, :N_EXPERTS] + t1[:, N_EXPERTS:] + t2 + rb_ref[...]
        idxs, vals = [], []
        for _ in range(TOP_K):
            mx = jnp.max(lg, axis=-1, keepdims=True)
            ix = jnp.min(jnp.where(lg == mx, iota, float(N_EXPERTS)), axis=-1, keepdims=True)
            idxs.append(ix)
            vals.append(mx)
            lg = jnp.where(iota == ix, -jnp.inf, lg)
        exs = [jnp.exp(v - vals[0]) for v in vals]
        den = exs[0] + exs[1] + exs[2] + exs[3]

        onehot = [(iota == ix) for ix in idxs]
        cnt = jnp.zeros((sub, N_EXPERTS), F32)
        for oh in onehot:
            cnt = cnt + oh.astype(F32)
        prefix = jnp.dot(tri, cnt.astype(BF16), preferred_element_type=F32) + running

        e4 = jnp.zeros((sub, TOP_K), jnp.int32)
        p4 = jnp.zeros((sub, TOP_K), F32)
        r4 = jnp.zeros((sub, TOP_K), jnp.int32)
        for k in range(TOP_K):
            rk = jnp.sum(jnp.where(onehot[k], prefix, 0.0), axis=-1, keepdims=True)
            e4 = jnp.where(lane4 == k, idxs[k].astype(jnp.int32), e4)
            p4 = jnp.where(lane4 == k, exs[k] / den, p4)
            r4 = jnp.where(lane4 == k, rk.astype(jnp.int32), r4)
        e_ref[rs, :] = e4
        p_ref[rs, :] = p4
        rk_ref[rs, :] = r4
        running = running + jnp.sum(cnt, axis=0, keepdims=True)
    carry[...] = running
    cnt_ref[...] = running


def _outproj(ma, mb, w_bf, xs, t, g1, g2, modm, layer, rw, rb):
    tm = 512
    row = lambda i: layer * 8 + _mod_row(i, tm)
    modspec = lambda c: pl.BlockSpec((None, 1, D), lambda i: (row(i), 0, c))
    vec = pl.BlockSpec((1, D), lambda i: (0, 0))
    small = lambda dt: jax.ShapeDtypeStruct((t, TOP_K), dt)
    return pl.pallas_call(
        functools.partial(_outproj_kernel, tm=tm),
        grid=(t // tm,),
        in_specs=[
            pl.BlockSpec((tm, HALF), lambda i: (i, 0)),
            pl.BlockSpec((tm, HALF), lambda i: (i, 0)),
            pl.BlockSpec((D, D), lambda i: (0, 0)),
            pl.BlockSpec((tm, D), lambda i: (i, 0)),
            vec, vec, modspec(2), modspec(3), modspec(4),
            pl.BlockSpec((D, N_EXPERTS), lambda i: (0, 0)),
            pl.BlockSpec((1, N_EXPERTS), lambda i: (0, 0)),
        ],
        out_specs=[
            pl.BlockSpec((tm, D), lambda i: (i, 0)),
            pl.BlockSpec((tm * ROW_S, 128), lambda i: (i, 0)),
            pl.BlockSpec((tm, TOP_K), lambda i: (i, 0)),
            pl.BlockSpec((tm, TOP_K), lambda i: (i, 0)),
            pl.BlockSpec((tm, TOP_K), lambda i: (i, 0)),
            pl.BlockSpec((1, N_EXPERTS), lambda i: (0, 0)),
        ],
        out_shape=[
            jax.ShapeDtypeStruct((t, D), F32),
            jax.ShapeDtypeStruct((t * ROW_S, 128), U32),
            small(jnp.int32), small(F32), small(jnp.int32),
            jax.ShapeDtypeStruct((1, N_EXPERTS), F32),
        ],
        scratch_shapes=[pltpu.VMEM((1, N_EXPERTS), F32), pltpu.VMEM((D, 2 * N_EXPERTS), BF16),
                        pltpu.VMEM((tm, D), F32)],
        compiler_params=_cparams(("arbitrary",), VMEM_LIMIT),
        name="outproj_router",
    )(ma, mb, w_bf, xs, g1.reshape(1, D), g2.reshape(1, D), modm, modm, modm,
      rw, rb.reshape(1, N_EXPERTS))


def _invmap_kernel(pstart_ref, cnt_ref, padded_ref, dest_ref, inv_ref, *, n_pairs, n_slots):
    def unused(lo, hi):
        def body(s, c):
            inv_ref[s] = n_pairs + jnp.bitwise_and(s, 2 * MOE_TM - 1)
            return c
        lax.fori_loop(lo, hi, body, 0)

    def per_expert(e, c):
        unused(pstart_ref[e] + cnt_ref[e], pstart_ref[e] + padded_ref[e])
        return c

    lax.fori_loop(0, N_EXPERTS, per_expert, 0)
    unused(pstart_ref[N_EXPERTS - 1] + padded_ref[N_EXPERTS - 1], n_slots)

    def body(i, c):
        inv_ref[dest_ref[i]] = i
        return c

    lax.fori_loop(0, n_pairs, body, 0, unroll=16)


def _invmap(dest, pstart, cnt, padded, n_slots):
    n_pairs = dest.shape[0]
    return pl.pallas_call(
        functools.partial(_invmap_kernel, n_pairs=n_pairs, n_slots=n_slots),
        grid_spec=pltpu.PrefetchScalarGridSpec(
            num_scalar_prefetch=3,
            grid=(1,),
            in_specs=[pl.BlockSpec(memory_space=pltpu.SMEM)],
            out_specs=pl.BlockSpec(memory_space=pltpu.SMEM),
        ),
        out_shape=jax.ShapeDtypeStruct((n_slots,), jnp.int32),
        compiler_params=_cparams(("arbitrary",)),
        name="moe_invmap",
    )(pstart, cnt, padded, dest)


W_CH = 256
W_NCH_GU = D // W_CH
W_NCH = W_NCH_GU + D_EXPERT // W_CH
W_STAGE = 4


def _expert_kernel(be_ref, nused_ref, first_ref, nxt_ref, lo_ref, hi_ref, slot_ref,
                   inv_cur, inv_next, inv_next2, inv_prev, bgu_ref, bd_ref, h_hbm, wgu_all, wd_all, y_hbm,
                   xbuf, obuf, x_s, act_s, w_buf, stage, gsem, ssem, sem, *, layer, n_tok):
    i = pl.program_id(0)
    n_live = nused_ref[0]
    s_cur = lax.rem(i, 3)
    s_p1 = lax.rem(i + 1, 3)
    s_p2 = lax.rem(i + 2, 3)
    wgu_hbm = wgu_all.at[layer]
    wd_hbm = wd_all.at[layer]

    def gather_copy(inv_ref, r, slot):
        tok = jnp.minimum(lax.shift_right_logical(inv_ref[r], 2), n_tok - 1)
        return pltpu.make_async_copy(_row_slab(h_hbm, tok), _row_slab(xbuf.at[slot], r), gsem.at[slot])

    n_pairs = n_tok * TOP_K

    def scatter_copy(inv_ref, r, slot, first_step=False):
        dst = inv_ref[r]
        if first_step:
            dst = jnp.where(i == 0, n_pairs + 2 * MOE_TM + r, dst)
        return pltpu.make_async_copy(_row_slab(obuf.at[slot], r), _row_slab(y_hbm, dst), ssem.at[slot])

    def looped(fn):
        def body(r, c):
            fn(r)
            return c
        lax.fori_loop(0, MOE_TM, body, 0)

    def start_chunk(e, c):
        s = lax.rem(c, W_STAGE)

        @pl.when(c < W_NCH_GU)
        def _():
            r0 = pl.multiple_of(c * W_CH, W_CH)
            pltpu.make_async_copy(wgu_hbm.at[e, pl.ds(r0, W_CH), :], stage.at[s], sem.at[s]).start()

        @pl.when(c >= W_NCH_GU)
        def _():
            r0 = pl.multiple_of((c - W_NCH_GU) * W_CH, W_CH)
            pltpu.make_async_copy(wd_hbm.at[e, pl.ds(r0, W_CH), :], stage.at[s], sem.at[s]).start()

    def finish_chunk(c, dst):
        s = lax.rem(c, W_STAGE)
        pltpu.make_async_copy(wgu_hbm.at[0, pl.ds(0, W_CH), :], stage.at[s], sem.at[s]).wait()
        w_buf[dst, pl.ds(pl.multiple_of(c * W_CH, W_CH), W_CH), :] = stage[s].astype(BF16)

    def stream(e, dst, c_lo, c_hi):
        def body(c, carry):
            finish_chunk(c, dst)

            @pl.when(c + W_STAGE < W_NCH)
            def _():
                start_chunk(e, c + W_STAGE)
            return carry
        lax.fori_loop(c_lo, c_hi, body, 0)

    def prime(e):
        for c in range(W_STAGE):
            start_chunk(e, c)

    @pl.when(i == 0)
    def _():
        looped(lambda r: gather_copy(inv_cur, r, 0).start())
        looped(lambda r: gather_copy(inv_next, r, 1).start())
        obuf[2] = jnp.zeros(obuf.shape[1:], U32)
        for region in range(2):
            dump = y_hbm.at[pl.ds((n_pairs + region * MOE_TM) * ROW_S, MOE_TM * ROW_S), :]
            zero_dump = pltpu.make_async_copy(obuf.at[2], dump, ssem.at[2])
            zero_dump.start()
            zero_dump.wait()
        prime(be_ref[0])
        stream(be_ref[0], slot_ref[0], 0, W_NCH)

    @pl.when(i < n_live)
    def _():
        e_next = nxt_ref[i]
        cur = slot_ref[i]

        @pl.when((first_ref[i] == 1) & (e_next >= 0))
        def _():
            prime(e_next)

        def block_body(c_cur, c_p1, c_p2):
            for r in range(MOE_TM):
                gather_copy(inv_cur, r, c_cur).wait()
            for s in range(ROW_S):
                hi, lo = _unpack_piece(xbuf[c_cur, pl.ds(s, MOE_TM, stride=ROW_S), :])
                x_s[:, s * 128:(s + 1) * 128] = hi.astype(BF16)
                x_s[:, (s + ROW_S) * 128:(s + ROW_S + 1) * 128] = lo.astype(BF16)
            nc = 256

            def gate_up(c_lo, c_hi):
                for c0 in range(c_lo, c_hi, nc):
                    g = (jnp.dot(x_s[...], w_buf[cur, :D, c0:c0 + nc], preferred_element_type=F32)
                         + bgu_ref[:, c0:c0 + nc])
                    u = (jnp.dot(x_s[...], w_buf[cur, :D, D_EXPERT + c0:D_EXPERT + c0 + nc],
                                 preferred_element_type=F32) + bgu_ref[:, D_EXPERT + c0:D_EXPERT + c0 + nc])
                    gate = jnp.minimum(g, SWIGLU_LIMIT)
                    up = jnp.clip(u, -SWIGLU_LIMIT, SWIGLU_LIMIT)
                    act_s[:, c0:c0 + nc] = (gate * jax.nn.sigmoid(SWIGLU_ALPHA * gate)
                                            * (up + 1.0)).astype(BF16)

            for r in range(MOE_TM):
                scatter_copy(inv_prev, r, c_p2, first_step=True).start()
            gate_up(0, D_EXPERT // 2)

            @pl.when(n_live > 0)
            def _():
                for r in range(MOE_TM // 2):
                    gather_copy(inv_next2, r, c_p2).start()
                gate_up(D_EXPERT // 2, D_EXPERT)

            @pl.when(first_ref[i] >= 0)
            def _():
                for r in range(MOE_TM // 2, MOE_TM):
                    gather_copy(inv_next2, r, c_p2).start()
                for c0 in range(0, D // 2, nc):
                    y_hi = (jnp.dot(act_s[...], w_buf[cur, D:, c0:c0 + nc], preferred_element_type=F32)
                            + bd_ref[:, c0:c0 + nc])
                    y_lo = (jnp.dot(act_s[...], w_buf[cur, D:, D // 2 + c0:D // 2 + c0 + nc],
                                    preferred_element_type=F32) + bd_ref[:, D // 2 + c0:D // 2 + c0 + nc])
                    for j in range(nc // 128):
                        s = c0 // 128 + j
                        obuf[c_cur, pl.ds(s, MOE_TM, stride=ROW_S), :] = _pack_pieces(
                            y_hi[:, j * 128:(j + 1) * 128], y_lo[:, j * 128:(j + 1) * 128])

            @pl.when(i >= 1)
            def _():
                for r in range(MOE_TM):
                    scatter_copy(inv_prev, r, c_p1).wait()

        for ring in range(3):
            @pl.when(s_cur == ring)
            def _():
                block_body(ring, (ring + 1) % 3, (ring + 2) % 3)

        @pl.when(e_next >= 0)
        def _():
            stream(e_next, 1 - cur, lo_ref[i], hi_ref[i])

        @pl.when(i == n_live - 1)
        def _():
            looped(lambda r: scatter_copy(inv_prev, r, s_p2).wait())
            looped(lambda r: scatter_copy(inv_cur, r, s_cur).start())
            looped(lambda r: scatter_copy(inv_cur, r, s_cur).wait())
            looped(lambda r: gather_copy(inv_next, r, s_p1).wait())
            looped(lambda r: gather_copy(inv_next2, r, s_p2).wait())


def _take(table, idx):
    ids = jnp.arange(table.shape[0], dtype=jnp.int32)
    return jnp.sum(jnp.where(idx[..., None] == ids, table, 0), axis=-1)


def _experts(h2, inv, block_e, nused, pstart, padded, layer, wgu, bgu, wd, bd, nblk, n_tok):
    nb_e = padded // MOE_TM
    pos = jnp.arange(nblk, dtype=jnp.int32) - _take(pstart // MOE_TM, block_e)
    nb = jnp.maximum(_take(nb_e, block_e), 1)
    lo = (pos * W_NCH) // nb
    hi = ((pos + 1) * W_NCH) // nb
    first = (pos == 0).astype(jnp.int32)
    eid = jnp.arange(N_EXPERTS, dtype=jnp.int32)
    later = jnp.where(nb_e > 0, eid, N_EXPERTS)
    nxt_e = jnp.concatenate([lax.cummin(later, reverse=True)[1:], jnp.full((1,), N_EXPERTS, jnp.int32)])
    nxt_e = jnp.where(nxt_e >= N_EXPERTS, -1, nxt_e)
    slot_e = (jnp.cumsum((nb_e > 0).astype(jnp.int32)) - 1) % 2
    i32 = lambda a: a.astype(jnp.int32)
    live = lambda i, nu: jnp.clip(i, 0, nu[0] - 1)
    sblk = lambda off: pl.BlockSpec((MOE_TM,), lambda i, be, nu, *_: (live(i + off, nu),),
                                    memory_space=pltpu.SMEM)
    bias = lambda n: pl.BlockSpec((None, 1, n), lambda i, be, nu, *_: (be[live(i, nu)], 0, 0))
    hbm = pl.BlockSpec(memory_space=pl.ANY)
    n_slabs = n_tok * TOP_K + 3 * MOE_TM
    return pl.pallas_call(
        functools.partial(_expert_kernel, layer=layer, n_tok=n_tok),
        grid_spec=pltpu.PrefetchScalarGridSpec(
            num_scalar_prefetch=7,
            grid=(nblk,),
            in_specs=[sblk(0), sblk(1), sblk(2), sblk(-1), bias(2 * D_EXPERT), bias(D), hbm, hbm, hbm],
            out_specs=hbm,
            scratch_shapes=[
                pltpu.VMEM((3, MOE_TM * ROW_S, 128), U32),
                pltpu.VMEM((3, MOE_TM * ROW_S, 128), U32),
                pltpu.VMEM((MOE_TM, D), BF16),
                pltpu.VMEM((MOE_TM, D_EXPERT), BF16),
                pltpu.VMEM((2, D + D_EXPERT, D), BF16),
                pltpu.VMEM((W_STAGE, W_CH, D), F32),
                pltpu.SemaphoreType.DMA((3,)),
                pltpu.SemaphoreType.DMA((3,)),
                pltpu.SemaphoreType.DMA((W_STAGE,)),
            ],
        ),
        out_shape=jax.ShapeDtypeStruct((n_slabs * ROW_S, 128), U32),
        compiler_params=_cparams(("arbitrary",), VMEM_LIMIT),
        name="moe_experts",
    )(block_e, nused, i32(first), i32(_take(nxt_e, block_e)), i32(lo), i32(hi), i32(_take(slot_e, block_e)),
      inv, inv, inv, inv, bgu.reshape(N_EXPERTS, 1, -1), bd.reshape(N_EXPERTS, 1, -1), h2, wgu, wd)


def _combine_kernel(p_ref, x_ref, g3_ref, ga2_ref, y_ref, xo_ref, acc_hi, acc_lo, *, tm):
    pair = TOP_K * ROW_S
    for r in range(tm):
        a_hi = a_lo = None
        for k in range(TOP_K):
            hi, lo = _unpack_piece(y_ref[pl.ds(r * pair + k * ROW_S, ROW_S), :])
            w = p_ref[r * TOP_K + k]
            a_hi = hi * w if k == 0 else a_hi + hi * w
            a_lo = lo * w if k == 0 else a_lo + lo * w
        acc_hi[pl.ds(r * ROW_S, ROW_S), :] = a_hi
        acc_lo[pl.ds(r * ROW_S, ROW_S), :] = a_lo
    y = jnp.concatenate([acc_hi[pl.ds(s, tm, stride=ROW_S), :] for s in range(ROW_S)]
                        + [acc_lo[pl.ds(s, tm, stride=ROW_S), :] for s in range(ROW_S)], axis=1)
    xo_ref[...] = x_ref[...] + ga2_ref[...] * (_rms(y) * g3_ref[...])


def _combine(y4, p4, xs, g3, modm, layer):
    tm = 128
    t = xs.shape[0]
    row = lambda i: layer * 8 + _mod_row(i, tm)
    return pl.pallas_call(
        functools.partial(_combine_kernel, tm=tm),
        grid=(t // tm,),
        in_specs=[
            pl.BlockSpec((tm * TOP_K,), lambda i: (i,), memory_space=pltpu.SMEM),
            pl.BlockSpec((tm, D), lambda i: (i, 0)),
            pl.BlockSpec((1, D), lambda i: (0, 0)),
            pl.BlockSpec((None, 1, D), lambda i: (row(i), 0, 5)),
            pl.BlockSpec((tm * TOP_K * ROW_S, 128), lambda i: (i, 0)),
        ],
        out_specs=pl.BlockSpec((tm, D), lambda i: (i, 0)),
        out_shape=jax.ShapeDtypeStruct((t, D), F32),
        scratch_shapes=[pltpu.VMEM((tm * ROW_S, 128), F32)] * 2,
        compiler_params=_cparams(("parallel",), VMEM_LIMIT),
        name="moe_combine",
    )(p4.reshape(-1), xs, g3.reshape(1, D), modm, y4)


def _moe(h2, e4, p4, r4, counts, xs, g3, modm, layer, wgu, bgu, wd, bd):
    t = e4.shape[0]
    nblk = -(-(t * TOP_K + N_EXPERTS * (MOE_TM - 1)) // MOE_TM)
    cnt = counts.reshape(N_EXPERTS).astype(jnp.int32)
    padded = (cnt + MOE_TM - 1) // MOE_TM * MOE_TM
    pend = jnp.cumsum(padded)
    pstart = pend - padded
    dest = (_take(pstart, e4) + r4).reshape(-1).astype(jnp.int32)
    nused = (pend[-1:] // MOE_TM).astype(jnp.int32)
    blk_start = jnp.arange(nblk, dtype=jnp.int32) * MOE_TM
    block_e = jnp.minimum(jnp.sum((pend[None, :] <= blk_start[:, None]).astype(jnp.int32), axis=1),
                          N_EXPERTS - 1).astype(jnp.int32)
    inv = _invmap(dest, pstart.astype(jnp.int32), cnt, padded.astype(jnp.int32), nblk * MOE_TM)
    y4 = _experts(h2, inv, block_e, nused, pstart, padded, layer, wgu, bgu, wd, bd, nblk, t)
    return _combine(y4, p4, xs, g3, modm, layer)


N_CTX_BLK = LC // HGRN_R
N_LAT_BLK = L // HGRN_R
N_SEQ_BLK = N_CTX_BLK + N_LAT_BLK


def _scan_block(b, s, rev):
    if rev:
        return jnp.where(s < N_CTX_BLK, T_LAT // HGRN_R + N_CTX_BLK * b + (N_CTX_BLK - 1 - s),
                         N_LAT_BLK * b + (N_SEQ_BLK - 1 - s))
    return jnp.where(s < N_CTX_BLK, T_LAT // HGRN_R + N_CTX_BLK * b + s, N_LAT_BLK * b + (s - N_CTX_BLK))


def _hgrn_kernel(*refs, rev):
    if rev:
        q_ref, f_ref, v_ref, lb_ref, of_ref, g_ref, on_ref, o_ref, st, qin_s, qd_s, ki_s, ke_s, v_s = refs
    else:
        q_ref, f_ref, v_ref, lb_ref, o_ref, st, qin_s, qd_s, ki_s, ke_s, v_s = refs

    @pl.when(pl.program_id(1) == 0)
    def _():
        st[...] = jnp.zeros_like(st)

    c = HGRN_C
    nc = HGRN_R // c
    r_i = lax.broadcasted_iota(jnp.int32, (c, c), 0)
    c_i = lax.broadcasted_iota(jnp.int32, (c, c), 1)
    mask = (c_i >= r_i) if rev else (c_i <= r_i)
    tri = mask.astype(F32)

    lb = lb_ref[...]
    decay = [None] * nc
    v_s[...] = v_ref[...].astype(BF16)
    for ci in range(nc):
        rs = slice(ci * c, (ci + 1) * c)
        sg = jax.nn.sigmoid(f_ref[rs, :])
        logf = jnp.log(lb + (1.0 - lb) * sg)
        kk = (1.0 - lb) * (1.0 - sg)
        q = q_ref[rs, :]
        qs = q * jax.nn.sigmoid(q)
        cum = jnp.dot(tri, logf, precision=HI, preferred_element_type=F32)
        total = cum[0:1] if rev else cum[c - 1:c]
        mid = cum[c // 2:c // 2 + 1]
        qin_s[rs, :] = (qs * jnp.exp(cum)).astype(BF16)
        qd_s[rs, :] = (qs * jnp.exp(cum - mid)).astype(BF16)
        ki_s[rs, :] = (kk * jnp.exp(mid - cum)).astype(BF16)
        ke_s[rs, :] = (kk * jnp.exp(total - cum)).astype(BF16)
        decay[ci] = jnp.exp(total)

    order = range(nc - 1, -1, -1) if rev else range(nc)
    for ci in order:
        rs = slice(ci * c, (ci + 1) * c)
        for h in range(HGRN_HEADS):
            hs = slice(h * 128, (h + 1) * 128)
            state = st[h]
            vb = v_s[rs, hs]
            sc = lax.dot_general(qd_s[rs, hs], ki_s[rs, hs], (((1,), (1,)), ((), ())),
                                 preferred_element_type=F32)
            sc = jnp.where(mask, sc, 0.0)
            intra = jnp.dot(sc.astype(BF16), vb, preferred_element_type=F32)
            inter = lax.dot_general(qin_s[rs, hs], state.astype(BF16), (((1,), (1,)), ((), ())),
                                    preferred_element_type=F32)
            st[h] = state * decay[ci][:, hs] + lax.dot_general(
                vb, ke_s[rs, hs], (((0,), (0,)), ((), ())), preferred_element_type=F32)
            o = intra + inter
            if rev:
                o = o + of_ref[rs, hs]
                g = g_ref[rs, hs]
                o_ref[rs, hs] = (_rms(o) * on_ref[:, hs] * (g * jax.nn.sigmoid(g))).astype(BF16)
            else:
                o_ref[rs, hs] = o


def _hgrn(pa, pb, lb, rev, of=None, onorm=None):
    r = HGRN_R
    blk = lambda b, s: _scan_block(b, s, rev)
    col = lambda c: pl.BlockSpec((r, HALF), lambda b, s: (blk(b, s), c))
    vec = pl.BlockSpec((1, HALF), lambda b, s: (0, 0))
    in_specs = [col(0), col(2 if rev else 1), col(3), vec]
    args = [pa, pa, pa, lb.reshape(1, HALF)]
    if rev:
        in_specs += [col(0), col(0), vec]
        args += [of, pb, onorm.reshape(1, HALF)]
    return pl.pallas_call(
        functools.partial(_hgrn_kernel, rev=rev),
        grid=(B, N_SEQ_BLK),
        in_specs=in_specs,
        out_specs=col(0),
        out_shape=jax.ShapeDtypeStruct((T_ALL, HALF), BF16 if rev else F32),
        scratch_shapes=[pltpu.VMEM((HGRN_HEADS, 128, 128), F32)] + [pltpu.VMEM((r, HALF), BF16)] * 5,
        compiler_params=_cparams(("parallel", "arbitrary"), VMEM_LIMIT),
        name="hgrn_bwd" if rev else "hgrn_fwd",
    )(*args)


def _lru_kernel(*refs, rev):
    if rev:
        (x_ref, xp_ref, xn_ref, cw_ref, cb_ref, wr_ref, br_ref, wi_ref, bi_ref, lam_ref,
         hf_ref, gt_ref, o_ref, xcat, a_s, u_s, h_s, carry) = refs
    else:
        (x_ref, xp_ref, xn_ref, cw_ref, cb_ref, wr_ref, br_ref, wi_ref, bi_ref, lam_ref,
         o_ref, xcat, a_s, u_s, h_s, carry) = refs
    s = pl.program_id(1)
    r = LRU_R

    @pl.when(s == 0)
    def _():
        carry[...] = jnp.zeros_like(carry)

    is_ctx = s < N_CTX_BLK
    if rev:
        j = jnp.where(is_ctx, N_CTX_BLK - 1 - s, N_SEQ_BLK - 1 - s)
    else:
        j = jnp.where(is_ctx, s, s - N_CTX_BLK)
    nb = jnp.where(is_ctx, N_CTX_BLK, N_LAT_BLK)
    xcat[0:8, :] = jnp.where(j == 0, 0.0, xp_ref[...])
    xcat[8:8 + r, :] = x_ref[...]
    xcat[8 + r:16 + r, :] = jnp.where(j == nb - 1, 0.0, xn_ref[...])
    xc = cb_ref[...] + xcat[pl.ds(6, r), :] * cw_ref[0:1, :]
    for t in range(1, 4):
        xc = xc + xcat[pl.ds(6 + t, r), :] * cw_ref[t:t + 1, :]

    lam = lam_ref[...]
    sp = jnp.maximum(-lam, 0.0) + jnp.log1p(jnp.exp(-jnp.abs(lam)))
    hd = HALF // LRU_HEADS
    for h in range(LRU_HEADS):
        cs = slice(h * hd, (h + 1) * hd)
        xh = xc[:, cs]
        rg = jax.nn.sigmoid(_bdot(xh, wr_ref[h]) + br_ref[:, cs])
        ig = jax.nn.sigmoid(_bdot(xh, wi_ref[h]) + bi_ref[:, cs])
        log_a = -LRU_C * rg * sp[:, cs]
        a = jnp.exp(log_a)
        a_s[:, cs] = a
        u_s[:, cs] = jnp.sqrt(-jnp.tanh(log_a) * (a * a + 1.0)) * (ig * xh)

    def step(t, h):
        tt = (r - 1 - t) if rev else t
        h = a_s[pl.ds(tt, 1), :] * h + u_s[pl.ds(tt, 1), :]
        h_s[pl.ds(tt, 1), :] = h
        return h

    carry[...] = lax.fori_loop(0, r, step, carry[...], unroll=8)
    if rev:
        o_ref[...] = ((h_s[...] + hf_ref[...]) * jax.nn.gelu(gt_ref[...])).astype(BF16)
    else:
        o_ref[...] = h_s[...]


def _lru(p, conv_w, conv_b, w_r, b_r, w_i, b_i, lam, rev, hf=None):
    r = LRU_R
    d = 1 if rev else 0
    blk = lambda b, s: _scan_block(b, s, rev)
    xcol = 2
    n8 = T_ALL // 8
    vec = lambda: pl.BlockSpec((None, 1, HALF), lambda b, s: (d, 0, 0))
    wspec = lambda: pl.BlockSpec((None, LRU_HEADS, 128, 128), lambda b, s: (d, 0, 0, 0))
    in_specs = [
        pl.BlockSpec((r, HALF), lambda b, s: (blk(b, s), xcol)),
        pl.BlockSpec((8, HALF), lambda b, s: (jnp.maximum(blk(b, s) * (r // 8) - 1, 0), xcol)),
        pl.BlockSpec((8, HALF), lambda b, s: (jnp.minimum((blk(b, s) + 1) * (r // 8), n8 - 1), xcol)),
        pl.BlockSpec((4, HALF), lambda b, s: (0, 0)),
        pl.BlockSpec((1, HALF), lambda b, s: (0, 0)),
        wspec(), vec(), wspec(), vec(), vec(),
    ]
    args = [p, p, p, conv_w, conv_b.reshape(1, HALF), w_r, b_r.reshape(2, 1, HALF), w_i,
            b_i.reshape(2, 1, HALF), lam.reshape(2, 1, HALF)]
    if rev:
        in_specs += [pl.BlockSpec((r, HALF), lambda b, s: (blk(b, s), 0)),
                     pl.BlockSpec((r, HALF), lambda b, s: (blk(b, s), xcol - 1))]
        args += [hf, p]
    return pl.pallas_call(
        functools.partial(_lru_kernel, rev=rev),
        grid=(B, N_SEQ_BLK),
        in_specs=in_specs,
        out_specs=pl.BlockSpec((r, HALF), lambda b, s: (blk(b, s), 0)),
        out_shape=jax.ShapeDtypeStruct((T_ALL, HALF), BF16 if rev else F32),
        scratch_shapes=[pltpu.VMEM((r + 16, HALF), F32), pltpu.VMEM((r, HALF), F32),
                        pltpu.VMEM((r, HALF), F32), pltpu.VMEM((r, HALF), F32),
                        pltpu.VMEM((1, HALF), F32)],
        compiler_params=_cparams(("parallel", "arbitrary"), VMEM_LIMIT),
        name="lru_bwd" if rev else "lru_fwd",
    )(*args)


def kernel(x, c, ctx, c_ctx, mod_w, mod_b, norm_g, ab_w_in, ab_w_out, gmlp_v_g, gmlp_ws, gmlp_bs, cd_w_in, cd_w_out, hgrn_lb, hgrn_onorm_g, lru_conv_w, lru_conv_b, lru_wr, lru_br, lru_wi, lru_bi, lru_lambda, router_w, router_b, exp_w_gu, exp_b_gu, exp_w_down, exp_b_down):
    cond8 = jnp.concatenate([c, c_ctx[None, :], jnp.zeros((5, D), F32)], axis=0)
    modm = _modulation(cond8, mod_w, mod_b).reshape(DEPTH * 8, 1, 6 * D)
    xs = jnp.concatenate([x.reshape(T_LAT, D), ctx.reshape(T_CTX, D)], axis=0)

    lb_soft = jax.nn.softmax(hgrn_lb.astype(F32), axis=0)
    lb_all = jnp.cumsum(lb_soft, axis=0) - lb_soft[0]

    (p0,) = _inproj(xs, norm_g[0, 0], modm, 0, [ab_w_in[0].astype(BF16)], BF16)
    zc, zs = _fnet_chan(p0)
    mix_a = jnp.concatenate([_fnet_pos(zc, zs, 0, L, 512, 1024),
                             _fnet_pos(zc, zs, T_LAT, LC, LC, LC)], axis=0)
    mix_g = _gmlp(p0, gmlp_v_g[0], gmlp_ws[0], gmlp_bs[0])
    xs, h2, e4, p4, r4, counts = _outproj(mix_a, mix_g, ab_w_out[0].astype(BF16), xs, T_ALL, norm_g[0, 1],
                                          norm_g[0, 2], modm, 0, router_w[0], router_b[0])
    xs = _moe(h2, e4, p4, r4, counts, xs, norm_g[0, 3], modm, 0,
              exp_w_gu, exp_b_gu[0], exp_w_down, exp_b_down[0])

    w_cd = [cd_w_in[0, :, :4 * HALF].astype(BF16), cd_w_in[0, :, 4 * HALF:].astype(BF16)]
    p1a, p1b = _inproj(xs, norm_g[1, 0], modm, 1, w_cd, F32)
    o_f = _hgrn(p1a, p1b, lb_all[1], False)
    hg = _hgrn(p1a, p1b, lb_all[1], True, o_f, hgrn_onorm_g[0])
    lru_args = (lru_conv_w[0], lru_conv_b[0], lru_wr[0], lru_br[0], lru_wi[0], lru_bi[0], lru_lambda[0])
    h_f = _lru(p1b, *lru_args, False)
    lr = _lru(p1b, *lru_args, True, h_f)
    xl, h2, e4, p4, r4, counts = _outproj(hg, lr, cd_w_out[0].astype(BF16), xs, T_LAT, norm_g[1, 1],
                                          norm_g[1, 2], modm, 1, router_w[1], router_b[1])
    xl = _moe(h2, e4, p4, r4, counts, xl, norm_g[1, 3], modm, 1,
              exp_w_gu, exp_b_gu[1], exp_w_down, exp_b_down[1])
    return xl.reshape(B, L, D)
```

```python
import functools

import numpy as np
import jax
import jax.numpy as jnp
from jax import lax
from jax.experimental import pallas as pl
from jax.experimental.pallas import tpu as pltpu

F32 = jnp.float32
BF16 = jnp.bfloat16
HI = lax.Precision.HIGHEST

D = 2048
B = 2
L = 4096
LC = 256
T_LAT = B * L
T_CTX = B * LC
T_ALL = T_LAT + T_CTX
DEPTH = 2
EPS = 1e-6

HALF = D // 2
N_EXPERTS = 32
TOP_K = 4
D_EXPERT = D // 2
SWIGLU_LIMIT = 7.0
SWIGLU_ALPHA = 1.702
MOE_TM = 256

FNET_GROUPS = 4
FNET_GD = HALF // FNET_GROUPS
GMLP_HEADS = 8
GMLP_CHUNK = 128
HGRN_HEADS = 8
HGRN_R = 256
HGRN_C = 64
LRU_HEADS = 8
LRU_R = HGRN_R
LRU_C = 8.0

VMEM_LIMIT = 56 * 1024 * 1024


def _cparams(sem, vmem=None):
    return pltpu.CompilerParams(dimension_semantics=sem, vmem_limit_bytes=vmem)


def _mod_row(i, tm):
    return jnp.where(i < L // tm, 0, jnp.where(i < 2 * L // tm, 1, 2))


def _rms(x):
    return x * lax.rsqrt(jnp.mean(x * x, axis=-1, keepdims=True) + EPS)


def _bdot(a, b):
    return jnp.dot(a.astype(BF16), b.astype(BF16), preferred_element_type=F32)


ROW_S = D // 256
U32 = jnp.uint32
HI_MASK = 0xFFFF0000


def _row_slab(ref, r):
    return ref.at[pl.ds(pl.multiple_of(r * ROW_S, ROW_S), ROW_S), :]


def _pack_pieces(hi, lo):
    hb = pltpu.bitcast(hi.astype(BF16).astype(F32), U32)
    lb = pltpu.bitcast(lo.astype(BF16).astype(F32), U32)
    return jnp.bitwise_or(jnp.bitwise_and(hb, U32(HI_MASK)), jnp.right_shift(lb, U32(16)))


def _unpack_piece(w):
    hi = pltpu.bitcast(jnp.bitwise_and(w, U32(HI_MASK)), F32)
    lo = pltpu.bitcast(jnp.left_shift(w, U32(16)), F32)
    return hi, lo


def _store_rows(ref, val, row0=0):
    n = val.shape[0]
    for s in range(ROW_S):
        ref[pl.ds(row0 * ROW_S + s, n, stride=ROW_S), :] = _pack_pieces(
            val[:, s * 128:(s + 1) * 128], val[:, (s + ROW_S) * 128:(s + ROW_S + 1) * 128])


def _mod_kernel(c_ref, w_ref, b_ref, o_ref):
    cnd = c_ref[...]
    s = cnd * jax.nn.sigmoid(cnd)
    o_ref[...] = _bdot(s, w_ref[...]) + b_ref[...]


def _modulation(cond8, mod_w, mod_b):
    tn = 1024
    n = mod_w.shape[-1]
    return pl.pallas_call(
        _mod_kernel,
        grid=(DEPTH, n // tn),
        in_specs=[
            pl.BlockSpec((8, D), lambda l, j: (0, 0)),
            pl.BlockSpec((None, D, tn), lambda l, j: (l, 0, j)),
            pl.BlockSpec((None, 1, tn), lambda l, j: (l, 0, j)),
        ],
        out_specs=pl.BlockSpec((None, 8, tn), lambda l, j: (l, 0, j)),
        out_shape=jax.ShapeDtypeStruct((DEPTH, 8, n), F32),
        compiler_params=_cparams(("parallel", "parallel"), VMEM_LIMIT),
        name="modulation",
    )(cond8, mod_w, mod_b.reshape(DEPTH, 1, n))


def _inproj_kernel(*refs, nw, n_lat_tiles):
    if n_lat_tiles is None:
        x_ref, g_ref, sh_ref, sc_ref = refs[:4]
        w_refs, o_refs = refs[4:4 + nw], refs[4 + nw:]
        x = x_ref[...]
    else:
        xl_ref, xc_ref, g_ref, sh_ref, sc_ref = refs[:5]
        w_refs, o_refs, xs_ref = refs[5:5 + nw], refs[5 + nw:5 + 2 * nw], refs[5 + 2 * nw]
        i = pl.program_id(0)

        @pl.when(i < n_lat_tiles)
        def _():
            xs_ref[...] = xl_ref[...]

        @pl.when(i >= n_lat_tiles)
        def _():
            xs_ref[...] = xc_ref[...]

        x = xs_ref[...]
    y = _rms(x) * g_ref[...]
    h = (y * (1.0 + sc_ref[...]) + sh_ref[...]).astype(BF16)
    for w_ref, o_ref in zip(w_refs, o_refs):
        o_ref[...] = jnp.dot(h, w_ref[...], preferred_element_type=F32).astype(o_ref.dtype)


def _inproj(src, g, modm, layer, ws_bf, out_dtype):
    tm = 256
    row = lambda i: layer * 8 + _mod_row(i, tm)
    pair = isinstance(src, tuple)
    if pair:
        n_lat_tiles = src[0].shape[0] // tm
        t = src[0].shape[0] + src[1].shape[0]
        x_specs = [pl.BlockSpec((tm, D), lambda i: (jnp.minimum(i, n_lat_tiles - 1), 0)),
                   pl.BlockSpec((tm, D), lambda i: (jnp.maximum(i - n_lat_tiles, 0), 0))]
        x_args = list(src)
    else:
        n_lat_tiles = None
        t = src.shape[0]
        x_specs = [pl.BlockSpec((tm, D), lambda i: (i, 0))]
        x_args = [src]
    out_specs = [pl.BlockSpec((tm, w.shape[1]), lambda i: (i, 0)) for w in ws_bf]
    out_shape = [jax.ShapeDtypeStruct((t, w.shape[1]), out_dtype) for w in ws_bf]
    if pair:
        out_specs.append(pl.BlockSpec((tm, D), lambda i: (i, 0)))
        out_shape.append(jax.ShapeDtypeStruct((t, D), F32))
    return pl.pallas_call(
        functools.partial(_inproj_kernel, nw=len(ws_bf), n_lat_tiles=n_lat_tiles),
        grid=(t // tm,),
        in_specs=x_specs + [
            pl.BlockSpec((1, D), lambda i: (0, 0)),
            pl.BlockSpec((None, 1, D), lambda i: (row(i), 0, 0)),
            pl.BlockSpec((None, 1, D), lambda i: (row(i), 0, 1)),
        ] + [pl.BlockSpec(w.shape, lambda i: (0, 0), pipeline_mode=pl.Buffered(1)) for w in ws_bf],
        out_specs=out_specs,
        out_shape=out_shape,
        compiler_params=_cparams(("parallel",), VMEM_LIMIT),
        name="inproj",
    )(*x_args, g.reshape(1, D), modm, modm, *ws_bf)


def _fnet_chan_kernel(p_ref, cs_ref, zc_ref, zs_ref):
    for g in range(FNET_GROUPS):
        sl = slice(g * FNET_GD, (g + 1) * FNET_GD)
        z = jnp.dot(p_ref[:, sl].astype(BF16), cs_ref[...], preferred_element_type=F32)
        zc_ref[:, sl] = z[:, :FNET_GD].astype(BF16)
        zs_ref[:, sl] = z[:, FNET_GD:].astype(BF16)


def _fnet_chan(p):
    tm = 512
    t = p.shape[0]
    k = np.arange(FNET_GD)
    ang = 2.0 * np.pi * ((k[:, None] * k[None, :]) % FNET_GD) / FNET_GD
    cs = np.concatenate([np.cos(ang), np.sin(ang)], axis=1) / np.sqrt(FNET_GD)
    cs = jnp.asarray(cs, F32).astype(BF16)
    return pl.pallas_call(
        _fnet_chan_kernel,
        grid=(t // tm,),
        in_specs=[
            pl.BlockSpec((tm, HALF), lambda i: (i, 0)),
            pl.BlockSpec((FNET_GD, 2 * FNET_GD), lambda i: (0, 0)),
        ],
        out_specs=[pl.BlockSpec((tm, HALF), lambda i: (i, 0))] * 2,
        out_shape=[jax.ShapeDtypeStruct((t, HALF), BF16)] * 2,
        compiler_params=_cparams(("parallel",), VMEM_LIMIT),
        name="fnet_chan",
    )(p, cs)


def _fnet_pos_kernel(bre_ref, bim_ref, cre_ref, cim_ref, rre_ref, rim_ref, zc_ref, zs_ref, o_ref, acc):
    nt = pl.program_id(2)
    br, bi = bre_ref[...], bim_ref[...]
    cr, ci = cre_ref[...], cim_ref[...]
    rr, ri = rre_ref[...], rim_ref[...]
    tr = br * cr - bi * ci
    ti = br * ci + bi * cr
    er = (tr * rr - ti * ri).astype(BF16)
    ei = (tr * ri + ti * rr).astype(BF16)
    part = (jnp.dot(er, zc_ref[...], preferred_element_type=F32)
            + jnp.dot(ei, zs_ref[...], preferred_element_type=F32))

    @pl.when(nt == 0)
    def _():
        acc[...] = part

    @pl.when(nt > 0)
    def _():
        acc[...] += part

    @pl.when(nt == pl.num_programs(2) - 1)
    def _():
        o_ref[...] = acc[...].astype(BF16)


def _fnet_pos(zc, zs, row0, seq, tk, tn):
    nkt, nnt = seq // tk, seq // tn
    th = 2.0 * np.pi / seq
    kk = np.arange(tk)[:, None]
    nn = np.arange(tn)[None, :]
    base = th * ((kk * nn) % seq)
    n0 = (np.arange(nnt) * tn)[:, None, None]
    col = th * ((np.arange(tk)[None, :, None] * n0) % seq)
    k0 = (np.arange(nkt) * tk)[:, None, None, None]
    nfull = (np.arange(nnt) * tn)[None, :, None, None] + np.arange(tn)[None, None, None, :]
    row = th * ((k0 * nfull) % seq)
    scale = 1.0 / np.sqrt(seq)
    f = lambda a: jnp.asarray(a, F32)
    tabs = (f(np.cos(base)), f(-np.sin(base)), f(np.cos(col)), f(-np.sin(col)),
            f(np.cos(row) * scale), f(-np.sin(row) * scale))
    rb = row0 // tn
    ob = row0 // tk
    return pl.pallas_call(
        _fnet_pos_kernel,
        grid=(B, nkt, nnt),
        in_specs=[
            pl.BlockSpec((tk, tn), lambda b, k, n: (0, 0)),
            pl.BlockSpec((tk, tn), lambda b, k, n: (0, 0)),
            pl.BlockSpec((None, tk, 1), lambda b, k, n: (n, 0, 0)),
            pl.BlockSpec((None, tk, 1), lambda b, k, n: (n, 0, 0)),
            pl.BlockSpec((None, None, 1, tn), lambda b, k, n: (k, n, 0, 0)),
            pl.BlockSpec((None, None, 1, tn), lambda b, k, n: (k, n, 0, 0)),
            pl.BlockSpec((tn, HALF), lambda b, k, n: (rb + b * nnt + n, 0)),
            pl.BlockSpec((tn, HALF), lambda b, k, n: (rb + b * nnt + n, 0)),
        ],
        out_specs=pl.BlockSpec((tk, HALF), lambda b, k, n: (b * nkt + k, 0)),
        out_shape=jax.ShapeDtypeStruct((B * seq, HALF), BF16),
        scratch_shapes=[pltpu.VMEM((tk, HALF), F32)],
        compiler_params=_cparams(("parallel", "parallel", "arbitrary"), VMEM_LIMIT),
        name=f"fnet_pos_{seq}",
    )(*tabs, zc, zs)


def _gmlp_kernel(u_ref, v_ref, vg_ref, ws_ref, bst_ref, o_ref, *, tm):
    zu = jax.nn.gelu(u_ref[...].astype(F32))
    zv = jax.nn.gelu(v_ref[...].astype(F32))
    mu = jnp.mean(zv, axis=-1, keepdims=True)
    dv = zv - mu
    var = jnp.mean(dv * dv, axis=-1, keepdims=True)
    vn = dv * lax.rsqrt(var + EPS) * vg_ref[...]
    hd = HALF // GMLP_HEADS
    for c in range(tm // GMLP_CHUNK):
        rs = slice(c * GMLP_CHUNK, (c + 1) * GMLP_CHUNK)
        for h in range(GMLP_HEADS):
            cs = slice(h * hd, (h + 1) * hd)
            s = _bdot(ws_ref[h], vn[rs, cs])
            s = s + bst_ref[:, h:h + 1]
            o_ref[rs, cs] = (zu[rs, cs] * s).astype(BF16)


def _gmlp(p, v_g, w_s, b_s):
    tm = 256
    t = p.shape[0]
    return pl.pallas_call(
        functools.partial(_gmlp_kernel, tm=tm),
        grid=(t // tm,),
        in_specs=[
            pl.BlockSpec((tm, HALF), lambda i: (i, 1)),
            pl.BlockSpec((tm, HALF), lambda i: (i, 2)),
            pl.BlockSpec((1, HALF), lambda i: (0, 0)),
            pl.BlockSpec((GMLP_HEADS, GMLP_CHUNK, GMLP_CHUNK), lambda i: (0, 0, 0)),
            pl.BlockSpec((GMLP_CHUNK, GMLP_HEADS), lambda i: (0, 0)),
        ],
        out_specs=pl.BlockSpec((tm, HALF), lambda i: (i, 0)),
        out_shape=jax.ShapeDtypeStruct((t, HALF), BF16),
        compiler_params=_cparams(("parallel",), VMEM_LIMIT),
        name="gmlp",
    )(p, p, v_g.reshape(1, HALF), w_s, b_s.T)


def _outproj_kernel(ma_ref, mb_ref, w_ref, x_ref, g1_ref, g2_ref, ga1_ref, sh2_ref, sc2_ref,
                    rw_ref, rb_ref, xo_ref, h2_ref, e_ref, p_ref, rk_ref, cnt_ref, carry, rw2, m_s, *, tm):
    @pl.when(pl.program_id(0) == 0)
    def _():
        carry[...] = jnp.zeros_like(carry)
        rw = rw_ref[...]
        rw_hi = rw.astype(BF16)
        rw2[:, :N_EXPERTS] = rw_hi
        rw2[:, N_EXPERTS:] = (rw - rw_hi.astype(F32)).astype(BF16)

    sub = 128
    iota = lax.broadcasted_iota(jnp.int32, (sub, N_EXPERTS), 1).astype(F32)
    lane4 = lax.broadcasted_iota(jnp.int32, (sub, TOP_K), 1)
    r_i = lax.broadcasted_iota(jnp.int32, (sub, sub), 0)
    c_i = lax.broadcasted_iota(jnp.int32, (sub, sub), 1)
    tri = (c_i < r_i).astype(BF16)
    running = carry[...]
    for r0 in range(0, tm, sub):
        rs = slice(r0, r0 + sub)
        m_s[rs, :] = (jnp.dot(ma_ref[rs, :], w_ref[:HALF, :], preferred_element_type=F32)
                      + jnp.dot(mb_ref[rs, :], w_ref[HALF:, :], preferred_element_type=F32))
    for r0 in range(0, tm, sub):
        rs = slice(r0, r0 + sub)
        m = m_s[rs, :]
        xn = x_ref[rs, :] + ga1_ref[...] * (_rms(m) * g1_ref[...])
        xo_ref[rs, :] = xn
        h2 = (_rms(xn) * g2_ref[...]) * (1.0 + sc2_ref[...]) + sh2_ref[...]
        _store_rows(h2_ref, h2, r0)

        h_hi = h2.astype(BF16)
        h_lo = (h2 - h_hi.astype(F32)).astype(BF16)
        t1 = jnp.dot(h_hi, rw2[...], preferred_element_type=F32)
        t2 = jnp.dot(h_lo, rw2[:, :N_EXPERTS], preferred_element_type=F32)
        lg = t1[:, :N_EXPERTS] + t1[:, N_EXPERTS:] + t2 + rb_ref[...]
        idxs, vals = [], []
        for _ in range(TOP_K):
            mx = jnp.max(lg, axis=-1, keepdims=True)
            ix = jnp.min(jnp.where(lg == mx, iota, float(N_EXPERTS)), axis=-1, keepdims=True)
            idxs.append(ix)
            vals.append(mx)
            lg = jnp.where(iota == ix, -jnp.inf, lg)
        exs = [jnp.exp(v - vals[0]) for v in vals]
        den = exs[0] + exs[1] + exs[2] + exs[3]

        onehot = [(iota == ix) for ix in idxs]
        cnt = jnp.zeros((sub, N_EXPERTS), F32)
        for oh in onehot:
            cnt = cnt + oh.astype(F32)
        prefix = jnp.dot(tri, cnt.astype(BF16), preferred_element_type=F32) + running

        e4 = jnp.zeros((sub, TOP_K), jnp.int32)
        p4 = jnp.zeros((sub, TOP_K), F32)
        r4 = jnp.zeros((sub, TOP_K), jnp.int32)
        for k in range(TOP_K):
            rk = jnp.sum(jnp.where(onehot[k], prefix, 0.0), axis=-1, keepdims=True)
            e4 = jnp.where(lane4 == k, idxs[k].astype(jnp.int32), e4)
            p4 = jnp.where(lane4 == k, exs[k] / den, p4)
            r4 = jnp.where(lane4 == k, rk.astype(jnp.int32), r4)
        e_ref[rs, :] = e4
        p_ref[rs, :] = p4
        rk_ref[rs, :] = r4
        running = running + jnp.sum(cnt, axis=0, keepdims=True)
    carry[...] = running
    cnt_ref[...] = running


def _outproj(ma, mb, w_bf, xs, t, g1, g2, modm, layer, rw, rb):
    tm = 512
    row = lambda i: layer * 8 + _mod_row(i, tm)
    modspec = lambda c: pl.BlockSpec((None, 1, D), lambda i: (row(i), 0, c))
    vec = pl.BlockSpec((1, D), lambda i: (0, 0))
    small = lambda dt: jax.ShapeDtypeStruct((t, TOP_K), dt)
    return pl.pallas_call(
        functools.partial(_outproj_kernel, tm=tm),
        grid=(t // tm,),
        in_specs=[
            pl.BlockSpec((tm, HALF), lambda i: (i, 0)),
            pl.BlockSpec((tm, HALF), lambda i: (i, 0)),
            pl.BlockSpec((D, D), lambda i: (0, 0)),
            pl.BlockSpec((tm, D), lambda i: (i, 0)),
            vec, vec, modspec(2), modspec(3), modspec(4),
            pl.BlockSpec((D, N_EXPERTS), lambda i: (0, 0)),
            pl.BlockSpec((1, N_EXPERTS), lambda i: (0, 0)),
        ],
        out_specs=[
            pl.BlockSpec((tm, D), lambda i: (i, 0)),
            pl.BlockSpec((tm * ROW_S, 128), lambda i: (i, 0)),
            pl.BlockSpec((tm, TOP_K), lambda i: (i, 0)),
            pl.BlockSpec((tm, TOP_K), lambda i: (i, 0)),
            pl.BlockSpec((tm, TOP_K), lambda i: (i, 0)),
            pl.BlockSpec((1, N_EXPERTS), lambda i: (0, 0)),
        ],
        out_shape=[
            jax.ShapeDtypeStruct((t, D), F32),
            jax.ShapeDtypeStruct((t * ROW_S, 128), U32),
            small(jnp.int32), small(F32), small(jnp.int32),
            jax.ShapeDtypeStruct((1, N_EXPERTS), F32),
        ],
        scratch_shapes=[pltpu.VMEM((1, N_EXPERTS), F32), pltpu.VMEM((D, 2 * N_EXPERTS), BF16),
                        pltpu.VMEM((tm, D), F32)],
        compiler_params=_cparams(("arbitrary",), VMEM_LIMIT),
        name="outproj_router",
    )(ma, mb, w_bf, xs, g1.reshape(1, D), g2.reshape(1, D), modm, modm, modm,
      rw, rb.reshape(1, N_EXPERTS))


def _invmap_kernel(pstart_ref, cnt_ref, padded_ref, dest_ref, inv_ref, *, n_pairs, n_slots):
    def unused(lo, hi):
        def body(s, c):
            inv_ref[s] = n_pairs + jnp.bitwise_and(s, 2 * MOE_TM - 1)
            return c
        lax.fori_loop(lo, hi, body, 0)

    def per_expert(e, c):
        unused(pstart_ref[e] + cnt_ref[e], pstart_ref[e] + padded_ref[e])
        return c

    lax.fori_loop(0, N_EXPERTS, per_expert, 0)
    unused(pstart_ref[N_EXPERTS - 1] + padded_ref[N_EXPERTS - 1], n_slots)

    def body(i, c):
        inv_ref[dest_ref[i]] = i
        return c

    lax.fori_loop(0, n_pairs, body, 0, unroll=16)


def _invmap(dest, pstart, cnt, padded, n_slots):
    n_pairs = dest.shape[0]
    return pl.pallas_call(
        functools.partial(_invmap_kernel, n_pairs=n_pairs, n_slots=n_slots),
        grid_spec=pltpu.PrefetchScalarGridSpec(
            num_scalar_prefetch=3,
            grid=(1,),
            in_specs=[pl.BlockSpec(memory_space=pltpu.SMEM)],
            out_specs=pl.BlockSpec(memory_space=pltpu.SMEM),
        ),
        out_shape=jax.ShapeDtypeStruct((n_slots,), jnp.int32),
        compiler_params=_cparams(("arbitrary",)),
        name="moe_invmap",
    )(pstart, cnt, padded, dest)


W_CH = 256
W_NCH_GU = D // W_CH
W_NCH = W_NCH_GU + D_EXPERT // W_CH
W_STAGE = 4


def _expert_kernel(be_ref, nused_ref, first_ref, nxt_ref, lo_ref, hi_ref, slot_ref,
                   inv_cur, inv_next, inv_next2, inv_prev, bgu_ref, bd_ref, h_hbm, wgu_all, wd_all, y_hbm,
                   xbuf, obuf, x_s, act_s, w_buf, stage, gsem, ssem, sem, *, layer, n_tok):
    i = pl.program_id(0)
    n_live = nused_ref[0]
    s_cur = lax.rem(i, 3)
    s_p1 = lax.rem(i + 1, 3)
    s_p2 = lax.rem(i + 2, 3)
    wgu_hbm = wgu_all.at[layer]
    wd_hbm = wd_all.at[layer]

    def gather_copy(inv_ref, r, slot):
        tok = jnp.minimum(lax.shift_right_logical(inv_ref[r], 2), n_tok - 1)
        return pltpu.make_async_copy(_row_slab(h_hbm, tok), _row_slab(xbuf.at[slot], r), gsem.at[slot])

    n_pairs = n_tok * TOP_K

    def scatter_copy(inv_ref, r, slot, first_step=False):
        dst = inv_ref[r]
        if first_step:
            dst = jnp.where(i == 0, n_pairs + 2 * MOE_TM + r, dst)
        return pltpu.make_async_copy(_row_slab(obuf.at[slot], r), _row_slab(y_hbm, dst), ssem.at[slot])

    def looped(fn):
        def body(r, c):
            fn(r)
            return c
        lax.fori_loop(0, MOE_TM, body, 0)

    def start_chunk(e, c):
        s = lax.rem(c, W_STAGE)

        @pl.when(c < W_NCH_GU)
        def _():
            r0 = pl.multiple_of(c * W_CH, W_CH)
            pltpu.make_async_copy(wgu_hbm.at[e, pl.ds(r0, W_CH), :], stage.at[s], sem.at[s]).start()

        @pl.when(c >= W_NCH_GU)
        def _():
            r0 = pl.multiple_of((c - W_NCH_GU) * W_CH, W_CH)
            pltpu.make_async_copy(wd_hbm.at[e, pl.ds(r0, W_CH), :], stage.at[s], sem.at[s]).start()

    def finish_chunk(c, dst):
        s = lax.rem(c, W_STAGE)
        pltpu.make_async_copy(wgu_hbm.at[0, pl.ds(0, W_CH), :], stage.at[s], sem.at[s]).wait()
        w_buf[dst, pl.ds(pl.multiple_of(c * W_CH, W_CH), W_CH), :] = stage[s].astype(BF16)

    def stream(e, dst, c_lo, c_hi):
        def body(c, carry):
            finish_chunk(c, dst)

            @pl.when(c + W_STAGE < W_NCH)
            def _():
                start_chunk(e, c + W_STAGE)
            return carry
        lax.fori_loop(c_lo, c_hi, body, 0)

    def prime(e):
        for c in range(W_STAGE):
            start_chunk(e, c)

    @pl.when(i == 0)
    def _():
        looped(lambda r: gather_copy(inv_cur, r, 0).start())
        looped(lambda r: gather_copy(inv_next, r, 1).start())
        obuf[2] = jnp.zeros(obuf.shape[1:], U32)
        for region in range(2):
            dump = y_hbm.at[pl.ds((n_pairs + region * MOE_TM) * ROW_S, MOE_TM * ROW_S), :]
            zero_dump = pltpu.make_async_copy(obuf.at[2], dump, ssem.at[2])
            zero_dump.start()
            zero_dump.wait()
        prime(be_ref[0])
        stream(be_ref[0], slot_ref[0], 0, W_NCH)

    @pl.when(i < n_live)
    def _():
        e_next = nxt_ref[i]
        cur = slot_ref[i]

        @pl.when((first_ref[i] == 1) & (e_next >= 0))
        def _():
            prime(e_next)

        def block_body(c_cur, c_p1, c_p2):
            for r in range(MOE_TM):
                gather_copy(inv_cur, r, c_cur).wait()
            for s in range(ROW_S):
                hi, lo = _unpack_piece(xbuf[c_cur, pl.ds(s, MOE_TM, stride=ROW_S), :])
                x_s[:, s * 128:(s + 1) * 128] = hi.astype(BF16)
                x_s[:, (s + ROW_S) * 128:(s + ROW_S + 1) * 128] = lo.astype(BF16)
            nc = 256

            def gate_up(c_lo, c_hi):
                for c0 in range(c_lo, c_hi, nc):
                    g = (jnp.dot(x_s[...], w_buf[cur, :D, c0:c0 + nc], preferred_element_type=F32)
                         + bgu_ref[:, c0:c0 + nc])
                    u = (jnp.dot(x_s[...], w_buf[cur, :D, D_EXPERT + c0:D_EXPERT + c0 + nc],
                                 preferred_element_type=F32) + bgu_ref[:, D_EXPERT + c0:D_EXPERT + c0 + nc])
                    gate = jnp.minimum(g, SWIGLU_LIMIT)
                    up = jnp.clip(u, -SWIGLU_LIMIT, SWIGLU_LIMIT)
                    act_s[:, c0:c0 + nc] = (gate * jax.nn.sigmoid(SWIGLU_ALPHA * gate)
                                            * (up + 1.0)).astype(BF16)

            for r in range(MOE_TM):
                scatter_copy(inv_prev, r, c_p2, first_step=True).start()
            gate_up(0, D_EXPERT // 2)

            @pl.when(n_live > 0)
            def _():
                for r in range(MOE_TM // 2):
                    gather_copy(inv_next2, r, c_p2).start()
                gate_up(D_EXPERT // 2, D_EXPERT)

            @pl.when(first_ref[i] >= 0)
            def _():
                for r in range(MOE_TM // 2, MOE_TM):
                    gather_copy(inv_next2, r, c_p2).start()
                for c0 in range(0, D // 2, nc):
                    y_hi = (jnp.dot(act_s[...], w_buf[cur, D:, c0:c0 + nc], preferred_element_type=F32)
                            + bd_ref[:, c0:c0 + nc])
                    y_lo = (jnp.dot(act_s[...], w_buf[cur, D:, D // 2 + c0:D // 2 + c0 + nc],
                                    preferred_element_type=F32) + bd_ref[:, D // 2 + c0:D // 2 + c0 + nc])
                    for j in range(nc // 128):
                        s = c0 // 128 + j
                        obuf[c_cur, pl.ds(s, MOE_TM, stride=ROW_S), :] = _pack_pieces(
                            y_hi[:, j * 128:(j + 1) * 128], y_lo[:, j * 128:(j + 1) * 128])

            @pl.when(i >= 1)
            def _():
                for r in range(MOE_TM):
                    scatter_copy(inv_prev, r, c_p1).wait()

        for ring in range(3):
            @pl.when(s_cur == ring)
            def _():
                block_body(ring, (ring + 1) % 3, (ring + 2) % 3)

        @pl.when(e_next >= 0)
        def _():
            stream(e_next, 1 - cur, lo_ref[i], hi_ref[i])

        @pl.when(i == n_live - 1)
        def _():
            looped(lambda r: scatter_copy(inv_prev, r, s_p2).wait())
            looped(lambda r: scatter_copy(inv_cur, r, s_cur).start())
            looped(lambda r: scatter_copy(inv_cur, r, s_cur).wait())
            looped(lambda r: gather_copy(inv_next, r, s_p1).wait())
            looped(lambda r: gather_copy(inv_next2, r, s_p2).wait())


def _take(table, idx):
    ids = jnp.arange(table.shape[0], dtype=jnp.int32)
    return jnp.sum(jnp.where(idx[..., None] == ids, table, 0), axis=-1)


def _experts(h2, inv, block_e, nused, pstart, padded, layer, wgu, bgu, wd, bd, nblk, n_tok):
    nb_e = padded // MOE_TM
    pos = jnp.arange(nblk, dtype=jnp.int32) - _take(pstart // MOE_TM, block_e)
    nb = jnp.maximum(_take(nb_e, block_e), 1)
    lo = (pos * W_NCH) // nb
    hi = ((pos + 1) * W_NCH) // nb
    first = (pos == 0).astype(jnp.int32)
    eid = jnp.arange(N_EXPERTS, dtype=jnp.int32)
    later = jnp.where(nb_e > 0, eid, N_EXPERTS)
    nxt_e = jnp.concatenate([lax.cummin(later, reverse=True)[1:], jnp.full((1,), N_EXPERTS, jnp.int32)])
    nxt_e = jnp.where(nxt_e >= N_EXPERTS, -1, nxt_e)
    slot_e = (jnp.cumsum((nb_e > 0).astype(jnp.int32)) - 1) % 2
    i32 = lambda a: a.astype(jnp.int32)
    live = lambda i, nu: jnp.clip(i, 0, nu[0] - 1)
    sblk = lambda off: pl.BlockSpec((MOE_TM,), lambda i, be, nu, *_: (live(i + off, nu),),
                                    memory_space=pltpu.SMEM)
    bias = lambda n: pl.BlockSpec((None, 1, n), lambda i, be, nu, *_: (be[live(i, nu)], 0, 0))
    hbm = pl.BlockSpec(memory_space=pl.ANY)
    n_slabs = n_tok * TOP_K + 3 * MOE_TM
    return pl.pallas_call(
        functools.partial(_expert_kernel, layer=layer, n_tok=n_tok),
        grid_spec=pltpu.PrefetchScalarGridSpec(
            num_scalar_prefetch=7,
            grid=(nblk,),
            in_specs=[sblk(0), sblk(1), sblk(2), sblk(-1), bias(2 * D_EXPERT), bias(D), hbm, hbm, hbm],
            out_specs=hbm,
            scratch_shapes=[
                pltpu.VMEM((3, MOE_TM * ROW_S, 128), U32),
                pltpu.VMEM((3, MOE_TM * ROW_S, 128), U32),
                pltpu.VMEM((MOE_TM, D), BF16),
                pltpu.VMEM((MOE_TM, D_EXPERT), BF16),
                pltpu.VMEM((2, D + D_EXPERT, D), BF16),
                pltpu.VMEM((W_STAGE, W_CH, D), F32),
                pltpu.SemaphoreType.DMA((3,)),
                pltpu.SemaphoreType.DMA((3,)),
                pltpu.SemaphoreType.DMA((W_STAGE,)),
            ],
        ),
        out_shape=jax.ShapeDtypeStruct((n_slabs * ROW_S, 128), U32),
        compiler_params=_cparams(("arbitrary",), VMEM_LIMIT),
        name="moe_experts",
    )(block_e, nused, i32(first), i32(_take(nxt_e, block_e)), i32(lo), i32(hi), i32(_take(slot_e, block_e)),
      inv, inv, inv, inv, bgu.reshape(N_EXPERTS, 1, -1), bd.reshape(N_EXPERTS, 1, -1), h2, wgu, wd)


def _combine_kernel(p_ref, x_ref, g3_ref, ga2_ref, y_ref, xo_ref, acc_hi, acc_lo, *, tm):
    pair = TOP_K * ROW_S
    for r in range(tm):
        a_hi = a_lo = None
        for k in range(TOP_K):
            hi, lo = _unpack_piece(y_ref[pl.ds(r * pair + k * ROW_S, ROW_S), :])
            w = p_ref[r * TOP_K + k]
            a_hi = hi * w if k == 0 else a_hi + hi * w
            a_lo = lo * w if k == 0 else a_lo + lo * w
        acc_hi[pl.ds(r * ROW_S, ROW_S), :] = a_hi
        acc_lo[pl.ds(r * ROW_S, ROW_S), :] = a_lo
    y = jnp.concatenate([acc_hi[pl.ds(s, tm, stride=ROW_S), :] for s in range(ROW_S)]
                        + [acc_lo[pl.ds(s, tm, stride=ROW_S), :] for s in range(ROW_S)], axis=1)
    xo_ref[...] = x_ref[...] + ga2_ref[...] * (_rms(y) * g3_ref[...])


def _combine(y4, p4, xs, g3, modm, layer):
    tm = 128
    t = xs.shape[0]
    row = lambda i: layer * 8 + _mod_row(i, tm)
    return pl.pallas_call(
        functools.partial(_combine_kernel, tm=tm),
        grid=(t // tm,),
        in_specs=[
            pl.BlockSpec((tm * TOP_K,), lambda i: (i,), memory_space=pltpu.SMEM),
            pl.BlockSpec((tm, D), lambda i: (i, 0)),
            pl.BlockSpec((1, D), lambda i: (0, 0)),
            pl.BlockSpec((None, 1, D), lambda i: (row(i), 0, 5)),
            pl.BlockSpec((tm * TOP_K * ROW_S, 128), lambda i: (i, 0)),
        ],
        out_specs=pl.BlockSpec((tm, D), lambda i: (i, 0)),
        out_shape=jax.ShapeDtypeStruct((t, D), F32),
        scratch_shapes=[pltpu.VMEM((tm * ROW_S, 128), F32)] * 2,
        compiler_params=_cparams(("parallel",), VMEM_LIMIT),
        name="moe_combine",
    )(p4.reshape(-1), xs, g3.reshape(1, D), modm, y4)


def _moe(h2, e4, p4, r4, counts, xs, g3, modm, layer, wgu, bgu, wd, bd):
    t = e4.shape[0]
    nblk = -(-(t * TOP_K + N_EXPERTS * (MOE_TM - 1)) // MOE_TM)
    cnt = counts.reshape(N_EXPERTS).astype(jnp.int32)
    padded = (cnt + MOE_TM - 1) // MOE_TM * MOE_TM
    pend = jnp.cumsum(padded)
    pstart = pend - padded
    dest = (_take(pstart, e4) + r4).reshape(-1).astype(jnp.int32)
    nused = (pend[-1:] // MOE_TM).astype(jnp.int32)
    blk_start = jnp.arange(nblk, dtype=jnp.int32) * MOE_TM
    block_e = jnp.minimum(jnp.sum((pend[None, :] <= blk_start[:, None]).astype(jnp.int32), axis=1),
                          N_EXPERTS - 1).astype(jnp.int32)
    inv = _invmap(dest, pstart.astype(jnp.int32), cnt, padded.astype(jnp.int32), nblk * MOE_TM)
    y4 = _experts(h2, inv, block_e, nused, pstart, padded, layer, wgu, bgu, wd, bd, nblk, t)
    return _combine(y4, p4, xs, g3, modm, layer)


N_CTX_BLK = LC // HGRN_R
N_LAT_BLK = L // HGRN_R
N_SEQ_BLK = N_CTX_BLK + N_LAT_BLK


def _scan_block(b, s, rev):
    if rev:
        return jnp.where(s < N_CTX_BLK, T_LAT // HGRN_R + N_CTX_BLK * b + (N_CTX_BLK - 1 - s),
                         N_LAT_BLK * b + (N_SEQ_BLK - 1 - s))
    return jnp.where(s < N_CTX_BLK, T_LAT // HGRN_R + N_CTX_BLK * b + s, N_LAT_BLK * b + (s - N_CTX_BLK))


def _hgrn_kernel(*refs, rev):
    if rev:
        q_ref, f_ref, v_ref, lb_ref, of_ref, g_ref, on_ref, o_ref, st, qin_s, qd_s, ki_s, ke_s, v_s = refs
    else:
        q_ref, f_ref, v_ref, lb_ref, o_ref, st, qin_s, qd_s, ki_s, ke_s, v_s = refs

    @pl.when(pl.program_id(1) == 0)
    def _():
        st[...] = jnp.zeros_like(st)

    c = HGRN_C
    nc = HGRN_R // c
    r_i = lax.broadcasted_iota(jnp.int32, (c, c), 0)
    c_i = lax.broadcasted_iota(jnp.int32, (c, c), 1)
    mask = (c_i >= r_i) if rev else (c_i <= r_i)
    tri = mask.astype(F32)

    lb = lb_ref[...]
    decay = [None] * nc
    v_s[...] = v_ref[...].astype(BF16)
    for ci in range(nc):
        rs = slice(ci * c, (ci + 1) * c)
        sg = jax.nn.sigmoid(f_ref[rs, :])
        logf = jnp.log(lb + (1.0 - lb) * sg)
        kk = (1.0 - lb) * (1.0 - sg)
        q = q_ref[rs, :]
        qs = q * jax.nn.sigmoid(q)
        cum = jnp.dot(tri, logf, precision=HI, preferred_element_type=F32)
        total = cum[0:1] if rev else cum[c - 1:c]
        mid = cum[c // 2:c // 2 + 1]
        qin_s[rs, :] = (qs * jnp.exp(cum)).astype(BF16)
        qd_s[rs, :] = (qs * jnp.exp(cum - mid)).astype(BF16)
        ki_s[rs, :] = (kk * jnp.exp(mid - cum)).astype(BF16)
        ke_s[rs, :] = (kk * jnp.exp(total - cum)).astype(BF16)
        decay[ci] = jnp.exp(total)

    order = range(nc - 1, -1, -1) if rev else range(nc)
    for ci in order:
        rs = slice(ci * c, (ci + 1) * c)
        for h in range(HGRN_HEADS):
            hs = slice(h * 128, (h + 1) * 128)
            state = st[h]
            vb = v_s[rs, hs]
            sc = lax.dot_general(qd_s[rs, hs], ki_s[rs, hs], (((1,), (1,)), ((), ())),
                                 preferred_element_type=F32)
            sc = jnp.where(mask, sc, 0.0)
            intra = jnp.dot(sc.astype(BF16), vb, preferred_element_type=F32)
            inter = lax.dot_general(qin_s[rs, hs], state.astype(BF16), (((1,), (1,)), ((), ())),
                                    preferred_element_type=F32)
            st[h] = state * decay[ci][:, hs] + lax.dot_general(
                vb, ke_s[rs, hs], (((0,), (0,)), ((), ())), preferred_element_type=F32)
            o = intra + inter
            if rev:
                o = o + of_ref[rs, hs]
                g = g_ref[rs, hs]
                o_ref[rs, hs] = (_rms(o) * on_ref[:, hs] * (g * jax.nn.sigmoid(g))).astype(BF16)
            else:
                o_ref[rs, hs] = o


def _hgrn(pa, pb, lb, rev, of=None, onorm=None):
    r = HGRN_R
    blk = lambda b, s: _scan_block(b, s, rev)
    col = lambda c: pl.BlockSpec((r, HALF), lambda b, s: (blk(b, s), c))
    vec = pl.BlockSpec((1, HALF), lambda b, s: (0, 0))
    in_specs = [col(0), col(2 if rev else 1), col(3), vec]
    args = [pa, pa, pa, lb.reshape(1, HALF)]
    if rev:
        in_specs += [col(0), col(0), vec]
        args += [of, pb, onorm.reshape(1, HALF)]
    return pl.pallas_call(
        functools.partial(_hgrn_kernel, rev=rev),
        grid=(B, N_SEQ_BLK),
        in_specs=in_specs,
        out_specs=col(0),
        out_shape=jax.ShapeDtypeStruct((T_ALL, HALF), BF16 if rev else F32),
        scratch_shapes=[pltpu.VMEM((HGRN_HEADS, 128, 128), F32)] + [pltpu.VMEM((r, HALF), BF16)] * 5,
        compiler_params=_cparams(("parallel", "arbitrary"), VMEM_LIMIT),
        name="hgrn_bwd" if rev else "hgrn_fwd",
    )(*args)


def _lru_kernel(*refs, rev):
    if rev:
        (x_ref, xp_ref, xn_ref, cw_ref, cb_ref, wr_ref, br_ref, wi_ref, bi_ref, lam_ref,
         hf_ref, gt_ref, o_ref, xcat, a_s, u_s, h_s, carry) = refs
    else:
        (x_ref, xp_ref, xn_ref, cw_ref, cb_ref, wr_ref, br_ref, wi_ref, bi_ref, lam_ref,
         o_ref, xcat, a_s, u_s, h_s, carry) = refs
    s = pl.program_id(1)
    r = LRU_R

    @pl.when(s == 0)
    def _():
        carry[...] = jnp.zeros_like(carry)

    is_ctx = s < N_CTX_BLK
    if rev:
        j = jnp.where(is_ctx, N_CTX_BLK - 1 - s, N_SEQ_BLK - 1 - s)
    else:
        j = jnp.where(is_ctx, s, s - N_CTX_BLK)
    nb = jnp.where(is_ctx, N_CTX_BLK, N_LAT_BLK)
    xcat[0:8, :] = jnp.where(j == 0, 0.0, xp_ref[...])
    xcat[8:8 + r, :] = x_ref[...]
    xcat[8 + r:16 + r, :] = jnp.where(j == nb - 1, 0.0, xn_ref[...])
    xc = cb_ref[...] + xcat[pl.ds(6, r), :] * cw_ref[0:1, :]
    for t in range(1, 4):
        xc = xc + xcat[pl.ds(6 + t, r), :] * cw_ref[t:t + 1, :]

    lam = lam_ref[...]
    sp = jnp.maximum(-lam, 0.0) + jnp.log1p(jnp.exp(-jnp.abs(lam)))
    hd = HALF // LRU_HEADS
    for h in range(LRU_HEADS):
        cs = slice(h * hd, (h + 1) * hd)
        xh = xc[:, cs]
        rg = jax.nn.sigmoid(_bdot(xh, wr_ref[h]) + br_ref[:, cs])
        ig = jax.nn.sigmoid(_bdot(xh, wi_ref[h]) + bi_ref[:, cs])
        log_a = -LRU_C * rg * sp[:, cs]
        a = jnp.exp(log_a)
        a_s[:, cs] = a
        u_s[:, cs] = jnp.sqrt(-jnp.tanh(log_a) * (a * a + 1.0)) * (ig * xh)

    def step(t, h):
        tt = (r - 1 - t) if rev else t
        h = a_s[pl.ds(tt, 1), :] * h + u_s[pl.ds(tt, 1), :]
        h_s[pl.ds(tt, 1), :] = h
        return h

    carry[...] = lax.fori_loop(0, r, step, carry[...], unroll=8)
    if rev:
        o_ref[...] = ((h_s[...] + hf_ref[...]) * jax.nn.gelu(gt_ref[...])).astype(BF16)
    else:
        o_ref[...] = h_s[...]


def _lru(p, conv_w, conv_b, w_r, b_r, w_i, b_i, lam, rev, hf=None):
    r = LRU_R
    d = 1 if rev else 0
    blk = lambda b, s: _scan_block(b, s, rev)
    xcol = 2
    n8 = T_ALL // 8
    vec = lambda: pl.BlockSpec((None, 1, HALF), lambda b, s: (d, 0, 0))
    wspec = lambda: pl.BlockSpec((None, LRU_HEADS, 128, 128), lambda b, s: (d, 0, 0, 0))
    in_specs = [
        pl.BlockSpec((r, HALF), lambda b, s: (blk(b, s), xcol)),
        pl.BlockSpec((8, HALF), lambda b, s: (jnp.maximum(blk(b, s) * (r // 8) - 1, 0), xcol)),
        pl.BlockSpec((8, HALF), lambda b, s: (jnp.minimum((blk(b, s) + 1) * (r // 8), n8 - 1), xcol)),
        pl.BlockSpec((4, HALF), lambda b, s: (0, 0)),
        pl.BlockSpec((1, HALF), lambda b, s: (0, 0)),
        wspec(), vec(), wspec(), vec(), vec(),
    ]
    args = [p, p, p, conv_w, conv_b.reshape(1, HALF), w_r, b_r.reshape(2, 1, HALF), w_i,
            b_i.reshape(2, 1, HALF), lam.reshape(2, 1, HALF)]
    if rev:
        in_specs += [pl.BlockSpec((r, HALF), lambda b, s: (blk(b, s), 0)),
                     pl.BlockSpec((r, HALF), lambda b, s: (blk(b, s), xcol - 1))]
        args += [hf, p]
    return pl.pallas_call(
        functools.partial(_lru_kernel, rev=rev),
        grid=(B, N_SEQ_BLK),
        in_specs=in_specs,
        out_specs=pl.BlockSpec((r, HALF), lambda b, s: (blk(b, s), 0)),
        out_shape=jax.ShapeDtypeStruct((T_ALL, HALF), BF16 if rev else F32),
        scratch_shapes=[pltpu.VMEM((r + 16, HALF), F32), pltpu.VMEM((r, HALF), F32),
                        pltpu.VMEM((r, HALF), F32), pltpu.VMEM((r, HALF), F32),
                        pltpu.VMEM((1, HALF), F32)],
        compiler_params=_cparams(("parallel", "arbitrary"), VMEM_LIMIT),
        name="lru_bwd" if rev else "lru_fwd",
    )(*args)


def kernel(x, c, ctx, c_ctx, mod_w, mod_b, norm_g, ab_w_in, ab_w_out, gmlp_v_g, gmlp_ws, gmlp_bs, cd_w_in, cd_w_out, hgrn_lb, hgrn_onorm_g, lru_conv_w, lru_conv_b, lru_wr, lru_br, lru_wi, lru_bi, lru_lambda, router_w, router_b, exp_w_gu, exp_b_gu, exp_w_down, exp_b_down):
    cond8 = jnp.concatenate([c, c_ctx[None, :], jnp.zeros((5, D), F32)], axis=0)
    modm = _modulation(cond8, mod_w, mod_b).reshape(DEPTH * 8, 1, 6 * D)

    lb_soft = jax.nn.softmax(hgrn_lb.astype(F32), axis=0)
    lb_all = jnp.cumsum(lb_soft, axis=0) - lb_soft[0]

    p0, xs = _inproj((x.reshape(T_LAT, D), ctx.reshape(T_CTX, D)), norm_g[0, 0], modm, 0,
                     [ab_w_in[0].astype(BF16)], BF16)
    zc, zs = _fnet_chan(p0)
    mix_a = jnp.concatenate([_fnet_pos(zc, zs, 0, L, 512, 1024),
                             _fnet_pos(zc, zs, T_LAT, LC, LC, LC)], axis=0)
    mix_g = _gmlp(p0, gmlp_v_g[0], gmlp_ws[0], gmlp_bs[0])
    xs, h2, e4, p4, r4, counts = _outproj(mix_a, mix_g, ab_w_out[0].astype(BF16), xs, T_ALL, norm_g[0, 1],
                                          norm_g[0, 2], modm, 0, router_w[0], router_b[0])
    xs = _moe(h2, e4, p4, r4, counts, xs, norm_g[0, 3], modm, 0,
              exp_w_gu, exp_b_gu[0], exp_w_down, exp_b_down[0])

    w_cd = [cd_w_in[0, :, :4 * HALF].astype(BF16), cd_w_in[0, :, 4 * HALF:].astype(BF16)]
    p1a, p1b = _inproj(xs, norm_g[1, 0], modm, 1, w_cd, F32)
    o_f = _hgrn(p1a, p1b, lb_all[1], False)
    hg = _hgrn(p1a, p1b, lb_all[1], True, o_f, hgrn_onorm_g[0])
    lru_args = (lru_conv_w[0], lru_conv_b[0], lru_wr[0], lru_br[0], lru_wi[0], lru_bi[0], lru_lambda[0])
    h_f = _lru(p1b, *lru_args, False)
    lr = _lru(p1b, *lru_args, True, h_f)
    xl, h2, e4, p4, r4, counts = _outproj(hg, lr, cd_w_out[0].astype(BF16), xs, T_LAT, norm_g[1, 1],
                                          norm_g[1, 2], modm, 1, router_w[1], router_b[1])
    xl = _moe(h2, e4, p4, r4, counts, xl, norm_g[1, 3], modm, 1,
              exp_w_gu, exp_b_gu[1], exp_w_down, exp_b_down[1])
    return xl.reshape(B, L, D)
```

```python
import functools

import numpy as np
import jax
import jax.numpy as jnp
from jax import lax
from jax.experimental import pallas as pl
from jax.experimental.pallas import tpu as pltpu

F32 = jnp.float32
BF16 = jnp.bfloat16
HI = lax.Precision.HIGHEST

D = 2048
B = 2
L = 4096
LC = 256
T_LAT = B * L
T_CTX = B * LC
T_ALL = T_LAT + T_CTX
DEPTH = 2
EPS = 1e-6

HALF = D // 2
N_EXPERTS = 32
TOP_K = 4
D_EXPERT = D // 2
SWIGLU_LIMIT = 7.0
SWIGLU_ALPHA = 1.702
MOE_TM = 256

FNET_GROUPS = 4
FNET_GD = HALF // FNET_GROUPS
GMLP_HEADS = 8
GMLP_CHUNK = 128
HGRN_HEADS = 8
HGRN_R = 256
HGRN_C = 64
LRU_HEADS = 8
LRU_R = HGRN_R
LRU_C = 8.0

VMEM_LIMIT = 56 * 1024 * 1024


def _cparams(sem, vmem=None):
    return pltpu.CompilerParams(dimension_semantics=sem, vmem_limit_bytes=vmem)


def _mod_row(i, tm):
    return jnp.where(i < L // tm, 0, jnp.where(i < 2 * L // tm, 1, 2))


def _rms(x):
    return x * lax.rsqrt(jnp.mean(x * x, axis=-1, keepdims=True) + EPS)


def _bdot(a, b):
    return jnp.dot(a.astype(BF16), b.astype(BF16), preferred_element_type=F32)


ROW_S = D // 256
U32 = jnp.uint32
HI_MASK = 0xFFFF0000


def _row_slab(ref, r):
    return ref.at[pl.ds(pl.multiple_of(r * ROW_S, ROW_S), ROW_S), :]


def _pack_pieces(hi, lo):
    hb = pltpu.bitcast(hi.astype(BF16).astype(F32), U32)
    lb = pltpu.bitcast(lo.astype(BF16).astype(F32), U32)
    return jnp.bitwise_or(jnp.bitwise_and(hb, U32(HI_MASK)), jnp.right_shift(lb, U32(16)))


def _unpack_piece(w):
    hi = pltpu.bitcast(jnp.bitwise_and(w, U32(HI_MASK)), F32)
    lo = pltpu.bitcast(jnp.left_shift(w, U32(16)), F32)
    return hi, lo


def _store_rows(ref, val, row0=0):
    n = val.shape[0]
    for s in range(ROW_S):
        ref[pl.ds(row0 * ROW_S + s, n, stride=ROW_S), :] = _pack_pieces(
            val[:, s * 128:(s + 1) * 128], val[:, (s + ROW_S) * 128:(s + ROW_S + 1) * 128])


def _mod_kernel(c_ref, w_ref, b_ref, o_ref):
    cnd = c_ref[...]
    s = cnd * jax.nn.sigmoid(cnd)
    o_ref[...] = _bdot(s, w_ref[...]) + b_ref[...]


def _modulation(cond8, mod_w, mod_b):
    tn = 1024
    n = mod_w.shape[-1]
    return pl.pallas_call(
        _mod_kernel,
        grid=(DEPTH, n // tn),
        in_specs=[
            pl.BlockSpec((8, D), lambda l, j: (0, 0)),
            pl.BlockSpec((None, D, tn), lambda l, j: (l, 0, j)),
            pl.BlockSpec((None, 1, tn), lambda l, j: (l, 0, j)),
        ],
        out_specs=pl.BlockSpec((None, 8, tn), lambda l, j: (l, 0, j)),
        out_shape=jax.ShapeDtypeStruct((DEPTH, 8, n), F32),
        compiler_params=_cparams(("parallel", "parallel"), VMEM_LIMIT),
        name="modulation",
    )(cond8, mod_w, mod_b.reshape(DEPTH, 1, n))


def _inproj_kernel(*refs, nw, n_lat_tiles):
    if n_lat_tiles is None:
        x_ref, g_ref, sh_ref, sc_ref = refs[:4]
        w_refs, o_refs = refs[4:4 + nw], refs[4 + nw:]
        x = x_ref[...]
    else:
        xl_ref, xc_ref, g_ref, sh_ref, sc_ref = refs[:5]
        w_refs, o_refs, xs_ref = refs[5:5 + nw], refs[5 + nw:5 + 2 * nw], refs[5 + 2 * nw]
        i = pl.program_id(0)

        @pl.when(i < n_lat_tiles)
        def _():
            xs_ref[...] = xl_ref[...]

        @pl.when(i >= n_lat_tiles)
        def _():
            xs_ref[...] = xc_ref[...]

        x = xs_ref[...]
    y = _rms(x) * g_ref[...]
    h = (y * (1.0 + sc_ref[...]) + sh_ref[...]).astype(BF16)
    for w_ref, o_ref in zip(w_refs, o_refs):
        o_ref[...] = jnp.dot(h, w_ref[...], preferred_element_type=F32).astype(o_ref.dtype)


def _inproj(src, g, modm, layer, ws_bf, out_dtype):
    tm = 256
    row = lambda i: layer * 8 + _mod_row(i, tm)
    pair = isinstance(src, tuple)
    if pair:
        n_lat_tiles = src[0].shape[0] // tm
        t = src[0].shape[0] + src[1].shape[0]
        x_specs = [pl.BlockSpec((tm, D), lambda i: (jnp.minimum(i, n_lat_tiles - 1), 0)),
                   pl.BlockSpec((tm, D), lambda i: (jnp.maximum(i - n_lat_tiles, 0), 0))]
        x_args = list(src)
    else:
        n_lat_tiles = None
        t = src.shape[0]
        x_specs = [pl.BlockSpec((tm, D), lambda i: (i, 0))]
        x_args = [src]
    out_specs = [pl.BlockSpec((tm, w.shape[1]), lambda i: (i, 0)) for w in ws_bf]
    out_shape = [jax.ShapeDtypeStruct((t, w.shape[1]), out_dtype) for w in ws_bf]
    if pair:
        out_specs.append(pl.BlockSpec((tm, D), lambda i: (i, 0)))
        out_shape.append(jax.ShapeDtypeStruct((t, D), F32))
    return pl.pallas_call(
        functools.partial(_inproj_kernel, nw=len(ws_bf), n_lat_tiles=n_lat_tiles),
        grid=(t // tm,),
        in_specs=x_specs + [
            pl.BlockSpec((1, D), lambda i: (0, 0)),
            pl.BlockSpec((None, 1, D), lambda i: (row(i), 0, 0)),
            pl.BlockSpec((None, 1, D), lambda i: (row(i), 0, 1)),
        ] + [pl.BlockSpec(w.shape, lambda i: (0, 0), pipeline_mode=pl.Buffered(1)) for w in ws_bf],
        out_specs=out_specs,
        out_shape=out_shape,
        compiler_params=_cparams(("parallel",), VMEM_LIMIT),
        name="inproj",
    )(*x_args, g.reshape(1, D), modm, modm, *ws_bf)


def _fnet_chan_kernel(p_ref, cs_ref, zc_ref, zs_ref):
    for g in range(FNET_GROUPS):
        sl = slice(g * FNET_GD, (g + 1) * FNET_GD)
        z = jnp.dot(p_ref[:, sl].astype(BF16), cs_ref[...], preferred_element_type=F32)
        zc_ref[:, sl] = z[:, :FNET_GD].astype(BF16)
        zs_ref[:, sl] = z[:, FNET_GD:].astype(BF16)


def _fnet_chan(p):
    tm = 512
    t = p.shape[0]
    k = np.arange(FNET_GD)
    ang = 2.0 * np.pi * ((k[:, None] * k[None, :]) % FNET_GD) / FNET_GD
    cs = np.concatenate([np.cos(ang), np.sin(ang)], axis=1) / np.sqrt(FNET_GD)
    cs = jnp.asarray(cs, F32).astype(BF16)
    return pl.pallas_call(
        _fnet_chan_kernel,
        grid=(t // tm,),
        in_specs=[
            pl.BlockSpec((tm, HALF), lambda i: (i, 0)),
            pl.BlockSpec((FNET_GD, 2 * FNET_GD), lambda i: (0, 0)),
        ],
        out_specs=[pl.BlockSpec((tm, HALF), lambda i: (i, 0))] * 2,
        out_shape=[jax.ShapeDtypeStruct((t, HALF), BF16)] * 2,
        compiler_params=_cparams(("parallel",), VMEM_LIMIT),
        name="fnet_chan",
    )(p, cs)


def _fnet_pos_kernel(bre_ref, bim_ref, cre_ref, cim_ref, rre_ref, rim_ref, zc_ref, zs_ref, o_ref, acc):
    nt = pl.program_id(2)
    br, bi = bre_ref[...], bim_ref[...]
    cr, ci = cre_ref[...], cim_ref[...]
    rr, ri = rre_ref[...], rim_ref[...]
    tr = br * cr - bi * ci
    ti = br * ci + bi * cr
    er = (tr * rr - ti * ri).astype(BF16)
    ei = (tr * ri + ti * rr).astype(BF16)
    part = (jnp.dot(er, zc_ref[...], preferred_element_type=F32)
            + jnp.dot(ei, zs_ref[...], preferred_element_type=F32))

    @pl.when(nt == 0)
    def _():
        acc[...] = part

    @pl.when(nt > 0)
    def _():
        acc[...] += part

    @pl.when(nt == pl.num_programs(2) - 1)
    def _():
        o_ref[...] = acc[...].astype(BF16)


def _fnet_pos(zc, zs, row0, seq, tk, tn):
    nkt, nnt = seq // tk, seq // tn
    th = 2.0 * np.pi / seq
    kk = np.arange(tk)[:, None]
    nn = np.arange(tn)[None, :]
    base = th * ((kk * nn) % seq)
    n0 = (np.arange(nnt) * tn)[:, None, None]
    col = th * ((np.arange(tk)[None, :, None] * n0) % seq)
    k0 = (np.arange(nkt) * tk)[:, None, None, None]
    nfull = (np.arange(nnt) * tn)[None, :, None, None] + np.arange(tn)[None, None, None, :]
    row = th * ((k0 * nfull) % seq)
    scale = 1.0 / np.sqrt(seq)
    f = lambda a: jnp.asarray(a, F32)
    tabs = (f(np.cos(base)), f(-np.sin(base)), f(np.cos(col)), f(-np.sin(col)),
            f(np.cos(row) * scale), f(-np.sin(row) * scale))
    rb = row0 // tn
    ob = row0 // tk
    return pl.pallas_call(
        _fnet_pos_kernel,
        grid=(B, nkt, nnt),
        in_specs=[
            pl.BlockSpec((tk, tn), lambda b, k, n: (0, 0)),
            pl.BlockSpec((tk, tn), lambda b, k, n: (0, 0)),
            pl.BlockSpec((None, tk, 1), lambda b, k, n: (n, 0, 0)),
            pl.BlockSpec((None, tk, 1), lambda b, k, n: (n, 0, 0)),
            pl.BlockSpec((None, None, 1, tn), lambda b, k, n: (k, n, 0, 0)),
            pl.BlockSpec((None, None, 1, tn), lambda b, k, n: (k, n, 0, 0)),
            pl.BlockSpec((tn, HALF), lambda b, k, n: (rb + b * nnt + n, 0)),
            pl.BlockSpec((tn, HALF), lambda b, k, n: (rb + b * nnt + n, 0)),
        ],
        out_specs=pl.BlockSpec((tk, HALF), lambda b, k, n: (b * nkt + k, 0)),
        out_shape=jax.ShapeDtypeStruct((B * seq, HALF), BF16),
        scratch_shapes=[pltpu.VMEM((tk, HALF), F32)],
        compiler_params=_cparams(("parallel", "parallel", "arbitrary"), VMEM_LIMIT),
        name=f"fnet_pos_{seq}",
    )(*tabs, zc, zs)


def _gmlp_kernel(u_ref, v_ref, vg_ref, ws_ref, bst_ref, o_ref, *, tm):
    zu = jax.nn.gelu(u_ref[...].astype(F32))
    zv = jax.nn.gelu(v_ref[...].astype(F32))
    mu = jnp.mean(zv, axis=-1, keepdims=True)
    dv = zv - mu
    var = jnp.mean(dv * dv, axis=-1, keepdims=True)
    vn = dv * lax.rsqrt(var + EPS) * vg_ref[...]
    hd = HALF // GMLP_HEADS
    for c in range(tm // GMLP_CHUNK):
        rs = slice(c * GMLP_CHUNK, (c + 1) * GMLP_CHUNK)
        for h in range(GMLP_HEADS):
            cs = slice(h * hd, (h + 1) * hd)
            s = _bdot(ws_ref[h], vn[rs, cs])
            s = s + bst_ref[:, h:h + 1]
            o_ref[rs, cs] = (zu[rs, cs] * s).astype(BF16)


def _gmlp(p, v_g, w_s, b_s):
    tm = 256
    t = p.shape[0]
    return pl.pallas_call(
        functools.partial(_gmlp_kernel, tm=tm),
        grid=(t // tm,),
        in_specs=[
            pl.BlockSpec((tm, HALF), lambda i: (i, 1)),
            pl.BlockSpec((tm, HALF), lambda i: (i, 2)),
            pl.BlockSpec((1, HALF), lambda i: (0, 0)),
            pl.BlockSpec((GMLP_HEADS, GMLP_CHUNK, GMLP_CHUNK), lambda i: (0, 0, 0)),
            pl.BlockSpec((GMLP_CHUNK, GMLP_HEADS), lambda i: (0, 0)),
        ],
        out_specs=pl.BlockSpec((tm, HALF), lambda i: (i, 0)),
        out_shape=jax.ShapeDtypeStruct((t, HALF), BF16),
        compiler_params=_cparams(("parallel",), VMEM_LIMIT),
        name="gmlp",
    )(p, p, v_g.reshape(1, HALF), w_s, b_s.T)


def _outproj_kernel(ma_ref, mb_ref, w_ref, x_ref, g1_ref, g2_ref, ga1_ref, sh2_ref, sc2_ref,
                    rw_ref, rb_ref, xo_ref, h2_ref, e_ref, p_ref, rk_ref, cnt_ref, carry, rw2, m_s, *, tm):
    @pl.when(pl.program_id(0) == 0)
    def _():
        carry[...] = jnp.zeros_like(carry)
        rw = rw_ref[...]
        rw_hi = rw.astype(BF16)
        rw2[:, :N_EXPERTS] = rw_hi
        rw2[:, N_EXPERTS:] = (rw - rw_hi.astype(F32)).astype(BF16)

    sub = 256
    iota = lax.broadcasted_iota(jnp.int32, (sub, N_EXPERTS), 1).astype(F32)
    lane4 = lax.broadcasted_iota(jnp.int32, (sub, TOP_K), 1)
    r_i = lax.broadcasted_iota(jnp.int32, (sub, sub), 0)
    c_i = lax.broadcasted_iota(jnp.int32, (sub, sub), 1)
    tri = (c_i < r_i).astype(BF16)
    running = carry[...]
    for r0 in range(0, tm, sub):
        rs = slice(r0, r0 + sub)
        m_s[rs, :] = (jnp.dot(ma_ref[rs, :], w_ref[:HALF, :], preferred_element_type=F32)
                      + jnp.dot(mb_ref[rs, :], w_ref[HALF:, :], preferred_element_type=F32))
    for r0 in range(0, tm, sub):
        rs = slice(r0, r0 + sub)
        m = m_s[rs, :]
        xn = x_ref[rs, :] + ga1_ref[...] * (_rms(m) * g1_ref[...])
        xo_ref[rs, :] = xn
        h2 = (_rms(xn) * g2_ref[...]) * (1.0 + sc2_ref[...]) + sh2_ref[...]
        _store_rows(h2_ref, h2, r0)

        h_hi = h2.astype(BF16)
        h_lo = (h2 - h_hi.astype(F32)).astype(BF16)
        t1 = jnp.dot(h_hi, rw2[...], preferred_element_type=F32)
        t2 = jnp.dot(h_lo, rw2[:, :N_EXPERTS], preferred_element_type=F32)
        lg = t1[:, :N_EXPERTS] + t1[:, N_EXPERTS:] + t2 + rb_ref[...]
        idxs, vals = [], []
        for _ in range(TOP_K):
            mx = jnp.max(lg, axis=-1, keepdims=True)
            ix = jnp.min(jnp.where(lg == mx, iota, float(N_EXPERTS)), axis=-1, keepdims=True)
            idxs.append(ix)
            vals.append(mx)
            lg = jnp.where(iota == ix, -jnp.inf, lg)
        exs = [jnp.exp(v - vals[0]) for v in vals]
        den = exs[0] + exs[1] + exs[2] + exs[3]

        onehot = [(iota == ix) for ix in idxs]
        cnt = jnp.zeros((sub, N_EXPERTS), F32)
        for oh in onehot:
            cnt = cnt + oh.astype(F32)
        prefix = jnp.dot(tri, cnt.astype(BF16), preferred_element_type=F32) + running

        e4 = jnp.zeros((sub, TOP_K), jnp.int32)
        p4 = jnp.zeros((sub, TOP_K), F32)
        r4 = jnp.zeros((sub, TOP_K), jnp.int32)
        for k in range(TOP_K):
            rk = jnp.sum(jnp.where(onehot[k], prefix, 0.0), axis=-1, keepdims=True)
            e4 = jnp.where(lane4 == k, idxs[k].astype(jnp.int32), e4)
            p4 = jnp.where(lane4 == k, exs[k] / den, p4)
            r4 = jnp.where(lane4 == k, rk.astype(jnp.int32), r4)
        e_ref[rs, :] = e4
        p_ref[rs, :] = p4
        rk_ref[rs, :] = r4
        running = running + jnp.sum(cnt, axis=0, keepdims=True)
    carry[...] = running
    cnt_ref[...] = running


def _outproj(ma, mb, w_bf, xs, t, g1, g2, modm, layer, rw, rb):
    tm = 512
    row = lambda i: layer * 8 + _mod_row(i, tm)
    modspec = lambda c: pl.BlockSpec((None, 1, D), lambda i: (row(i), 0, c))
    vec = pl.BlockSpec((1, D), lambda i: (0, 0))
    small = lambda dt: jax.ShapeDtypeStruct((t, TOP_K), dt)
    return pl.pallas_call(
        functools.partial(_outproj_kernel, tm=tm),
        grid=(t // tm,),
        in_specs=[
            pl.BlockSpec((tm, HALF), lambda i: (i, 0)),
            pl.BlockSpec((tm, HALF), lambda i: (i, 0)),
            pl.BlockSpec((D, D), lambda i: (0, 0)),
            pl.BlockSpec((tm, D), lambda i: (i, 0)),
            vec, vec, modspec(2), modspec(3), modspec(4),
            pl.BlockSpec((D, N_EXPERTS), lambda i: (0, 0)),
            pl.BlockSpec((1, N_EXPERTS), lambda i: (0, 0)),
        ],
        out_specs=[
            pl.BlockSpec((tm, D), lambda i: (i, 0)),
            pl.BlockSpec((tm * ROW_S, 128), lambda i: (i, 0)),
            pl.BlockSpec((tm, TOP_K), lambda i: (i, 0)),
            pl.BlockSpec((tm, TOP_K), lambda i: (i, 0)),
            pl.BlockSpec((tm, TOP_K), lambda i: (i, 0)),
            pl.BlockSpec((1, N_EXPERTS), lambda i: (0, 0)),
        ],
        out_shape=[
            jax.ShapeDtypeStruct((t, D), F32),
            jax.ShapeDtypeStruct((t * ROW_S, 128), U32),
            small(jnp.int32), small(F32), small(jnp.int32),
            jax.ShapeDtypeStruct((1, N_EXPERTS), F32),
        ],
        scratch_shapes=[pltpu.VMEM((1, N_EXPERTS), F32), pltpu.VMEM((D, 2 * N_EXPERTS), BF16),
                        pltpu.VMEM((tm, D), F32)],
        compiler_params=_cparams(("arbitrary",), VMEM_LIMIT),
        name="outproj_router",
    )(ma, mb, w_bf, xs, g1.reshape(1, D), g2.reshape(1, D), modm, modm, modm,
      rw, rb.reshape(1, N_EXPERTS))


def _invmap_kernel(pstart_ref, cnt_ref, padded_ref, dest_ref, inv_ref, *, n_pairs, n_slots):
    def unused(lo, hi):
        def body(s, c):
            inv_ref[s] = n_pairs + jnp.bitwise_and(s, 2 * MOE_TM - 1)
            return c
        lax.fori_loop(lo, hi, body, 0)

    def per_expert(e, c):
        unused(pstart_ref[e] + cnt_ref[e], pstart_ref[e] + padded_ref[e])
        return c

    lax.fori_loop(0, N_EXPERTS, per_expert, 0)
    unused(pstart_ref[N_EXPERTS - 1] + padded_ref[N_EXPERTS - 1], n_slots)

    def body(i, c):
        inv_ref[dest_ref[i]] = i
        return c

    lax.fori_loop(0, n_pairs, body, 0, unroll=16)


def _invmap(dest, pstart, cnt, padded, n_slots):
    n_pairs = dest.shape[0]
    return pl.pallas_call(
        functools.partial(_invmap_kernel, n_pairs=n_pairs, n_slots=n_slots),
        grid_spec=pltpu.PrefetchScalarGridSpec(
            num_scalar_prefetch=3,
            grid=(1,),
            in_specs=[pl.BlockSpec(memory_space=pltpu.SMEM)],
            out_specs=pl.BlockSpec(memory_space=pltpu.SMEM),
        ),
        out_shape=jax.ShapeDtypeStruct((n_slots,), jnp.int32),
        compiler_params=_cparams(("arbitrary",)),
        name="moe_invmap",
    )(pstart, cnt, padded, dest)


W_CH = 256
W_NCH_GU = D // W_CH
W_NCH = W_NCH_GU + D_EXPERT // W_CH
W_STAGE = 4


def _expert_kernel(be_ref, nused_ref, first_ref, nxt_ref, lo_ref, hi_ref, slot_ref,
                   inv_cur, inv_next, inv_next2, inv_prev, bgu_ref, bd_ref, h_hbm, wgu_all, wd_all, y_hbm,
                   xbuf, obuf, x_s, act_s, w_buf, stage, gsem, ssem, sem, *, layer, n_tok):
    i = pl.program_id(0)
    n_live = nused_ref[0]
    s_cur = lax.rem(i, 3)
    s_p1 = lax.rem(i + 1, 3)
    s_p2 = lax.rem(i + 2, 3)
    wgu_hbm = wgu_all.at[layer]
    wd_hbm = wd_all.at[layer]

    def gather_copy(inv_ref, r, slot):
        tok = jnp.minimum(lax.shift_right_logical(inv_ref[r], 2), n_tok - 1)
        return pltpu.make_async_copy(_row_slab(h_hbm, tok), _row_slab(xbuf.at[slot], r), gsem.at[slot])

    n_pairs = n_tok * TOP_K

    def scatter_copy(inv_ref, r, slot, first_step=False):
        dst = inv_ref[r]
        if first_step:
            dst = jnp.where(i == 0, n_pairs + 2 * MOE_TM + r, dst)
        return pltpu.make_async_copy(_row_slab(obuf.at[slot], r), _row_slab(y_hbm, dst), ssem.at[slot])

    def looped(fn):
        def body(r, c):
            fn(r)
            return c
        lax.fori_loop(0, MOE_TM, body, 0)

    def start_chunk(e, c):
        s = lax.rem(c, W_STAGE)

        @pl.when(c < W_NCH_GU)
        def _():
            r0 = pl.multiple_of(c * W_CH, W_CH)
            pltpu.make_async_copy(wgu_hbm.at[e, pl.ds(r0, W_CH), :], stage.at[s], sem.at[s]).start()

        @pl.when(c >= W_NCH_GU)
        def _():
            r0 = pl.multiple_of((c - W_NCH_GU) * W_CH, W_CH)
            pltpu.make_async_copy(wd_hbm.at[e, pl.ds(r0, W_CH), :], stage.at[s], sem.at[s]).start()

    def finish_chunk(c, dst):
        s = lax.rem(c, W_STAGE)
        pltpu.make_async_copy(wgu_hbm.at[0, pl.ds(0, W_CH), :], stage.at[s], sem.at[s]).wait()
        w_buf[dst, pl.ds(pl.multiple_of(c * W_CH, W_CH), W_CH), :] = stage[s].astype(BF16)

    def stream(e, dst, c_lo, c_hi):
        def body(c, carry):
            finish_chunk(c, dst)

            @pl.when(c + W_STAGE < W_NCH)
            def _():
                start_chunk(e, c + W_STAGE)
            return carry
        lax.fori_loop(c_lo, c_hi, body, 0)

    def prime(e):
        for c in range(W_STAGE):
            start_chunk(e, c)

    @pl.when(i == 0)
    def _():
        looped(lambda r: gather_copy(inv_cur, r, 0).start())
        looped(lambda r: gather_copy(inv_next, r, 1).start())
        obuf[2] = jnp.zeros(obuf.shape[1:], U32)
        for region in range(2):
            dump = y_hbm.at[pl.ds((n_pairs + region * MOE_TM) * ROW_S, MOE_TM * ROW_S), :]
            zero_dump = pltpu.make_async_copy(obuf.at[2], dump, ssem.at[2])
            zero_dump.start()
            zero_dump.wait()
        prime(be_ref[0])
        stream(be_ref[0], slot_ref[0], 0, W_NCH)

    @pl.when(i < n_live)
    def _():
        e_next = nxt_ref[i]
        cur = slot_ref[i]

        @pl.when((first_ref[i] == 1) & (e_next >= 0))
        def _():
            prime(e_next)

        def block_body(c_cur, c_p1, c_p2):
            for r in range(MOE_TM):
                gather_copy(inv_cur, r, c_cur).wait()
            for s in range(ROW_S):
                hi, lo = _unpack_piece(xbuf[c_cur, pl.ds(s, MOE_TM, stride=ROW_S), :])
                x_s[:, s * 128:(s + 1) * 128] = hi.astype(BF16)
                x_s[:, (s + ROW_S) * 128:(s + ROW_S + 1) * 128] = lo.astype(BF16)
            nc = 256

            def gate_up(c_lo, c_hi):
                for c0 in range(c_lo, c_hi, nc):
                    g = (jnp.dot(x_s[...], w_buf[cur, :D, c0:c0 + nc], preferred_element_type=F32)
                         + bgu_ref[:, c0:c0 + nc])
                    u = (jnp.dot(x_s[...], w_buf[cur, :D, D_EXPERT + c0:D_EXPERT + c0 + nc],
                                 preferred_element_type=F32) + bgu_ref[:, D_EXPERT + c0:D_EXPERT + c0 + nc])
                    gate = jnp.minimum(g, SWIGLU_LIMIT)
                    up = jnp.clip(u, -SWIGLU_LIMIT, SWIGLU_LIMIT)
                    act_s[:, c0:c0 + nc] = (gate * jax.nn.sigmoid(SWIGLU_ALPHA * gate)
                                            * (up + 1.0)).astype(BF16)

            for r in range(MOE_TM):
                scatter_copy(inv_prev, r, c_p2, first_step=True).start()
            gate_up(0, D_EXPERT // 2)

            @pl.when(n_live > 0)
            def _():
                for r in range(MOE_TM // 2):
                    gather_copy(inv_next2, r, c_p2).start()
                gate_up(D_EXPERT // 2, D_EXPERT)

            @pl.when(first_ref[i] >= 0)
            def _():
                for r in range(MOE_TM // 2, MOE_TM):
                    gather_copy(inv_next2, r, c_p2).start()
                for c0 in range(0, D // 2, nc):
                    y_hi = (jnp.dot(act_s[...], w_buf[cur, D:, c0:c0 + nc], preferred_element_type=F32)
                            + bd_ref[:, c0:c0 + nc])
                    y_lo = (jnp.dot(act_s[...], w_buf[cur, D:, D // 2 + c0:D // 2 + c0 + nc],
                                    preferred_element_type=F32) + bd_ref[:, D // 2 + c0:D // 2 + c0 + nc])
                    for j in range(nc // 128):
                        s = c0 // 128 + j
                        obuf[c_cur, pl.ds(s, MOE_TM, stride=ROW_S), :] = _pack_pieces(
                            y_hi[:, j * 128:(j + 1) * 128], y_lo[:, j * 128:(j + 1) * 128])

            @pl.when(i >= 1)
            def _():
                for r in range(MOE_TM):
                    scatter_copy(inv_prev, r, c_p1).wait()

        for ring in range(3):
            @pl.when(s_cur == ring)
            def _():
                block_body(ring, (ring + 1) % 3, (ring + 2) % 3)

        @pl.when(e_next >= 0)
        def _():
            stream(e_next, 1 - cur, lo_ref[i], hi_ref[i])

        @pl.when(i == n_live - 1)
        def _():
            looped(lambda r: scatter_copy(inv_prev, r, s_p2).wait())
            looped(lambda r: scatter_copy(inv_cur, r, s_cur).start())
            looped(lambda r: scatter_copy(inv_cur, r, s_cur).wait())
            looped(lambda r: gather_copy(inv_next, r, s_p1).wait())
            looped(lambda r: gather_copy(inv_next2, r, s_p2).wait())


def _take(table, idx):
    ids = jnp.arange(table.shape[0], dtype=jnp.int32)
    return jnp.sum(jnp.where(idx[..., None] == ids, table, 0), axis=-1)


def _experts(h2, inv, block_e, nused, pstart, padded, layer, wgu, bgu, wd, bd, nblk, n_tok):
    nb_e = padded // MOE_TM
    pos = jnp.arange(nblk, dtype=jnp.int32) - _take(pstart // MOE_TM, block_e)
    nb = jnp.maximum(_take(nb_e, block_e), 1)
    lo = (pos * W_NCH) // nb
    hi = ((pos + 1) * W_NCH) // nb
    first = (pos == 0).astype(jnp.int32)
    eid = jnp.arange(N_EXPERTS, dtype=jnp.int32)
    later = jnp.where(nb_e > 0, eid, N_EXPERTS)
    nxt_e = jnp.concatenate([lax.cummin(later, reverse=True)[1:], jnp.full((1,), N_EXPERTS, jnp.int32)])
    nxt_e = jnp.where(nxt_e >= N_EXPERTS, -1, nxt_e)
    slot_e = (jnp.cumsum((nb_e > 0).astype(jnp.int32)) - 1) % 2
    i32 = lambda a: a.astype(jnp.int32)
    live = lambda i, nu: jnp.clip(i, 0, nu[0] - 1)
    sblk = lambda off: pl.BlockSpec((MOE_TM,), lambda i, be, nu, *_: (live(i + off, nu),),
                                    memory_space=pltpu.SMEM)
    bias = lambda n: pl.BlockSpec((None, 1, n), lambda i, be, nu, *_: (be[live(i, nu)], 0, 0))
    hbm = pl.BlockSpec(memory_space=pl.ANY)
    n_slabs = n_tok * TOP_K + 3 * MOE_TM
    return pl.pallas_call(
        functools.partial(_expert_kernel, layer=layer, n_tok=n_tok),
        grid_spec=pltpu.PrefetchScalarGridSpec(
            num_scalar_prefetch=7,
            grid=(nblk,),
            in_specs=[sblk(0), sblk(1), sblk(2), sblk(-1), bias(2 * D_EXPERT), bias(D), hbm, hbm, hbm],
            out_specs=hbm,
            scratch_shapes=[
                pltpu.VMEM((3, MOE_TM * ROW_S, 128), U32),
                pltpu.VMEM((3, MOE_TM * ROW_S, 128), U32),
                pltpu.VMEM((MOE_TM, D), BF16),
                pltpu.VMEM((MOE_TM, D_EXPERT), BF16),
                pltpu.VMEM((2, D + D_EXPERT, D), BF16),
                pltpu.VMEM((W_STAGE, W_CH, D), F32),
                pltpu.SemaphoreType.DMA((3,)),
                pltpu.SemaphoreType.DMA((3,)),
                pltpu.SemaphoreType.DMA((W_STAGE,)),
            ],
        ),
        out_shape=jax.ShapeDtypeStruct((n_slabs * ROW_S, 128), U32),
        compiler_params=_cparams(("arbitrary",), VMEM_LIMIT),
        name="moe_experts",
    )(block_e, nused, i32(first), i32(_take(nxt_e, block_e)), i32(lo), i32(hi), i32(_take(slot_e, block_e)),
      inv, inv, inv, inv, bgu.reshape(N_EXPERTS, 1, -1), bd.reshape(N_EXPERTS, 1, -1), h2, wgu, wd)


def _combine_kernel(p_ref, x_ref, g3_ref, ga2_ref, y_ref, xo_ref, acc_hi, acc_lo, *, tm):
    pair = TOP_K * ROW_S
    for r in range(tm):
        a_hi = a_lo = None
        for k in range(TOP_K):
            hi, lo = _unpack_piece(y_ref[pl.ds(r * pair + k * ROW_S, ROW_S), :])
            w = p_ref[r * TOP_K + k]
            a_hi = hi * w if k == 0 else a_hi + hi * w
            a_lo = lo * w if k == 0 else a_lo + lo * w
        acc_hi[pl.ds(r * ROW_S, ROW_S), :] = a_hi
        acc_lo[pl.ds(r * ROW_S, ROW_S), :] = a_lo
    y = jnp.concatenate([acc_hi[pl.ds(s, tm, stride=ROW_S), :] for s in range(ROW_S)]
                        + [acc_lo[pl.ds(s, tm, stride=ROW_S), :] for s in range(ROW_S)], axis=1)
    xo_ref[...] = x_ref[...] + ga2_ref[...] * (_rms(y) * g3_ref[...])


def _combine(y4, p4, xs, g3, modm, layer):
    tm = 128
    t = xs.shape[0]
    row = lambda i: layer * 8 + _mod_row(i, tm)
    return pl.pallas_call(
        functools.partial(_combine_kernel, tm=tm),
        grid=(t // tm,),
        in_specs=[
            pl.BlockSpec((tm * TOP_K,), lambda i: (i,), memory_space=pltpu.SMEM),
            pl.BlockSpec((tm, D), lambda i: (i, 0)),
            pl.BlockSpec((1, D), lambda i: (0, 0)),
            pl.BlockSpec((None, 1, D), lambda i: (row(i), 0, 5)),
            pl.BlockSpec((tm * TOP_K * ROW_S, 128), lambda i: (i, 0)),
        ],
        out_specs=pl.BlockSpec((tm, D), lambda i: (i, 0)),
        out_shape=jax.ShapeDtypeStruct((t, D), F32),
        scratch_shapes=[pltpu.VMEM((tm * ROW_S, 128), F32)] * 2,
        compiler_params=_cparams(("parallel",), VMEM_LIMIT),
        name="moe_combine",
    )(p4.reshape(-1), xs, g3.reshape(1, D), modm, y4)


def _moe(h2, e4, p4, r4, counts, xs, g3, modm, layer, wgu, bgu, wd, bd):
    t = e4.shape[0]
    nblk = -(-(t * TOP_K + N_EXPERTS * (MOE_TM - 1)) // MOE_TM)
    cnt = counts.reshape(N_EXPERTS).astype(jnp.int32)
    padded = (cnt + MOE_TM - 1) // MOE_TM * MOE_TM
    pend = jnp.cumsum(padded)
    pstart = pend - padded
    dest = (_take(pstart, e4) + r4).reshape(-1).astype(jnp.int32)
    nused = (pend[-1:] // MOE_TM).astype(jnp.int32)
    blk_start = jnp.arange(nblk, dtype=jnp.int32) * MOE_TM
    block_e = jnp.minimum(jnp.sum((pend[None, :] <= blk_start[:, None]).astype(jnp.int32), axis=1),
                          N_EXPERTS - 1).astype(jnp.int32)
    inv = _invmap(dest, pstart.astype(jnp.int32), cnt, padded.astype(jnp.int32), nblk * MOE_TM)
    y4 = _experts(h2, inv, block_e, nused, pstart, padded, layer, wgu, bgu, wd, bd, nblk, t)
    return _combine(y4, p4, xs, g3, modm, layer)


N_CTX_BLK = LC // HGRN_R
N_LAT_BLK = L // HGRN_R
N_SEQ_BLK = N_CTX_BLK + N_LAT_BLK


def _scan_block(b, s, rev):
    if rev:
        return jnp.where(s < N_CTX_BLK, T_LAT // HGRN_R + N_CTX_BLK * b + (N_CTX_BLK - 1 - s),
                         N_LAT_BLK * b + (N_SEQ_BLK - 1 - s))
    return jnp.where(s < N_CTX_BLK, T_LAT // HGRN_R + N_CTX_BLK * b + s, N_LAT_BLK * b + (s - N_CTX_BLK))


def _hgrn_kernel(*refs, rev):
    if rev:
        q_ref, f_ref, v_ref, lb_ref, of_ref, g_ref, on_ref, o_ref, st, qin_s, qd_s, ki_s, ke_s, v_s = refs
    else:
        q_ref, f_ref, v_ref, lb_ref, o_ref, st, qin_s, qd_s, ki_s, ke_s, v_s = refs

    @pl.when(pl.program_id(1) == 0)
    def _():
        st[...] = jnp.zeros_like(st)

    c = HGRN_C
    nc = HGRN_R // c
    r_i = lax.broadcasted_iota(jnp.int32, (c, c), 0)
    c_i = lax.broadcasted_iota(jnp.int32, (c, c), 1)
    mask = (c_i >= r_i) if rev else (c_i <= r_i)
    tri = mask.astype(F32)

    lb = lb_ref[...]
    decay = [None] * nc
    v_s[...] = v_ref[...].astype(BF16)
    for ci in range(nc):
        rs = slice(ci * c, (ci + 1) * c)
        sg = jax.nn.sigmoid(f_ref[rs, :])
        logf = jnp.log(lb + (1.0 - lb) * sg)
        kk = (1.0 - lb) * (1.0 - sg)
        q = q_ref[rs, :]
        qs = q * jax.nn.sigmoid(q)
        cum = jnp.dot(tri, logf, precision=HI, preferred_element_type=F32)
        total = cum[0:1] if rev else cum[c - 1:c]
        mid = cum[c // 2:c // 2 + 1]
        qin_s[rs, :] = (qs * jnp.exp(cum)).astype(BF16)
        qd_s[rs, :] = (qs * jnp.exp(cum - mid)).astype(BF16)
        ki_s[rs, :] = (kk * jnp.exp(mid - cum)).astype(BF16)
        ke_s[rs, :] = (kk * jnp.exp(total - cum)).astype(BF16)
        decay[ci] = jnp.exp(total)

    order = range(nc - 1, -1, -1) if rev else range(nc)
    for ci in order:
        rs = slice(ci * c, (ci + 1) * c)
        for h in range(HGRN_HEADS):
            hs = slice(h * 128, (h + 1) * 128)
            state = st[h]
            vb = v_s[rs, hs]
            sc = lax.dot_general(qd_s[rs, hs], ki_s[rs, hs], (((1,), (1,)), ((), ())),
                                 preferred_element_type=F32)
            sc = jnp.where(mask, sc, 0.0)
            intra = jnp.dot(sc.astype(BF16), vb, preferred_element_type=F32)
            inter = lax.dot_general(qin_s[rs, hs], state.astype(BF16), (((1,), (1,)), ((), ())),
                                    preferred_element_type=F32)
            st[h] = state * decay[ci][:, hs] + lax.dot_general(
                vb, ke_s[rs, hs], (((0,), (0,)), ((), ())), preferred_element_type=F32)
            o = intra + inter
            if rev:
                o = o + of_ref[rs, hs]
                g = g_ref[rs, hs]
                o_ref[rs, hs] = (_rms(o) * on_ref[:, hs] * (g * jax.nn.sigmoid(g))).astype(BF16)
            else:
                o_ref[rs, hs] = o


def _hgrn(pa, pb, lb, rev, of=None, onorm=None):
    r = HGRN_R
    blk = lambda b, s: _scan_block(b, s, rev)
    col = lambda c: pl.BlockSpec((r, HALF), lambda b, s: (blk(b, s), c))
    vec = pl.BlockSpec((1, HALF), lambda b, s: (0, 0))
    in_specs = [col(0), col(2 if rev else 1), col(3), vec]
    args = [pa, pa, pa, lb.reshape(1, HALF)]
    if rev:
        in_specs += [col(0), col(0), vec]
        args += [of, pb, onorm.reshape(1, HALF)]
    return pl.pallas_call(
        functools.partial(_hgrn_kernel, rev=rev),
        grid=(B, N_SEQ_BLK),
        in_specs=in_specs,
        out_specs=col(0),
        out_shape=jax.ShapeDtypeStruct((T_ALL, HALF), BF16 if rev else F32),
        scratch_shapes=[pltpu.VMEM((HGRN_HEADS, 128, 128), F32)] + [pltpu.VMEM((r, HALF), BF16)] * 5,
        compiler_params=_cparams(("parallel", "arbitrary"), VMEM_LIMIT),
        name="hgrn_bwd" if rev else "hgrn_fwd",
    )(*args)


def _lru_kernel(*refs, rev):
    if rev:
        (x_ref, xp_ref, xn_ref, cw_ref, cb_ref, wr_ref, br_ref, wi_ref, bi_ref, lam_ref,
         hf_ref, gt_ref, o_ref, xcat, a_s, u_s, h_s, carry) = refs
    else:
        (x_ref, xp_ref, xn_ref, cw_ref, cb_ref, wr_ref, br_ref, wi_ref, bi_ref, lam_ref,
         o_ref, xcat, a_s, u_s, h_s, carry) = refs
    s = pl.program_id(1)
    r = LRU_R

    @pl.when(s == 0)
    def _():
        carry[...] = jnp.zeros_like(carry)

    is_ctx = s < N_CTX_BLK
    if rev:
        j = jnp.where(is_ctx, N_CTX_BLK - 1 - s, N_SEQ_BLK - 1 - s)
    else:
        j = jnp.where(is_ctx, s, s - N_CTX_BLK)
    nb = jnp.where(is_ctx, N_CTX_BLK, N_LAT_BLK)
    xcat[0:8, :] = jnp.where(j == 0, 0.0, xp_ref[...])
    xcat[8:8 + r, :] = x_ref[...]
    xcat[8 + r:16 + r, :] = jnp.where(j == nb - 1, 0.0, xn_ref[...])
    xc = cb_ref[...] + xcat[pl.ds(6, r), :] * cw_ref[0:1, :]
    for t in range(1, 4):
        xc = xc + xcat[pl.ds(6 + t, r), :] * cw_ref[t:t + 1, :]

    lam = lam_ref[...]
    sp = jnp.maximum(-lam, 0.0) + jnp.log1p(jnp.exp(-jnp.abs(lam)))
    hd = HALF // LRU_HEADS
    for h in range(LRU_HEADS):
        cs = slice(h * hd, (h + 1) * hd)
        xh = xc[:, cs]
        rg = jax.nn.sigmoid(_bdot(xh, wr_ref[h]) + br_ref[:, cs])
        ig = jax.nn.sigmoid(_bdot(xh, wi_ref[h]) + bi_ref[:, cs])
        log_a = -LRU_C * rg * sp[:, cs]
        a = jnp.exp(log_a)
        a_s[:, cs] = a
        u_s[:, cs] = jnp.sqrt(-jnp.tanh(log_a) * (a * a + 1.0)) * (ig * xh)

    def step(t, h):
        tt = (r - 1 - t) if rev else t
        h = a_s[pl.ds(tt, 1), :] * h + u_s[pl.ds(tt, 1), :]
        h_s[pl.ds(tt, 1), :] = h
        return h

    carry[...] = lax.fori_loop(0, r, step, carry[...], unroll=8)
    if rev:
        o_ref[...] = ((h_s[...] + hf_ref[...]) * jax.nn.gelu(gt_ref[...])).astype(BF16)
    else:
        o_ref[...] = h_s[...]


def _lru(p, conv_w, conv_b, w_r, b_r, w_i, b_i, lam, rev, hf=None):
    r = LRU_R
    d = 1 if rev else 0
    blk = lambda b, s: _scan_block(b, s, rev)
    xcol = 2
    n8 = T_ALL // 8
    vec = lambda: pl.BlockSpec((None, 1, HALF), lambda b, s: (d, 0, 0))
    wspec = lambda: pl.BlockSpec((None, LRU_HEADS, 128, 128), lambda b, s: (d, 0, 0, 0))
    in_specs = [
        pl.BlockSpec((r, HALF), lambda b, s: (blk(b, s), xcol)),
        pl.BlockSpec((8, HALF), lambda b, s: (jnp.maximum(blk(b, s) * (r // 8) - 1, 0), xcol)),
        pl.BlockSpec((8, HALF), lambda b, s: (jnp.minimum((blk(b, s) + 1) * (r // 8), n8 - 1), xcol)),
        pl.BlockSpec((4, HALF), lambda b, s: (0, 0)),
        pl.BlockSpec((1, HALF), lambda b, s: (0, 0)),
        wspec(), vec(), wspec(), vec(), vec(),
    ]
    args = [p, p, p, conv_w, conv_b.reshape(1, HALF), w_r, b_r.reshape(2, 1, HALF), w_i,
            b_i.reshape(2, 1, HALF), lam.reshape(2, 1, HALF)]
    if rev:
        in_specs += [pl.BlockSpec((r, HALF), lambda b, s: (blk(b, s), 0)),
                     pl.BlockSpec((r, HALF), lambda b, s: (blk(b, s), xcol - 1))]
        args += [hf, p]
    return pl.pallas_call(
        functools.partial(_lru_kernel, rev=rev),
        grid=(B, N_SEQ_BLK),
        in_specs=in_specs,
        out_specs=pl.BlockSpec((r, HALF), lambda b, s: (blk(b, s), 0)),
        out_shape=jax.ShapeDtypeStruct((T_ALL, HALF), BF16 if rev else F32),
        scratch_shapes=[pltpu.VMEM((r + 16, HALF), F32), pltpu.VMEM((r, HALF), F32),
                        pltpu.VMEM((r, HALF), F32), pltpu.VMEM((r, HALF), F32),
                        pltpu.VMEM((1, HALF), F32)],
        compiler_params=_cparams(("parallel", "arbitrary"), VMEM_LIMIT),
        name="lru_bwd" if rev else "lru_fwd",
    )(*args)


def kernel(x, c, ctx, c_ctx, mod_w, mod_b, norm_g, ab_w_in, ab_w_out, gmlp_v_g, gmlp_ws, gmlp_bs, cd_w_in, cd_w_out, hgrn_lb, hgrn_onorm_g, lru_conv_w, lru_conv_b, lru_wr, lru_br, lru_wi, lru_bi, lru_lambda, router_w, router_b, exp_w_gu, exp_b_gu, exp_w_down, exp_b_down):
    cond8 = jnp.concatenate([c, c_ctx[None, :], jnp.zeros((5, D), F32)], axis=0)
    modm = _modulation(cond8, mod_w, mod_b).reshape(DEPTH * 8, 1, 6 * D)

    lb_soft = jax.nn.softmax(hgrn_lb.astype(F32), axis=0)
    lb_all = jnp.cumsum(lb_soft, axis=0) - lb_soft[0]

    p0, xs = _inproj((x.reshape(T_LAT, D), ctx.reshape(T_CTX, D)), norm_g[0, 0], modm, 0,
                     [ab_w_in[0].astype(BF16)], BF16)
    zc, zs = _fnet_chan(p0)
    mix_a = jnp.concatenate([_fnet_pos(zc, zs, 0, L, 512, 1024),
                             _fnet_pos(zc, zs, T_LAT, LC, LC, LC)], axis=0)
    mix_g = _gmlp(p0, gmlp_v_g[0], gmlp_ws[0], gmlp_bs[0])
    xs, h2, e4, p4, r4, counts = _outproj(mix_a, mix_g, ab_w_out[0].astype(BF16), xs, T_ALL, norm_g[0, 1],
                                          norm_g[0, 2], modm, 0, router_w[0], router_b[0])
    xs = _moe(h2, e4, p4, r4, counts, xs, norm_g[0, 3], modm, 0,
              exp_w_gu, exp_b_gu[0], exp_w_down, exp_b_down[0])

    w_cd = [cd_w_in[0, :, :4 * HALF].astype(BF16), cd_w_in[0, :, 4 * HALF:].astype(BF16)]
    p1a, p1b = _inproj(xs, norm_g[1, 0], modm, 1, w_cd, F32)
    o_f = _hgrn(p1a, p1b, lb_all[1], False)
    hg = _hgrn(p1a, p1b, lb_all[1], True, o_f, hgrn_onorm_g[0])
    lru_args = (lru_conv_w[0], lru_conv_b[0], lru_wr[0], lru_br[0], lru_wi[0], lru_bi[0], lru_lambda[0])
    h_f = _lru(p1b, *lru_args, False)
    lr = _lru(p1b, *lru_args, True, h_f)
    xl, h2, e4, p4, r4, counts = _outproj(hg, lr, cd_w_out[0].astype(BF16), xs, T_LAT, norm_g[1, 1],
                                          norm_g[1, 2], modm, 1, router_w[1], router_b[1])
    xl = _moe(h2, e4, p4, r4, counts, xl, norm_g[1, 3], modm, 1,
              exp_w_gu, exp_b_gu[1], exp_w_down, exp_b_down[1])
    return xl.reshape(B, L, D)
```

```python
import functools

import numpy as np
import jax
import jax.numpy as jnp
from jax import lax
from jax.experimental import pallas as pl
from jax.experimental.pallas import tpu as pltpu

F32 = jnp.float32
BF16 = jnp.bfloat16
HI = lax.Precision.HIGHEST

D = 2048
B = 2
L = 4096
LC = 256
T_LAT = B * L
T_CTX = B * LC
T_ALL = T_LAT + T_CTX
DEPTH = 2
EPS = 1e-6

HALF = D // 2
N_EXPERTS = 32
TOP_K = 4
D_EXPERT = D // 2
SWIGLU_LIMIT = 7.0
SWIGLU_ALPHA = 1.702
MOE_TM = 256

FNET_GROUPS = 4
FNET_GD = HALF // FNET_GROUPS
GMLP_HEADS = 8
GMLP_CHUNK = 128
HGRN_HEADS = 8
HGRN_R = 256
HGRN_C = 64
LRU_HEADS = 8
LRU_R = HGRN_R
LRU_C = 8.0

VMEM_LIMIT = 56 * 1024 * 1024


def _cparams(sem, vmem=None):
    return pltpu.CompilerParams(dimension_semantics=sem, vmem_limit_bytes=vmem)


def _mod_row(i, tm):
    return jnp.where(i < L // tm, 0, jnp.where(i < 2 * L // tm, 1, 2))


def _rms(x):
    return x * lax.rsqrt(jnp.mean(x * x, axis=-1, keepdims=True) + EPS)


def _bdot(a, b):
    return jnp.dot(a.astype(BF16), b.astype(BF16), preferred_element_type=F32)


ROW_S = D // 256
U32 = jnp.uint32
HI_MASK = 0xFFFF0000


def _row_slab(ref, r):
    return ref.at[pl.ds(pl.multiple_of(r * ROW_S, ROW_S), ROW_S), :]


def _pack_pieces(hi, lo):
    hb = pltpu.bitcast(hi.astype(BF16).astype(F32), U32)
    lb = pltpu.bitcast(lo.astype(BF16).astype(F32), U32)
    return jnp.bitwise_or(jnp.bitwise_and(hb, U32(HI_MASK)), jnp.right_shift(lb, U32(16)))


def _unpack_piece(w):
    hi = pltpu.bitcast(jnp.bitwise_and(w, U32(HI_MASK)), F32)
    lo = pltpu.bitcast(jnp.left_shift(w, U32(16)), F32)
    return hi, lo


def _store_rows(ref, val, row0=0):
    n = val.shape[0]
    for s in range(ROW_S):
        ref[pl.ds(row0 * ROW_S + s, n, stride=ROW_S), :] = _pack_pieces(
            val[:, s * 128:(s + 1) * 128], val[:, (s + ROW_S) * 128:(s + ROW_S + 1) * 128])


def _mod_kernel(c_ref, w_ref, b_ref, o_ref):
    cnd = c_ref[...]
    s = cnd * jax.nn.sigmoid(cnd)
    o_ref[...] = _bdot(s, w_ref[...]) + b_ref[...]


def _modulation(cond8, mod_w, mod_b):
    tn = 1024
    n = mod_w.shape[-1]
    return pl.pallas_call(
        _mod_kernel,
        grid=(DEPTH, n // tn),
        in_specs=[
            pl.BlockSpec((8, D), lambda l, j: (0, 0)),
            pl.BlockSpec((None, D, tn), lambda l, j: (l, 0, j)),
            pl.BlockSpec((None, 1, tn), lambda l, j: (l, 0, j)),
        ],
        out_specs=pl.BlockSpec((None, 8, tn), lambda l, j: (l, 0, j)),
        out_shape=jax.ShapeDtypeStruct((DEPTH, 8, n), F32),
        compiler_params=_cparams(("parallel", "parallel"), VMEM_LIMIT),
        name="modulation",
    )(cond8, mod_w, mod_b.reshape(DEPTH, 1, n))


def _inproj_kernel(*refs, nw, n_lat_tiles):
    if n_lat_tiles is None:
        x_ref, g_ref, sh_ref, sc_ref = refs[:4]
        w_refs, o_refs = refs[4:4 + nw], refs[4 + nw:]
        x = x_ref[...]
    else:
        xl_ref, xc_ref, g_ref, sh_ref, sc_ref = refs[:5]
        w_refs, o_refs, xs_ref = refs[5:5 + nw], refs[5 + nw:5 + 2 * nw], refs[5 + 2 * nw]
        i = pl.program_id(0)

        @pl.when(i < n_lat_tiles)
        def _():
            xs_ref[...] = xl_ref[...]

        @pl.when(i >= n_lat_tiles)
        def _():
            xs_ref[...] = xc_ref[...]

        x = xs_ref[...]
    y = _rms(x) * g_ref[...]
    h = (y * (1.0 + sc_ref[...]) + sh_ref[...]).astype(BF16)
    for w_ref, o_ref in zip(w_refs, o_refs):
        o_ref[...] = jnp.dot(h, w_ref[...], preferred_element_type=F32).astype(o_ref.dtype)


def _inproj(src, g, modm, layer, ws_bf, out_dtype):
    tm = 256
    row = lambda i: layer * 8 + _mod_row(i, tm)
    pair = isinstance(src, tuple)
    if pair:
        n_lat_tiles = src[0].shape[0] // tm
        t = src[0].shape[0] + src[1].shape[0]
        x_specs = [pl.BlockSpec((tm, D), lambda i: (jnp.minimum(i, n_lat_tiles - 1), 0)),
                   pl.BlockSpec((tm, D), lambda i: (jnp.maximum(i - n_lat_tiles, 0), 0))]
        x_args = list(src)
    else:
        n_lat_tiles = None
        t = src.shape[0]
        x_specs = [pl.BlockSpec((tm, D), lambda i: (i, 0))]
        x_args = [src]
    out_specs = [pl.BlockSpec((tm, w.shape[1]), lambda i: (i, 0)) for w in ws_bf]
    out_shape = [jax.ShapeDtypeStruct((t, w.shape[1]), out_dtype) for w in ws_bf]
    if pair:
        out_specs.append(pl.BlockSpec((tm, D), lambda i: (i, 0)))
        out_shape.append(jax.ShapeDtypeStruct((t, D), F32))
    return pl.pallas_call(
        functools.partial(_inproj_kernel, nw=len(ws_bf), n_lat_tiles=n_lat_tiles),
        grid=(t // tm,),
        in_specs=x_specs + [
            pl.BlockSpec((1, D), lambda i: (0, 0)),
            pl.BlockSpec((None, 1, D), lambda i: (row(i), 0, 0)),
            pl.BlockSpec((None, 1, D), lambda i: (row(i), 0, 1)),
        ] + [pl.BlockSpec(w.shape, lambda i: (0, 0), pipeline_mode=pl.Buffered(1)) for w in ws_bf],
        out_specs=out_specs,
        out_shape=out_shape,
        compiler_params=_cparams(("parallel",), VMEM_LIMIT),
        name="inproj",
    )(*x_args, g.reshape(1, D), modm, modm, *ws_bf)


def _fnet_chan_kernel(p_ref, cs_ref, zc_ref, zs_ref):
    for g in range(FNET_GROUPS):
        sl = slice(g * FNET_GD, (g + 1) * FNET_GD)
        z = jnp.dot(p_ref[:, sl].astype(BF16), cs_ref[...], preferred_element_type=F32)
        zc_ref[:, sl] = z[:, :FNET_GD].astype(BF16)
        zs_ref[:, sl] = z[:, FNET_GD:].astype(BF16)


def _fnet_chan(p):
    tm = 512
    t = p.shape[0]
    k = np.arange(FNET_GD)
    ang = 2.0 * np.pi * ((k[:, None] * k[None, :]) % FNET_GD) / FNET_GD
    cs = np.concatenate([np.cos(ang), np.sin(ang)], axis=1) / np.sqrt(FNET_GD)
    cs = jnp.asarray(cs, F32).astype(BF16)
    return pl.pallas_call(
        _fnet_chan_kernel,
        grid=(t // tm,),
        in_specs=[
            pl.BlockSpec((tm, HALF), lambda i: (i, 0)),
            pl.BlockSpec((FNET_GD, 2 * FNET_GD), lambda i: (0, 0)),
        ],
        out_specs=[pl.BlockSpec((tm, HALF), lambda i: (i, 0))] * 2,
        out_shape=[jax.ShapeDtypeStruct((t, HALF), BF16)] * 2,
        compiler_params=_cparams(("parallel",), VMEM_LIMIT),
        name="fnet_chan",
    )(p, cs)


def _fnet_pos_kernel(bre_ref, bim_ref, cre_ref, cim_ref, rre_ref, rim_ref, zc_ref, zs_ref, o_ref, acc):
    nt = pl.program_id(2)
    br, bi = bre_ref[...], bim_ref[...]
    cr, ci = cre_ref[...], cim_ref[...]
    rr, ri = rre_ref[...], rim_ref[...]
    tr = br * cr - bi * ci
    ti = br * ci + bi * cr
    er = (tr * rr - ti * ri).astype(BF16)
    ei = (tr * ri + ti * rr).astype(BF16)
    part = (jnp.dot(er, zc_ref[...], preferred_element_type=F32)
            + jnp.dot(ei, zs_ref[...], preferred_element_type=F32))

    @pl.when(nt == 0)
    def _():
        acc[...] = part

    @pl.when(nt > 0)
    def _():
        acc[...] += part

    @pl.when(nt == pl.num_programs(2) - 1)
    def _():
        o_ref[...] = acc[...].astype(BF16)


def _fnet_pos(zc, zs, row0, seq, tk, tn):
    nkt, nnt = seq // tk, seq // tn
    th = 2.0 * np.pi / seq
    kk = np.arange(tk)[:, None]
    nn = np.arange(tn)[None, :]
    base = th * ((kk * nn) % seq)
    n0 = (np.arange(nnt) * tn)[:, None, None]
    col = th * ((np.arange(tk)[None, :, None] * n0) % seq)
    k0 = (np.arange(nkt) * tk)[:, None, None, None]
    nfull = (np.arange(nnt) * tn)[None, :, None, None] + np.arange(tn)[None, None, None, :]
    row = th * ((k0 * nfull) % seq)
    scale = 1.0 / np.sqrt(seq)
    f = lambda a: jnp.asarray(a, F32)
    tabs = (f(np.cos(base)), f(-np.sin(base)), f(np.cos(col)), f(-np.sin(col)),
            f(np.cos(row) * scale), f(-np.sin(row) * scale))
    rb = row0 // tn
    ob = row0 // tk
    return pl.pallas_call(
        _fnet_pos_kernel,
        grid=(B, nkt, nnt),
        in_specs=[
            pl.BlockSpec((tk, tn), lambda b, k, n: (0, 0)),
            pl.BlockSpec((tk, tn), lambda b, k, n: (0, 0)),
            pl.BlockSpec((None, tk, 1), lambda b, k, n: (n, 0, 0)),
            pl.BlockSpec((None, tk, 1), lambda b, k, n: (n, 0, 0)),
            pl.BlockSpec((None, None, 1, tn), lambda b, k, n: (k, n, 0, 0)),
            pl.BlockSpec((None, None, 1, tn), lambda b, k, n: (k, n, 0, 0)),
            pl.BlockSpec((tn, HALF), lambda b, k, n: (rb + b * nnt + n, 0)),
            pl.BlockSpec((tn, HALF), lambda b, k, n: (rb + b * nnt + n, 0)),
        ],
        out_specs=pl.BlockSpec((tk, HALF), lambda b, k, n: (b * nkt + k, 0)),
        out_shape=jax.ShapeDtypeStruct((B * seq, HALF), BF16),
        scratch_shapes=[pltpu.VMEM((tk, HALF), F32)],
        compiler_params=_cparams(("parallel", "parallel", "arbitrary"), VMEM_LIMIT),
        name=f"fnet_pos_{seq}",
    )(*tabs, zc, zs)


def _gmlp_kernel(u_ref, v_ref, vg_ref, ws_ref, bst_ref, o_ref, *, tm):
    zu = jax.nn.gelu(u_ref[...].astype(F32))
    zv = jax.nn.gelu(v_ref[...].astype(F32))
    mu = jnp.mean(zv, axis=-1, keepdims=True)
    dv = zv - mu
    var = jnp.mean(dv * dv, axis=-1, keepdims=True)
    vn = dv * lax.rsqrt(var + EPS) * vg_ref[...]
    hd = HALF // GMLP_HEADS
    for c in range(tm // GMLP_CHUNK):
        rs = slice(c * GMLP_CHUNK, (c + 1) * GMLP_CHUNK)
        for h in range(GMLP_HEADS):
            cs = slice(h * hd, (h + 1) * hd)
            s = _bdot(ws_ref[h], vn[rs, cs])
            s = s + bst_ref[:, h:h + 1]
            o_ref[rs, cs] = (zu[rs, cs] * s).astype(BF16)


def _gmlp(p, v_g, w_s, b_s):
    tm = 256
    t = p.shape[0]
    return pl.pallas_call(
        functools.partial(_gmlp_kernel, tm=tm),
        grid=(t // tm,),
        in_specs=[
            pl.BlockSpec((tm, HALF), lambda i: (i, 1)),
            pl.BlockSpec((tm, HALF), lambda i: (i, 2)),
            pl.BlockSpec((1, HALF), lambda i: (0, 0)),
            pl.BlockSpec((GMLP_HEADS, GMLP_CHUNK, GMLP_CHUNK), lambda i: (0, 0, 0)),
            pl.BlockSpec((GMLP_CHUNK, GMLP_HEADS), lambda i: (0, 0)),
        ],
        out_specs=pl.BlockSpec((tm, HALF), lambda i: (i, 0)),
        out_shape=jax.ShapeDtypeStruct((t, HALF), BF16),
        compiler_params=_cparams(("parallel",), VMEM_LIMIT),
        name="gmlp",
    )(p, p, v_g.reshape(1, HALF), w_s, b_s.T)


def _outproj_kernel(ma_ref, mb_ref, w_ref, x_ref, g1_ref, g2_ref, ga1_ref, sh2_ref, sc2_ref,
                    rw_ref, rb_ref, xo_ref, h2_ref, e_ref, p_ref, rk_ref, cnt_ref, carry, rw2, m_s, *, tm):
    @pl.when(pl.program_id(0) == 0)
    def _():
        carry[...] = jnp.zeros_like(carry)
        rw = rw_ref[...]
        rw_hi = rw.astype(BF16)
        rw2[:, :N_EXPERTS] = rw_hi
        rw2[:, N_EXPERTS:] = (rw - rw_hi.astype(F32)).astype(BF16)

    sub = 256
    iota = lax.broadcasted_iota(jnp.int32, (sub, N_EXPERTS), 1).astype(F32)
    lane4 = lax.broadcasted_iota(jnp.int32, (sub, TOP_K), 1)
    r_i = lax.broadcasted_iota(jnp.int32, (sub, sub), 0)
    c_i = lax.broadcasted_iota(jnp.int32, (sub, sub), 1)
    tri = (c_i < r_i).astype(BF16)
    running = carry[...]
    for r0 in range(0, tm, sub):
        rs = slice(r0, r0 + sub)
        m_s[rs, :] = (jnp.dot(ma_ref[rs, :], w_ref[:HALF, :], preferred_element_type=F32)
                      + jnp.dot(mb_ref[rs, :], w_ref[HALF:, :], preferred_element_type=F32))
    for r0 in range(0, tm, sub):
        rs = slice(r0, r0 + sub)
        m = m_s[rs, :]
        xn = x_ref[rs, :] + ga1_ref[...] * (_rms(m) * g1_ref[...])
        xo_ref[rs, :] = xn
        h2 = (_rms(xn) * g2_ref[...]) * (1.0 + sc2_ref[...]) + sh2_ref[...]
        _store_rows(h2_ref, h2, r0)

        h_hi = h2.astype(BF16)
        h_lo = (h2 - h_hi.astype(F32)).astype(BF16)
        t1 = jnp.dot(h_hi, rw2[...], preferred_element_type=F32)
        t2 = jnp.dot(h_lo, rw2[:, :N_EXPERTS], preferred_element_type=F32)
        lg = t1[:, :N_EXPERTS] + t1[:, N_EXPERTS:] + t2 + rb_ref[...]
        idxs, vals = [], []
        for _ in range(TOP_K):
            mx = jnp.max(lg, axis=-1, keepdims=True)
            ix = jnp.min(jnp.where(lg == mx, iota, float(N_EXPERTS)), axis=-1, keepdims=True)
            idxs.append(ix)
            vals.append(mx)
            lg = jnp.where(iota == ix, -jnp.inf, lg)
        exs = [jnp.exp(v - vals[0]) for v in vals]
        den = exs[0] + exs[1] + exs[2] + exs[3]

        onehot = [(iota == ix) for ix in idxs]
        cnt = jnp.zeros((sub, N_EXPERTS), F32)
        for oh in onehot:
            cnt = cnt + oh.astype(F32)
        prefix = jnp.dot(tri, cnt.astype(BF16), preferred_element_type=F32) + running

        e4 = jnp.zeros((sub, TOP_K), jnp.int32)
        p4 = jnp.zeros((sub, TOP_K), F32)
        r4 = jnp.zeros((sub, TOP_K), jnp.int32)
        for k in range(TOP_K):
            rk = jnp.sum(jnp.where(onehot[k], prefix, 0.0), axis=-1, keepdims=True)
            e4 = jnp.where(lane4 == k, idxs[k].astype(jnp.int32), e4)
            p4 = jnp.where(lane4 == k, exs[k] / den, p4)
            r4 = jnp.where(lane4 == k, rk.astype(jnp.int32), r4)
        e_ref[rs, :] = e4
        p_ref[rs, :] = p4
        rk_ref[rs, :] = r4
        running = running + jnp.sum(cnt, axis=0, keepdims=True)
    carry[...] = running
    cnt_ref[...] = running


def _outproj(ma, mb, w_bf, xs, t, g1, g2, modm, layer, rw, rb):
    tm = 512
    row = lambda i: layer * 8 + _mod_row(i, tm)
    modspec = lambda c: pl.BlockSpec((None, 1, D), lambda i: (row(i), 0, c))
    vec = pl.BlockSpec((1, D), lambda i: (0, 0))
    small = lambda dt: jax.ShapeDtypeStruct((t, TOP_K), dt)
    return pl.pallas_call(
        functools.partial(_outproj_kernel, tm=tm),
        grid=(t // tm,),
        in_specs=[
            pl.BlockSpec((tm, HALF), lambda i: (i, 0)),
            pl.BlockSpec((tm, HALF), lambda i: (i, 0)),
            pl.BlockSpec((D, D), lambda i: (0, 0)),
            pl.BlockSpec((tm, D), lambda i: (i, 0)),
            vec, vec, modspec(2), modspec(3), modspec(4),
            pl.BlockSpec((D, N_EXPERTS), lambda i: (0, 0)),
            pl.BlockSpec((1, N_EXPERTS), lambda i: (0, 0)),
        ],
        out_specs=[
            pl.BlockSpec((tm, D), lambda i: (i, 0)),
            pl.BlockSpec((tm * ROW_S, 128), lambda i: (i, 0)),
            pl.BlockSpec((tm, TOP_K), lambda i: (i, 0)),
            pl.BlockSpec((tm, TOP_K), lambda i: (i, 0)),
            pl.BlockSpec((tm, TOP_K), lambda i: (i, 0)),
            pl.BlockSpec((1, N_EXPERTS), lambda i: (0, 0)),
        ],
        out_shape=[
            jax.ShapeDtypeStruct((t, D), F32),
            jax.ShapeDtypeStruct((t * ROW_S, 128), U32),
            small(jnp.int32), small(F32), small(jnp.int32),
            jax.ShapeDtypeStruct((1, N_EXPERTS), F32),
        ],
        scratch_shapes=[pltpu.VMEM((1, N_EXPERTS), F32), pltpu.VMEM((D, 2 * N_EXPERTS), BF16),
                        pltpu.VMEM((tm, D), F32)],
        compiler_params=_cparams(("arbitrary",), VMEM_LIMIT),
        name="outproj_router",
    )(ma, mb, w_bf, xs, g1.reshape(1, D), g2.reshape(1, D), modm, modm, modm,
      rw, rb.reshape(1, N_EXPERTS))


def _invmap_kernel(pstart_ref, cnt_ref, padded_ref, dest_ref, inv_ref, *, n_pairs, n_slots):
    def unused(lo, hi):
        def body(s, c):
            inv_ref[s] = n_pairs + jnp.bitwise_and(s, 2 * MOE_TM - 1)
            return c
        lax.fori_loop(lo, hi, body, 0)

    def per_expert(e, c):
        unused(pstart_ref[e] + cnt_ref[e], pstart_ref[e] + padded_ref[e])
        return c

    lax.fori_loop(0, N_EXPERTS, per_expert, 0)
    unused(pstart_ref[N_EXPERTS - 1] + padded_ref[N_EXPERTS - 1], n_slots)

    def body(i, c):
        inv_ref[dest_ref[i]] = i
        return c

    lax.fori_loop(0, n_pairs, body, 0, unroll=16)


def _invmap(dest, pstart, cnt, padded, n_slots):
    n_pairs = dest.shape[0]
    return pl.pallas_call(
        functools.partial(_invmap_kernel, n_pairs=n_pairs, n_slots=n_slots),
        grid_spec=pltpu.PrefetchScalarGridSpec(
            num_scalar_prefetch=3,
            grid=(1,),
            in_specs=[pl.BlockSpec(memory_space=pltpu.SMEM)],
            out_specs=pl.BlockSpec(memory_space=pltpu.SMEM),
        ),
        out_shape=jax.ShapeDtypeStruct((n_slots,), jnp.int32),
        compiler_params=_cparams(("arbitrary",)),
        name="moe_invmap",
    )(pstart, cnt, padded, dest)


W_CH = 256
W_NCH_GU = D // W_CH
W_NCH = W_NCH_GU + D_EXPERT // W_CH
W_STAGE = 4


def _expert_kernel(be_ref, nused_ref, first_ref, nxt_ref, lo_ref, hi_ref, slot_ref,
                   inv_cur, inv_next, inv_next2, inv_prev, bgu_ref, bd_ref, h_hbm, wgu_all, wd_all, y_hbm,
                   xbuf, obuf, x_s, act_s, w_buf, stage, gsem, ssem, sem, *, layer, n_tok):
    i = pl.program_id(0)
    n_live = nused_ref[0]
    s_cur = lax.rem(i, 3)
    s_p1 = lax.rem(i + 1, 3)
    s_p2 = lax.rem(i + 2, 3)
    wgu_hbm = wgu_all.at[layer]
    wd_hbm = wd_all.at[layer]

    def gather_copy(inv_ref, r, slot):
        tok = jnp.minimum(lax.shift_right_logical(inv_ref[r], 2), n_tok - 1)
        return pltpu.make_async_copy(_row_slab(h_hbm, tok), _row_slab(xbuf.at[slot], r), gsem.at[slot])

    n_pairs = n_tok * TOP_K

    def scatter_copy(inv_ref, r, slot, first_step=False):
        dst = inv_ref[r]
        if first_step:
            dst = jnp.where(i == 0, n_pairs + 2 * MOE_TM + r, dst)
        return pltpu.make_async_copy(_row_slab(obuf.at[slot], r), _row_slab(y_hbm, dst), ssem.at[slot])

    def looped(fn):
        def body(r, c):
            fn(r)
            return c
        lax.fori_loop(0, MOE_TM, body, 0)

    def start_chunk(e, c):
        s = lax.rem(c, W_STAGE)

        @pl.when(c < W_NCH_GU)
        def _():
            r0 = pl.multiple_of(c * W_CH, W_CH)
            pltpu.make_async_copy(wgu_hbm.at[e, pl.ds(r0, W_CH), :], stage.at[s], sem.at[s]).start()

        @pl.when(c >= W_NCH_GU)
        def _():
            r0 = pl.multiple_of((c - W_NCH_GU) * W_CH, W_CH)
            pltpu.make_async_copy(wd_hbm.at[e, pl.ds(r0, W_CH), :], stage.at[s], sem.at[s]).start()

    def finish_chunk(c, dst):
        s = lax.rem(c, W_STAGE)
        pltpu.make_async_copy(wgu_hbm.at[0, pl.ds(0, W_CH), :], stage.at[s], sem.at[s]).wait()
        w_buf[dst, pl.ds(pl.multiple_of(c * W_CH, W_CH), W_CH), :] = stage[s].astype(BF16)

    def stream(e, dst, c_lo, c_hi):
        def body(c, carry):
            finish_chunk(c, dst)

            @pl.when(c + W_STAGE < W_NCH)
            def _():
                start_chunk(e, c + W_STAGE)
            return carry
        lax.fori_loop(c_lo, c_hi, body, 0)

    def prime(e):
        for c in range(W_STAGE):
            start_chunk(e, c)

    @pl.when(i == 0)
    def _():
        looped(lambda r: gather_copy(inv_cur, r, 0).start())
        looped(lambda r: gather_copy(inv_next, r, 1).start())
        obuf[2] = jnp.zeros(obuf.shape[1:], U32)
        for region in range(2):
            dump = y_hbm.at[pl.ds((n_pairs + region * MOE_TM) * ROW_S, MOE_TM * ROW_S), :]
            zero_dump = pltpu.make_async_copy(obuf.at[2], dump, ssem.at[2])
            zero_dump.start()
            zero_dump.wait()
        prime(be_ref[0])
        stream(be_ref[0], slot_ref[0], 0, W_NCH)

    @pl.when(i < n_live)
    def _():
        e_next = nxt_ref[i]
        cur = slot_ref[i]

        @pl.when((first_ref[i] == 1) & (e_next >= 0))
        def _():
            prime(e_next)

        def block_body(c_cur, c_p1, c_p2):
            for r in range(MOE_TM):
                gather_copy(inv_cur, r, c_cur).wait()
            for s in range(ROW_S):
                hi, lo = _unpack_piece(xbuf[c_cur, pl.ds(s, MOE_TM, stride=ROW_S), :])
                x_s[:, s * 128:(s + 1) * 128] = hi.astype(BF16)
                x_s[:, (s + ROW_S) * 128:(s + ROW_S + 1) * 128] = lo.astype(BF16)
            nc = 256

            def gate_up(c_lo, c_hi):
                for c0 in range(c_lo, c_hi, nc):
                    g = (jnp.dot(x_s[...], w_buf[cur, :D, c0:c0 + nc], preferred_element_type=F32)
                         + bgu_ref[:, c0:c0 + nc])
                    u = (jnp.dot(x_s[...], w_buf[cur, :D, D_EXPERT + c0:D_EXPERT + c0 + nc],
                                 preferred_element_type=F32) + bgu_ref[:, D_EXPERT + c0:D_EXPERT + c0 + nc])
                    gate = jnp.minimum(g, SWIGLU_LIMIT)
                    up = jnp.clip(u, -SWIGLU_LIMIT, SWIGLU_LIMIT)
                    act_s[:, c0:c0 + nc] = (gate * jax.nn.sigmoid(SWIGLU_ALPHA * gate)
                                            * (up + 1.0)).astype(BF16)

            for r in range(MOE_TM):
                scatter_copy(inv_prev, r, c_p2, first_step=True).start(priority=r % 2)
            gate_up(0, D_EXPERT // 2)

            @pl.when(n_live > 0)
            def _():
                for r in range(MOE_TM // 2):
                    gather_copy(inv_next2, r, c_p2).start()
                gate_up(D_EXPERT // 2, D_EXPERT)

            @pl.when(first_ref[i] >= 0)
            def _():
                for r in range(MOE_TM // 2, MOE_TM):
                    gather_copy(inv_next2, r, c_p2).start()
                for c0 in range(0, D // 2, nc):
                    y_hi = (jnp.dot(act_s[...], w_buf[cur, D:, c0:c0 + nc], preferred_element_type=F32)
                            + bd_ref[:, c0:c0 + nc])
                    y_lo = (jnp.dot(act_s[...], w_buf[cur, D:, D // 2 + c0:D // 2 + c0 + nc],
                                    preferred_element_type=F32) + bd_ref[:, D // 2 + c0:D // 2 + c0 + nc])
                    for j in range(nc // 128):
                        s = c0 // 128 + j
                        obuf[c_cur, pl.ds(s, MOE_TM, stride=ROW_S), :] = _pack_pieces(
                            y_hi[:, j * 128:(j + 1) * 128], y_lo[:, j * 128:(j + 1) * 128])

            @pl.when(i >= 1)
            def _():
                for r in range(MOE_TM):
                    scatter_copy(inv_prev, r, c_p1).wait()

        for ring in range(3):
            @pl.when(s_cur == ring)
            def _():
                block_body(ring, (ring + 1) % 3, (ring + 2) % 3)

        @pl.when(e_next >= 0)
        def _():
            stream(e_next, 1 - cur, lo_ref[i], hi_ref[i])

        @pl.when(i == n_live - 1)
        def _():
            looped(lambda r: scatter_copy(inv_prev, r, s_p2).wait())
            looped(lambda r: scatter_copy(inv_cur, r, s_cur).start())
            looped(lambda r: scatter_copy(inv_cur, r, s_cur).wait())
            looped(lambda r: gather_copy(inv_next, r, s_p1).wait())
            looped(lambda r: gather_copy(inv_next2, r, s_p2).wait())


def _take(table, idx):
    ids = jnp.arange(table.shape[0], dtype=jnp.int32)
    return jnp.sum(jnp.where(idx[..., None] == ids, table, 0), axis=-1)


def _experts(h2, inv, block_e, nused, pstart, padded, layer, wgu, bgu, wd, bd, nblk, n_tok):
    nb_e = padded // MOE_TM
    pos = jnp.arange(nblk, dtype=jnp.int32) - _take(pstart // MOE_TM, block_e)
    nb = jnp.maximum(_take(nb_e, block_e), 1)
    lo = (pos * W_NCH) // nb
    hi = ((pos + 1) * W_NCH) // nb
    first = (pos == 0).astype(jnp.int32)
    eid = jnp.arange(N_EXPERTS, dtype=jnp.int32)
    later = jnp.where(nb_e > 0, eid, N_EXPERTS)
    nxt_e = jnp.concatenate([lax.cummin(later, reverse=True)[1:], jnp.full((1,), N_EXPERTS, jnp.int32)])
    nxt_e = jnp.where(nxt_e >= N_EXPERTS, -1, nxt_e)
    slot_e = (jnp.cumsum((nb_e > 0).astype(jnp.int32)) - 1) % 2
    i32 = lambda a: a.astype(jnp.int32)
    live = lambda i, nu: jnp.clip(i, 0, nu[0] - 1)
    sblk = lambda off: pl.BlockSpec((MOE_TM,), lambda i, be, nu, *_: (live(i + off, nu),),
                                    memory_space=pltpu.SMEM)
    bias = lambda n: pl.BlockSpec((None, 1, n), lambda i, be, nu, *_: (be[live(i, nu)], 0, 0))
    hbm = pl.BlockSpec(memory_space=pl.ANY)
    n_slabs = n_tok * TOP_K + 3 * MOE_TM
    return pl.pallas_call(
        functools.partial(_expert_kernel, layer=layer, n_tok=n_tok),
        grid_spec=pltpu.PrefetchScalarGridSpec(
            num_scalar_prefetch=7,
            grid=(nblk,),
            in_specs=[sblk(0), sblk(1), sblk(2), sblk(-1), bias(2 * D_EXPERT), bias(D), hbm, hbm, hbm],
            out_specs=hbm,
            scratch_shapes=[
                pltpu.VMEM((3, MOE_TM * ROW_S, 128), U32),
                pltpu.VMEM((3, MOE_TM * ROW_S, 128), U32),
                pltpu.VMEM((MOE_TM, D), BF16),
                pltpu.VMEM((MOE_TM, D_EXPERT), BF16),
                pltpu.VMEM((2, D + D_EXPERT, D), BF16),
                pltpu.VMEM((W_STAGE, W_CH, D), F32),
                pltpu.SemaphoreType.DMA((3,)),
                pltpu.SemaphoreType.DMA((3,)),
                pltpu.SemaphoreType.DMA((W_STAGE,)),
            ],
        ),
        out_shape=jax.ShapeDtypeStruct((n_slabs * ROW_S, 128), U32),
        compiler_params=_cparams(("arbitrary",), VMEM_LIMIT),
        name="moe_experts",
    )(block_e, nused, i32(first), i32(_take(nxt_e, block_e)), i32(lo), i32(hi), i32(_take(slot_e, block_e)),
      inv, inv, inv, inv, bgu.reshape(N_EXPERTS, 1, -1), bd.reshape(N_EXPERTS, 1, -1), h2, wgu, wd)


def _combine_kernel(p_ref, x_ref, g3_ref, ga2_ref, y_ref, xo_ref, acc_hi, acc_lo, *, tm):
    pair = TOP_K * ROW_S
    for r in range(tm):
        a_hi = a_lo = None
        for k in range(TOP_K):
            hi, lo = _unpack_piece(y_ref[pl.ds(r * pair + k * ROW_S, ROW_S), :])
            w = p_ref[r * TOP_K + k]
            a_hi = hi * w if k == 0 else a_hi + hi * w
            a_lo = lo * w if k == 0 else a_lo + lo * w
        acc_hi[pl.ds(r * ROW_S, ROW_S), :] = a_hi
        acc_lo[pl.ds(r * ROW_S, ROW_S), :] = a_lo
    y = jnp.concatenate([acc_hi[pl.ds(s, tm, stride=ROW_S), :] for s in range(ROW_S)]
                        + [acc_lo[pl.ds(s, tm, stride=ROW_S), :] for s in range(ROW_S)], axis=1)
    xo_ref[...] = x_ref[...] + ga2_ref[...] * (_rms(y) * g3_ref[...])


def _combine(y4, p4, xs, g3, modm, layer):
    tm = 128
    t = xs.shape[0]
    row = lambda i: layer * 8 + _mod_row(i, tm)
    return pl.pallas_call(
        functools.partial(_combine_kernel, tm=tm),
        grid=(t // tm,),
        in_specs=[
            pl.BlockSpec((tm * TOP_K,), lambda i: (i,), memory_space=pltpu.SMEM),
            pl.BlockSpec((tm, D), lambda i: (i, 0)),
            pl.BlockSpec((1, D), lambda i: (0, 0)),
            pl.BlockSpec((None, 1, D), lambda i: (row(i), 0, 5)),
            pl.BlockSpec((tm * TOP_K * ROW_S, 128), lambda i: (i, 0)),
        ],
        out_specs=pl.BlockSpec((tm, D), lambda i: (i, 0)),
        out_shape=jax.ShapeDtypeStruct((t, D), F32),
        scratch_shapes=[pltpu.VMEM((tm * ROW_S, 128), F32)] * 2,
        compiler_params=_cparams(("parallel",), VMEM_LIMIT),
        name="moe_combine",
    )(p4.reshape(-1), xs, g3.reshape(1, D), modm, y4)


def _moe(h2, e4, p4, r4, counts, xs, g3, modm, layer, wgu, bgu, wd, bd):
    t = e4.shape[0]
    nblk = -(-(t * TOP_K + N_EXPERTS * (MOE_TM - 1)) // MOE_TM)
    cnt = counts.reshape(N_EXPERTS).astype(jnp.int32)
    padded = (cnt + MOE_TM - 1) // MOE_TM * MOE_TM
    pend = jnp.cumsum(padded)
    pstart = pend - padded
    dest = (_take(pstart, e4) + r4).reshape(-1).astype(jnp.int32)
    nused = (pend[-1:] // MOE_TM).astype(jnp.int32)
    blk_start = jnp.arange(nblk, dtype=jnp.int32) * MOE_TM
    block_e = jnp.minimum(jnp.sum((pend[None, :] <= blk_start[:, None]).astype(jnp.int32), axis=1),
                          N_EXPERTS - 1).astype(jnp.int32)
    inv = _invmap(dest, pstart.astype(jnp.int32), cnt, padded.astype(jnp.int32), nblk * MOE_TM)
    y4 = _experts(h2, inv, block_e, nused, pstart, padded, layer, wgu, bgu, wd, bd, nblk, t)
    return _combine(y4, p4, xs, g3, modm, layer)


N_CTX_BLK = LC // HGRN_R
N_LAT_BLK = L // HGRN_R
N_SEQ_BLK = N_CTX_BLK + N_LAT_BLK


def _scan_block(b, s, rev):
    if rev:
        return jnp.where(s < N_CTX_BLK, T_LAT // HGRN_R + N_CTX_BLK * b + (N_CTX_BLK - 1 - s),
                         N_LAT_BLK * b + (N_SEQ_BLK - 1 - s))
    return jnp.where(s < N_CTX_BLK, T_LAT // HGRN_R + N_CTX_BLK * b + s, N_LAT_BLK * b + (s - N_CTX_BLK))


def _hgrn_kernel(*refs, rev):
    if rev:
        q_ref, f_ref, v_ref, lb_ref, of_ref, g_ref, on_ref, o_ref, st, qin_s, qd_s, ki_s, ke_s, v_s = refs
    else:
        q_ref, f_ref, v_ref, lb_ref, o_ref, st, qin_s, qd_s, ki_s, ke_s, v_s = refs

    @pl.when(pl.program_id(1) == 0)
    def _():
        st[...] = jnp.zeros_like(st)

    c = HGRN_C
    nc = HGRN_R // c
    r_i = lax.broadcasted_iota(jnp.int32, (c, c), 0)
    c_i = lax.broadcasted_iota(jnp.int32, (c, c), 1)
    mask = (c_i >= r_i) if rev else (c_i <= r_i)
    tri = mask.astype(F32)

    lb = lb_ref[...]
    decay = [None] * nc
    v_s[...] = v_ref[...].astype(BF16)
    for ci in range(nc):
        rs = slice(ci * c, (ci + 1) * c)
        sg = jax.nn.sigmoid(f_ref[rs, :])
        logf = jnp.log(lb + (1.0 - lb) * sg)
        kk = (1.0 - lb) * (1.0 - sg)
        q = q_ref[rs, :]
        qs = q * jax.nn.sigmoid(q)
        cum = jnp.dot(tri, logf, precision=HI, preferred_element_type=F32)
        total = cum[0:1] if rev else cum[c - 1:c]
        mid = cum[c // 2:c // 2 + 1]
        qin_s[rs, :] = (qs * jnp.exp(cum)).astype(BF16)
        qd_s[rs, :] = (qs * jnp.exp(cum - mid)).astype(BF16)
        ki_s[rs, :] = (kk * jnp.exp(mid - cum)).astype(BF16)
        ke_s[rs, :] = (kk * jnp.exp(total - cum)).astype(BF16)
        decay[ci] = jnp.exp(total)

    order = range(nc - 1, -1, -1) if rev else range(nc)
    for ci in order:
        rs = slice(ci * c, (ci + 1) * c)
        for h in range(HGRN_HEADS):
            hs = slice(h * 128, (h + 1) * 128)
            state = st[h]
            vb = v_s[rs, hs]
            sc = lax.dot_general(qd_s[rs, hs], ki_s[rs, hs], (((1,), (1,)), ((), ())),
                                 preferred_element_type=F32)
            sc = jnp.where(mask, sc, 0.0)
            intra = jnp.dot(sc.astype(BF16), vb, preferred_element_type=F32)
            inter = lax.dot_general(qin_s[rs, hs], state.astype(BF16), (((1,), (1,)), ((), ())),
                                    preferred_element_type=F32)
            st[h] = state * decay[ci][:, hs] + lax.dot_general(
                vb, ke_s[rs, hs], (((0,), (0,)), ((), ())), preferred_element_type=F32)
            o = intra + inter
            if rev:
                o = o + of_ref[rs, hs]
                g = g_ref[rs, hs]
                o_ref[rs, hs] = (_rms(o) * on_ref[:, hs] * (g * jax.nn.sigmoid(g))).astype(BF16)
            else:
                o_ref[rs, hs] = o


def _hgrn(pa, pb, lb, rev, of=None, onorm=None):
    r = HGRN_R
    blk = lambda b, s: _scan_block(b, s, rev)
    col = lambda c: pl.BlockSpec((r, HALF), lambda b, s: (blk(b, s), c))
    vec = pl.BlockSpec((1, HALF), lambda b, s: (0, 0))
    in_specs = [col(0), col(2 if rev else 1), col(3), vec]
    args = [pa, pa, pa, lb.reshape(1, HALF)]
    if rev:
        in_specs += [col(0), col(0), vec]
        args += [of, pb, onorm.reshape(1, HALF)]
    return pl.pallas_call(
        functools.partial(_hgrn_kernel, rev=rev),
        grid=(B, N_SEQ_BLK),
        in_specs=in_specs,
        out_specs=col(0),
        out_shape=jax.ShapeDtypeStruct((T_ALL, HALF), BF16 if rev else F32),
        scratch_shapes=[pltpu.VMEM((HGRN_HEADS, 128, 128), F32)] + [pltpu.VMEM((r, HALF), BF16)] * 5,
        compiler_params=_cparams(("parallel", "arbitrary"), VMEM_LIMIT),
        name="hgrn_bwd" if rev else "hgrn_fwd",
    )(*args)


def _lru_kernel(*refs, rev):
    if rev:
        (x_ref, xp_ref, xn_ref, cw_ref, cb_ref, wr_ref, br_ref, wi_ref, bi_ref, lam_ref,
         hf_ref, gt_ref, o_ref, xcat, a_s, u_s, h_s, carry) = refs
    else:
        (x_ref, xp_ref, xn_ref, cw_ref, cb_ref, wr_ref, br_ref, wi_ref, bi_ref, lam_ref,
         o_ref, xcat, a_s, u_s, h_s, carry) = refs
    s = pl.program_id(1)
    r = LRU_R

    @pl.when(s == 0)
    def _():
        carry[...] = jnp.zeros_like(carry)

    is_ctx = s < N_CTX_BLK
    if rev:
        j = jnp.where(is_ctx, N_CTX_BLK - 1 - s, N_SEQ_BLK - 1 - s)
    else:
        j = jnp.where(is_ctx, s, s - N_CTX_BLK)
    nb = jnp.where(is_ctx, N_CTX_BLK, N_LAT_BLK)
    xcat[0:8, :] = jnp.where(j == 0, 0.0, xp_ref[...])
    xcat[8:8 + r, :] = x_ref[...]
    xcat[8 + r:16 + r, :] = jnp.where(j == nb - 1, 0.0, xn_ref[...])
    xc = cb_ref[...] + xcat[pl.ds(6, r), :] * cw_ref[0:1, :]
    for t in range(1, 4):
        xc = xc + xcat[pl.ds(6 + t, r), :] * cw_ref[t:t + 1, :]

    lam = lam_ref[...]
    sp = jnp.maximum(-lam, 0.0) + jnp.log1p(jnp.exp(-jnp.abs(lam)))
    hd = HALF // LRU_HEADS
    for h in range(LRU_HEADS):
        cs = slice(h * hd, (h + 1) * hd)
        xh = xc[:, cs]
        rg = jax.nn.sigmoid(_bdot(xh, wr_ref[h]) + br_ref[:, cs])
        ig = jax.nn.sigmoid(_bdot(xh, wi_ref[h]) + bi_ref[:, cs])
        log_a = -LRU_C * rg * sp[:, cs]
        a = jnp.exp(log_a)
        a_s[:, cs] = a
        u_s[:, cs] = jnp.sqrt(-jnp.tanh(log_a) * (a * a + 1.0)) * (ig * xh)

    def step(t, h):
        tt = (r - 1 - t) if rev else t
        h = a_s[pl.ds(tt, 1), :] * h + u_s[pl.ds(tt, 1), :]
        h_s[pl.ds(tt, 1), :] = h
        return h

    carry[...] = lax.fori_loop(0, r, step, carry[...], unroll=8)
    if rev:
        o_ref[...] = ((h_s[...] + hf_ref[...]) * jax.nn.gelu(gt_ref[...])).astype(BF16)
    else:
        o_ref[...] = h_s[...]


def _lru(p, conv_w, conv_b, w_r, b_r, w_i, b_i, lam, rev, hf=None):
    r = LRU_R
    d = 1 if rev else 0
    blk = lambda b, s: _scan_block(b, s, rev)
    xcol = 2
    n8 = T_ALL // 8
    vec = lambda: pl.BlockSpec((None, 1, HALF), lambda b, s: (d, 0, 0))
    wspec = lambda: pl.BlockSpec((None, LRU_HEADS, 128, 128), lambda b, s: (d, 0, 0, 0))
    in_specs = [
        pl.BlockSpec((r, HALF), lambda b, s: (blk(b, s), xcol)),
        pl.BlockSpec((8, HALF), lambda b, s: (jnp.maximum(blk(b, s) * (r // 8) - 1, 0), xcol)),
        pl.BlockSpec((8, HALF), lambda b, s: (jnp.minimum((blk(b, s) + 1) * (r // 8), n8 - 1), xcol)),
        pl.BlockSpec((4, HALF), lambda b, s: (0, 0)),
        pl.BlockSpec((1, HALF), lambda b, s: (0, 0)),
        wspec(), vec(), wspec(), vec(), vec(),
    ]
    args = [p, p, p, conv_w, conv_b.reshape(1, HALF), w_r, b_r.reshape(2, 1, HALF), w_i,
            b_i.reshape(2, 1, HALF), lam.reshape(2, 1, HALF)]
    if rev:
        in_specs += [pl.BlockSpec((r, HALF), lambda b, s: (blk(b, s), 0)),
                     pl.BlockSpec((r, HALF), lambda b, s: (blk(b, s), xcol - 1))]
        args += [hf, p]
    return pl.pallas_call(
        functools.partial(_lru_kernel, rev=rev),
        grid=(B, N_SEQ_BLK),
        in_specs=in_specs,
        out_specs=pl.BlockSpec((r, HALF), lambda b, s: (blk(b, s), 0)),
        out_shape=jax.ShapeDtypeStruct((T_ALL, HALF), BF16 if rev else F32),
        scratch_shapes=[pltpu.VMEM((r + 16, HALF), F32), pltpu.VMEM((r, HALF), F32),
                        pltpu.VMEM((r, HALF), F32), pltpu.VMEM((r, HALF), F32),
                        pltpu.VMEM((1, HALF), F32)],
        compiler_params=_cparams(("parallel", "arbitrary"), VMEM_LIMIT),
        name="lru_bwd" if rev else "lru_fwd",
    )(*args)


def kernel(x, c, ctx, c_ctx, mod_w, mod_b, norm_g, ab_w_in, ab_w_out, gmlp_v_g, gmlp_ws, gmlp_bs, cd_w_in, cd_w_out, hgrn_lb, hgrn_onorm_g, lru_conv_w, lru_conv_b, lru_wr, lru_br, lru_wi, lru_bi, lru_lambda, router_w, router_b, exp_w_gu, exp_b_gu, exp_w_down, exp_b_down):
    cond8 = jnp.concatenate([c, c_ctx[None, :], jnp.zeros((5, D), F32)], axis=0)
    modm = _modulation(cond8, mod_w, mod_b).reshape(DEPTH * 8, 1, 6 * D)

    lb_soft = jax.nn.softmax(hgrn_lb.astype(F32), axis=0)
    lb_all = jnp.cumsum(lb_soft, axis=0) - lb_soft[0]

    p0, xs = _inproj((x.reshape(T_LAT, D), ctx.reshape(T_CTX, D)), norm_g[0, 0], modm, 0,
                     [ab_w_in[0].astype(BF16)], BF16)
    zc, zs = _fnet_chan(p0)
    mix_a = jnp.concatenate([_fnet_pos(zc, zs, 0, L, 512, 1024),
                             _fnet_pos(zc, zs, T_LAT, LC, LC, LC)], axis=0)
    mix_g = _gmlp(p0, gmlp_v_g[0], gmlp_ws[0], gmlp_bs[0])
    xs, h2, e4, p4, r4, counts = _outproj(mix_a, mix_g, ab_w_out[0].astype(BF16), xs, T_ALL, norm_g[0, 1],
                                          norm_g[0, 2], modm, 0, router_w[0], router_b[0])
    xs = _moe(h2, e4, p4, r4, counts, xs, norm_g[0, 3], modm, 0,
              exp_w_gu, exp_b_gu[0], exp_w_down, exp_b_down[0])

    w_cd = [cd_w_in[0, :, :4 * HALF].astype(BF16), cd_w_in[0, :, 4 * HALF:].astype(BF16)]
    p1a, p1b = _inproj(xs, norm_g[1, 0], modm, 1, w_cd, F32)
    o_f = _hgrn(p1a, p1b, lb_all[1], False)
    hg = _hgrn(p1a, p1b, lb_all[1], True, o_f, hgrn_onorm_g[0])
    lru_args = (lru_conv_w[0], lru_conv_b[0], lru_wr[0], lru_br[0], lru_wi[0], lru_bi[0], lru_lambda[0])
    h_f = _lru(p1b, *lru_args, False)
    lr = _lru(p1b, *lru_args, True, h_f)
    xl, h2, e4, p4, r4, counts = _outproj(hg, lr, cd_w_out[0].astype(BF16), xs, T_LAT, norm_g[1, 1],
                                          norm_g[1, 2], modm, 1, router_w[1], router_b[1])
    xl = _moe(h2, e4, p4, r4, counts, xl, norm_g[1, 3], modm, 1,
              exp_w_gu, exp_b_gu[1], exp_w_down, exp_b_down[1])
    return xl.reshape(B, L, D)
```
